```python
import math
import jax, jax.numpy as jnp
from jax import lax
import numpy as np

D_MODEL = 4096
BATCH = 8
SEQ = 4096
DEPTH = 4

N_MIXERS = 4
N_META = 16
BRANCH = D_MODEL
ALPHA = (2.0 * DEPTH) ** 0.25
BETA = (8.0 * DEPTH) ** -0.25
LN_EPS = 1e-5
RMS_EPS = 1e-6
ROPE_BASE = 10000.0

LRU_HEADS = 16
LRU_BLOCK = BRANCH // LRU_HEADS
CONV_W = 4
LRU_C = 8.0

POOL_WINDOWS = (2, 4, 8, 16)
POOL_GROUP = BRANCH // len(POOL_WINDOWS)

MLA_HEADS = 32
MLA_NOPE = 128
MLA_ROPE = 64
MLA_V = BRANCH // MLA_HEADS
Q_LORA = 1024
KV_LORA = 512
ATT_BLOCK = 128

RET_HEADS = 16
RET_DK = BRANCH // RET_HEADS
RET_DV = BRANCH // RET_HEADS
RET_CHUNK = 128

kernel_name = 'hybrid_interleaved_rglru_pool_mla_retention'


def layer_norm(x, g, b):
    xf = x.astype(jnp.float32)
    mu = jnp.mean(xf, -1, keepdims=True)
    var = jnp.mean(jnp.square(xf - mu), -1, keepdims=True)
    return ((xf - mu) * lax.rsqrt(var + LN_EPS) * g.astype(jnp.float32) + b.astype(jnp.float32)).astype(x.dtype)


def rms_norm(x, g):
    xf = x.astype(jnp.float32)
    return (xf * lax.rsqrt(jnp.mean(jnp.square(xf), -1, keepdims=True) + RMS_EPS) * g.astype(jnp.float32)).astype(x.dtype)


def rope_tables(T, d):
    inv = ROPE_BASE ** (-jnp.arange(0, d, 2, dtype=jnp.float32) / d)
    ang = jnp.arange(T, dtype=jnp.float32)[:, None] * inv[None, :]
    return jnp.cos(ang), jnp.sin(ang)


def apply_rope(x, cos, sin):
    x1, x2 = jnp.split(x.astype(jnp.float32), 2, axis=-1)
    return jnp.concatenate([x1 * cos - x2 * sin, x2 * cos + x1 * sin], -1).astype(x.dtype)


def rglru_mixer(h, w_in, conv_w, conv_b, w_a, b_a, w_x, b_x, lam, w_out):
    B, T, _ = h.shape
    u, g = jnp.split(h @ w_in, 2, axis=-1)
    u = lax.conv_general_dilated(u, conv_w, window_strides=(1,), padding=[(CONV_W - 1, 0)],
                                 dimension_numbers=('NWC', 'WIO', 'NWC'),
                                 feature_group_count=BRANCH) + conv_b
    ub = u.reshape(B, T, LRU_HEADS, LRU_BLOCK)
    r = jax.nn.sigmoid(jnp.einsum('bthi,hij->bthj', ub, w_a).reshape(B, T, BRANCH) + b_a)
    i = jax.nn.sigmoid(jnp.einsum('bthi,hij->bthj', ub, w_x).reshape(B, T, BRANCH) + b_x)
    log_a = LRU_C * r.astype(jnp.float32) * (-jax.nn.softplus(-lam.astype(jnp.float32)))
    a = jnp.exp(log_a)
    xin = (u * i).astype(jnp.float32) * jnp.sqrt(-jnp.expm1(2.0 * log_a))

    def combine(left, right):
        a1, b1 = left
        a2, b2 = right
        return a1 * a2, a2 * b1 + b2

    _, hs = lax.associative_scan(combine, (a, xin), axis=1)
    return (hs.astype(h.dtype) * jax.nn.silu(g)) @ w_out


def pool_mixer(h, w_in, w_grp, scale, w_out):
    B, T, _ = h.shape
    u, g = jnp.split(h @ w_in, 2, axis=-1)
    uf = u.astype(jnp.float32)
    c = jnp.cumsum(uf, axis=1)
    t1 = jnp.arange(1, T + 1, dtype=jnp.float32)[:, None]
    outs = []
    for gi, w in enumerate(POOL_WINDOWS):
        sl = slice(gi * POOL_GROUP, (gi + 1) * POOL_GROUP)
        cg = c[..., sl]
        shifted = jnp.pad(cg, ((0, 0), (w, 0), (0, 0)))[:, :T]
        mean = (cg - shifted) / jnp.minimum(t1, float(w))
        outs.append(mean - uf[..., sl])
    p = jnp.stack(outs, axis=2).astype(h.dtype)
    mixed = jnp.einsum('btgc,gcd->btgd', p, w_grp).reshape(B, T, BRANCH) * scale
    return (mixed * jax.nn.silu(g)) @ w_out


def mla_mixer(h, w_in, q_norm, w_uq, kv_norm, w_ukv, w_out, cos, sin):
    B, T, _ = h.shape
    g, cq, ckv, kr = jnp.split(h @ w_in, [BRANCH, BRANCH + Q_LORA, BRANCH + Q_LORA + KV_LORA], axis=-1)
    q = (rms_norm(cq, q_norm) @ w_uq).reshape(B, T, MLA_HEADS, MLA_NOPE + MLA_ROPE)
    q_nope = q[..., :MLA_NOPE]
    q_rope = apply_rope(q[..., MLA_NOPE:], cos[:, None, :], sin[:, None, :])
    kv = (rms_norm(ckv, kv_norm) @ w_ukv).reshape(B, T, MLA_HEADS, MLA_NOPE + MLA_V)
    k_nope, v = kv[..., :MLA_NOPE], kv[..., MLA_NOPE:]
    k_rope = apply_rope(kr, cos, sin)
    sc = (MLA_NOPE + MLA_ROPE) ** -0.5
    n_blk = (T - N_META) // ATT_BLOCK
    blocks = [(0, N_META)] + [(N_META + j * ATT_BLOCK, N_META + (j + 1) * ATT_BLOCK) for j in range(n_blk)]
    outs = []
    for qs, qe in blocks:
        s = (jnp.einsum('bqhd,bkhd->bhqk', q_nope[:, qs:qe], k_nope[:, :qe], preferred_element_type=jnp.float32)
             + jnp.einsum('bqhd,bkd->bhqk', q_rope[:, qs:qe], k_rope[:, :qe], preferred_element_type=jnp.float32)) * sc
        mask = jnp.arange(qs, qe)[:, None] >= jnp.arange(qe)[None, :]
        s = jnp.where(mask, s, -1e30)
        pr = jax.nn.softmax(s, axis=-1).astype(v.dtype)
        outs.append(jnp.einsum('bhqk,bkhd->bqhd', pr, v[:, :qe]))
    o = jnp.concatenate(outs, axis=1).reshape(B, T, BRANCH)
    return (o * jax.nn.silu(g)) @ w_out


def retention_mixer(h, w_in, w_out, cos, sin):
    B, T, _ = h.shape
    q, k, v, g = jnp.split(h @ w_in, 4, axis=-1)
    q = apply_rope(q.reshape(B, T, RET_HEADS, RET_DK), cos[:, None, :], sin[:, None, :])
    k = apply_rope(k.reshape(B, T, RET_HEADS, RET_DK), cos[:, None, :], sin[:, None, :]) * (RET_DK ** -0.5)
    v = v.reshape(B, T, RET_HEADS, RET_DV)
    log_g = jnp.log(1.0 - 2.0 ** (-5.0 - jnp.arange(RET_HEADS, dtype=jnp.float32)))
    qf, kf, vf = [t.astype(jnp.float32).transpose(0, 2, 1, 3) for t in (q, k, v)]

    def chunk_step(state, qkv):
        qc, kc, vc = qkv
        C = qc.shape[2]
        j = jnp.arange(C, dtype=jnp.float32)
        diff = j[:, None] - j[None, :]
        decay = jnp.where(diff >= 0, jnp.exp(jnp.maximum(diff, 0.0)[None] * log_g[:, None, None]), 0.0)
        intra = jnp.einsum('bhcm,bhme->bhce', jnp.einsum('bhcd,bhmd->bhcm', qc, kc) * decay, vc)
        inter = jnp.einsum('bhcd,bhde->bhce', qc, state) * jnp.exp((j + 1.0)[None, :, None] * log_g[:, None, None])
        kw = kc * jnp.exp((C - 1.0 - j)[None, :, None] * log_g[:, None, None])
        new_state = jnp.exp(C * log_g)[None, :, None, None] * state + jnp.einsum('bhcd,bhce->bhde', kw, vc)
        return new_state, intra + inter

    state0 = jnp.zeros((B, RET_HEADS, RET_DK, RET_DV), jnp.float32)
    state, o_meta = chunk_step(state0, (qf[:, :, :N_META], kf[:, :, :N_META], vf[:, :, :N_META]))
    nc = (T - N_META) // RET_CHUNK

    def to_chunks(t):
        return t[:, :, N_META:].reshape(B, RET_HEADS, nc, RET_CHUNK, t.shape[-1]).transpose(2, 0, 1, 3, 4)

    _, o_real = lax.scan(chunk_step, state, (to_chunks(qf), to_chunks(kf), to_chunks(vf)))
    o_real = o_real.transpose(1, 2, 0, 3, 4).reshape(B, RET_HEADS, nc * RET_CHUNK, RET_DV)
    o = jnp.concatenate([o_meta, o_real], axis=2)
    mu = jnp.mean(o, -1, keepdims=True)
    var = jnp.mean(jnp.square(o - mu), -1, keepdims=True)
    o = (o - mu) * lax.rsqrt(var + LN_EPS)
    o = o.transpose(0, 2, 1, 3).reshape(B, T, BRANCH).astype(h.dtype)
    return (o * jax.nn.silu(g)) @ w_out


def _fwd_setup_inputs(seed: int = 0) -> dict:
    key = jax.random.key(seed)
    ks = iter(jax.random.split(key, 48))

    def nrm(shape, s):
        return jax.random.normal(next(ks), shape, jnp.float32) * s

    def gain(n):
        return 1.0 + nrm((n,), 0.02)

    d_in = D_MODEL ** -0.5
    d_out = BETA * BRANCH ** -0.5
    rad = jax.random.uniform(next(ks), (BRANCH,), jnp.float32, 0.9, 0.999)
    a0 = rad ** (1.0 / LRU_C)
    lam = jnp.log(a0) - jnp.log1p(-a0)
    return {
        'x': nrm((BATCH, SEQ, D_MODEL), 1.0),
        'meta_tokens': nrm((N_META, D_MODEL), 1.0),
        'l0_w_in': nrm((D_MODEL, 2 * BRANCH), d_in),
        'l0_conv_w': nrm((CONV_W, 1, BRANCH), CONV_W ** -0.5),
        'l0_conv_b': nrm((BRANCH,), 0.01),
        'l0_w_a': nrm((LRU_HEADS, LRU_BLOCK, LRU_BLOCK), LRU_BLOCK ** -0.5),
        'l0_b_a': nrm((BRANCH,), 0.01),
        'l0_w_x': nrm((LRU_HEADS, LRU_BLOCK, LRU_BLOCK), LRU_BLOCK ** -0.5),
        'l0_b_x': nrm((BRANCH,), 0.01),
        'l0_lam': lam,
        'l0_w_out': nrm((BRANCH, D_MODEL), d_out),
        'l0_ln_g': gain(D_MODEL),
        'l0_ln_b': nrm((D_MODEL,), 0.01),
        'l1_w_in': nrm((D_MODEL, 2 * BRANCH), d_in),
        'l1_w_grp': nrm((len(POOL_WINDOWS), POOL_GROUP, POOL_GROUP), POOL_GROUP ** -0.5),
        'l1_scale': gain(BRANCH),
        'l1_w_out': nrm((BRANCH, D_MODEL), d_out),
        'l1_ln_g': gain(D_MODEL),
        'l1_ln_b': nrm((D_MODEL,), 0.01),
        'l2_w_in': nrm((D_MODEL, BRANCH + Q_LORA + KV_LORA + MLA_ROPE), d_in),
        'l2_q_norm': gain(Q_LORA),
        'l2_w_uq': nrm((Q_LORA, MLA_HEADS * (MLA_NOPE + MLA_ROPE)), Q_LORA ** -0.5),
        'l2_kv_norm': gain(KV_LORA),
        'l2_w_ukv': nrm((KV_LORA, MLA_HEADS * (MLA_NOPE + MLA_V)), KV_LORA ** -0.5),
        'l2_w_out': nrm((BRANCH, D_MODEL), d_out),
        'l2_ln_g': gain(D_MODEL),
        'l2_ln_b': nrm((D_MODEL,), 0.01),
        'l3_w_in': nrm((D_MODEL, 4 * BRANCH), d_in),
        'l3_w_out': nrm((BRANCH, D_MODEL), d_out),
        'l3_ln_g': gain(D_MODEL),
        'l3_ln_b': nrm((D_MODEL,), 0.01),
    }


def _fwd_reference(x, meta_tokens,
              l0_w_in, l0_conv_w, l0_conv_b, l0_w_a, l0_b_a, l0_w_x, l0_b_x, l0_lam, l0_w_out, l0_ln_g, l0_ln_b,
              l1_w_in, l1_w_grp, l1_scale, l1_w_out, l1_ln_g, l1_ln_b,
              l2_w_in, l2_q_norm, l2_w_uq, l2_kv_norm, l2_w_ukv, l2_w_out, l2_ln_g, l2_ln_b,
              l3_w_in, l3_w_out, l3_ln_g, l3_ln_b):
    B = x.shape[0]
    meta = jnp.broadcast_to(meta_tokens.astype(x.dtype)[None], (B, N_META, D_MODEL))
    h = jnp.concatenate([meta, x], axis=1)
    T = h.shape[1]
    cos_mla, sin_mla = rope_tables(T, MLA_ROPE)
    cos_ret, sin_ret = rope_tables(T, RET_DK)
    layer_fns = (
        lambda t: rglru_mixer(t, l0_w_in, l0_conv_w, l0_conv_b, l0_w_a, l0_b_a, l0_w_x, l0_b_x, l0_lam, l0_w_out),
        lambda t: pool_mixer(t, l1_w_in, l1_w_grp, l1_scale, l1_w_out),
        lambda t: mla_mixer(t, l2_w_in, l2_q_norm, l2_w_uq, l2_kv_norm, l2_w_ukv, l2_w_out, cos_mla, sin_mla),
        lambda t: retention_mixer(t, l3_w_in, l3_w_out, cos_ret, sin_ret),
    )
    norms = ((l0_ln_g, l0_ln_b), (l1_ln_g, l1_ln_b), (l2_ln_g, l2_ln_b), (l3_ln_g, l3_ln_b))
    for i in range(DEPTH):
        h = layer_norm(ALPHA * h + layer_fns[i](h), norms[i][0], norms[i][1])
    return h[:, N_META:]


import jax as _jax
import jax.numpy as _jnp

TWIN_FORMAT = 'train_step'
FWD_PARAMS = ['x', 'meta_tokens', 'l0_w_in', 'l0_conv_w', 'l0_conv_b', 'l0_w_a', 'l0_b_a', 'l0_w_x', 'l0_b_x', 'l0_lam', 'l0_w_out', 'l0_ln_g', 'l0_ln_b', 'l1_w_in', 'l1_w_grp', 'l1_scale', 'l1_w_out', 'l1_ln_g', 'l1_ln_b', 'l2_w_in', 'l2_q_norm', 'l2_w_uq', 'l2_kv_norm', 'l2_w_ukv', 'l2_w_out', 'l2_ln_g', 'l2_ln_b', 'l3_w_in', 'l3_w_out', 'l3_ln_g', 'l3_ln_b']
TWIN_WEIGHTS = ['meta_tokens', 'l0_w_in', 'l0_conv_w', 'l0_conv_b', 'l0_w_a', 'l0_b_a', 'l0_w_x', 'l0_b_x', 'l0_lam', 'l0_w_out', 'l0_ln_g', 'l0_ln_b', 'l1_w_in', 'l1_w_grp', 'l1_scale', 'l1_w_out', 'l1_ln_g', 'l1_ln_b', 'l2_w_in', 'l2_q_norm', 'l2_w_uq', 'l2_kv_norm', 'l2_w_ukv', 'l2_w_out', 'l2_ln_g', 'l2_ln_b', 'l3_w_in', 'l3_w_out', 'l3_ln_g', 'l3_ln_b']
TWIN_DIFF_INPUT = 'x'
TWIN_INPUTS = ['x', 'meta_tokens', 'l0_w_in', 'l0_conv_w', 'l0_conv_b', 'l0_w_a', 'l0_b_a', 'l0_w_x', 'l0_b_x', 'l0_lam', 'l0_w_out', 'l0_ln_g', 'l0_ln_b', 'l1_w_in', 'l1_w_grp', 'l1_scale', 'l1_w_out', 'l1_ln_g', 'l1_ln_b', 'l2_w_in', 'l2_q_norm', 'l2_w_uq', 'l2_kv_norm', 'l2_w_ukv', 'l2_w_out', 'l2_ln_g', 'l2_ln_b', 'l3_w_in', 'l3_w_out', 'l3_ln_g', 'l3_ln_b', 'loss_target', 'm_meta_tokens', 'm_l0_w_in', 'm_l0_conv_w', 'm_l0_conv_b', 'm_l0_w_a', 'm_l0_b_a', 'm_l0_w_x', 'm_l0_b_x', 'm_l0_lam', 'm_l0_w_out', 'm_l0_ln_g', 'm_l0_ln_b', 'm_l1_w_in', 'm_l1_w_grp', 'm_l1_scale', 'm_l1_w_out', 'm_l1_ln_g', 'm_l1_ln_b', 'm_l2_w_in', 'm_l2_q_norm', 'm_l2_w_uq', 'm_l2_kv_norm', 'm_l2_w_ukv', 'm_l2_w_out', 'm_l2_ln_g', 'm_l2_ln_b', 'm_l3_w_in', 'm_l3_w_out', 'm_l3_ln_g', 'm_l3_ln_b', 'v_meta_tokens', 'v_l0_w_in', 'v_l0_conv_w', 'v_l0_conv_b', 'v_l0_w_a', 'v_l0_b_a', 'v_l0_w_x', 'v_l0_b_x', 'v_l0_lam', 'v_l0_w_out', 'v_l0_ln_g', 'v_l0_ln_b', 'v_l1_w_in', 'v_l1_w_grp', 'v_l1_scale', 'v_l1_w_out', 'v_l1_ln_g', 'v_l1_ln_b', 'v_l2_w_in', 'v_l2_q_norm', 'v_l2_w_uq', 'v_l2_kv_norm', 'v_l2_w_ukv', 'v_l2_w_out', 'v_l2_ln_g', 'v_l2_ln_b', 'v_l3_w_in', 'v_l3_w_out', 'v_l3_ln_g', 'v_l3_ln_b']
TWIN_OUTPUTS = ['loss', 'grad_x', 'grad_meta_tokens', 'grad_l0_w_in', 'grad_l0_conv_w', 'grad_l0_conv_b', 'grad_l0_w_a', 'grad_l0_b_a', 'grad_l0_w_x', 'grad_l0_b_x', 'grad_l0_lam', 'grad_l0_w_out', 'grad_l0_ln_g', 'grad_l0_ln_b', 'grad_l1_w_in', 'grad_l1_w_grp', 'grad_l1_scale', 'grad_l1_w_out', 'grad_l1_ln_g', 'grad_l1_ln_b', 'grad_l2_w_in', 'grad_l2_q_norm', 'grad_l2_w_uq', 'grad_l2_kv_norm', 'grad_l2_w_ukv', 'grad_l2_w_out', 'grad_l2_ln_g', 'grad_l2_ln_b', 'grad_l3_w_in', 'grad_l3_w_out', 'grad_l3_ln_g', 'grad_l3_ln_b', 'delta_meta_tokens', 'delta_l0_w_in', 'delta_l0_conv_w', 'delta_l0_conv_b', 'delta_l0_w_a', 'delta_l0_b_a', 'delta_l0_w_x', 'delta_l0_b_x', 'delta_l0_lam', 'delta_l0_w_out', 'delta_l0_ln_g', 'delta_l0_ln_b', 'delta_l1_w_in', 'delta_l1_w_grp', 'delta_l1_scale', 'delta_l1_w_out', 'delta_l1_ln_g', 'delta_l1_ln_b', 'delta_l2_w_in', 'delta_l2_q_norm', 'delta_l2_w_uq', 'delta_l2_kv_norm', 'delta_l2_w_ukv', 'delta_l2_w_out', 'delta_l2_ln_g', 'delta_l2_ln_b', 'delta_l3_w_in', 'delta_l3_w_out', 'delta_l3_ln_g', 'delta_l3_ln_b', 'new_m_meta_tokens', 'new_m_l0_w_in', 'new_m_l0_conv_w', 'new_m_l0_conv_b', 'new_m_l0_w_a', 'new_m_l0_b_a', 'new_m_l0_w_x', 'new_m_l0_b_x', 'new_m_l0_lam', 'new_m_l0_w_out', 'new_m_l0_ln_g', 'new_m_l0_ln_b', 'new_m_l1_w_in', 'new_m_l1_w_grp', 'new_m_l1_scale', 'new_m_l1_w_out', 'new_m_l1_ln_g', 'new_m_l1_ln_b', 'new_m_l2_w_in', 'new_m_l2_q_norm', 'new_m_l2_w_uq', 'new_m_l2_kv_norm', 'new_m_l2_w_ukv', 'new_m_l2_w_out', 'new_m_l2_ln_g', 'new_m_l2_ln_b', 'new_m_l3_w_in', 'new_m_l3_w_out', 'new_m_l3_ln_g', 'new_m_l3_ln_b', 'new_v_meta_tokens', 'new_v_l0_w_in', 'new_v_l0_conv_w', 'new_v_l0_conv_b', 'new_v_l0_w_a', 'new_v_l0_b_a', 'new_v_l0_w_x', 'new_v_l0_b_x', 'new_v_l0_lam', 'new_v_l0_w_out', 'new_v_l0_ln_g', 'new_v_l0_ln_b', 'new_v_l1_w_in', 'new_v_l1_w_grp', 'new_v_l1_scale', 'new_v_l1_w_out', 'new_v_l1_ln_g', 'new_v_l1_ln_b', 'new_v_l2_w_in', 'new_v_l2_q_norm', 'new_v_l2_w_uq', 'new_v_l2_kv_norm', 'new_v_l2_w_ukv', 'new_v_l2_w_out', 'new_v_l2_ln_g', 'new_v_l2_ln_b', 'new_v_l3_w_in', 'new_v_l3_w_out', 'new_v_l3_ln_g', 'new_v_l3_ln_b']
TWIN_LEAF_KINDS = {'loss': 'loss', 'grad_x': 'grad_x', 'grad_meta_tokens': 'grad_w', 'grad_l0_w_in': 'grad_w', 'grad_l0_conv_w': 'grad_w', 'grad_l0_conv_b': 'grad_w', 'grad_l0_w_a': 'grad_w', 'grad_l0_b_a': 'grad_w', 'grad_l0_w_x': 'grad_w', 'grad_l0_b_x': 'grad_w', 'grad_l0_lam': 'grad_w', 'grad_l0_w_out': 'grad_w', 'grad_l0_ln_g': 'grad_w', 'grad_l0_ln_b': 'grad_w', 'grad_l1_w_in': 'grad_w', 'grad_l1_w_grp': 'grad_w', 'grad_l1_scale': 'grad_w', 'grad_l1_w_out': 'grad_w', 'grad_l1_ln_g': 'grad_w', 'grad_l1_ln_b': 'grad_w', 'grad_l2_w_in': 'grad_w', 'grad_l2_q_norm': 'grad_w', 'grad_l2_w_uq': 'grad_w', 'grad_l2_kv_norm': 'grad_w', 'grad_l2_w_ukv': 'grad_w', 'grad_l2_w_out': 'grad_w', 'grad_l2_ln_g': 'grad_w', 'grad_l2_ln_b': 'grad_w', 'grad_l3_w_in': 'grad_w', 'grad_l3_w_out': 'grad_w', 'grad_l3_ln_g': 'grad_w', 'grad_l3_ln_b': 'grad_w', 'delta_meta_tokens': 'delta_w', 'delta_l0_w_in': 'delta_w', 'delta_l0_conv_w': 'delta_w', 'delta_l0_conv_b': 'delta_w', 'delta_l0_w_a': 'delta_w', 'delta_l0_b_a': 'delta_w', 'delta_l0_w_x': 'delta_w', 'delta_l0_b_x': 'delta_w', 'delta_l0_lam': 'delta_w', 'delta_l0_w_out': 'delta_w', 'delta_l0_ln_g': 'delta_w', 'delta_l0_ln_b': 'delta_w', 'delta_l1_w_in': 'delta_w', 'delta_l1_w_grp': 'delta_w', 'delta_l1_scale': 'delta_w', 'delta_l1_w_out': 'delta_w', 'delta_l1_ln_g': 'delta_w', 'delta_l1_ln_b': 'delta_w', 'delta_l2_w_in': 'delta_w', 'delta_l2_q_norm': 'delta_w', 'delta_l2_w_uq': 'delta_w', 'delta_l2_kv_norm': 'delta_w', 'delta_l2_w_ukv': 'delta_w', 'delta_l2_w_out': 'delta_w', 'delta_l2_ln_g': 'delta_w', 'delta_l2_ln_b': 'delta_w', 'delta_l3_w_in': 'delta_w', 'delta_l3_w_out': 'delta_w', 'delta_l3_ln_g': 'delta_w', 'delta_l3_ln_b': 'delta_w', 'new_m_meta_tokens': 'new_m', 'new_m_l0_w_in': 'new_m', 'new_m_l0_conv_w': 'new_m', 'new_m_l0_conv_b': 'new_m', 'new_m_l0_w_a': 'new_m', 'new_m_l0_b_a': 'new_m', 'new_m_l0_w_x': 'new_m', 'new_m_l0_b_x': 'new_m', 'new_m_l0_lam': 'new_m', 'new_m_l0_w_out': 'new_m', 'new_m_l0_ln_g': 'new_m', 'new_m_l0_ln_b': 'new_m', 'new_m_l1_w_in': 'new_m', 'new_m_l1_w_grp': 'new_m', 'new_m_l1_scale': 'new_m', 'new_m_l1_w_out': 'new_m', 'new_m_l1_ln_g': 'new_m', 'new_m_l1_ln_b': 'new_m', 'new_m_l2_w_in': 'new_m', 'new_m_l2_q_norm': 'new_m', 'new_m_l2_w_uq': 'new_m', 'new_m_l2_kv_norm': 'new_m', 'new_m_l2_w_ukv': 'new_m', 'new_m_l2_w_out': 'new_m', 'new_m_l2_ln_g': 'new_m', 'new_m_l2_ln_b': 'new_m', 'new_m_l3_w_in': 'new_m', 'new_m_l3_w_out': 'new_m', 'new_m_l3_ln_g': 'new_m', 'new_m_l3_ln_b': 'new_m', 'new_v_meta_tokens': 'new_v', 'new_v_l0_w_in': 'new_v', 'new_v_l0_conv_w': 'new_v', 'new_v_l0_conv_b': 'new_v', 'new_v_l0_w_a': 'new_v', 'new_v_l0_b_a': 'new_v', 'new_v_l0_w_x': 'new_v', 'new_v_l0_b_x': 'new_v', 'new_v_l0_lam': 'new_v', 'new_v_l0_w_out': 'new_v', 'new_v_l0_ln_g': 'new_v', 'new_v_l0_ln_b': 'new_v', 'new_v_l1_w_in': 'new_v', 'new_v_l1_w_grp': 'new_v', 'new_v_l1_scale': 'new_v', 'new_v_l1_w_out': 'new_v', 'new_v_l1_ln_g': 'new_v', 'new_v_l1_ln_b': 'new_v', 'new_v_l2_w_in': 'new_v', 'new_v_l2_q_norm': 'new_v', 'new_v_l2_w_uq': 'new_v', 'new_v_l2_kv_norm': 'new_v', 'new_v_l2_w_ukv': 'new_v', 'new_v_l2_w_out': 'new_v', 'new_v_l2_ln_g': 'new_v', 'new_v_l2_ln_b': 'new_v', 'new_v_l3_w_in': 'new_v', 'new_v_l3_w_out': 'new_v', 'new_v_l3_ln_g': 'new_v', 'new_v_l3_ln_b': 'new_v'}


def _forward(args):
    return _fwd_reference(*[args[k] for k in FWD_PARAMS])


def _output_shape():
    out = _jax.eval_shape(lambda: _forward(_fwd_setup_inputs(0)))
    return out.shape, out.dtype

N_MICROBATCH = 1
ADAM_LR = 0.001
ADAM_B1 = 0.9
ADAM_B2 = 0.999
ADAM_EPS = 1e-08
ADAM_WD = 0.01
ADAM_STEP = 10
PER_EXAMPLE_BATCH_AXIS = {'x': 0, 'loss_target': 0}
SHARED_INPUTS = []
_WEIGHT_DTYPES = {'meta_tokens': _jnp.float32, 'l0_w_in': _jnp.float32, 'l0_conv_w': _jnp.float32, 'l0_conv_b': _jnp.float32, 'l0_w_a': _jnp.float32, 'l0_b_a': _jnp.float32, 'l0_w_x': _jnp.float32, 'l0_b_x': _jnp.float32, 'l0_lam': _jnp.float32, 'l0_w_out': _jnp.float32, 'l0_ln_g': _jnp.float32, 'l0_ln_b': _jnp.float32, 'l1_w_in': _jnp.float32, 'l1_w_grp': _jnp.float32, 'l1_scale': _jnp.float32, 'l1_w_out': _jnp.float32, 'l1_ln_g': _jnp.float32, 'l1_ln_b': _jnp.float32, 'l2_w_in': _jnp.float32, 'l2_q_norm': _jnp.float32, 'l2_w_uq': _jnp.float32, 'l2_kv_norm': _jnp.float32, 'l2_w_ukv': _jnp.float32, 'l2_w_out': _jnp.float32, 'l2_ln_g': _jnp.float32, 'l2_ln_b': _jnp.float32, 'l3_w_in': _jnp.float32, 'l3_w_out': _jnp.float32, 'l3_ln_g': _jnp.float32, 'l3_ln_b': _jnp.float32}
MOMENT_SCALE = {'meta_tokens': 9.750763e-04, 'l0_w_in': 4.322926e-03, 'l0_conv_w': 4.766841e-03, 'l0_conv_b': 5.815668e-02, 'l0_w_a': 1.256071e-03, 'l0_b_a': 1.121850e-03, 'l0_w_x': 2.232811e-03, 'l0_b_x': 1.807829e-03, 'l0_lam': 2.256320e-03, 'l0_w_out': 1.017996e-02, 'l0_ln_g': 2.592664e-01, 'l0_ln_b': 6.440687e-02, 'l1_w_in': 6.004963e-03, 'l1_w_grp': 5.910455e-03, 'l1_scale': 6.039564e-03, 'l1_w_out': 1.403876e-02, 'l1_ln_g': 2.655487e-01, 'l1_ln_b': 6.433107e-02, 'l2_w_in': 1.952682e-03, 'l2_q_norm': 2.183614e-03, 'l2_w_uq': 8.923609e-04, 'l2_kv_norm': 4.525442e-03, 'l2_w_ukv': 1.065835e-03, 'l2_w_out': 2.883241e-03, 'l2_ln_g': 2.657517e-01, 'l2_ln_b': 6.428192e-02, 'l3_w_in': 6.622087e-03, 'l3_w_out': 1.554544e-02, 'l3_ln_g': 8.008020e+00, 'l3_ln_b': 1.300363e-01}


def _to_microbatches(a, axis):
    t = _jnp.moveaxis(a, axis, 0)
    t = t.reshape((N_MICROBATCH, t.shape[0] // N_MICROBATCH) + t.shape[1:])
    return _jnp.moveaxis(t, 1, axis + 1)


def setup_inputs(seed: int = 0) -> dict:
    inp = _fwd_setup_inputs(seed)
    key = _jax.random.fold_in(_jax.random.key(seed), 7919)
    shape, _ = _output_shape()
    out = dict(inp)
    out["loss_target"] = _jax.random.normal(_jax.random.fold_in(key, 0), shape, _jnp.float32)
    for i, name in enumerate(TWIN_WEIGHTS):
        w = inp[name].astype(_jnp.float32)
        if MOMENT_SCALE is None:
            s = _jnp.sqrt(_jnp.mean(_jnp.square(w)) + 1e-30)
        else:
            s = MOMENT_SCALE[name]
        km, kv = _jax.random.split(_jax.random.fold_in(key, i + 1))
        out[name] = w
        out["m_" + name] = s * _jax.random.normal(km, w.shape, _jnp.float32)
        out["v_" + name] = (s * s) * _jax.random.uniform(kv, w.shape, _jnp.float32, 0.5, 1.5)
    if N_MICROBATCH > 1:
        for name, axis in PER_EXAMPLE_BATCH_AXIS.items():
            out[name] = _to_microbatches(out[name], axis)
    return {'x': out['x'], 'meta_tokens': out['meta_tokens'], 'l0_w_in': out['l0_w_in'], 'l0_conv_w': out['l0_conv_w'], 'l0_conv_b': out['l0_conv_b'], 'l0_w_a': out['l0_w_a'], 'l0_b_a': out['l0_b_a'], 'l0_w_x': out['l0_w_x'], 'l0_b_x': out['l0_b_x'], 'l0_lam': out['l0_lam'], 'l0_w_out': out['l0_w_out'], 'l0_ln_g': out['l0_ln_g'], 'l0_ln_b': out['l0_ln_b'], 'l1_w_in': out['l1_w_in'], 'l1_w_grp': out['l1_w_grp'], 'l1_scale': out['l1_scale'], 'l1_w_out': out['l1_w_out'], 'l1_ln_g': out['l1_ln_g'], 'l1_ln_b': out['l1_ln_b'], 'l2_w_in': out['l2_w_in'], 'l2_q_norm': out['l2_q_norm'], 'l2_w_uq': out['l2_w_uq'], 'l2_kv_norm': out['l2_kv_norm'], 'l2_w_ukv': out['l2_w_ukv'], 'l2_w_out': out['l2_w_out'], 'l2_ln_g': out['l2_ln_g'], 'l2_ln_b': out['l2_ln_b'], 'l3_w_in': out['l3_w_in'], 'l3_w_out': out['l3_w_out'], 'l3_ln_g': out['l3_ln_g'], 'l3_ln_b': out['l3_ln_b'], 'loss_target': out['loss_target'], 'm_meta_tokens': out['m_meta_tokens'], 'm_l0_w_in': out['m_l0_w_in'], 'm_l0_conv_w': out['m_l0_conv_w'], 'm_l0_conv_b': out['m_l0_conv_b'], 'm_l0_w_a': out['m_l0_w_a'], 'm_l0_b_a': out['m_l0_b_a'], 'm_l0_w_x': out['m_l0_w_x'], 'm_l0_b_x': out['m_l0_b_x'], 'm_l0_lam': out['m_l0_lam'], 'm_l0_w_out': out['m_l0_w_out'], 'm_l0_ln_g': out['m_l0_ln_g'], 'm_l0_ln_b': out['m_l0_ln_b'], 'm_l1_w_in': out['m_l1_w_in'], 'm_l1_w_grp': out['m_l1_w_grp'], 'm_l1_scale': out['m_l1_scale'], 'm_l1_w_out': out['m_l1_w_out'], 'm_l1_ln_g': out['m_l1_ln_g'], 'm_l1_ln_b': out['m_l1_ln_b'], 'm_l2_w_in': out['m_l2_w_in'], 'm_l2_q_norm': out['m_l2_q_norm'], 'm_l2_w_uq': out['m_l2_w_uq'], 'm_l2_kv_norm': out['m_l2_kv_norm'], 'm_l2_w_ukv': out['m_l2_w_ukv'], 'm_l2_w_out': out['m_l2_w_out'], 'm_l2_ln_g': out['m_l2_ln_g'], 'm_l2_ln_b': out['m_l2_ln_b'], 'm_l3_w_in': out['m_l3_w_in'], 'm_l3_w_out': out['m_l3_w_out'], 'm_l3_ln_g': out['m_l3_ln_g'], 'm_l3_ln_b': out['m_l3_ln_b'], 'v_meta_tokens': out['v_meta_tokens'], 'v_l0_w_in': out['v_l0_w_in'], 'v_l0_conv_w': out['v_l0_conv_w'], 'v_l0_conv_b': out['v_l0_conv_b'], 'v_l0_w_a': out['v_l0_w_a'], 'v_l0_b_a': out['v_l0_b_a'], 'v_l0_w_x': out['v_l0_w_x'], 'v_l0_b_x': out['v_l0_b_x'], 'v_l0_lam': out['v_l0_lam'], 'v_l0_w_out': out['v_l0_w_out'], 'v_l0_ln_g': out['v_l0_ln_g'], 'v_l0_ln_b': out['v_l0_ln_b'], 'v_l1_w_in': out['v_l1_w_in'], 'v_l1_w_grp': out['v_l1_w_grp'], 'v_l1_scale': out['v_l1_scale'], 'v_l1_w_out': out['v_l1_w_out'], 'v_l1_ln_g': out['v_l1_ln_g'], 'v_l1_ln_b': out['v_l1_ln_b'], 'v_l2_w_in': out['v_l2_w_in'], 'v_l2_q_norm': out['v_l2_q_norm'], 'v_l2_w_uq': out['v_l2_w_uq'], 'v_l2_kv_norm': out['v_l2_kv_norm'], 'v_l2_w_ukv': out['v_l2_w_ukv'], 'v_l2_w_out': out['v_l2_w_out'], 'v_l2_ln_g': out['v_l2_ln_g'], 'v_l2_ln_b': out['v_l2_ln_b'], 'v_l3_w_in': out['v_l3_w_in'], 'v_l3_w_out': out['v_l3_w_out'], 'v_l3_ln_g': out['v_l3_ln_g'], 'v_l3_ln_b': out['v_l3_ln_b']}


def _loss(weights, diff, rest, loss_target):
    with _jax.named_scope("forward"):
        args = {**rest, TWIN_DIFF_INPUT: diff, **{k: w.astype(_WEIGHT_DTYPES[k]) for k, w in weights.items()}}
        y = _forward(args)
    with _jax.named_scope("loss_head"):
        err = _jnp.square(y.astype(_jnp.float32) - loss_target)
        return 0.5 * _jnp.sum(_jnp.mean(err, axis=-1)) if err.ndim else 0.5 * err


def _adamw(w, g, m, v):
    m = ADAM_B1 * m + (1.0 - ADAM_B1) * g
    v = ADAM_B2 * v + (1.0 - ADAM_B2) * _jnp.square(g)
    m_hat = m / (1.0 - ADAM_B1 ** ADAM_STEP)
    v_hat = v / (1.0 - ADAM_B2 ** ADAM_STEP)
    delta = -ADAM_LR * (m_hat / (_jnp.sqrt(v_hat) + ADAM_EPS) + ADAM_WD * w)
    return delta, m, v


def reference(x, meta_tokens, l0_w_in, l0_conv_w, l0_conv_b, l0_w_a, l0_b_a, l0_w_x, l0_b_x, l0_lam, l0_w_out, l0_ln_g, l0_ln_b, l1_w_in, l1_w_grp, l1_scale, l1_w_out, l1_ln_g, l1_ln_b, l2_w_in, l2_q_norm, l2_w_uq, l2_kv_norm, l2_w_ukv, l2_w_out, l2_ln_g, l2_ln_b, l3_w_in, l3_w_out, l3_ln_g, l3_ln_b, loss_target, m_meta_tokens, m_l0_w_in, m_l0_conv_w, m_l0_conv_b, m_l0_w_a, m_l0_b_a, m_l0_w_x, m_l0_b_x, m_l0_lam, m_l0_w_out, m_l0_ln_g, m_l0_ln_b, m_l1_w_in, m_l1_w_grp, m_l1_scale, m_l1_w_out, m_l1_ln_g, m_l1_ln_b, m_l2_w_in, m_l2_q_norm, m_l2_w_uq, m_l2_kv_norm, m_l2_w_ukv, m_l2_w_out, m_l2_ln_g, m_l2_ln_b, m_l3_w_in, m_l3_w_out, m_l3_ln_g, m_l3_ln_b, v_meta_tokens, v_l0_w_in, v_l0_conv_w, v_l0_conv_b, v_l0_w_a, v_l0_b_a, v_l0_w_x, v_l0_b_x, v_l0_lam, v_l0_w_out, v_l0_ln_g, v_l0_ln_b, v_l1_w_in, v_l1_w_grp, v_l1_scale, v_l1_w_out, v_l1_ln_g, v_l1_ln_b, v_l2_w_in, v_l2_q_norm, v_l2_w_uq, v_l2_kv_norm, v_l2_w_ukv, v_l2_w_out, v_l2_ln_g, v_l2_ln_b, v_l3_w_in, v_l3_w_out, v_l3_ln_g, v_l3_ln_b):
    given = dict(x=x, meta_tokens=meta_tokens, l0_w_in=l0_w_in, l0_conv_w=l0_conv_w, l0_conv_b=l0_conv_b, l0_w_a=l0_w_a, l0_b_a=l0_b_a, l0_w_x=l0_w_x, l0_b_x=l0_b_x, l0_lam=l0_lam, l0_w_out=l0_w_out, l0_ln_g=l0_ln_g, l0_ln_b=l0_ln_b, l1_w_in=l1_w_in, l1_w_grp=l1_w_grp, l1_scale=l1_scale, l1_w_out=l1_w_out, l1_ln_g=l1_ln_g, l1_ln_b=l1_ln_b, l2_w_in=l2_w_in, l2_q_norm=l2_q_norm, l2_w_uq=l2_w_uq, l2_kv_norm=l2_kv_norm, l2_w_ukv=l2_w_ukv, l2_w_out=l2_w_out, l2_ln_g=l2_ln_g, l2_ln_b=l2_ln_b, l3_w_in=l3_w_in, l3_w_out=l3_w_out, l3_ln_g=l3_ln_g, l3_ln_b=l3_ln_b, loss_target=loss_target, m_meta_tokens=m_meta_tokens, m_l0_w_in=m_l0_w_in, m_l0_conv_w=m_l0_conv_w, m_l0_conv_b=m_l0_conv_b, m_l0_w_a=m_l0_w_a, m_l0_b_a=m_l0_b_a, m_l0_w_x=m_l0_w_x, m_l0_b_x=m_l0_b_x, m_l0_lam=m_l0_lam, m_l0_w_out=m_l0_w_out, m_l0_ln_g=m_l0_ln_g, m_l0_ln_b=m_l0_ln_b, m_l1_w_in=m_l1_w_in, m_l1_w_grp=m_l1_w_grp, m_l1_scale=m_l1_scale, m_l1_w_out=m_l1_w_out, m_l1_ln_g=m_l1_ln_g, m_l1_ln_b=m_l1_ln_b, m_l2_w_in=m_l2_w_in, m_l2_q_norm=m_l2_q_norm, m_l2_w_uq=m_l2_w_uq, m_l2_kv_norm=m_l2_kv_norm, m_l2_w_ukv=m_l2_w_ukv, m_l2_w_out=m_l2_w_out, m_l2_ln_g=m_l2_ln_g, m_l2_ln_b=m_l2_ln_b, m_l3_w_in=m_l3_w_in, m_l3_w_out=m_l3_w_out, m_l3_ln_g=m_l3_ln_g, m_l3_ln_b=m_l3_ln_b, v_meta_tokens=v_meta_tokens, v_l0_w_in=v_l0_w_in, v_l0_conv_w=v_l0_conv_w, v_l0_conv_b=v_l0_conv_b, v_l0_w_a=v_l0_w_a, v_l0_b_a=v_l0_b_a, v_l0_w_x=v_l0_w_x, v_l0_b_x=v_l0_b_x, v_l0_lam=v_l0_lam, v_l0_w_out=v_l0_w_out, v_l0_ln_g=v_l0_ln_g, v_l0_ln_b=v_l0_ln_b, v_l1_w_in=v_l1_w_in, v_l1_w_grp=v_l1_w_grp, v_l1_scale=v_l1_scale, v_l1_w_out=v_l1_w_out, v_l1_ln_g=v_l1_ln_g, v_l1_ln_b=v_l1_ln_b, v_l2_w_in=v_l2_w_in, v_l2_q_norm=v_l2_q_norm, v_l2_w_uq=v_l2_w_uq, v_l2_kv_norm=v_l2_kv_norm, v_l2_w_ukv=v_l2_w_ukv, v_l2_w_out=v_l2_w_out, v_l2_ln_g=v_l2_ln_g, v_l2_ln_b=v_l2_ln_b, v_l3_w_in=v_l3_w_in, v_l3_w_out=v_l3_w_out, v_l3_ln_g=v_l3_ln_g, v_l3_ln_b=v_l3_ln_b)
    weights = {n: given[n] for n in TWIN_WEIGHTS}
    shared = {n: given[n] for n in SHARED_INPUTS}
    per_example = {n: given[n] for n in ['x']}
    grad_fn = _jax.value_and_grad(_loss, argnums=(0, 1))

    def one_microbatch(ex, loss_target):
        ex = dict(ex)
        diff = ex.pop(TWIN_DIFF_INPUT)
        return grad_fn(weights, diff, {**shared, **ex}, loss_target)

    if N_MICROBATCH == 1:
        loss, (grad_w, grad_x) = one_microbatch(per_example, given["loss_target"])
    else:
        def body(carry, xs):
            loss_sum, grad_sum = carry
            l_k, (gw_k, gx_k) = one_microbatch(xs[0], xs[1])
            with _jax.named_scope("update"):
                return (loss_sum + l_k, _jax.tree.map(_jnp.add, grad_sum, gw_k)), gx_k

        init = (_jnp.zeros((), _jnp.float32), _jax.tree.map(_jnp.zeros_like, weights))
        (loss, grad_w), grad_x = _jax.lax.scan(body, init, (per_example, given["loss_target"]))
    with _jax.named_scope("update"):
        delta_w, new_m, new_v = {}, {}, {}
        for n in TWIN_WEIGHTS:
            delta_w[n], new_m[n], new_v[n] = _adamw(weights[n], grad_w[n], given["m_" + n], given["v_" + n])
    return (loss, grad_x, *[grad_w[n] for n in TWIN_WEIGHTS], *[delta_w[n] for n in TWIN_WEIGHTS],
            *[new_m[n] for n in TWIN_WEIGHTS], *[new_v[n] for n in TWIN_WEIGHTS])
```

```python
import functools
import math

import jax
import jax.numpy as jnp
from jax import lax
from jax.experimental import pallas as pl
from jax.experimental.pallas import tpu as pltpu

F32 = jnp.float32
BF = jnp.bfloat16

N_DEV = 8
N_META = 16
ALPHA = (2.0 * 4) ** 0.25
LN_EPS = 1e-5
RMS_EPS = 1e-6
ROPE_BASE = 10000.0
LRU_HEADS = 16
CONV_W = 4
LRU_C = 8.0
POOL_WINDOWS = (2, 4, 8, 16)
MLA_HEADS = 32
MLA_NOPE = 128
MLA_ROPE = 64
MLA_QK = 256
Q_LORA = 1024
KV_LORA = 512
RET_HEADS = 16
ADAM_LR = 0.001
ADAM_B1 = 0.9
ADAM_B2 = 0.999
ADAM_EPS = 1e-08
ADAM_WD = 0.01
ADAM_STEP = 10

ROW_ALIGN = 128
VMEM_LIMIT_BYTES = 56 * 1024 * 1024
TT_PREFS = (128, 64, 32, 16, 8)
ATT_PREFS = (384, 256, 128)
MM_M_PREFS = (1408, 1024, 512, 384, 256, 128)
MM_N_PREFS = (1024, 640, 512, 384, 256, 128)
MM_K_PREFS = (512, 384, 256, 128)
ADAM_BLOCK_ELEMS = 128 * 1024

WEIGHTS = ['meta_tokens', 'l0_w_in', 'l0_conv_w', 'l0_conv_b', 'l0_w_a', 'l0_b_a', 'l0_w_x', 'l0_b_x', 'l0_lam',
           'l0_w_out', 'l0_ln_g', 'l0_ln_b', 'l1_w_in', 'l1_w_grp', 'l1_scale', 'l1_w_out', 'l1_ln_g', 'l1_ln_b',
           'l2_w_in', 'l2_q_norm', 'l2_w_uq', 'l2_kv_norm', 'l2_w_ukv', 'l2_w_out', 'l2_ln_g', 'l2_ln_b',
           'l3_w_in', 'l3_w_out', 'l3_ln_g', 'l3_ln_b']
BIG = ['l0_w_in', 'l0_w_out', 'l1_w_in', 'l1_w_grp', 'l1_w_out', 'l2_w_in', 'l2_w_uq', 'l2_w_ukv', 'l2_w_out',
       'l3_w_in', 'l3_w_out']
SHARDED_F32 = ['meta_tokens', 'l0_conv_w', 'l0_w_a', 'l0_w_x']
REPLICATED = [n for n in WEIGHTS if n not in BIG and n not in SHARDED_F32]


def _pick(n, prefs):
    for p in prefs:
        if n % p == 0:
            return p
    return n


def _params(sem=None):
    kw = dict(vmem_limit_bytes=VMEM_LIMIT_BYTES)
    if sem is not None:
        kw['dimension_semantics'] = sem
    return pltpu.CompilerParams(**kw)


@functools.partial(jax.custom_vjp, nondiff_argnums=(1, 2))
def _roll(x, shift, axis):
    return pltpu.roll(x, shift, axis)


def _roll_fwd(x, shift, axis):
    return pltpu.roll(x, shift, axis), None


def _roll_bwd(shift, axis, _, g):
    n = g.shape[axis]
    return (pltpu.roll(g, (n - shift) % n, axis),)


_roll.defvjp(_roll_fwd, _roll_bwd)


@jax.custom_vjp
def _bdot(x, w):
    return jnp.dot(x.astype(BF), w.astype(BF), preferred_element_type=F32)


def _bdot_fwd(x, w):
    return _bdot(x, w), (x, w)


def _bdot_bwd(res, g):
    x, w = res
    gb = g.astype(BF)
    dx = lax.dot_general(gb, w.astype(BF), (((1,), (1,)), ((), ())), preferred_element_type=F32)
    dw = lax.dot_general(x.astype(BF), gb, (((0,), (0,)), ((), ())), preferred_element_type=F32)
    return dx, dw


_bdot.defvjp(_bdot_fwd, _bdot_bwd)


def _silu(g):
    return g * jax.nn.sigmoid(g)


def _softplus(x):
    return jnp.maximum(x, 0.0) + jnp.log1p(jnp.exp(-jnp.abs(x)))


def _mm_call(a, b, *, mode, out_dtype, name):
    if mode == 'nn':
        (mo, kc), (_, no) = a.shape, b.shape
    elif mode == 'nt':
        (mo, kc), (no, _) = a.shape, b.shape
    else:
        (kc, mo), (_, no) = a.shape, b.shape
    tm = _pick(mo, MM_M_PREFS)
    tn = no if no <= 2048 and no % 512 != 0 else _pick(no, MM_N_PREFS)
    tk = _pick(kc, MM_K_PREFS)
    nk = kc // tk
    if mode == 'nn':
        a_spec = pl.BlockSpec((tm, tk), lambda i, j, k: (i, k))
        b_spec = pl.BlockSpec((tk, tn), lambda i, j, k: (k, j))
        dims = (((1,), (0,)), ((), ()))
    elif mode == 'nt':
        a_spec = pl.BlockSpec((tm, tk), lambda i, j, k: (i, k))
        b_spec = pl.BlockSpec((tn, tk), lambda i, j, k: (j, k))
        dims = (((1,), (1,)), ((), ()))
    else:
        a_spec = pl.BlockSpec((tk, tm), lambda i, j, k: (k, i))
        b_spec = pl.BlockSpec((tk, tn), lambda i, j, k: (k, j))
        dims = (((0,), (0,)), ((), ()))

    def body(a_ref, b_ref, o_ref, acc_ref):
        k = pl.program_id(2)

        @pl.when(k == 0)
        def _():
            acc_ref[...] = jnp.zeros_like(acc_ref)

        acc_ref[...] += lax.dot_general(a_ref[...].astype(BF), b_ref[...].astype(BF), dims,
                                        preferred_element_type=F32)

        @pl.when(k == nk - 1)
        def _():
            o_ref[...] = acc_ref[...].astype(o_ref.dtype)

    return pl.pallas_call(
        body, name=name,
        grid=(mo // tm, no // tn, nk),
        in_specs=[a_spec, b_spec],
        out_specs=pl.BlockSpec((tm, tn), lambda i, j, k: (i, j)),
        out_shape=jax.ShapeDtypeStruct((mo, no), out_dtype),
        scratch_shapes=[pltpu.VMEM((tm, tn), F32)],
        compiler_params=_params(("parallel", "parallel", "arbitrary")),
    )(a, b)


def mm(x, w, name):
    @jax.custom_vjp
    def op(x, w):
        return _mm_call(x, w, mode='nn', out_dtype=F32, name=name)

    def op_fwd(x, w):
        return op(x, w), (x, w)

    def op_bwd(res, dy):
        x, w = res
        dx = _mm_call(dy, w, mode='nt', out_dtype=F32, name=name + '_dx')
        dw = _mm_call(x, dy, mode='tn', out_dtype=w.dtype, name=name + '_dw')
        return dx, dw

    op.defvjp(op_fwd, op_bwd)
    return op(x, w)


def _full_spec(p):
    nd = p.ndim
    return pl.BlockSpec(p.shape, lambda i: (0,) * nd)


def _load_rows(refs, pos, n_rows, halo, step_is_first):
    cur, prev = [], []
    for r in range(n_rows):
        cur.append(refs[pos][...])
        pos += 1
        if r in halo:
            keep = jnp.where(step_is_first, 0.0, 1.0).astype(F32)
            prev.append(refs[pos][...] * keep)
            pos += 1
        else:
            prev.append(None)
    return cur, prev, pos


def _join(cur, prev):
    return [c if p is None else jnp.concatenate([p, c], axis=0) for c, p in zip(cur, prev)]


def _rw_fwd(f, rows, params, outs, n_reduce, halo, tt, name):
    t_len = rows[0].shape[0]
    nt = t_len // tt
    n_rows, n_par, n_out = len(rows), len(params), len(outs)

    def body(*refs):
        i = pl.program_id(0)
        cur, prev, pos = _load_rows(refs, 0, n_rows, halo, i == 0)
        pvals = [refs[pos + k][...] for k in range(n_par)]
        pos += n_par
        res = f(_join(cur, prev), pvals)
        for k in range(n_out):
            refs[pos + k][...] = res[k].astype(refs[pos + k].dtype)
        pos += n_out
        for k in range(n_reduce):
            ref, val = refs[pos + k], res[n_out + k]

            @pl.when(i == 0)
            def _():
                ref[...] = val

            @pl.when(i > 0)
            def _():
                ref[...] += val

    in_specs, args = [], []
    for r, x in enumerate(rows):
        c = x.shape[1]
        in_specs.append(pl.BlockSpec((tt, c), lambda i: (i, 0)))
        args.append(x)
        if r in halo:
            in_specs.append(pl.BlockSpec((tt, c), lambda i: (jnp.maximum(i - 1, 0), 0)))
            args.append(x)
    for p in params:
        in_specs.append(_full_spec(p))
        args.append(p)
    out_specs = [pl.BlockSpec((tt, c), lambda i: (i, 0)) for c, _ in outs]
    out_shape = [jax.ShapeDtypeStruct((t_len, c), dt) for c, dt in outs]
    for _ in range(n_reduce):
        out_specs.append(pl.BlockSpec((1, 1), lambda i: (0, 0)))
        out_shape.append(jax.ShapeDtypeStruct((1, 1), F32))
    return pl.pallas_call(
        body, name=name, grid=(nt,), in_specs=in_specs, out_specs=out_specs, out_shape=out_shape,
        compiler_params=_params(("arbitrary",)),
    )(*args)


def _rw_bwd(f, rows, params, cts, outs, n_reduce, halo, nd_rows, nd_params, tt, name):
    t_len = rows[0].shape[0]
    nt = t_len // tt
    n_rows, n_par, n_out = len(rows), len(params), len(outs)
    d_rows = [r for r in range(n_rows) if r not in nd_rows]
    d_pars = [k for k in range(n_par) if k not in nd_params]
    h_rows = [r for r in d_rows if r in halo]

    def blk(j):
        return nt - 1 - j

    def body(*refs):
        j = pl.program_id(0)
        cur, prev, pos = _load_rows(refs, 0, n_rows, halo, blk(j) == 0)
        pvals = [refs[pos + k][...] for k in range(n_par)]
        pos += n_par
        ct_vals = [refs[pos + k][...] for k in range(n_out + n_reduce)]
        pos += n_out + n_reduce

        def g(dcur, dprev, dpar):
            c, p, q = list(cur), list(prev), list(pvals)
            for r, v in zip(d_rows, dcur):
                c[r] = v
            for r, v in zip(h_rows, dprev):
                p[r] = v
            for k, v in zip(d_pars, dpar):
                q[k] = v
            res = f(_join(c, p), q)
            return tuple(res[k].astype(outs[k][1]) for k in range(n_out)) + tuple(res[n_out:])

        _, vjp = jax.vjp(g, [cur[r] for r in d_rows], [prev[r] for r in h_rows], [pvals[k] for k in d_pars])
        g_cur, g_prev, g_par = vjp(tuple(ct_vals))

        drow_refs = refs[pos:pos + len(d_rows)]
        pos += len(d_rows)
        dpar_refs = refs[pos:pos + len(d_pars)]
        pos += len(d_pars)
        carry_refs = refs[pos:pos + len(h_rows)]

        for n, r in enumerate(d_rows):
            if r in halo:
                cref = carry_refs[h_rows.index(r)]

                @pl.when(j == 0)
                def _():
                    cref[...] = jnp.zeros_like(cref)

                drow_refs[n][...] = g_cur[n] + cref[...]
                cref[...] = g_prev[h_rows.index(r)]
            else:
                drow_refs[n][...] = g_cur[n]
        for n in range(len(d_pars)):
            ref, val = dpar_refs[n], g_par[n]

            @pl.when(j == 0)
            def _():
                ref[...] = val

            @pl.when(j > 0)
            def _():
                ref[...] += val

    in_specs, args = [], []
    for r, x in enumerate(rows):
        c = x.shape[1]
        in_specs.append(pl.BlockSpec((tt, c), lambda j: (blk(j), 0)))
        args.append(x)
        if r in halo:
            in_specs.append(pl.BlockSpec((tt, c), lambda j: (jnp.maximum(blk(j) - 1, 0), 0)))
            args.append(x)
    for p in params:
        in_specs.append(_full_spec(p))
        args.append(p)
    for k, (c, _) in enumerate(outs):
        in_specs.append(pl.BlockSpec((tt, c), lambda j: (blk(j), 0)))
        args.append(cts[k])
    for k in range(n_reduce):
        in_specs.append(pl.BlockSpec((1, 1), lambda j: (0, 0)))
        args.append(cts[n_out + k])
    out_specs, out_shape, scratch = [], [], []
    for r in d_rows:
        c = rows[r].shape[1]
        out_specs.append(pl.BlockSpec((tt, c), lambda j: (blk(j), 0)))
        out_shape.append(jax.ShapeDtypeStruct((t_len, c), F32))
        if r in halo:
            scratch.append(pltpu.VMEM((tt, c), F32))
    for k in d_pars:
        out_specs.append(_full_spec(params[k]))
        out_shape.append(jax.ShapeDtypeStruct(params[k].shape, F32))
    res = pl.pallas_call(
        body, name=name + '_bwd', grid=(nt,), in_specs=in_specs, out_specs=out_specs, out_shape=out_shape,
        scratch_shapes=scratch, compiler_params=_params(("arbitrary",)),
    )(*args)
    g_rows = [jnp.zeros_like(x) for x in rows]
    g_pars = [jnp.zeros_like(p) for p in params]
    for n, r in enumerate(d_rows):
        g_rows[r] = res[n]
    for n, k in enumerate(d_pars):
        g_pars[k] = res[len(d_rows) + n]
    return g_rows, g_pars


def rowwise(f, rows, params, outs, *, name, n_reduce=0, halo=(), nd_rows=(), nd_params=(), tt=None):
    rows, params = list(rows), list(params)
    tt = tt or _pick(rows[0].shape[0], TT_PREFS)

    @jax.custom_vjp
    def op(rows, params):
        return tuple(_rw_fwd(f, rows, params, outs, n_reduce, halo, tt, name))

    def op_fwd(rows, params):
        return op(rows, params), (rows, params)

    def op_bwd(res, cts):
        rows, params = res
        return tuple(_rw_bwd(f, rows, params, list(cts), outs, n_reduce, halo, nd_rows, nd_params, tt, name))

    op.defvjp(op_fwd, op_bwd)
    return op(rows, params)


def _scan_call(a, b, mul, *, reverse, name):
    t_len, c_len = a.shape
    tt = _pick(t_len, TT_PREFS)
    tc = _pick(c_len, (512, 256, 128))
    nt = t_len // tt
    n_in = 2 if mul is None else 3

    def body(*refs):
        a_ref, b_ref = refs[0], refs[1]
        o_ref = refs[n_in]
        carry = refs[-1]
        t = pl.program_id(1)
        av, bv = a_ref[...], b_ref[...]
        row = lax.broadcasted_iota(jnp.int32, av.shape, 0)
        s = 1
        while s < tt:
            if reverse:
                ok = row < tt - s
                a_sh = jnp.where(ok, pltpu.roll(av, tt - s, 0), 1.0)
                b_sh = jnp.where(ok, pltpu.roll(bv, tt - s, 0), 0.0)
            else:
                ok = row >= s
                a_sh = jnp.where(ok, pltpu.roll(av, s, 0), 1.0)
                b_sh = jnp.where(ok, pltpu.roll(bv, s, 0), 0.0)
            bv = av * b_sh + bv
            av = av * a_sh
            s *= 2

        @pl.when(t == 0)
        def _():
            carry[...] = jnp.zeros_like(carry)

        hs = bv + av * carry[...]
        o_ref[...] = hs
        edge = 0 if reverse else tt - 1
        carry[...] = o_ref[edge:edge + 1, :]
        if mul is not None:
            refs[n_in + 1][...] = hs * refs[2][...]

    def idx(c, t):
        return ((nt - 1 - t) if reverse else t, c)

    spec = pl.BlockSpec((tt, tc), idx)
    n_out = 1 if mul is None else 2
    res = pl.pallas_call(
        body, name=name, grid=(c_len // tc, nt),
        in_specs=[spec] * n_in, out_specs=[spec] * n_out,
        out_shape=[jax.ShapeDtypeStruct((t_len, c_len), F32)] * n_out,
        scratch_shapes=[pltpu.VMEM((1, tc), F32)],
        compiler_params=_params(("parallel", "arbitrary")),
    )(*([a, b] if mul is None else [a, b, mul]))
    return res


def linear_scan(a, b, name):
    @jax.custom_vjp
    def op(a, b):
        return _scan_call(a, b, None, reverse=False, name=name)[0]

    def op_fwd(a, b):
        hs = op(a, b)
        return hs, (a, hs)

    def op_bwd(res, dhs):
        a, hs = res
        a_next = jnp.concatenate([a[1:], jnp.ones_like(a[:1])], axis=0)
        hs_prev = jnp.concatenate([jnp.zeros_like(hs[:1]), hs[:-1]], axis=0)
        lam, da = _scan_call(a_next, dhs, hs_prev, reverse=True, name=name + '_bwd')
        return da, lam

    op.defvjp(op_fwd, op_bwd)
    return op(a, b)


def _att_scores(q_ref, k_ref, lg_ref, i, j, blk, softmax, scale):
    s = lax.dot_general(q_ref[...].astype(BF), k_ref[...].astype(BF), (((1,), (1,)), ((), ())),
                        preferred_element_type=F32)
    qpos = i * blk + lax.broadcasted_iota(jnp.int32, (blk, blk), 0)
    kpos = j * blk + lax.broadcasted_iota(jnp.int32, (blk, blk), 1)
    if softmax:
        return jnp.where(qpos >= kpos, s * scale, -1e30), None
    diff = (qpos - kpos).astype(F32)
    dec = jnp.where(diff >= 0.0, jnp.exp(jnp.maximum(diff, 0.0) * lg_ref[0:1, 0:1]), 0.0)
    return s, dec


def _att_fwd_call(q, k, v, lgt, *, heads, dqk, dv, softmax, scale, name):
    t_len = q.shape[0]
    blk = _pick(t_len, ATT_PREFS)
    nb = t_len // blk

    def body(q_ref, k_ref, v_ref, lg_ref, o_ref, lse_ref, m_sc, l_sc, acc_sc):
        i, j = pl.program_id(1), pl.program_id(2)

        @pl.when(j == 0)
        def _():
            m_sc[...] = jnp.full_like(m_sc, -1e30)
            l_sc[...] = jnp.zeros_like(l_sc)
            acc_sc[...] = jnp.zeros_like(acc_sc)

        @pl.when(j <= i)
        def _():
            s, dec = _att_scores(q_ref, k_ref, lg_ref, i, j, blk, softmax, scale)
            vb = v_ref[...].astype(BF)
            if softmax:
                m_new = jnp.maximum(m_sc[...], jnp.max(s, axis=-1, keepdims=True))
                corr = jnp.exp(m_sc[...] - m_new)
                p = jnp.exp(s - m_new)
                l_sc[...] = corr * l_sc[...] + jnp.sum(p, axis=-1, keepdims=True)
                acc_sc[...] = corr * acc_sc[...] + jnp.dot(p.astype(BF), vb, preferred_element_type=F32)
                m_sc[...] = m_new
            else:
                acc_sc[...] += jnp.dot((s * dec).astype(BF), vb, preferred_element_type=F32)

        @pl.when(j == nb - 1)
        def _():
            if softmax:
                o_ref[...] = acc_sc[...] / l_sc[...]
                lse_ref[...] = m_sc[...] + jnp.log(l_sc[...])
            else:
                o_ref[...] = acc_sc[...]
                lse_ref[...] = jnp.zeros_like(lse_ref)

    return pl.pallas_call(
        body, name=name, grid=(heads, nb, nb),
        in_specs=[
            pl.BlockSpec((blk, dqk), lambda h, i, j: (i, h)),
            pl.BlockSpec((blk, dqk), lambda h, i, j: (jnp.minimum(j, i), h)),
            pl.BlockSpec((blk, dv), lambda h, i, j: (jnp.minimum(j, i), h)),
            pl.BlockSpec((None, 1, 128), lambda h, i, j: (h, 0, 0)),
        ],
        out_specs=[
            pl.BlockSpec((blk, dv), lambda h, i, j: (i, h)),
            pl.BlockSpec((None, blk, 1), lambda h, i, j: (h, i, 0)),
        ],
        out_shape=[jax.ShapeDtypeStruct((t_len, heads * dv), F32), jax.ShapeDtypeStruct((heads, t_len, 1), F32)],
        scratch_shapes=[pltpu.VMEM((blk, 1), F32), pltpu.VMEM((blk, 1), F32), pltpu.VMEM((blk, dv), F32)],
        compiler_params=_params(("parallel", "parallel", "arbitrary")),
    )(q, k, v, lgt)


def _att_ds(q_ref, k_ref, v_ref, lg_ref, o_ref, do_ref, lse_ref, i, j, blk, softmax, scale):
    s, dec = _att_scores(q_ref, k_ref, lg_ref, i, j, blk, softmax, scale)
    do = do_ref[...]
    dp = lax.dot_general(do.astype(BF), v_ref[...].astype(BF), (((1,), (1,)), ((), ())),
                         preferred_element_type=F32)
    if softmax:
        p = jnp.exp(s - lse_ref[...])
        delta = jnp.sum(do * o_ref[...], axis=-1, keepdims=True)
        return p, p * (dp - delta) * scale
    return s * dec, dp * dec


def _att_bwd_call(q, k, v, lgt, o, lse, do, *, heads, dqk, dv, softmax, scale, name):
    t_len = q.shape[0]
    blk = _pick(t_len, ATT_PREFS)
    nb = t_len // blk

    def dq_body(q_ref, k_ref, v_ref, lg_ref, o_ref, do_ref, lse_ref, dq_ref, acc):
        i, j = pl.program_id(1), pl.program_id(2)

        @pl.when(j == 0)
        def _():
            acc[...] = jnp.zeros_like(acc)

        @pl.when(j <= i)
        def _():
            _, ds = _att_ds(q_ref, k_ref, v_ref, lg_ref, o_ref, do_ref, lse_ref, i, j, blk, softmax, scale)
            acc[...] += jnp.dot(ds.astype(BF), k_ref[...].astype(BF), preferred_element_type=F32)

        @pl.when(j == nb - 1)
        def _():
            dq_ref[...] = acc[...]

    def kv_idx(h, i, j):
        return (jnp.minimum(j, i), h)

    dq = pl.pallas_call(
        dq_body, name=name + '_dq', grid=(heads, nb, nb),
        in_specs=[
            pl.BlockSpec((blk, dqk), lambda h, i, j: (i, h)),
            pl.BlockSpec((blk, dqk), kv_idx),
            pl.BlockSpec((blk, dv), kv_idx),
            pl.BlockSpec((None, 1, 128), lambda h, i, j: (h, 0, 0)),
            pl.BlockSpec((blk, dv), lambda h, i, j: (i, h)),
            pl.BlockSpec((blk, dv), lambda h, i, j: (i, h)),
            pl.BlockSpec((None, blk, 1), lambda h, i, j: (h, i, 0)),
        ],
        out_specs=pl.BlockSpec((blk, dqk), lambda h, i, j: (i, h)),
        out_shape=jax.ShapeDtypeStruct(q.shape, F32),
        scratch_shapes=[pltpu.VMEM((blk, dqk), F32)],
        compiler_params=_params(("parallel", "parallel", "arbitrary")),
    )(q, k, v, lgt, o, do, lse)

    def dkv_body(q_ref, k_ref, v_ref, lg_ref, o_ref, do_ref, lse_ref, dk_ref, dv_ref, dk_acc, dv_acc):
        j, i = pl.program_id(1), pl.program_id(2)

        @pl.when(i == 0)
        def _():
            dk_acc[...] = jnp.zeros_like(dk_acc)
            dv_acc[...] = jnp.zeros_like(dv_acc)

        @pl.when(i >= j)
        def _():
            p, ds = _att_ds(q_ref, k_ref, v_ref, lg_ref, o_ref, do_ref, lse_ref, i, j, blk, softmax, scale)
            tn = (((0,), (0,)), ((), ()))
            dv_acc[...] += lax.dot_general(p.astype(BF), do_ref[...].astype(BF), tn, preferred_element_type=F32)
            dk_acc[...] += lax.dot_general(ds.astype(BF), q_ref[...].astype(BF), tn, preferred_element_type=F32)

        @pl.when(i == nb - 1)
        def _():
            dk_ref[...] = dk_acc[...]
            dv_ref[...] = dv_acc[...]

    def q_idx(h, j, i):
        return (jnp.maximum(i, j), h)

    dk, dvv = pl.pallas_call(
        dkv_body, name=name + '_dkv', grid=(heads, nb, nb),
        in_specs=[
            pl.BlockSpec((blk, dqk), q_idx),
            pl.BlockSpec((blk, dqk), lambda h, j, i: (j, h)),
            pl.BlockSpec((blk, dv), lambda h, j, i: (j, h)),
            pl.BlockSpec((None, 1, 128), lambda h, j, i: (h, 0, 0)),
            pl.BlockSpec((blk, dv), q_idx),
            pl.BlockSpec((blk, dv), q_idx),
            pl.BlockSpec((None, blk, 1), lambda h, j, i: (h, jnp.maximum(i, j), 0)),
        ],
        out_specs=[
            pl.BlockSpec((blk, dqk), lambda h, j, i: (j, h)),
            pl.BlockSpec((blk, dv), lambda h, j, i: (j, h)),
        ],
        out_shape=[jax.ShapeDtypeStruct(k.shape, F32), jax.ShapeDtypeStruct(v.shape, F32)],
        scratch_shapes=[pltpu.VMEM((blk, dqk), F32), pltpu.VMEM((blk, dv), F32)],
        compiler_params=_params(("parallel", "parallel", "arbitrary")),
    )(q, k, v, lgt, o, do, lse)
    return dq, dk, dvv


def attention(q, k, v, lgt, *, heads, softmax, scale, name):
    dqk, dv = q.shape[1] // heads, v.shape[1] // heads
    kw = dict(heads=heads, dqk=dqk, dv=dv, softmax=softmax, scale=scale, name=name)

    @jax.custom_vjp
    def op(q, k, v):
        return _att_fwd_call(q, k, v, lgt, **kw)[0]

    def op_fwd(q, k, v):
        o, lse = _att_fwd_call(q, k, v, lgt, **kw)
        return o, (q, k, v, o, lse)

    def op_bwd(res, do):
        q, k, v, o, lse = res
        return _att_bwd_call(q, k, v, lgt, o, lse, do, **kw)

    op.defvjp(op_fwd, op_bwd)
    return op(q, k, v)


def _exchange(arr, *, gather, name):
    blk_shape = arr.shape if gather else arr.shape[1:]

    def body(in_ref, out_ref, send_sems, recv_sems, local_sem):
        x, y, c = lax.axis_index("x"), lax.axis_index("y"), lax.axis_index("c")
        me = 4 * x + 2 * y + c
        peers = [(x, y, 1 - c), (1 - x, y, c), (x, 1 - y, c), (1 - x, 1 - y, c),
                 (1 - x, y, 1 - c), (x, 1 - y, 1 - c), (1 - x, 1 - y, 1 - c)]
        copies = []
        for n, (px, py, pc) in enumerate(peers):
            src = in_ref if gather else in_ref.at[4 * px + 2 * py + pc]
            copies.append(pltpu.make_async_remote_copy(
                src_ref=src, dst_ref=out_ref.at[me], send_sem=send_sems.at[n], recv_sem=recv_sems.at[n],
                device_id=(px, py, pc), device_id_type=pl.DeviceIdType.MESH))
        local = pltpu.make_async_copy(in_ref if gather else in_ref.at[me], out_ref.at[me], local_sem)
        for cp in copies:
            cp.start()
        local.start()
        for cp in copies:
            cp.wait()
        local.wait()

    return pl.pallas_call(
        body, name=name,
        in_specs=[pl.BlockSpec(memory_space=pl.ANY)],
        out_specs=pl.BlockSpec(memory_space=pl.ANY),
        out_shape=jax.ShapeDtypeStruct((N_DEV,) + tuple(blk_shape), arr.dtype),
        scratch_shapes=[pltpu.SemaphoreType.DMA((N_DEV - 1,)), pltpu.SemaphoreType.DMA((N_DEV - 1,)),
                        pltpu.SemaphoreType.DMA],
    )(arr)


def _adamw_call(contrib, w, m, v, name):
    r_len, c_len = w.shape
    cap = max(ADAM_BLOCK_ELEMS // c_len, 1)
    tr = r_len
    for cand in (512, 256, 128, 64, 32, 16):
        if cand <= cap and r_len % cand == 0:
            tr = cand
            break

    def body(c_ref, w_ref, m_ref, v_ref, g_ref, d_ref, mo_ref, vo_ref):
        g = c_ref[0].astype(F32)
        for n in range(1, N_DEV):
            g = g + c_ref[n].astype(F32)
        m_new = ADAM_B1 * m_ref[...] + (1.0 - ADAM_B1) * g
        v_new = ADAM_B2 * v_ref[...] + (1.0 - ADAM_B2) * jnp.square(g)
        m_hat = m_new / (1.0 - ADAM_B1 ** ADAM_STEP)
        v_hat = v_new / (1.0 - ADAM_B2 ** ADAM_STEP)
        g_ref[...] = g
        d_ref[...] = -ADAM_LR * (m_hat / (jnp.sqrt(v_hat) + ADAM_EPS) + ADAM_WD * w_ref[...])
        mo_ref[...] = m_new
        vo_ref[...] = v_new

    spec = pl.BlockSpec((tr, c_len), lambda i: (i, 0))
    return pl.pallas_call(
        body, name=name, grid=(r_len // tr,),
        in_specs=[pl.BlockSpec((N_DEV, tr, c_len), lambda i: (0, i, 0)), spec, spec, spec],
        out_specs=[spec] * 4,
        out_shape=[jax.ShapeDtypeStruct((r_len, c_len), F32)] * 4,
        compiler_params=_params(("parallel",)),
    )(contrib, w, m, v)


def _rope_lanes(x, cc, s_lo, s_hi):
    return x * cc + _roll(x, 32, 1) * s_lo + _roll(x, 96, 1) * s_hi


def _f_lru(tt, branch):
    lb = branch // LRU_HEADS

    def f(rows, params):
        (xcat,) = rows
        cw, cb, wa, ba, wx, bx, lam = params
        conv = cb
        for j in range(CONV_W):
            sh = CONV_W - 1 - j
            xs = xcat if sh == 0 else _roll(xcat, sh, 0)
            conv = conv + cw[j:j + 1, :] * xs[tt:, :]
        rs, gs = [], []
        for h in range(LRU_HEADS):
            ub = conv[:, h * lb:(h + 1) * lb]
            rs.append(_bdot(ub, wa[h]))
            gs.append(_bdot(ub, wx[h]))
        r = jax.nn.sigmoid(jnp.concatenate(rs, axis=-1) + ba)
        gate = jax.nn.sigmoid(jnp.concatenate(gs, axis=-1) + bx)
        log_a = LRU_C * r * (-_softplus(-lam))
        a = jnp.exp(log_a)
        one_minus_a2 = -jnp.tanh(log_a) * (jnp.exp(2.0 * log_a) + 1.0)
        return a, (conv * gate) * jnp.sqrt(one_minus_a2)

    return f


def _f_gate(rows, params):
    hs, g = rows
    return (hs * _silu(g),)


def _f_ln(rows, params):
    h, br = rows
    g, b = params
    pre = ALPHA * h + br
    mu = jnp.mean(pre, axis=-1, keepdims=True)
    var = jnp.mean(jnp.square(pre - mu), axis=-1, keepdims=True)
    return ((pre - mu) * lax.rsqrt(var + LN_EPS) * g + b,)


def _f_pool(tt, branch):
    grp = branch // len(POOL_WINDOWS)

    def f(rows, params):
        xcat, tidx = rows
        sums, acc, w = [], xcat, 1
        while w < POOL_WINDOWS[-1]:
            acc = acc + _roll(acc, w, 0)
            w *= 2
            sums.append(acc[tt:, :])
        u = xcat[tt:, :]
        outs = []
        for gi, w in enumerate(POOL_WINDOWS):
            sl = slice(gi * grp, (gi + 1) * grp)
            outs.append(sums[gi][:, sl] / jnp.minimum(tidx + 1.0, float(w)) - u[:, sl])
        return tuple(outs)

    return f


def _f_gate_pool(rows, params):
    m0, m1, m2, m3, g = rows
    (scale,) = params
    return (jnp.concatenate([m0, m1, m2, m3], axis=-1) * scale * _silu(g),)


def _rms(x, g):
    return x * lax.rsqrt(jnp.mean(jnp.square(x), axis=-1, keepdims=True) + RMS_EPS) * g


def _f_mla_pre(rows, params):
    c, cc, s_lo, s_hi = rows
    qn, kvn = params
    cq = c[:, :Q_LORA]
    ckv = c[:, Q_LORA:Q_LORA + KV_LORA]
    kr = c[:, Q_LORA + KV_LORA:]
    return _rms(cq, qn), _rms(ckv, kvn), _rope_lanes(kr, cc, s_lo, s_hi)


def _f_rope_q(rows, params):
    qc, cc, s_lo, s_hi = rows
    out = []
    for h in range(MLA_HEADS):
        out.append(qc[:, h * MLA_QK:h * MLA_QK + MLA_NOPE])
        out.append(_rope_lanes(qc[:, h * MLA_QK + MLA_NOPE:(h + 1) * MLA_QK], cc, s_lo, s_hi))
    return (jnp.concatenate(out, axis=-1),)


def _f_kcat(dv):
    per = MLA_NOPE + dv

    def f(rows, params):
        kv, krr = rows
        ks, vs = [], []
        for h in range(MLA_HEADS):
            ks.append(kv[:, h * per:h * per + MLA_NOPE])
            ks.append(krr)
            vs.append(kv[:, h * per + MLA_NOPE:(h + 1) * per])
        return jnp.concatenate(ks, axis=-1), jnp.concatenate(vs, axis=-1)

    return f


def _f_rope_ret(dk):
    half = dk // 2

    def f(rows, params):
        q, k, cos, sin = rows
        qs, ks = [], []
        for h in range(RET_HEADS):
            for src, dst, mult in ((q, qs, 1.0), (k, ks, dk ** -0.5)):
                x1 = src[:, h * dk:h * dk + half]
                x2 = src[:, h * dk + half:(h + 1) * dk]
                dst.append((x1 * cos - x2 * sin) * mult)
                dst.append((x2 * cos + x1 * sin) * mult)
        return jnp.concatenate(qs, axis=-1), jnp.concatenate(ks, axis=-1)

    return f


def _f_gate_gn(dv):
    def f(rows, params):
        o, g = rows
        out = []
        for h in range(RET_HEADS):
            oh = o[:, h * dv:(h + 1) * dv]
            mu = jnp.mean(oh, axis=-1, keepdims=True)
            var = jnp.mean(jnp.square(oh - mu), axis=-1, keepdims=True)
            out.append((oh - mu) * lax.rsqrt(var + LN_EPS))
        return (jnp.concatenate(out, axis=-1) * _silu(g),)

    return f


def _f_loss(rows, params):
    h, tgt, mask = rows
    per_row = jnp.mean(jnp.square(h - tgt), axis=-1, keepdims=True) * mask
    return (0.5 * jnp.sum(per_row, axis=0, keepdims=True),)


def _cols(g):
    return jnp.transpose(g, (1, 0, 2)).reshape(g.shape[1], -1)


def _rows(g):
    return g.reshape(-1, g.shape[-1])


def _rope_tables(t_pad, d):
    inv = ROPE_BASE ** (-jnp.arange(0, d, 2, dtype=F32) / d)
    ang = jnp.arange(t_pad, dtype=F32)[:, None] * inv[None, :]
    return jnp.cos(ang), jnp.sin(ang)


def _local_loss(x2d, tgt_pad, G, S, *, t_pad):
    seq, d_model = x2d.shape
    branch = d_model
    t_real = N_META + seq
    tt = _pick(t_pad, TT_PREFS)
    row2 = lambda v: v.reshape(1, -1)

    meta = _cols(G['meta_tokens'])
    h = jnp.concatenate([meta, x2d, jnp.zeros((t_pad - t_real, d_model), F32)], axis=0)
    tidx = jnp.arange(t_pad, dtype=F32)[:, None]
    rowmask = ((tidx >= N_META) & (tidx < t_real)).astype(F32)

    def post_ln(h, br, layer):
        return rowwise(_f_ln, [h, br], [row2(S[f'l{layer}_ln_g']), row2(S[f'l{layer}_ln_b'])],
                       [(d_model, F32)], name=f'l{layer}_ln')[0]

    w_in = _cols(G['l0_w_in'])
    u_pre = mm(h, w_in[:, :branch], 'l0_in_u')
    g = mm(h, w_in[:, branch:], 'l0_in_g')
    conv_w = jnp.transpose(G['l0_conv_w'], (1, 2, 0, 3)).reshape(CONV_W, branch)
    w_a = jnp.transpose(G['l0_w_a'], (1, 0, 2, 3)).reshape(LRU_HEADS, branch // LRU_HEADS, branch // LRU_HEADS)
    w_x = jnp.transpose(G['l0_w_x'], (1, 0, 2, 3)).reshape(LRU_HEADS, branch // LRU_HEADS, branch // LRU_HEADS)
    a, xin = rowwise(_f_lru(tt, branch), [u_pre],
                     [conv_w, row2(S['l0_conv_b']), w_a, row2(S['l0_b_a']), w_x, row2(S['l0_b_x']),
                      row2(S['l0_lam'])],
                     [(branch, F32), (branch, F32)], name='l0_lru', halo=(0,), tt=tt)
    hs = linear_scan(a, xin, 'l0_scan')
    z = rowwise(_f_gate, [hs, g], [], [(branch, F32)], name='l0_gate')[0]
    h = post_ln(h, mm(z, _rows(G['l0_w_out']), 'l0_out'), 0)

    w_in = _cols(G['l1_w_in'])
    u = mm(h, w_in[:, :branch], 'l1_in_u')
    g = mm(h, w_in[:, branch:], 'l1_in_g')
    grp = branch // len(POOL_WINDOWS)
    ps = rowwise(_f_pool(tt, branch), [u, tidx], [], [(grp, F32)] * len(POOL_WINDOWS), name='l1_pool',
                 halo=(0,), nd_rows=(1,), tt=tt)
    w_grp = jnp.transpose(G['l1_w_grp'], (1, 0, 2, 3)).reshape(len(POOL_WINDOWS), grp, grp)
    mixed = [mm(ps[gi], w_grp[gi], f'l1_grp{gi}') for gi in range(len(POOL_WINDOWS))]
    z = rowwise(_f_gate_pool, mixed + [g], [row2(S['l1_scale'])], [(branch, F32)], name='l1_gate')[0]
    h = post_ln(h, mm(z, _rows(G['l1_w_out']), 'l1_out'), 1)

    dv = branch // MLA_HEADS
    w_in = _cols(G['l2_w_in'])
    n_lat = Q_LORA + KV_LORA + MLA_ROPE
    w_lat = jnp.pad(w_in[:, branch:], ((0, 0), (0, 128 - MLA_ROPE)))
    g = mm(h, w_in[:, :branch], 'l2_in_g')
    c = mm(h, w_lat, 'l2_in_c')
    cos, sin = _rope_tables(t_pad, MLA_ROPE)
    zz = jnp.zeros_like(cos)
    t_cc = jnp.concatenate([cos, cos, zz, zz], axis=-1)
    t_lo = jnp.concatenate([zz, sin, zz, zz], axis=-1)
    t_hi = jnp.concatenate([-sin, zz, zz, zz], axis=-1)
    cqn, ckvn, krr = rowwise(_f_mla_pre, [c, t_cc, t_lo, t_hi], [row2(S['l2_q_norm']), row2(S['l2_kv_norm'])],
                             [(Q_LORA, F32), (KV_LORA, F32), (128, F32)], name='l2_pre', nd_rows=(1, 2, 3))
    w_uq = _cols(G['l2_w_uq']).reshape(Q_LORA, MLA_HEADS, MLA_NOPE + MLA_ROPE)
    w_uq = jnp.pad(w_uq, ((0, 0), (0, 0), (0, MLA_QK - MLA_NOPE - MLA_ROPE))).reshape(Q_LORA, MLA_HEADS * MLA_QK)
    qc = mm(cqn, w_uq, 'l2_uq')
    kv = mm(ckvn, _cols(G['l2_w_ukv']), 'l2_ukv')
    qcr = rowwise(_f_rope_q, [qc, t_cc, t_lo, t_hi], [], [(MLA_HEADS * MLA_QK, F32)], name='l2_rope_q',
                  nd_rows=(1, 2, 3))[0]
    kcat, v = rowwise(_f_kcat(dv), [kv, krr], [], [(MLA_HEADS * MLA_QK, F32), (branch, F32)], name='l2_kcat')
    no_decay = jnp.zeros((MLA_HEADS, 1, 128), F32)
    o = attention(qcr, kcat, v, no_decay, heads=MLA_HEADS, softmax=True,
                  scale=(MLA_NOPE + MLA_ROPE) ** -0.5, name='l2_att')
    z = rowwise(_f_gate, [o, g], [], [(branch, F32)], name='l2_gate')[0]
    h = post_ln(h, mm(z, _rows(G['l2_w_out']), 'l2_out'), 2)

    dk = branch // RET_HEADS
    w_in = _cols(G['l3_w_in'])
    q, k, v, g = [mm(h, w_in[:, n * branch:(n + 1) * branch], f'l3_in_{n}') for n in range(4)]
    cos, sin = _rope_tables(t_pad, dk)
    qr, kr = rowwise(_f_rope_ret(dk), [q, k, cos, sin], [], [(branch, F32), (branch, F32)], name='l3_rope',
                     nd_rows=(2, 3))
    log_g = jnp.log(1.0 - 2.0 ** (-5.0 - jnp.arange(RET_HEADS, dtype=F32)))
    lgt = jnp.broadcast_to(log_g[:, None, None], (RET_HEADS, 1, 128))
    o = attention(qr, kr, v, lgt, heads=RET_HEADS, softmax=False, scale=1.0, name='l3_ret')
    z = rowwise(_f_gate_gn(dk), [o, g], [], [(branch, F32)], name='l3_gate')[0]
    h = post_ln(h, mm(z, _rows(G['l3_w_out']), 'l3_out'), 3)

    loss = rowwise(_f_loss, [h, tgt_pad, rowmask], [], [], name='loss', n_reduce=1, nd_rows=(1, 2))[0]
    return loss[0, 0]


def _as2d(a):
    return a.reshape(-1, a.shape[-1])


def kernel(x, meta_tokens, l0_w_in, l0_conv_w, l0_conv_b, l0_w_a, l0_b_a, l0_w_x, l0_b_x, l0_lam, l0_w_out, l0_ln_g, l0_ln_b, l1_w_in, l1_w_grp, l1_scale, l1_w_out, l1_ln_g, l1_ln_b, l2_w_in, l2_q_norm, l2_w_uq, l2_kv_norm, l2_w_ukv, l2_w_out, l2_ln_g, l2_ln_b, l3_w_in, l3_w_out, l3_ln_g, l3_ln_b, loss_target, m_meta_tokens, m_l0_w_in, m_l0_conv_w, m_l0_conv_b, m_l0_w_a, m_l0_b_a, m_l0_w_x, m_l0_b_x, m_l0_lam, m_l0_w_out, m_l0_ln_g, m_l0_ln_b, m_l1_w_in, m_l1_w_grp, m_l1_scale, m_l1_w_out, m_l1_ln_g, m_l1_ln_b, m_l2_w_in, m_l2_q_norm, m_l2_w_uq, m_l2_kv_norm, m_l2_w_ukv, m_l2_w_out, m_l2_ln_g, m_l2_ln_b, m_l3_w_in, m_l3_w_out, m_l3_ln_g, m_l3_ln_b, v_meta_tokens, v_l0_w_in, v_l0_conv_w, v_l0_conv_b, v_l0_w_a, v_l0_b_a, v_l0_w_x, v_l0_b_x, v_l0_lam, v_l0_w_out, v_l0_ln_g, v_l0_ln_b, v_l1_w_in, v_l1_w_grp, v_l1_scale, v_l1_w_out, v_l1_ln_g, v_l1_ln_b, v_l2_w_in, v_l2_q_norm, v_l2_w_uq, v_l2_kv_norm, v_l2_w_ukv, v_l2_w_out, v_l2_ln_g, v_l2_ln_b, v_l3_w_in, v_l3_w_out, v_l3_ln_g, v_l3_ln_b):
    args = (meta_tokens, l0_w_in, l0_conv_w, l0_conv_b, l0_w_a, l0_b_a, l0_w_x, l0_b_x, l0_lam, l0_w_out, l0_ln_g, l0_ln_b, l1_w_in, l1_w_grp, l1_scale, l1_w_out, l1_ln_g, l1_ln_b, l2_w_in, l2_q_norm, l2_w_uq, l2_kv_norm, l2_w_ukv, l2_w_out, l2_ln_g, l2_ln_b, l3_w_in, l3_w_out, l3_ln_g, l3_ln_b)
    moms = (m_meta_tokens, m_l0_w_in, m_l0_conv_w, m_l0_conv_b, m_l0_w_a, m_l0_b_a, m_l0_w_x, m_l0_b_x, m_l0_lam, m_l0_w_out, m_l0_ln_g, m_l0_ln_b, m_l1_w_in, m_l1_w_grp, m_l1_scale, m_l1_w_out, m_l1_ln_g, m_l1_ln_b, m_l2_w_in, m_l2_q_norm, m_l2_w_uq, m_l2_kv_norm, m_l2_w_ukv, m_l2_w_out, m_l2_ln_g, m_l2_ln_b, m_l3_w_in, m_l3_w_out, m_l3_ln_g, m_l3_ln_b)
    vels = (v_meta_tokens, v_l0_w_in, v_l0_conv_w, v_l0_conv_b, v_l0_w_a, v_l0_b_a, v_l0_w_x, v_l0_b_x, v_l0_lam, v_l0_w_out, v_l0_ln_g, v_l0_ln_b, v_l1_w_in, v_l1_w_grp, v_l1_scale, v_l1_w_out, v_l1_ln_g, v_l1_ln_b, v_l2_w_in, v_l2_q_norm, v_l2_w_uq, v_l2_kv_norm, v_l2_w_ukv, v_l2_w_out, v_l2_ln_g, v_l2_ln_b, v_l3_w_in, v_l3_w_out, v_l3_ln_g, v_l3_ln_b)
    W = dict(zip(WEIGHTS, args))
    M = dict(zip(WEIGHTS, moms))
    V = dict(zip(WEIGHTS, vels))

    seq, d_model = x.shape[1], x.shape[2]
    t_real = N_META + seq
    t_pad = -(-t_real // ROW_ALIGN) * ROW_ALIGN
    x2d = x[0]
    tgt_pad = jnp.pad(loss_target[0], ((N_META, t_pad - t_real), (0, 0)))

    G = {}
    for n in BIG:
        G[n] = _exchange(W[n].astype(BF), gather=True, name='ag_' + n)
    for n in SHARDED_F32:
        G[n] = _exchange(W[n], gather=True, name='ag_' + n)
    S = {n: W[n] for n in REPLICATED}

    loss, (gx, gG, gS) = jax.value_and_grad(
        functools.partial(_local_loss, t_pad=t_pad), argnums=(0, 2, 3))(x2d, tgt_pad, G, S)

    contrib = {n: _exchange(gG[n], gather=False, name='rs_' + n) for n in BIG + SHARDED_F32}
    flat = jnp.concatenate([gS[n].reshape(-1) for n in REPLICATED]).reshape(-1, 128)
    flat_all = _exchange(flat, gather=True, name='ag_small_grads')

    out_g, out_d, out_m, out_v = {}, {}, {}, {}
    for n in BIG + SHARDED_F32:
        shp = W[n].shape
        res = _adamw_call(contrib[n].reshape((N_DEV,) + _as2d(W[n]).shape), _as2d(W[n]), _as2d(M[n]), _as2d(V[n]),
                          'adamw_' + n)
        out_g[n], out_d[n], out_m[n], out_v[n] = [r.reshape(shp) for r in res]
    cat = lambda D: jnp.concatenate([D[n].reshape(-1) for n in REPLICATED]).reshape(-1, 128)
    res = _adamw_call(flat_all, cat(W), cat(M), cat(V), 'adamw_small')
    off = 0
    for n in REPLICATED:
        size = W[n].size
        for dst, r in zip((out_g, out_d, out_m, out_v), res):
            dst[n] = r.reshape(-1)[off:off + size].reshape(W[n].shape)
        off += size

    loss = lax.psum(loss, ("x", "y", "c"))
    return (loss, gx[None], *[out_g[n] for n in WEIGHTS], *[out_d[n] for n in WEIGHTS],
            *[out_m[n] for n in WEIGHTS], *[out_v[n] for n in WEIGHTS])
```

```python
import functools

import jax
import jax.numpy as jnp
from jax import lax
from jax.experimental import pallas as pl
from jax.experimental.pallas import tpu as pltpu

F32 = jnp.float32
BF = jnp.bfloat16

N_DEV = 8
N_META = 16
ALPHA = (2.0 * 4) ** 0.25
LN_EPS = 1e-5
RMS_EPS = 1e-6
ROPE_BASE = 10000.0
LRU_HEADS = 16
CONV_W = 4
LRU_C = 8.0
POOL_WINDOWS = (2, 4, 8, 16)
MLA_HEADS = 32
MLA_NOPE = 128
MLA_ROPE = 64
MLA_QK = 256
Q_LORA = 1024
KV_LORA = 512
RET_HEADS = 16
ADAM_LR = 0.001
ADAM_B1 = 0.9
ADAM_B2 = 0.999
ADAM_EPS = 1e-08
ADAM_WD = 0.01
ADAM_STEP = 10

ROW_ALIGN = 128
VMEM_LIMIT_BYTES = 56 * 1024 * 1024
TT_PREFS = (128, 64, 32, 16, 8)
ATT_PREFS = (384, 256, 128)
MM_M_PREFS = (1408, 1024, 512, 384, 256, 128)
MM_N_PREFS = (1024, 640, 512, 384, 256, 128)
MM_K_PREFS = (512, 384, 256, 128)
ADAM_BLOCK_ELEMS = 128 * 1024
EXCH_MS_PER_MB = 0.0857
EXCH_CHUNK_MS = 0.1
MXU_FLOPS_PER_MS = 6.0e11

WEIGHTS = ['meta_tokens', 'l0_w_in', 'l0_conv_w', 'l0_conv_b', 'l0_w_a', 'l0_b_a', 'l0_w_x', 'l0_b_x', 'l0_lam',
           'l0_w_out', 'l0_ln_g', 'l0_ln_b', 'l1_w_in', 'l1_w_grp', 'l1_scale', 'l1_w_out', 'l1_ln_g', 'l1_ln_b',
           'l2_w_in', 'l2_q_norm', 'l2_w_uq', 'l2_kv_norm', 'l2_w_ukv', 'l2_w_out', 'l2_ln_g', 'l2_ln_b',
           'l3_w_in', 'l3_w_out', 'l3_ln_g', 'l3_ln_b']
BIG = ['l0_w_in', 'l0_w_out', 'l1_w_in', 'l1_w_grp', 'l1_w_out', 'l2_w_in', 'l2_w_uq', 'l2_w_ukv', 'l2_w_out',
       'l3_w_in', 'l3_w_out']
SHARDED_F32 = ['meta_tokens', 'l0_conv_w', 'l0_w_a', 'l0_w_x']
REPLICATED = [n for n in WEIGHTS if n not in BIG and n not in SHARDED_F32]
GATHER_ORDER = ['meta_tokens', 'l0_w_in', 'l0_conv_w', 'l0_w_a', 'l0_w_x', 'l0_w_out', 'l1_w_in', 'l1_w_grp',
                'l1_w_out', 'l2_w_in', 'l2_w_uq', 'l2_w_ukv', 'l2_w_out', 'l3_w_in', 'l3_w_out']


def _pick(n, prefs):
    for p in prefs:
        if n % p == 0:
            return p
    return n


def _exchange_copies(jobs, in_refs, out_refs, send_sems, recv_sems, local_sems):
    x, y, c = lax.axis_index("x"), lax.axis_index("y"), lax.axis_index("c")
    me = 4 * x + 2 * y + c
    peers = [(x, y, 1 - c), (1 - x, y, c), (x, 1 - y, c), (1 - x, 1 - y, c),
             (1 - x, y, 1 - c), (x, 1 - y, 1 - c), (1 - x, 1 - y, 1 - c)]
    copies = []
    for n, (_, gather) in enumerate(jobs):
        for p, (px, py, pc) in enumerate(peers):
            src = in_refs[n] if gather else in_refs[n].at[4 * px + 2 * py + pc]
            k = n * (N_DEV - 1) + p
            copies.append(pltpu.make_async_remote_copy(
                src_ref=src, dst_ref=out_refs[n].at[me], send_sem=send_sems.at[k], recv_sem=recv_sems.at[k],
                device_id=(px, py, pc), device_id_type=pl.DeviceIdType.MESH))
        copies.append(pltpu.make_async_copy(in_refs[n] if gather else in_refs[n].at[me], out_refs[n].at[me],
                                            local_sems.at[n]))
    return copies


def _exchange_shapes(jobs):
    shapes = []
    for arr, gather in jobs:
        blk = arr.shape if gather else arr.shape[1:]
        shapes.append(jax.ShapeDtypeStruct((N_DEV,) + tuple(blk), arr.dtype))
    return shapes


def _exchange_scratch(jobs):
    n = len(jobs)
    return [pltpu.SemaphoreType.DMA((n * (N_DEV - 1),)), pltpu.SemaphoreType.DMA((n * (N_DEV - 1),)),
            pltpu.SemaphoreType.DMA((n,))]


def _call(body, *, name, grid, in_specs, out_specs, out_shape, args, scratch=(), sem=None, jobs=()):
    jobs = list(jobs)
    n_in, n_out, n_sc, n_job = len(in_specs), len(out_specs), len(scratch), len(jobs)
    hbm = pl.BlockSpec(memory_space=pl.ANY)

    def kern(*refs):
        ins = refs[:n_in]
        job_ins = refs[n_in:n_in + n_job]
        pos = n_in + n_job
        outs = refs[pos:pos + n_out]
        job_outs = refs[pos + n_out:pos + n_out + n_job]
        pos += n_out + n_job
        scr = refs[pos:pos + n_sc]
        if n_job:
            ids = [pl.program_id(d) for d in range(len(grid))]
            first = functools.reduce(jnp.logical_and, [i == 0 for i in ids])
            last = functools.reduce(jnp.logical_and, [i == g - 1 for i, g in zip(ids, grid)])
            copies = _exchange_copies(jobs, job_ins, job_outs, *refs[pos + n_sc:])

            @pl.when(first)
            def _():
                for cp in copies:
                    cp.start()

        body(ins, outs, scr)
        if n_job:
            @pl.when(last)
            def _():
                for cp in copies:
                    cp.wait()

    kw = dict(vmem_limit_bytes=VMEM_LIMIT_BYTES)
    if sem is not None:
        kw['dimension_semantics'] = tuple("arbitrary" for _ in grid) if n_job else sem
    res = pl.pallas_call(
        kern, name=name, grid=grid,
        in_specs=list(in_specs) + [hbm] * n_job,
        out_specs=list(out_specs) + [hbm] * n_job,
        out_shape=list(out_shape) + _exchange_shapes(jobs),
        scratch_shapes=list(scratch) + (_exchange_scratch(jobs) if n_job else []),
        compiler_params=pltpu.CompilerParams(**kw),
    )(*args, *[a for a, _ in jobs])
    return list(res[:n_out]), list(res[n_out:])


def _exchange_alone(jobs, name):
    def body(*refs):
        n = len(jobs)
        copies = _exchange_copies(jobs, refs[:n], refs[n:2 * n], *refs[2 * n:])
        for cp in copies:
            cp.start()
        for cp in copies:
            cp.wait()

    hbm = pl.BlockSpec(memory_space=pl.ANY)
    return list(pl.pallas_call(
        body, name=name, in_specs=[hbm] * len(jobs), out_specs=[hbm] * len(jobs),
        out_shape=_exchange_shapes(jobs), scratch_shapes=_exchange_scratch(jobs),
    )(*[a for a, _ in jobs]))


class _Schedule:
    def __init__(self):
        self.pending = []
        self.done = {}
        self.chunks = {}
        self.count = 0

    def push(self, name, arr, gather, row_axis):
        mb = arr.size * arr.dtype.itemsize / (1 if gather else N_DEV) / 1e6
        cost = mb * EXCH_MS_PER_MB
        rows = arr.shape[row_axis]
        n = 1
        leading = row_axis == (0 if gather else 1)
        while leading and cost / n > EXCH_CHUNK_MS and rows % (2 * n) == 0 and rows // (2 * n) >= 16:
            n *= 2
        self.chunks[name] = n
        step = rows // n
        for k in range(n):
            piece = lax.slice_in_dim(arr, k * step, (k + 1) * step, axis=row_axis) if n > 1 else arr
            self.pending.append(((name, k), piece, gather, cost / n))

    def take(self, budget):
        jobs, spent = [], 0.0
        while self.pending and spent + 0.5 * self.pending[0][3] <= budget:
            job = self.pending.pop(0)
            jobs.append(job)
            spent += job[3]
        return jobs

    def deliver(self, jobs, results):
        for (key, _, _, _), r in zip(jobs, results):
            self.done[key] = r

    def get(self, name):
        mine = [j for j in self.pending if j[0][0] == name]
        if mine:
            self.pending = [j for j in self.pending if j[0][0] != name]
            self.count += 1
            self.deliver(mine, _exchange_alone([(j[1], j[2]) for j in mine], f'exchange_{self.count}_{name}'))
        parts = [self.done.pop((name, k)) for k in range(self.chunks[name])]
        return parts

    def flush(self):
        if self.pending:
            jobs, self.pending = self.pending, []
            self.count += 1
            self.deliver(jobs, _exchange_alone([(j[1], j[2]) for j in jobs], f'exchange_{self.count}_rest'))


def _ride(sch, budget, fn, **kw):
    jobs = sch.take(budget) if sch is not None else []
    outs, exch = fn(jobs=[(j[1], j[2]) for j in jobs], **kw)
    if jobs:
        sch.deliver(jobs, exch)
    return outs


@functools.partial(jax.custom_vjp, nondiff_argnums=(1, 2))
def _roll(x, shift, axis):
    return pltpu.roll(x, shift, axis)


def _roll_fwd(x, shift, axis):
    return pltpu.roll(x, shift, axis), None


def _roll_bwd(shift, axis, _, g):
    n = g.shape[axis]
    return (pltpu.roll(g, (n - shift) % n, axis),)


_roll.defvjp(_roll_fwd, _roll_bwd)


@jax.custom_vjp
def _bdot(x, w):
    return jnp.dot(x.astype(BF), w.astype(BF), preferred_element_type=F32)


def _bdot_fwd(x, w):
    return _bdot(x, w), (x, w)


def _bdot_bwd(res, g):
    x, w = res
    gb = g.astype(BF)
    dx = lax.dot_general(gb, w.astype(BF), (((1,), (1,)), ((), ())), preferred_element_type=F32)
    dw = lax.dot_general(x.astype(BF), gb, (((0,), (0,)), ((), ())), preferred_element_type=F32)
    return dx, dw


_bdot.defvjp(_bdot_fwd, _bdot_bwd)


def _silu(g):
    return g * jax.nn.sigmoid(g)


def _softplus(x):
    return jnp.maximum(x, 0.0) + jnp.log1p(jnp.exp(-jnp.abs(x)))


def _mm_call(a, b, *, mode, out_dtype, name, add=None, jobs=()):
    if mode == 'nn':
        (mo, kc), (_, no) = a.shape, b.shape
    elif mode == 'nt':
        (mo, kc), (no, _) = a.shape, b.shape
    else:
        (kc, mo), (_, no) = a.shape, b.shape
    tm = _pick(mo, MM_M_PREFS)
    tn = no if no <= 2048 and no % 512 != 0 else _pick(no, MM_N_PREFS)
    tk = _pick(kc, MM_K_PREFS)
    nk = kc // tk
    if mode == 'nn':
        a_spec = pl.BlockSpec((tm, tk), lambda i, j, k: (i, k))
        b_spec = pl.BlockSpec((tk, tn), lambda i, j, k: (k, j))
        dims = (((1,), (0,)), ((), ()))
    elif mode == 'nt':
        a_spec = pl.BlockSpec((tm, tk), lambda i, j, k: (i, k))
        b_spec = pl.BlockSpec((tn, tk), lambda i, j, k: (j, k))
        dims = (((1,), (1,)), ((), ()))
    else:
        a_spec = pl.BlockSpec((tk, tm), lambda i, j, k: (k, i))
        b_spec = pl.BlockSpec((tk, tn), lambda i, j, k: (k, j))
        dims = (((0,), (0,)), ((), ()))
    o_spec = pl.BlockSpec((tm, tn), lambda i, j, k: (i, j))

    def body(ins, outs, scr):
        a_ref, b_ref = ins[0], ins[1]
        (o_ref,), (acc_ref,) = outs, scr
        k = pl.program_id(2)

        @pl.when(k == 0)
        def _():
            acc_ref[...] = jnp.zeros_like(acc_ref) if add is None else ins[2][...]

        acc_ref[...] += lax.dot_general(a_ref[...].astype(BF), b_ref[...].astype(BF), dims,
                                        preferred_element_type=F32)

        @pl.when(k == nk - 1)
        def _():
            o_ref[...] = acc_ref[...].astype(o_ref.dtype)

    outs, exch = _call(
        body, name=name, grid=(mo // tm, no // tn, nk),
        in_specs=[a_spec, b_spec] + ([o_spec] if add is not None else []),
        out_specs=[o_spec], out_shape=[jax.ShapeDtypeStruct((mo, no), out_dtype)],
        args=[a, b] + ([add] if add is not None else []),
        scratch=[pltpu.VMEM((tm, tn), F32)], sem=("parallel", "parallel", "arbitrary"), jobs=jobs)
    return outs, exch


def mm(sch, a, b, name, mode='nn', out_dtype=F32, add=None):
    if mode == 'nn':
        flops = 2.0 * a.shape[0] * a.shape[1] * b.shape[1]
    elif mode == 'nt':
        flops = 2.0 * a.shape[0] * a.shape[1] * b.shape[0]
    else:
        flops = 2.0 * a.shape[0] * a.shape[1] * b.shape[1]
    return _ride(sch, flops / MXU_FLOPS_PER_MS, _mm_call, a=a, b=b, mode=mode, out_dtype=out_dtype, name=name,
                 add=add)[0]


def _full_spec(p):
    nd = p.ndim
    return pl.BlockSpec(p.shape, lambda i: (0,) * nd)


def _load_rows(refs, n_rows, halo, step_is_first):
    cur, prev, pos = [], [], 0
    for r in range(n_rows):
        cur.append(refs[pos][...].astype(F32))
        pos += 1
        if r in halo:
            keep = jnp.where(step_is_first, 0.0, 1.0).astype(F32)
            prev.append(refs[pos][...].astype(F32) * keep)
            pos += 1
        else:
            prev.append(None)
    return cur, prev, pos


def _join(cur, prev):
    return [c if p is None else jnp.concatenate([p, c], axis=0) for c, p in zip(cur, prev)]


def _rw_fwd(f, rows, params, outs, *, name, n_reduce=0, halo=(), tt=None, jobs=()):
    t_len = rows[0].shape[0]
    tt = tt or _pick(t_len, TT_PREFS)
    nt = t_len // tt
    n_rows, n_par, n_out = len(rows), len(params), len(outs)

    def body(ins, orefs, scr):
        i = pl.program_id(0)
        cur, prev, pos = _load_rows(ins, n_rows, halo, i == 0)
        pvals = [ins[pos + k][...] for k in range(n_par)]
        res = f(_join(cur, prev), pvals)
        n_f = len(res) - n_reduce
        for k, (_, _, src) in enumerate(outs):
            orefs[k][...] = res[src].astype(orefs[k].dtype)
        for k in range(n_reduce):
            ref, val = orefs[n_out + k], res[n_f + k]

            @pl.when(i == 0)
            def _():
                ref[...] = val

            @pl.when(i > 0)
            def _():
                ref[...] += val

    in_specs, args = [], []
    for r, x in enumerate(rows):
        c = x.shape[1]
        in_specs.append(pl.BlockSpec((tt, c), lambda i: (i, 0)))
        args.append(x)
        if r in halo:
            in_specs.append(pl.BlockSpec((tt, c), lambda i: (jnp.maximum(i - 1, 0), 0)))
            args.append(x)
    for p in params:
        in_specs.append(_full_spec(p))
        args.append(p)
    out_specs = [pl.BlockSpec((tt, c), lambda i: (i, 0)) for c, _, _ in outs]
    out_shape = [jax.ShapeDtypeStruct((t_len, c), dt) for c, dt, _ in outs]
    for _ in range(n_reduce):
        out_specs.append(pl.BlockSpec((1, 1), lambda i: (0, 0)))
        out_shape.append(jax.ShapeDtypeStruct((1, 1), F32))
    return _call(body, name=name, grid=(nt,), in_specs=in_specs, out_specs=out_specs, out_shape=out_shape,
                 args=args, sem=("arbitrary",), jobs=jobs)


def _rw_bwd(f, rows, params, cts, *, name, n_reduce=0, halo=(), nd_rows=(), nd_params=(), tt=None, jobs=()):
    t_len = rows[0].shape[0]
    tt = tt or _pick(t_len, TT_PREFS)
    nt = t_len // tt
    n_rows, n_par, n_ct = len(rows), len(params), len(cts)
    d_rows = [r for r in range(n_rows) if r not in nd_rows]
    d_pars = [k for k in range(n_par) if k not in nd_params]
    h_rows = [r for r in d_rows if r in halo]

    def blk(j):
        return nt - 1 - j

    def body(ins, orefs, carry_refs):
        j = pl.program_id(0)
        cur, prev, pos = _load_rows(ins, n_rows, halo, blk(j) == 0)
        pvals = [ins[pos + k][...] for k in range(n_par)]
        pos += n_par
        ct_vals = [ins[pos + k][...] for k in range(n_ct)]

        def g(dcur, dprev, dpar):
            c, p, q = list(cur), list(prev), list(pvals)
            for r, v in zip(d_rows, dcur):
                c[r] = v
            for r, v in zip(h_rows, dprev):
                p[r] = v
            for k, v in zip(d_pars, dpar):
                q[k] = v
            return tuple(f(_join(c, p), q))

        _, vjp = jax.vjp(g, [cur[r] for r in d_rows], [prev[r] for r in h_rows], [pvals[k] for k in d_pars])
        g_cur, g_prev, g_par = vjp(tuple(ct_vals))

        for n, r in enumerate(d_rows):
            if r in halo:
                cref = carry_refs[h_rows.index(r)]

                @pl.when(j == 0)
                def _():
                    cref[...] = jnp.zeros_like(cref)

                orefs[n][...] = g_cur[n] + cref[...]
                cref[...] = g_prev[h_rows.index(r)]
            else:
                orefs[n][...] = g_cur[n]
        for n in range(len(d_pars)):
            ref, val = orefs[len(d_rows) + n], g_par[n]

            @pl.when(j == 0)
            def _():
                ref[...] = val

            @pl.when(j > 0)
            def _():
                ref[...] += val

    in_specs, args = [], []
    for r, x in enumerate(rows):
        c = x.shape[1]
        in_specs.append(pl.BlockSpec((tt, c), lambda j: (blk(j), 0)))
        args.append(x)
        if r in halo:
            in_specs.append(pl.BlockSpec((tt, c), lambda j: (jnp.maximum(blk(j) - 1, 0), 0)))
            args.append(x)
    for p in params:
        in_specs.append(_full_spec(p))
        args.append(p)
    for ct in cts:
        if ct.shape == (1, 1):
            in_specs.append(pl.BlockSpec((1, 1), lambda j: (0, 0)))
        else:
            in_specs.append(pl.BlockSpec((tt, ct.shape[1]), lambda j: (blk(j), 0)))
        args.append(ct)
    out_specs, out_shape, scratch = [], [], []
    for r in d_rows:
        c = rows[r].shape[1]
        out_specs.append(pl.BlockSpec((tt, c), lambda j: (blk(j), 0)))
        out_shape.append(jax.ShapeDtypeStruct((t_len, c), F32))
        if r in halo:
            scratch.append(pltpu.VMEM((tt, c), F32))
    for k in d_pars:
        out_specs.append(_full_spec(params[k]))
        out_shape.append(jax.ShapeDtypeStruct(params[k].shape, F32))
    return _call(body, name=name + '_bwd', grid=(nt,), in_specs=in_specs, out_specs=out_specs,
                 out_shape=out_shape, args=args, scratch=scratch, sem=("arbitrary",), jobs=jobs)


def rw_fwd(sch, budget, f, rows, params, outs, **kw):
    return _ride(sch, budget, functools.partial(_rw_fwd, f, list(rows), list(params), outs), **kw)


def rw_bwd(sch, budget, f, rows, params, cts, **kw):
    return _ride(sch, budget, functools.partial(_rw_bwd, f, list(rows), list(params), list(cts)), **kw)


def _scan_call(a, b, mul, *, reverse, name, jobs=()):
    t_len, c_len = a.shape
    tt = _pick(t_len, TT_PREFS)
    tc = _pick(c_len, (512, 256, 128))
    nt = t_len // tt

    def body(ins, orefs, scr):
        (carry,) = scr
        t = pl.program_id(1)
        av, bv = ins[0][...], ins[1][...]
        row = lax.broadcasted_iota(jnp.int32, av.shape, 0)
        s = 1
        while s < tt:
            if reverse:
                ok = row < tt - s
                a_sh = jnp.where(ok, pltpu.roll(av, tt - s, 0), 1.0)
                b_sh = jnp.where(ok, pltpu.roll(bv, tt - s, 0), 0.0)
            else:
                ok = row >= s
                a_sh = jnp.where(ok, pltpu.roll(av, s, 0), 1.0)
                b_sh = jnp.where(ok, pltpu.roll(bv, s, 0), 0.0)
            bv = av * b_sh + bv
            av = av * a_sh
            s *= 2

        @pl.when(t == 0)
        def _():
            carry[...] = jnp.zeros_like(carry)

        hs = bv + av * carry[...]
        orefs[0][...] = hs
        edge = 0 if reverse else tt - 1
        carry[...] = orefs[0][edge:edge + 1, :]
        if mul is not None:
            orefs[1][...] = hs * ins[2][...]

    def idx(c, t):
        return ((nt - 1 - t) if reverse else t, c)

    spec = pl.BlockSpec((tt, tc), idx)
    n_in, n_out = (2, 1) if mul is None else (3, 2)
    return _call(body, name=name, grid=(c_len // tc, nt), in_specs=[spec] * n_in, out_specs=[spec] * n_out,
                 out_shape=[jax.ShapeDtypeStruct((t_len, c_len), F32)] * n_out,
                 args=[a, b] if mul is None else [a, b, mul],
                 scratch=[pltpu.VMEM((1, tc), F32)], sem=("parallel", "arbitrary"), jobs=jobs)


def _att_scores(q_ref, k_ref, lg_ref, i, j, blk, softmax, scale):
    s = lax.dot_general(q_ref[...].astype(BF), k_ref[...].astype(BF), (((1,), (1,)), ((), ())),
                        preferred_element_type=F32)
    qpos = i * blk + lax.broadcasted_iota(jnp.int32, (blk, blk), 0)
    kpos = j * blk + lax.broadcasted_iota(jnp.int32, (blk, blk), 1)
    if softmax:
        return jnp.where(qpos >= kpos, s * scale, -1e30), None
    diff = (qpos - kpos).astype(F32)
    dec = jnp.where(diff >= 0.0, jnp.exp(jnp.maximum(diff, 0.0) * lg_ref[0:1, 0:1]), 0.0)
    return s, dec


def _att_fwd_call(q, k, v, lgt, *, heads, softmax, scale, name, jobs=()):
    t_len = q.shape[0]
    dqk, dv = q.shape[1] // heads, v.shape[1] // heads
    blk = _pick(t_len, ATT_PREFS)
    nb = t_len // blk

    def body(ins, orefs, scr):
        q_ref, k_ref, v_ref, lg_ref = ins
        o_ref, lse_ref = orefs
        m_sc, l_sc, acc_sc = scr
        i, j = pl.program_id(1), pl.program_id(2)

        @pl.when(j == 0)
        def _():
            m_sc[...] = jnp.full_like(m_sc, -1e30)
            l_sc[...] = jnp.zeros_like(l_sc)
            acc_sc[...] = jnp.zeros_like(acc_sc)

        @pl.when(j <= i)
        def _():
            s, dec = _att_scores(q_ref, k_ref, lg_ref, i, j, blk, softmax, scale)
            vb = v_ref[...].astype(BF)
            if softmax:
                m_new = jnp.maximum(m_sc[...], jnp.max(s, axis=-1, keepdims=True))
                corr = jnp.exp(m_sc[...] - m_new)
                p = jnp.exp(s - m_new)
                l_sc[...] = corr * l_sc[...] + jnp.sum(p, axis=-1, keepdims=True)
                acc_sc[...] = corr * acc_sc[...] + jnp.dot(p.astype(BF), vb, preferred_element_type=F32)
                m_sc[...] = m_new
            else:
                acc_sc[...] += jnp.dot((s * dec).astype(BF), vb, preferred_element_type=F32)

        @pl.when(j == nb - 1)
        def _():
            if softmax:
                o_ref[...] = acc_sc[...] / l_sc[...]
                lse_ref[...] = m_sc[...] + jnp.log(l_sc[...])
            else:
                o_ref[...] = acc_sc[...]
                lse_ref[...] = jnp.zeros_like(lse_ref)

    return _call(
        body, name=name, grid=(heads, nb, nb),
        in_specs=[
            pl.BlockSpec((blk, dqk), lambda h, i, j: (i, h)),
            pl.BlockSpec((blk, dqk), lambda h, i, j: (jnp.minimum(j, i), h)),
            pl.BlockSpec((blk, dv), lambda h, i, j: (jnp.minimum(j, i), h)),
            pl.BlockSpec((None, 1, 128), lambda h, i, j: (h, 0, 0)),
        ],
        out_specs=[
            pl.BlockSpec((blk, dv), lambda h, i, j: (i, h)),
            pl.BlockSpec((None, blk, 1), lambda h, i, j: (h, i, 0)),
        ],
        out_shape=[jax.ShapeDtypeStruct((t_len, heads * dv), F32), jax.ShapeDtypeStruct((heads, t_len, 1), F32)],
        args=[q, k, v, lgt],
        scratch=[pltpu.VMEM((blk, 1), F32), pltpu.VMEM((blk, 1), F32), pltpu.VMEM((blk, dv), F32)],
        sem=("parallel", "parallel", "arbitrary"), jobs=jobs)


def _att_ds(ins, i, j, blk, softmax, scale):
    q_ref, k_ref, v_ref, lg_ref, o_ref, do_ref, lse_ref = ins
    s, dec = _att_scores(q_ref, k_ref, lg_ref, i, j, blk, softmax, scale)
    do = do_ref[...]
    dp = lax.dot_general(do.astype(BF), v_ref[...].astype(BF), (((1,), (1,)), ((), ())),
                         preferred_element_type=F32)
    if softmax:
        p = jnp.exp(s - lse_ref[...])
        delta = jnp.sum(do * o_ref[...], axis=-1, keepdims=True)
        return p, p * (dp - delta) * scale
    return s * dec, dp * dec


def _att_dq_call(q, k, v, lgt, o, lse, do, *, heads, softmax, scale, name, jobs=()):
    t_len = q.shape[0]
    dqk, dv = q.shape[1] // heads, v.shape[1] // heads
    blk = _pick(t_len, ATT_PREFS)
    nb = t_len // blk

    def body(ins, orefs, scr):
        (acc,) = scr
        i, j = pl.program_id(1), pl.program_id(2)

        @pl.when(j == 0)
        def _():
            acc[...] = jnp.zeros_like(acc)

        @pl.when(j <= i)
        def _():
            _, ds = _att_ds(ins, i, j, blk, softmax, scale)
            acc[...] += jnp.dot(ds.astype(BF), ins[1][...].astype(BF), preferred_element_type=F32)

        @pl.when(j == nb - 1)
        def _():
            orefs[0][...] = acc[...]

    def kv_idx(h, i, j):
        return (jnp.minimum(j, i), h)

    return _call(
        body, name=name + '_dq', grid=(heads, nb, nb),
        in_specs=[
            pl.BlockSpec((blk, dqk), lambda h, i, j: (i, h)),
            pl.BlockSpec((blk, dqk), kv_idx),
            pl.BlockSpec((blk, dv), kv_idx),
            pl.BlockSpec((None, 1, 128), lambda h, i, j: (h, 0, 0)),
            pl.BlockSpec((blk, dv), lambda h, i, j: (i, h)),
            pl.BlockSpec((blk, dv), lambda h, i, j: (i, h)),
            pl.BlockSpec((None, blk, 1), lambda h, i, j: (h, i, 0)),
        ],
        out_specs=[pl.BlockSpec((blk, dqk), lambda h, i, j: (i, h))],
        out_shape=[jax.ShapeDtypeStruct(q.shape, F32)],
        args=[q, k, v, lgt, o, do, lse],
        scratch=[pltpu.VMEM((blk, dqk), F32)], sem=("parallel", "parallel", "arbitrary"), jobs=jobs)


def _att_dkv_call(q, k, v, lgt, o, lse, do, *, heads, softmax, scale, name, jobs=()):
    t_len = q.shape[0]
    dqk, dv = q.shape[1] // heads, v.shape[1] // heads
    blk = _pick(t_len, ATT_PREFS)
    nb = t_len // blk

    def body(ins, orefs, scr):
        dk_acc, dv_acc = scr
        j, i = pl.program_id(1), pl.program_id(2)

        @pl.when(i == 0)
        def _():
            dk_acc[...] = jnp.zeros_like(dk_acc)
            dv_acc[...] = jnp.zeros_like(dv_acc)

        @pl.when(i >= j)
        def _():
            p, ds = _att_ds(ins, i, j, blk, softmax, scale)
            tn = (((0,), (0,)), ((), ()))
            dv_acc[...] += lax.dot_general(p.astype(BF), ins[5][...].astype(BF), tn, preferred_element_type=F32)
            dk_acc[...] += lax.dot_general(ds.astype(BF), ins[0][...].astype(BF), tn, preferred_element_type=F32)

        @pl.when(i == nb - 1)
        def _():
            orefs[0][...] = dk_acc[...]
            orefs[1][...] = dv_acc[...]

    def q_idx(h, j, i):
        return (jnp.maximum(i, j), h)

    return _call(
        body, name=name + '_dkv', grid=(heads, nb, nb),
        in_specs=[
            pl.BlockSpec((blk, dqk), q_idx),
            pl.BlockSpec((blk, dqk), lambda h, j, i: (j, h)),
            pl.BlockSpec((blk, dv), lambda h, j, i: (j, h)),
            pl.BlockSpec((None, 1, 128), lambda h, j, i: (h, 0, 0)),
            pl.BlockSpec((blk, dv), q_idx),
            pl.BlockSpec((blk, dv), q_idx),
            pl.BlockSpec((None, blk, 1), lambda h, j, i: (h, jnp.maximum(i, j), 0)),
        ],
        out_specs=[
            pl.BlockSpec((blk, dqk), lambda h, j, i: (j, h)),
            pl.BlockSpec((blk, dv), lambda h, j, i: (j, h)),
        ],
        out_shape=[jax.ShapeDtypeStruct(k.shape, F32), jax.ShapeDtypeStruct(v.shape, F32)],
        args=[q, k, v, lgt, o, do, lse],
        scratch=[pltpu.VMEM((blk, dqk), F32), pltpu.VMEM((blk, dv), F32)],
        sem=("parallel", "parallel", "arbitrary"), jobs=jobs)


def _adamw_call(contribs, w, m, v, *, name, jobs=()):
    r_len, c_len = w.shape
    n_chunk = len(contribs)
    r_chunk = r_len // n_chunk
    cap = max(min(ADAM_BLOCK_ELEMS, 4 * ADAM_BLOCK_ELEMS // n_chunk) // c_len, 1)
    tr = r_chunk
    for cand in (512, 256, 128, 64, 32, 16):
        if cand <= cap and r_chunk % cand == 0:
            tr = cand
            break
    per = r_chunk // tr

    def body(ins, orefs, scr):
        w_ref, m_ref, v_ref = ins[n_chunk:]
        g_ref, d_ref, mo_ref, vo_ref = orefs
        i = pl.program_id(0)

        def update(c_ref):
            g = c_ref[0].astype(F32)
            for n in range(1, N_DEV):
                g = g + c_ref[n].astype(F32)
            m_new = ADAM_B1 * m_ref[...] + (1.0 - ADAM_B1) * g
            v_new = ADAM_B2 * v_ref[...] + (1.0 - ADAM_B2) * jnp.square(g)
            m_hat = m_new / (1.0 - ADAM_B1 ** ADAM_STEP)
            v_hat = v_new / (1.0 - ADAM_B2 ** ADAM_STEP)
            g_ref[...] = g
            d_ref[...] = -ADAM_LR * (m_hat / (jnp.sqrt(v_hat) + ADAM_EPS) + ADAM_WD * w_ref[...])
            mo_ref[...] = m_new
            vo_ref[...] = v_new

        if n_chunk == 1:
            update(ins[0])
        else:
            for n in range(n_chunk):
                @pl.when(i // per == n)
                def _():
                    update(ins[n])

    spec = pl.BlockSpec((tr, c_len), lambda i: (i, 0))
    c_specs = [pl.BlockSpec((N_DEV, tr, c_len), functools.partial(
        lambda i, n: (0, jnp.clip(i - n * per, 0, per - 1), 0), n=n)) for n in range(n_chunk)]
    return _call(body, name=name, grid=(r_len // tr,), in_specs=c_specs + [spec, spec, spec],
                 out_specs=[spec] * 4, out_shape=[jax.ShapeDtypeStruct((r_len, c_len), F32)] * 4,
                 args=list(contribs) + [w, m, v], sem=("arbitrary",), jobs=jobs)


def _rope_lanes(x, cc, s_lo, s_hi):
    return x * cc + _roll(x, 32, 1) * s_lo + _roll(x, 96, 1) * s_hi


def _f_lru(tt, branch):
    lb = branch // LRU_HEADS

    def f(rows, params):
        (xcat,) = rows
        cw, cb, wa, ba, wx, bx, lam = params
        conv = cb
        for j in range(CONV_W):
            sh = CONV_W - 1 - j
            xs = xcat if sh == 0 else _roll(xcat, sh, 0)
            conv = conv + cw[j:j + 1, :] * xs[tt:, :]
        rs, gs = [], []
        for h in range(LRU_HEADS):
            ub = conv[:, h * lb:(h + 1) * lb]
            rs.append(_bdot(ub, wa[h]))
            gs.append(_bdot(ub, wx[h]))
        r = jax.nn.sigmoid(jnp.concatenate(rs, axis=-1) + ba)
        gate = jax.nn.sigmoid(jnp.concatenate(gs, axis=-1) + bx)
        log_a = LRU_C * r * (-_softplus(-lam))
        a = jnp.exp(log_a)
        one_minus_a2 = -jnp.tanh(log_a) * (jnp.exp(2.0 * log_a) + 1.0)
        return a, (conv * gate) * jnp.sqrt(one_minus_a2)

    return f


def _f_gate(rows, params):
    hs, g = rows
    return (hs * _silu(g),)


def _f_ln(rows, params):
    h, br = rows
    g, b = params
    pre = ALPHA * h + br
    mu = jnp.mean(pre, axis=-1, keepdims=True)
    var = jnp.mean(jnp.square(pre - mu), axis=-1, keepdims=True)
    return ((pre - mu) * lax.rsqrt(var + LN_EPS) * g + b,)


def _f_pool(tt, branch):
    grp = branch // len(POOL_WINDOWS)

    def f(rows, params):
        xcat, tidx = rows
        sums, acc, w = [], xcat, 1
        while w < POOL_WINDOWS[-1]:
            acc = acc + _roll(acc, w, 0)
            w *= 2
            sums.append(acc[tt:, :])
        u = xcat[tt:, :]
        outs = []
        for gi, w in enumerate(POOL_WINDOWS):
            sl = slice(gi * grp, (gi + 1) * grp)
            outs.append(sums[gi][:, sl] / jnp.minimum(tidx + 1.0, float(w)) - u[:, sl])
        return tuple(outs)

    return f


def _f_gate_pool(rows, params):
    m0, m1, m2, m3, g = rows
    (scale,) = params
    return (jnp.concatenate([m0, m1, m2, m3], axis=-1) * scale * _silu(g),)


def _rms(x, g):
    return x * lax.rsqrt(jnp.mean(jnp.square(x), axis=-1, keepdims=True) + RMS_EPS) * g


def _f_mla_pre(rows, params):
    c, cc, s_lo, s_hi = rows
    qn, kvn = params
    cq = c[:, :Q_LORA]
    ckv = c[:, Q_LORA:Q_LORA + KV_LORA]
    kr = c[:, Q_LORA + KV_LORA:]
    return _rms(cq, qn), _rms(ckv, kvn), _rope_lanes(kr, cc, s_lo, s_hi)


def _f_rope_q(rows, params):
    qc, cc, s_lo, s_hi = rows
    out = []
    for h in range(MLA_HEADS):
        out.append(qc[:, h * MLA_QK:h * MLA_QK + MLA_NOPE])
        out.append(_rope_lanes(qc[:, h * MLA_QK + MLA_NOPE:(h + 1) * MLA_QK], cc, s_lo, s_hi))
    return (jnp.concatenate(out, axis=-1),)


def _f_kcat(dv):
    per = MLA_NOPE + dv

    def f(rows, params):
        kv, krr = rows
        ks, vs = [], []
        for h in range(MLA_HEADS):
            ks.append(kv[:, h * per:h * per + MLA_NOPE])
            ks.append(krr)
            vs.append(kv[:, h * per + MLA_NOPE:(h + 1) * per])
        return jnp.concatenate(ks, axis=-1), jnp.concatenate(vs, axis=-1)

    return f


def _f_rope_ret(dk):
    half = dk // 2

    def f(rows, params):
        q, k, cos, sin = rows
        qs, ks = [], []
        for h in range(RET_HEADS):
            for src, dst, mult in ((q, qs, 1.0), (k, ks, dk ** -0.5)):
                x1 = src[:, h * dk:h * dk + half]
                x2 = src[:, h * dk + half:(h + 1) * dk]
                dst.append((x1 * cos - x2 * sin) * mult)
                dst.append((x2 * cos + x1 * sin) * mult)
        return jnp.concatenate(qs, axis=-1), jnp.concatenate(ks, axis=-1)

    return f


def _f_gate_gn(dv):
    def f(rows, params):
        o, g = rows
        out = []
        for h in range(RET_HEADS):
            oh = o[:, h * dv:(h + 1) * dv]
            mu = jnp.mean(oh, axis=-1, keepdims=True)
            var = jnp.mean(jnp.square(oh - mu), axis=-1, keepdims=True)
            out.append((oh - mu) * lax.rsqrt(var + LN_EPS))
        return (jnp.concatenate(out, axis=-1) * _silu(g),)

    return f


def _f_loss(rows, params):
    h, tgt, mask = rows
    per_row = jnp.mean(jnp.square(h - tgt), axis=-1, keepdims=True) * mask
    return (0.5 * jnp.sum(per_row, axis=0, keepdims=True),)


def _cat(parts, axis=1):
    return parts[0] if len(parts) == 1 else jnp.concatenate(parts, axis=axis)


def _cols(g):
    return jnp.transpose(g, (1, 0, 2)).reshape(g.shape[1], -1)


def _uncols(w):
    k, n = w.shape
    return jnp.transpose(w.reshape(k, N_DEV, n // N_DEV), (1, 0, 2))


def _heads(g):
    return jnp.transpose(g, (1, 0, 2, 3)).reshape(g.shape[1], -1, g.shape[3])


def _unheads(w):
    h, r, c = w.shape
    return jnp.transpose(w.reshape(h, N_DEV, r // N_DEV, c), (1, 0, 2, 3))


def _rope_tables(t_pad, d):
    inv = ROPE_BASE ** (-jnp.arange(0, d, 2, dtype=F32) / d)
    ang = jnp.arange(t_pad, dtype=F32)[:, None] * inv[None, :]
    return jnp.cos(ang), jnp.sin(ang)


def _row2(v):
    return v.reshape(1, -1)


def _train_local(sch, x2d, tgt_pad, S, *, t_pad):
    seq, d_model = x2d.shape
    branch = d_model
    t_real = N_META + seq
    tt = _pick(t_pad, TT_PREFS)
    n_win = len(POOL_WINDOWS)
    grp = branch // n_win
    lb = branch // LRU_HEADS
    dv2 = branch // MLA_HEADS
    dk3 = branch // RET_HEADS
    gS = {}
    RW, RWB = 0.06, 0.1

    def ln_fwd(h, br, layer):
        h1, hb1 = rw_fwd(sch, RW, _f_ln, [h, br], [_row2(S[f'l{layer}_ln_g']), _row2(S[f'l{layer}_ln_b'])],
                         [(d_model, F32, 0), (d_model, BF, 0)], name=f'l{layer}_ln')
        return h1, hb1

    def ln_bwd(h, br, dh1, layer):
        dh, dbr, dg, db = rw_bwd(sch, RWB, _f_ln, [h, br],
                                 [_row2(S[f'l{layer}_ln_g']), _row2(S[f'l{layer}_ln_b'])], [dh1],
                                 name=f'l{layer}_ln')
        gS[f'l{layer}_ln_g'], gS[f'l{layer}_ln_b'] = dg.reshape(-1), db.reshape(-1)
        return dh, dbr

    tidx = jnp.arange(t_pad, dtype=F32)[:, None]
    rowmask = ((tidx >= N_META) & (tidx < t_real)).astype(F32)

    meta = _cols(_cat(sch.get('meta_tokens')))
    h0 = jnp.concatenate([meta, x2d, jnp.zeros((t_pad - t_real, d_model), F32)], axis=0)
    hb0 = h0.astype(BF)

    w0_in = _cols(_cat(sch.get('l0_w_in')))
    w0_u, w0_g = w0_in[:, :branch], w0_in[:, branch:]
    u0 = mm(sch, hb0, w0_u, 'l0_in_u')
    g0 = mm(sch, hb0, w0_g, 'l0_in_g')
    conv_w = jnp.transpose(_cat(sch.get('l0_conv_w')), (1, 2, 0, 3)).reshape(CONV_W, branch)
    w_a = _heads(_cat(sch.get('l0_w_a'), axis=2))
    w_x = _heads(_cat(sch.get('l0_w_x'), axis=2))
    p0 = [conv_w, _row2(S['l0_conv_b']), w_a, _row2(S['l0_b_a']), w_x, _row2(S['l0_b_x']), _row2(S['l0_lam'])]
    f_lru = _f_lru(tt, branch)
    a0, xin0 = rw_fwd(sch, 0.13, f_lru, [u0], p0, [(branch, F32, 0), (branch, F32, 1)], name='l0_lru',
                      halo=(0,), tt=tt)
    hs0 = _ride(sch, 0.22, _scan_call, a=a0, b=xin0, mul=None, reverse=False, name='l0_scan')[0]
    (z0,) = rw_fwd(sch, RW, _f_gate, [hs0, g0], [], [(branch, BF, 0)], name='l0_gate')
    w0_out = _cat(sch.get('l0_w_out')).reshape(branch, d_model)
    br0 = mm(sch, z0, w0_out, 'l0_out')
    h1, hb1 = ln_fwd(h0, br0, 0)

    w1_in = _cols(_cat(sch.get('l1_w_in')))
    w1_u, w1_g = w1_in[:, :branch], w1_in[:, branch:]
    u1 = mm(sch, hb1, w1_u, 'l1_in_u')
    g1 = mm(sch, hb1, w1_g, 'l1_in_g')
    f_pool = _f_pool(tt, branch)
    ps1 = rw_fwd(sch, RW, f_pool, [u1, tidx], [], [(grp, BF, gi) for gi in range(n_win)], name='l1_pool',
                 halo=(0,), tt=tt)
    w1_grp = _heads(_cat(sch.get('l1_w_grp'), axis=2))
    mixed1 = [mm(sch, ps1[gi], w1_grp[gi], f'l1_grp{gi}') for gi in range(n_win)]
    p1 = [_row2(S['l1_scale'])]
    (z1,) = rw_fwd(sch, RW, _f_gate_pool, mixed1 + [g1], p1, [(branch, BF, 0)], name='l1_gate')
    w1_out = _cat(sch.get('l1_w_out')).reshape(branch, d_model)
    br1 = mm(sch, z1, w1_out, 'l1_out')
    h2, hb2 = ln_fwd(h1, br1, 1)

    w2_in = _cols(_cat(sch.get('l2_w_in')))
    w2_g = w2_in[:, :branch]
    w2_lat = jnp.pad(w2_in[:, branch:], ((0, 0), (0, 128 - MLA_ROPE)))
    g2 = mm(sch, hb2, w2_g, 'l2_in_g')
    c2 = mm(sch, hb2, w2_lat, 'l2_in_c')
    cos, sin = _rope_tables(t_pad, MLA_ROPE)
    zz = jnp.zeros_like(cos)
    tabs = [jnp.concatenate([cos, cos, zz, zz], axis=-1), jnp.concatenate([zz, sin, zz, zz], axis=-1),
            jnp.concatenate([-sin, zz, zz, zz], axis=-1)]
    p2 = [_row2(S['l2_q_norm']), _row2(S['l2_kv_norm'])]
    cqn2, ckvn2, krr2 = rw_fwd(sch, 0.04, _f_mla_pre, [c2] + tabs, p2,
                               [(Q_LORA, BF, 0), (KV_LORA, BF, 1), (128, F32, 2)], name='l2_pre')
    w2_uq = _cols(_cat(sch.get('l2_w_uq'))).reshape(Q_LORA, MLA_HEADS, MLA_NOPE + MLA_ROPE)
    w2_uq = jnp.pad(w2_uq, ((0, 0), (0, 0), (0, MLA_QK - MLA_NOPE - MLA_ROPE))).reshape(Q_LORA, MLA_HEADS * MLA_QK)
    w2_ukv = _cols(_cat(sch.get('l2_w_ukv')))
    qc2 = mm(sch, cqn2, w2_uq, 'l2_uq')
    kv2 = mm(sch, ckvn2, w2_ukv, 'l2_ukv')
    (qcr2,) = rw_fwd(sch, 0.09, _f_rope_q, [qc2] + tabs, [], [(MLA_HEADS * MLA_QK, BF, 0)], name='l2_rope_q')
    f_kcat = _f_kcat(dv2)
    kcat2, v2 = rw_fwd(sch, 0.1, f_kcat, [kv2, krr2], [], [(MLA_HEADS * MLA_QK, BF, 0), (branch, BF, 1)],
                       name='l2_kcat')
    no_decay = jnp.zeros((MLA_HEADS, 1, 128), F32)
    att2 = dict(heads=MLA_HEADS, softmax=True, scale=(MLA_NOPE + MLA_ROPE) ** -0.5, name='l2_att')
    o2, lse2 = _ride(sch, 2.5, _att_fwd_call, q=qcr2, k=kcat2, v=v2, lgt=no_decay, **att2)
    (z2,) = rw_fwd(sch, RW, _f_gate, [o2, g2], [], [(branch, BF, 0)], name='l2_gate')
    w2_out = _cat(sch.get('l2_w_out')).reshape(branch, d_model)
    br2 = mm(sch, z2, w2_out, 'l2_out')
    h3, hb3 = ln_fwd(h2, br2, 2)

    w3_in = _cols(_cat(sch.get('l3_w_in')))
    w3 = [w3_in[:, n * branch:(n + 1) * branch] for n in range(4)]
    q3 = mm(sch, hb3, w3[0], 'l3_in_q')
    k3 = mm(sch, hb3, w3[1], 'l3_in_k')
    v3 = mm(sch, hb3, w3[2], 'l3_in_v', out_dtype=BF)
    g3 = mm(sch, hb3, w3[3], 'l3_in_g')
    cs3 = list(_rope_tables(t_pad, dk3))
    f_rope3 = _f_rope_ret(dk3)
    qr3, kr3 = rw_fwd(sch, 0.09, f_rope3, [q3, k3] + cs3, [], [(branch, BF, 0), (branch, BF, 1)], name='l3_rope')
    log_g = jnp.log(1.0 - 2.0 ** (-5.0 - jnp.arange(RET_HEADS, dtype=F32)))
    lgt = jnp.broadcast_to(log_g[:, None, None], (RET_HEADS, 1, 128))
    att3 = dict(heads=RET_HEADS, softmax=False, scale=1.0, name='l3_ret')
    o3, lse3 = _ride(sch, 1.1, _att_fwd_call, q=qr3, k=kr3, v=v3, lgt=lgt, **att3)
    f_gn = _f_gate_gn(dk3)
    (z3,) = rw_fwd(sch, 0.08, f_gn, [o3, g3], [], [(branch, BF, 0)], name='l3_gate')
    w3_out = _cat(sch.get('l3_w_out')).reshape(branch, d_model)
    br3 = mm(sch, z3, w3_out, 'l3_out')
    (h4,) = rw_fwd(sch, RW, _f_ln, [h3, br3], [_row2(S['l3_ln_g']), _row2(S['l3_ln_b'])], [(d_model, F32, 0)],
                   name='l3_ln')

    (loss,) = rw_fwd(sch, 0.05, _f_loss, [h4, tgt_pad, rowmask], [], [], name='loss', n_reduce=1)

    (dh4,) = rw_bwd(sch, 0.07, _f_loss, [h4, tgt_pad, rowmask], [], [jnp.ones((1, 1), F32)], name='loss',
                    n_reduce=1, nd_rows=(1, 2))

    dh3, dbr3 = ln_bwd(h3, br3, dh4, 3)
    dz3 = mm(sch, dbr3, w3_out, 'l3_out_dx', mode='nt')
    sch.push('l3_w_out', mm(sch, z3, dbr3, 'l3_out_dw', mode='tn', out_dtype=BF).reshape(N_DEV, -1, d_model),
             False, 1)
    do3, dg3 = rw_bwd(sch, 0.13, f_gn, [o3, g3], [], [dz3], name='l3_gate')
    (dqr3,) = _ride(sch, 1.2, _att_dq_call, q=qr3, k=kr3, v=v3, lgt=lgt, o=o3, lse=lse3, do=do3, **att3)
    dkr3, dv3 = _ride(sch, 1.3, _att_dkv_call, q=qr3, k=kr3, v=v3, lgt=lgt, o=o3, lse=lse3, do=do3, **att3)
    dq3, dk3_ = rw_bwd(sch, 0.13, f_rope3, [q3, k3] + cs3, [], [dqr3, dkr3], name='l3_rope', nd_rows=(2, 3))
    d3 = [dq3, dk3_, dv3, dg3]
    for n in range(4):
        dh3 = mm(sch, d3[n], w3[n], f'l3_in_dx{n}', mode='nt', add=dh3)
    dw3 = [mm(sch, hb3, d3[n], f'l3_in_dw{n}', mode='tn', out_dtype=BF) for n in range(4)]
    sch.push('l3_w_in', _uncols(jnp.concatenate(dw3, axis=1)), False, 1)

    dh2, dbr2 = ln_bwd(h2, br2, dh3, 2)
    dz2 = mm(sch, dbr2, w2_out, 'l2_out_dx', mode='nt')
    sch.push('l2_w_out', mm(sch, z2, dbr2, 'l2_out_dw', mode='tn', out_dtype=BF).reshape(N_DEV, -1, d_model),
             False, 1)
    do2, dg2 = rw_bwd(sch, 0.1, _f_gate, [o2, g2], [], [dz2], name='l2_gate')
    (dqcr2,) = _ride(sch, 2.3, _att_dq_call, q=qcr2, k=kcat2, v=v2, lgt=no_decay, o=o2, lse=lse2, do=do2, **att2)
    dkcat2, dv2_ = _ride(sch, 2.5, _att_dkv_call, q=qcr2, k=kcat2, v=v2, lgt=no_decay, o=o2, lse=lse2, do=do2,
                         **att2)
    dkv2, dkrr2 = rw_bwd(sch, 0.14, f_kcat, [kv2, krr2], [], [dkcat2, dv2_], name='l2_kcat')
    (dqc2,) = rw_bwd(sch, 0.13, _f_rope_q, [qc2] + tabs, [], [dqcr2], name='l2_rope_q', nd_rows=(1, 2, 3))
    dckvn2 = mm(sch, dkv2, w2_ukv, 'l2_ukv_dx', mode='nt')
    dcqn2 = mm(sch, dqc2, w2_uq, 'l2_uq_dx', mode='nt')
    sch.push('l2_w_ukv', _uncols(mm(sch, ckvn2, dkv2, 'l2_ukv_dw', mode='tn', out_dtype=BF)), False, 1)
    dw_uq = mm(sch, cqn2, dqc2, 'l2_uq_dw', mode='tn', out_dtype=BF)
    dw_uq = dw_uq.reshape(Q_LORA, MLA_HEADS, MLA_QK)[:, :, :MLA_NOPE + MLA_ROPE].reshape(Q_LORA, -1)
    sch.push('l2_w_uq', _uncols(dw_uq), False, 1)
    dc2, dqn, dkvn = rw_bwd(sch, 0.05, _f_mla_pre, [c2] + tabs, p2, [dcqn2, dckvn2, dkrr2], name='l2_pre',
                            nd_rows=(1, 2, 3))
    gS['l2_q_norm'], gS['l2_kv_norm'] = dqn.reshape(-1), dkvn.reshape(-1)
    dh2 = mm(sch, dg2, w2_g, 'l2_in_g_dx', mode='nt', add=dh2)
    dh2 = mm(sch, dc2, w2_lat, 'l2_in_c_dx', mode='nt', add=dh2)
    dw2_g = mm(sch, hb2, dg2, 'l2_in_g_dw', mode='tn', out_dtype=BF)
    dw2_lat = mm(sch, hb2, dc2, 'l2_in_c_dw', mode='tn', out_dtype=BF)
    n_lat = Q_LORA + KV_LORA + MLA_ROPE
    sch.push('l2_w_in', _uncols(jnp.concatenate([dw2_g, dw2_lat[:, :n_lat]], axis=1)), False, 1)

    dh1, dbr1 = ln_bwd(h1, br1, dh2, 1)
    dz1 = mm(sch, dbr1, w1_out, 'l1_out_dx', mode='nt')
    sch.push('l1_w_out', mm(sch, z1, dbr1, 'l1_out_dw', mode='tn', out_dtype=BF).reshape(N_DEV, -1, d_model),
             False, 1)
    res = rw_bwd(sch, 0.11, _f_gate_pool, mixed1 + [g1], p1, [dz1], name='l1_gate')
    dmixed1, dg1, dscale = res[:n_win], res[n_win], res[n_win + 1]
    gS['l1_scale'] = dscale.reshape(-1)
    dps1 = [mm(sch, dmixed1[gi], w1_grp[gi], f'l1_grp{gi}_dx', mode='nt') for gi in range(n_win)]
    dw_grp = jnp.stack([mm(sch, ps1[gi], dmixed1[gi], f'l1_grp{gi}_dw', mode='tn', out_dtype=BF)
                        for gi in range(n_win)])
    sch.push('l1_w_grp', _unheads(dw_grp), False, 2)
    (du1,) = rw_bwd(sch, 0.09, f_pool, [u1, tidx], [], dps1, name='l1_pool', halo=(0,), nd_rows=(1,), tt=tt)
    dh1 = mm(sch, du1, w1_u, 'l1_in_u_dx', mode='nt', add=dh1)
    dh1 = mm(sch, dg1, w1_g, 'l1_in_g_dx', mode='nt', add=dh1)
    dw1 = [mm(sch, hb1, du1, 'l1_in_u_dw', mode='tn', out_dtype=BF),
           mm(sch, hb1, dg1, 'l1_in_g_dw', mode='tn', out_dtype=BF)]
    sch.push('l1_w_in', _uncols(jnp.concatenate(dw1, axis=1)), False, 1)

    dh0, dbr0 = ln_bwd(h0, br0, dh1, 0)
    dz0 = mm(sch, dbr0, w0_out, 'l0_out_dx', mode='nt')
    sch.push('l0_w_out', mm(sch, z0, dbr0, 'l0_out_dw', mode='tn', out_dtype=BF).reshape(N_DEV, -1, d_model),
             False, 1)
    dhs0, dg0 = rw_bwd(sch, 0.1, _f_gate, [hs0, g0], [], [dz0], name='l0_gate')
    a_next = jnp.concatenate([a0[1:], jnp.ones_like(a0[:1])], axis=0)
    hs_prev = jnp.concatenate([jnp.zeros_like(hs0[:1]), hs0[:-1]], axis=0)
    dxin0, da0 = _ride(sch, 0.25, _scan_call, a=a_next, b=dhs0, mul=hs_prev, reverse=True, name='l0_scan_bwd')
    res = rw_bwd(sch, 0.3, f_lru, [u0], p0, [da0, dxin0], name='l0_lru', halo=(0,), tt=tt)
    du0 = res[0]
    gS['l0_conv_b'], gS['l0_b_a'], gS['l0_b_x'], gS['l0_lam'] = [res[k].reshape(-1) for k in (2, 4, 6, 7)]
    sch.push('l0_w_a', _unheads(res[3]), False, 2)
    sch.push('l0_w_x', _unheads(res[5]), False, 2)
    sch.push('l0_conv_w', jnp.transpose(res[1].reshape(CONV_W, 1, N_DEV, -1), (2, 0, 1, 3)), False, 3)
    dh0 = mm(sch, du0, w0_u, 'l0_in_u_dx', mode='nt', add=dh0)
    dh0 = mm(sch, dg0, w0_g, 'l0_in_g_dx', mode='nt', add=dh0)
    dw0 = [mm(sch, hb0, du0, 'l0_in_u_dw', mode='tn', out_dtype=BF),
           mm(sch, hb0, dg0, 'l0_in_g_dw', mode='tn', out_dtype=BF)]
    sch.push('l0_w_in', _uncols(jnp.concatenate(dw0, axis=1)), False, 1)
    sch.push('meta_tokens', _uncols(dh0[:N_META]), False, 1)

    return loss[0, 0], dh0[N_META:t_real], gS


def _as2d(a):
    return a.reshape(-1, a.shape[-1])


def kernel(x, meta_tokens, l0_w_in, l0_conv_w, l0_conv_b, l0_w_a, l0_b_a, l0_w_x, l0_b_x, l0_lam, l0_w_out, l0_ln_g, l0_ln_b, l1_w_in, l1_w_grp, l1_scale, l1_w_out, l1_ln_g, l1_ln_b, l2_w_in, l2_q_norm, l2_w_uq, l2_kv_norm, l2_w_ukv, l2_w_out, l2_ln_g, l2_ln_b, l3_w_in, l3_w_out, l3_ln_g, l3_ln_b, loss_target, m_meta_tokens, m_l0_w_in, m_l0_conv_w, m_l0_conv_b, m_l0_w_a, m_l0_b_a, m_l0_w_x, m_l0_b_x, m_l0_lam, m_l0_w_out, m_l0_ln_g, m_l0_ln_b, m_l1_w_in, m_l1_w_grp, m_l1_scale, m_l1_w_out, m_l1_ln_g, m_l1_ln_b, m_l2_w_in, m_l2_q_norm, m_l2_w_uq, m_l2_kv_norm, m_l2_w_ukv, m_l2_w_out, m_l2_ln_g, m_l2_ln_b, m_l3_w_in, m_l3_w_out, m_l3_ln_g, m_l3_ln_b, v_meta_tokens, v_l0_w_in, v_l0_conv_w, v_l0_conv_b, v_l0_w_a, v_l0_b_a, v_l0_w_x, v_l0_b_x, v_l0_lam, v_l0_w_out, v_l0_ln_g, v_l0_ln_b, v_l1_w_in, v_l1_w_grp, v_l1_scale, v_l1_w_out, v_l1_ln_g, v_l1_ln_b, v_l2_w_in, v_l2_q_norm, v_l2_w_uq, v_l2_kv_norm, v_l2_w_ukv, v_l2_w_out, v_l2_ln_g, v_l2_ln_b, v_l3_w_in, v_l3_w_out, v_l3_ln_g, v_l3_ln_b):
    args = (meta_tokens, l0_w_in, l0_conv_w, l0_conv_b, l0_w_a, l0_b_a, l0_w_x, l0_b_x, l0_lam, l0_w_out, l0_ln_g, l0_ln_b, l1_w_in, l1_w_grp, l1_scale, l1_w_out, l1_ln_g, l1_ln_b, l2_w_in, l2_q_norm, l2_w_uq, l2_kv_norm, l2_w_ukv, l2_w_out, l2_ln_g, l2_ln_b, l3_w_in, l3_w_out, l3_ln_g, l3_ln_b)
    moms = (m_meta_tokens, m_l0_w_in, m_l0_conv_w, m_l0_conv_b, m_l0_w_a, m_l0_b_a, m_l0_w_x, m_l0_b_x, m_l0_lam, m_l0_w_out, m_l0_ln_g, m_l0_ln_b, m_l1_w_in, m_l1_w_grp, m_l1_scale, m_l1_w_out, m_l1_ln_g, m_l1_ln_b, m_l2_w_in, m_l2_q_norm, m_l2_w_uq, m_l2_kv_norm, m_l2_w_ukv, m_l2_w_out, m_l2_ln_g, m_l2_ln_b, m_l3_w_in, m_l3_w_out, m_l3_ln_g, m_l3_ln_b)
    vels = (v_meta_tokens, v_l0_w_in, v_l0_conv_w, v_l0_conv_b, v_l0_w_a, v_l0_b_a, v_l0_w_x, v_l0_b_x, v_l0_lam, v_l0_w_out, v_l0_ln_g, v_l0_ln_b, v_l1_w_in, v_l1_w_grp, v_l1_scale, v_l1_w_out, v_l1_ln_g, v_l1_ln_b, v_l2_w_in, v_l2_q_norm, v_l2_w_uq, v_l2_kv_norm, v_l2_w_ukv, v_l2_w_out, v_l2_ln_g, v_l2_ln_b, v_l3_w_in, v_l3_w_out, v_l3_ln_g, v_l3_ln_b)
    W = dict(zip(WEIGHTS, args))
    M = dict(zip(WEIGHTS, moms))
    V = dict(zip(WEIGHTS, vels))

    seq = x.shape[1]
    t_real = N_META + seq
    t_pad = -(-t_real // ROW_ALIGN) * ROW_ALIGN
    tgt_pad = jnp.pad(loss_target[0], ((N_META, t_pad - t_real), (0, 0)))

    sch = _Schedule()
    for n in GATHER_ORDER:
        shard = W[n].astype(BF) if n in BIG else W[n]
        sch.push(n, shard, True, shard.ndim - 2)
    S = {n: W[n] for n in REPLICATED}

    loss, gx, gS = _train_local(sch, x[0], tgt_pad, S, t_pad=t_pad)

    flat = jnp.concatenate([gS[n].reshape(-1) for n in REPLICATED]).reshape(-1, 128)
    sch.push('small_grads', flat, True, 0)

    out_g, out_d, out_m, out_v = {}, {}, {}, {}
    order = ['l3_w_out', 'l3_w_in', 'l2_w_out', 'l2_w_ukv', 'l2_w_uq', 'l2_w_in', 'l1_w_out', 'l1_w_grp',
             'l1_w_in', 'l0_w_out', 'l0_w_a', 'l0_w_x', 'l0_conv_w', 'l0_w_in', 'meta_tokens']
    for n in order:
        shp = W[n].shape
        w2, m2, v2 = _as2d(W[n]), _as2d(M[n]), _as2d(V[n])
        parts = [p.reshape((N_DEV, -1, w2.shape[1])) for p in sch.get(n)]
        res = _ride(sch, w2.size * 1.5e-8, _adamw_call, contribs=parts, w=w2, m=m2, v=v2, name='adamw_' + n)
        out_g[n], out_d[n], out_m[n], out_v[n] = [r.reshape(shp) for r in res]
    cat = lambda D: jnp.concatenate([D[n].reshape(-1) for n in REPLICATED]).reshape(-1, 128)
    res = _adamw_call(sch.get('small_grads'), cat(W), cat(M), cat(V), name='adamw_small')[0]
    off = 0
    for n in REPLICATED:
        size = W[n].size
        for dst, r in zip((out_g, out_d, out_m, out_v), res):
            dst[n] = r.reshape(-1)[off:off + size].reshape(W[n].shape)
        off += size
    sch.flush()

    loss = lax.psum(loss, ("x", "y", "c"))
    return (loss, gx[None], *[out_g[n] for n in WEIGHTS], *[out_d[n] for n in WEIGHTS],
            *[out_m[n] for n in WEIGHTS], *[out_v[n] for n in WEIGHTS])
```

```python
import functools

import jax
import jax.numpy as jnp
from jax import lax
from jax.experimental import pallas as pl
from jax.experimental.pallas import tpu as pltpu

F32 = jnp.float32
BF = jnp.bfloat16

N_DEV = 8
N_META = 16
ALPHA = (2.0 * 4) ** 0.25
LN_EPS = 1e-5
RMS_EPS = 1e-6
ROPE_BASE = 10000.0
LRU_HEADS = 16
CONV_W = 4
LRU_C = 8.0
POOL_WINDOWS = (2, 4, 8, 16)
MLA_HEADS = 32
MLA_NOPE = 128
MLA_ROPE = 64
MLA_QK = 256
Q_LORA = 1024
KV_LORA = 512
RET_HEADS = 16
ADAM_LR = 0.001
ADAM_B1 = 0.9
ADAM_B2 = 0.999
ADAM_EPS = 1e-08
ADAM_WD = 0.01
ADAM_STEP = 10

ROW_ALIGN = 128
VMEM_LIMIT_BYTES = 56 * 1024 * 1024
TT_PREFS = (128, 64, 32, 16, 8)
ATT_PREFS = (384, 256, 128)
MM_M_PREFS = (1408, 1024, 512, 384, 256, 128)
MM_N_PREFS = (1024, 640, 512, 384, 256, 128)
MM_K_PREFS = (512, 384, 256, 128)
ADAM_BLOCK_ELEMS = 128 * 1024
EXCH_MS_PER_MB = 0.0857
EXCH_CHUNK_MS = 0.1
MXU_FLOPS_PER_MS = 6.0e11

WEIGHTS = ['meta_tokens', 'l0_w_in', 'l0_conv_w', 'l0_conv_b', 'l0_w_a', 'l0_b_a', 'l0_w_x', 'l0_b_x', 'l0_lam',
           'l0_w_out', 'l0_ln_g', 'l0_ln_b', 'l1_w_in', 'l1_w_grp', 'l1_scale', 'l1_w_out', 'l1_ln_g', 'l1_ln_b',
           'l2_w_in', 'l2_q_norm', 'l2_w_uq', 'l2_kv_norm', 'l2_w_ukv', 'l2_w_out', 'l2_ln_g', 'l2_ln_b',
           'l3_w_in', 'l3_w_out', 'l3_ln_g', 'l3_ln_b']
BIG = ['l0_w_in', 'l0_w_out', 'l1_w_in', 'l1_w_grp', 'l1_w_out', 'l2_w_in', 'l2_w_uq', 'l2_w_ukv', 'l2_w_out',
       'l3_w_in', 'l3_w_out']
SHARDED_F32 = ['meta_tokens', 'l0_conv_w', 'l0_w_a', 'l0_w_x']
REPLICATED = [n for n in WEIGHTS if n not in BIG and n not in SHARDED_F32]
GATHER_ORDER = ['meta_tokens', 'l0_w_in', 'l0_conv_w', 'l0_w_a', 'l0_w_x', 'l0_w_out', 'l1_w_in', 'l1_w_grp',
                'l1_w_out', 'l2_w_in', 'l2_w_uq', 'l2_w_ukv', 'l2_w_out', 'l3_w_in', 'l3_w_out']


def _pick(n, prefs):
    for p in prefs:
        if n % p == 0:
            return p
    return n


def _exchange_copies(jobs, in_refs, out_refs, send_sems, recv_sems, local_sems):
    x, y, c = lax.axis_index("x"), lax.axis_index("y"), lax.axis_index("c")
    me = 4 * x + 2 * y + c
    peers = [(x, y, 1 - c), (1 - x, y, c), (x, 1 - y, c), (1 - x, 1 - y, c),
             (1 - x, y, 1 - c), (x, 1 - y, 1 - c), (1 - x, 1 - y, 1 - c)]
    copies = []
    for n, (_, gather) in enumerate(jobs):
        for p, (px, py, pc) in enumerate(peers):
            src = in_refs[n] if gather else in_refs[n].at[4 * px + 2 * py + pc]
            k = n * (N_DEV - 1) + p
            copies.append(pltpu.make_async_remote_copy(
                src_ref=src, dst_ref=out_refs[n].at[me], send_sem=send_sems.at[k], recv_sem=recv_sems.at[k],
                device_id=(px, py, pc), device_id_type=pl.DeviceIdType.MESH))
        copies.append(pltpu.make_async_copy(in_refs[n] if gather else in_refs[n].at[me], out_refs[n].at[me],
                                            local_sems.at[n]))
    return copies


def _exchange_shapes(jobs):
    shapes = []
    for arr, gather in jobs:
        blk = arr.shape if gather else arr.shape[1:]
        shapes.append(jax.ShapeDtypeStruct((N_DEV,) + tuple(blk), arr.dtype))
    return shapes


def _exchange_scratch(jobs):
    n = len(jobs)
    return [pltpu.SemaphoreType.DMA((n * (N_DEV - 1),)), pltpu.SemaphoreType.DMA((n * (N_DEV - 1),)),
            pltpu.SemaphoreType.DMA((n,))]


def _call(body, *, name, grid, in_specs, out_specs, out_shape, args, scratch=(), sem=None, jobs=()):
    jobs = list(jobs)
    n_in, n_out, n_sc, n_job = len(in_specs), len(out_specs), len(scratch), len(jobs)
    hbm = pl.BlockSpec(memory_space=pl.ANY)

    def kern(*refs):
        ins = refs[:n_in]
        job_ins = refs[n_in:n_in + n_job]
        pos = n_in + n_job
        outs = refs[pos:pos + n_out]
        job_outs = refs[pos + n_out:pos + n_out + n_job]
        pos += n_out + n_job
        scr = refs[pos:pos + n_sc]
        if n_job:
            ids = [pl.program_id(d) for d in range(len(grid))]
            first = functools.reduce(jnp.logical_and, [i == 0 for i in ids])
            last = functools.reduce(jnp.logical_and, [i == g - 1 for i, g in zip(ids, grid)])
            copies = _exchange_copies(jobs, job_ins, job_outs, *refs[pos + n_sc:])

            @pl.when(first)
            def _():
                for cp in copies:
                    cp.start()

        body(ins, outs, scr)
        if n_job:
            @pl.when(last)
            def _():
                for cp in copies:
                    cp.wait()

    kw = dict(vmem_limit_bytes=VMEM_LIMIT_BYTES)
    if sem is not None:
        kw['dimension_semantics'] = tuple("arbitrary" for _ in grid) if n_job else sem
    res = pl.pallas_call(
        kern, name=name, grid=grid,
        in_specs=list(in_specs) + [hbm] * n_job,
        out_specs=list(out_specs) + [hbm] * n_job,
        out_shape=list(out_shape) + _exchange_shapes(jobs),
        scratch_shapes=list(scratch) + (_exchange_scratch(jobs) if n_job else []),
        compiler_params=pltpu.CompilerParams(**kw),
    )(*args, *[a for a, _ in jobs])
    return list(res[:n_out]), list(res[n_out:])


def _exchange_alone(jobs, name):
    def body(*refs):
        n = len(jobs)
        copies = _exchange_copies(jobs, refs[:n], refs[n:2 * n], *refs[2 * n:])
        for cp in copies:
            cp.start()
        for cp in copies:
            cp.wait()

    hbm = pl.BlockSpec(memory_space=pl.ANY)
    return list(pl.pallas_call(
        body, name=name, in_specs=[hbm] * len(jobs), out_specs=[hbm] * len(jobs),
        out_shape=_exchange_shapes(jobs), scratch_shapes=_exchange_scratch(jobs),
    )(*[a for a, _ in jobs]))


class _Schedule:
    def __init__(self):
        self.pending = []
        self.done = {}
        self.chunks = {}
        self.count = 0

    def push(self, name, arr, gather, row_axis):
        mb = arr.size * arr.dtype.itemsize / (1 if gather else N_DEV) / 1e6
        cost = mb * EXCH_MS_PER_MB
        rows = arr.shape[row_axis]
        n = 1
        leading = row_axis == (0 if gather else 1)
        while leading and cost / n > EXCH_CHUNK_MS and rows % (2 * n) == 0 and rows // (2 * n) >= 16:
            n *= 2
        self.chunks[name] = n
        step = rows // n
        for k in range(n):
            piece = lax.slice_in_dim(arr, k * step, (k + 1) * step, axis=row_axis) if n > 1 else arr
            self.pending.append(((name, k), piece, gather, cost / n))

    def take(self, budget):
        jobs, spent = [], 0.0
        while self.pending and spent + 0.5 * self.pending[0][3] <= budget:
            job = self.pending.pop(0)
            jobs.append(job)
            spent += job[3]
        return jobs

    def deliver(self, jobs, results):
        for (key, _, _, _), r in zip(jobs, results):
            self.done[key] = r

    def get(self, name):
        mine = [j for j in self.pending if j[0][0] == name]
        if mine:
            self.pending = [j for j in self.pending if j[0][0] != name]
            self.count += 1
            self.deliver(mine, _exchange_alone([(j[1], j[2]) for j in mine], f'exchange_{self.count}_{name}'))
        parts = [self.done.pop((name, k)) for k in range(self.chunks[name])]
        return parts

    def flush(self):
        if self.pending:
            jobs, self.pending = self.pending, []
            self.count += 1
            self.deliver(jobs, _exchange_alone([(j[1], j[2]) for j in jobs], f'exchange_{self.count}_rest'))


def _ride(sch, budget, fn, **kw):
    jobs = sch.take(budget) if sch is not None else []
    outs, exch = fn(jobs=[(j[1], j[2]) for j in jobs], **kw)
    if jobs:
        sch.deliver(jobs, exch)
    return outs


@functools.partial(jax.custom_vjp, nondiff_argnums=(1, 2))
def _roll(x, shift, axis):
    return pltpu.roll(x, shift, axis)


def _roll_fwd(x, shift, axis):
    return pltpu.roll(x, shift, axis), None


def _roll_bwd(shift, axis, _, g):
    n = g.shape[axis]
    return (pltpu.roll(g, (n - shift) % n, axis),)


_roll.defvjp(_roll_fwd, _roll_bwd)


@jax.custom_vjp
def _bdot(x, w):
    return jnp.dot(x.astype(BF), w.astype(BF), preferred_element_type=F32)


def _bdot_fwd(x, w):
    return _bdot(x, w), (x, w)


def _bdot_bwd(res, g):
    x, w = res
    gb = g.astype(BF)
    dx = lax.dot_general(gb, w.astype(BF), (((1,), (1,)), ((), ())), preferred_element_type=F32)
    dw = lax.dot_general(x.astype(BF), gb, (((0,), (0,)), ((), ())), preferred_element_type=F32)
    return dx, dw


_bdot.defvjp(_bdot_fwd, _bdot_bwd)


def _silu(g):
    return g * jax.nn.sigmoid(g)


def _softplus(x):
    return jnp.maximum(x, 0.0) + jnp.log1p(jnp.exp(-jnp.abs(x)))


def _mm_call(a, b, *, mode, out_dtype, name, add=None, jobs=()):
    if mode == 'nn':
        (mo, kc), (_, no) = a.shape, b.shape
    elif mode == 'nt':
        (mo, kc), (no, _) = a.shape, b.shape
    else:
        (kc, mo), (_, no) = a.shape, b.shape
    tm = _pick(mo, MM_M_PREFS)
    tn = no if no <= 2048 and no % 512 != 0 else _pick(no, MM_N_PREFS)
    tk = _pick(kc, MM_K_PREFS)
    nk = kc // tk
    if mode == 'nn':
        a_spec = pl.BlockSpec((tm, tk), lambda i, j, k: (i, k))
        b_spec = pl.BlockSpec((tk, tn), lambda i, j, k: (k, j))
        dims = (((1,), (0,)), ((), ()))
    elif mode == 'nt':
        a_spec = pl.BlockSpec((tm, tk), lambda i, j, k: (i, k))
        b_spec = pl.BlockSpec((tn, tk), lambda i, j, k: (j, k))
        dims = (((1,), (1,)), ((), ()))
    else:
        a_spec = pl.BlockSpec((tk, tm), lambda i, j, k: (k, i))
        b_spec = pl.BlockSpec((tk, tn), lambda i, j, k: (k, j))
        dims = (((0,), (0,)), ((), ()))
    o_spec = pl.BlockSpec((tm, tn), lambda i, j, k: (i, j))

    def body(ins, outs, scr):
        a_ref, b_ref = ins[0], ins[1]
        (o_ref,), (acc_ref,) = outs, scr
        k = pl.program_id(2)

        @pl.when(k == 0)
        def _():
            acc_ref[...] = jnp.zeros_like(acc_ref) if add is None else ins[2][...]

        acc_ref[...] += lax.dot_general(a_ref[...].astype(BF), b_ref[...].astype(BF), dims,
                                        preferred_element_type=F32)

        @pl.when(k == nk - 1)
        def _():
            o_ref[...] = acc_ref[...].astype(o_ref.dtype)

    outs, exch = _call(
        body, name=name, grid=(mo // tm, no // tn, nk),
        in_specs=[a_spec, b_spec] + ([o_spec] if add is not None else []),
        out_specs=[o_spec], out_shape=[jax.ShapeDtypeStruct((mo, no), out_dtype)],
        args=[a, b] + ([add] if add is not None else []),
        scratch=[pltpu.VMEM((tm, tn), F32)], sem=("parallel", "parallel", "arbitrary"), jobs=jobs)
    return outs, exch


def mm(sch, a, b, name, mode='nn', out_dtype=F32, add=None):
    if mode == 'nn':
        flops = 2.0 * a.shape[0] * a.shape[1] * b.shape[1]
    elif mode == 'nt':
        flops = 2.0 * a.shape[0] * a.shape[1] * b.shape[0]
    else:
        flops = 2.0 * a.shape[0] * a.shape[1] * b.shape[1]
    return _ride(sch, flops / MXU_FLOPS_PER_MS, _mm_call, a=a, b=b, mode=mode, out_dtype=out_dtype, name=name,
                 add=add)[0]


def _full_spec(p):
    nd = p.ndim
    return pl.BlockSpec(p.shape, lambda i: (0,) * nd)


def _load_rows(refs, n_rows, halo, step_is_first):
    cur, prev, pos = [], [], 0
    for r in range(n_rows):
        cur.append(refs[pos][...].astype(F32))
        pos += 1
        if r in halo:
            keep = jnp.where(step_is_first, 0.0, 1.0).astype(F32)
            prev.append(refs[pos][...].astype(F32) * keep)
            pos += 1
        else:
            prev.append(None)
    return cur, prev, pos


def _join(cur, prev):
    return [c if p is None else jnp.concatenate([p, c], axis=0) for c, p in zip(cur, prev)]


def _rw_fwd(f, rows, params, outs, *, name, n_reduce=0, halo=(), tt=None, jobs=()):
    t_len = rows[0].shape[0]
    tt = tt or _pick(t_len, TT_PREFS)
    nt = t_len // tt
    n_rows, n_par, n_out = len(rows), len(params), len(outs)

    def body(ins, orefs, scr):
        i = pl.program_id(0)
        cur, prev, pos = _load_rows(ins, n_rows, halo, i == 0)
        pvals = [ins[pos + k][...] for k in range(n_par)]
        res = f(_join(cur, prev), pvals)
        n_f = len(res) - n_reduce
        for k, (_, _, src) in enumerate(outs):
            orefs[k][...] = res[src].astype(orefs[k].dtype)
        for k in range(n_reduce):
            ref, val = orefs[n_out + k], res[n_f + k]

            @pl.when(i == 0)
            def _():
                ref[...] = val

            @pl.when(i > 0)
            def _():
                ref[...] += val

    in_specs, args = [], []
    for r, x in enumerate(rows):
        c = x.shape[1]
        in_specs.append(pl.BlockSpec((tt, c), lambda i: (i, 0)))
        args.append(x)
        if r in halo:
            in_specs.append(pl.BlockSpec((tt, c), lambda i: (jnp.maximum(i - 1, 0), 0)))
            args.append(x)
    for p in params:
        in_specs.append(_full_spec(p))
        args.append(p)
    out_specs = [pl.BlockSpec((tt, c), lambda i: (i, 0)) for c, _, _ in outs]
    out_shape = [jax.ShapeDtypeStruct((t_len, c), dt) for c, dt, _ in outs]
    for _ in range(n_reduce):
        out_specs.append(pl.BlockSpec((1, 1), lambda i: (0, 0)))
        out_shape.append(jax.ShapeDtypeStruct((1, 1), F32))
    return _call(body, name=name, grid=(nt,), in_specs=in_specs, out_specs=out_specs, out_shape=out_shape,
                 args=args, sem=("arbitrary",), jobs=jobs)


def _rw_bwd(f, rows, params, cts, *, name, n_reduce=0, halo=(), nd_rows=(), nd_params=(), tt=None, jobs=()):
    t_len = rows[0].shape[0]
    tt = tt or _pick(t_len, TT_PREFS)
    nt = t_len // tt
    n_rows, n_par, n_ct = len(rows), len(params), len(cts)
    d_rows = [r for r in range(n_rows) if r not in nd_rows]
    d_pars = [k for k in range(n_par) if k not in nd_params]
    h_rows = [r for r in d_rows if r in halo]

    def blk(j):
        return nt - 1 - j

    def body(ins, orefs, carry_refs):
        j = pl.program_id(0)
        cur, prev, pos = _load_rows(ins, n_rows, halo, blk(j) == 0)
        pvals = [ins[pos + k][...] for k in range(n_par)]
        pos += n_par
        ct_vals = [ins[pos + k][...] for k in range(n_ct)]

        def g(dcur, dprev, dpar):
            c, p, q = list(cur), list(prev), list(pvals)
            for r, v in zip(d_rows, dcur):
                c[r] = v
            for r, v in zip(h_rows, dprev):
                p[r] = v
            for k, v in zip(d_pars, dpar):
                q[k] = v
            return tuple(f(_join(c, p), q))

        _, vjp = jax.vjp(g, [cur[r] for r in d_rows], [prev[r] for r in h_rows], [pvals[k] for k in d_pars])
        g_cur, g_prev, g_par = vjp(tuple(ct_vals))

        for n, r in enumerate(d_rows):
            if r in halo:
                cref = carry_refs[h_rows.index(r)]

                @pl.when(j == 0)
                def _():
                    cref[...] = jnp.zeros_like(cref)

                orefs[n][...] = g_cur[n] + cref[...]
                cref[...] = g_prev[h_rows.index(r)]
            else:
                orefs[n][...] = g_cur[n]
        for n in range(len(d_pars)):
            ref, val = orefs[len(d_rows) + n], g_par[n]

            @pl.when(j == 0)
            def _():
                ref[...] = val

            @pl.when(j > 0)
            def _():
                ref[...] += val

    in_specs, args = [], []
    for r, x in enumerate(rows):
        c = x.shape[1]
        in_specs.append(pl.BlockSpec((tt, c), lambda j: (blk(j), 0)))
        args.append(x)
        if r in halo:
            in_specs.append(pl.BlockSpec((tt, c), lambda j: (jnp.maximum(blk(j) - 1, 0), 0)))
            args.append(x)
    for p in params:
        in_specs.append(_full_spec(p))
        args.append(p)
    for ct in cts:
        if ct.shape == (1, 1):
            in_specs.append(pl.BlockSpec((1, 1), lambda j: (0, 0)))
        else:
            in_specs.append(pl.BlockSpec((tt, ct.shape[1]), lambda j: (blk(j), 0)))
        args.append(ct)
    out_specs, out_shape, scratch = [], [], []
    for r in d_rows:
        c = rows[r].shape[1]
        out_specs.append(pl.BlockSpec((tt, c), lambda j: (blk(j), 0)))
        out_shape.append(jax.ShapeDtypeStruct((t_len, c), F32))
        if r in halo:
            scratch.append(pltpu.VMEM((tt, c), F32))
    for k in d_pars:
        out_specs.append(_full_spec(params[k]))
        out_shape.append(jax.ShapeDtypeStruct(params[k].shape, F32))
    return _call(body, name=name + '_bwd', grid=(nt,), in_specs=in_specs, out_specs=out_specs,
                 out_shape=out_shape, args=args, scratch=scratch, sem=("arbitrary",), jobs=jobs)


def rw_fwd(sch, budget, f, rows, params, outs, **kw):
    return _ride(sch, budget, functools.partial(_rw_fwd, f, list(rows), list(params), outs), **kw)


def rw_bwd(sch, budget, f, rows, params, cts, **kw):
    return _ride(sch, budget, functools.partial(_rw_bwd, f, list(rows), list(params), list(cts)), **kw)


def _scan_call(a, b, mul, *, reverse, name, jobs=()):
    t_len, c_len = a.shape
    tt = _pick(t_len, TT_PREFS)
    tc = _pick(c_len, (512, 256, 128))
    nt = t_len // tt

    def body(ins, orefs, scr):
        (carry,) = scr
        t = pl.program_id(1)
        av, bv = ins[0][...], ins[1][...]
        row = lax.broadcasted_iota(jnp.int32, av.shape, 0)
        s = 1
        while s < tt:
            if reverse:
                ok = row < tt - s
                a_sh = jnp.where(ok, pltpu.roll(av, tt - s, 0), 1.0)
                b_sh = jnp.where(ok, pltpu.roll(bv, tt - s, 0), 0.0)
            else:
                ok = row >= s
                a_sh = jnp.where(ok, pltpu.roll(av, s, 0), 1.0)
                b_sh = jnp.where(ok, pltpu.roll(bv, s, 0), 0.0)
            bv = av * b_sh + bv
            av = av * a_sh
            s *= 2

        @pl.when(t == 0)
        def _():
            carry[...] = jnp.zeros_like(carry)

        hs = bv + av * carry[...]
        orefs[0][...] = hs
        edge = 0 if reverse else tt - 1
        carry[...] = orefs[0][edge:edge + 1, :]
        if mul is not None:
            orefs[1][...] = hs * ins[2][...]

    def idx(c, t):
        return ((nt - 1 - t) if reverse else t, c)

    spec = pl.BlockSpec((tt, tc), idx)
    n_in, n_out = (2, 1) if mul is None else (3, 2)
    return _call(body, name=name, grid=(c_len // tc, nt), in_specs=[spec] * n_in, out_specs=[spec] * n_out,
                 out_shape=[jax.ShapeDtypeStruct((t_len, c_len), F32)] * n_out,
                 args=[a, b] if mul is None else [a, b, mul],
                 scratch=[pltpu.VMEM((1, tc), F32)], sem=("parallel", "arbitrary"), jobs=jobs)


NT = (((1,), (1,)), ((), ()))


def _att_weights(s, rel, diag, lg, softmax, scale, t_axis):
    row = lax.broadcasted_iota(jnp.int32, s.shape, t_axis)
    col = lax.broadcasted_iota(jnp.int32, s.shape, 1 - t_axis)
    if softmax:
        s = s * scale
        return jnp.where(row >= col, s, -1e30) if diag else s
    diff = (rel + row - col).astype(F32)
    dec = jnp.exp(jnp.maximum(diff, 0.0) * lg)
    return jnp.where(diff >= 0.0, dec, 0.0) if diag else dec


def _att_fwd_call(q, k, v, lgt, *, heads, softmax, scale, name, jobs=()):
    t_len = q.shape[0]
    dqk, dv = q.shape[1] // heads, v.shape[1] // heads
    blk = _pick(t_len, ATT_PREFS)
    nb = t_len // blk

    def body(ins, orefs, scr):
        q_ref, k_ref, v_ref, lg_ref = ins
        o_ref, lse_ref = orefs
        m_sc, l_sc, acc_sc = scr
        i = pl.program_id(1)
        lg = lg_ref[0:1, 0:1]
        qb = q_ref[...]
        m_sc[...] = jnp.full_like(m_sc, -1e30)
        l_sc[...] = jnp.zeros_like(l_sc)
        acc_sc[...] = jnp.zeros_like(acc_sc)

        def step(j, diag):
            rows = pl.ds(pl.multiple_of(j * blk, blk), blk)
            s = lax.dot_general(qb, k_ref[rows, :], NT, preferred_element_type=F32)
            w = _att_weights(s, (i - j) * blk, diag, lg, softmax, scale, 0)
            vb = v_ref[rows, :]
            if softmax:
                m_new = jnp.maximum(m_sc[...], jnp.max(w, axis=-1, keepdims=True))
                corr = jnp.exp(m_sc[...] - m_new)
                p = jnp.exp(w - m_new)
                l_sc[...] = corr * l_sc[...] + jnp.sum(p, axis=-1, keepdims=True)
                acc_sc[...] = corr * acc_sc[...] + jnp.dot(p.astype(BF), vb, preferred_element_type=F32)
                m_sc[...] = m_new
            else:
                acc_sc[...] += jnp.dot((s * w).astype(BF), vb, preferred_element_type=F32)

        lax.fori_loop(0, i, lambda j, c: (step(j, False), c)[1], 0)
        step(i, True)
        if softmax:
            o_ref[...] = acc_sc[...] / l_sc[...]
            lse_ref[...] = m_sc[...] + jnp.log(l_sc[...])
        else:
            o_ref[...] = acc_sc[...]
            lse_ref[...] = jnp.zeros_like(lse_ref)

    return _call(
        body, name=name, grid=(heads, nb),
        in_specs=[
            pl.BlockSpec((blk, dqk), lambda h, i: (i, h)),
            pl.BlockSpec((t_len, dqk), lambda h, i: (0, h)),
            pl.BlockSpec((t_len, dv), lambda h, i: (0, h)),
            pl.BlockSpec((None, 1, 128), lambda h, i: (h, 0, 0)),
        ],
        out_specs=[
            pl.BlockSpec((blk, dv), lambda h, i: (i, h)),
            pl.BlockSpec((None, blk, 1), lambda h, i: (h, i, 0)),
        ],
        out_shape=[jax.ShapeDtypeStruct((t_len, heads * dv), F32), jax.ShapeDtypeStruct((heads, t_len, 1), F32)],
        args=[q, k, v, lgt],
        scratch=[pltpu.VMEM((blk, 1), F32), pltpu.VMEM((blk, 1), F32), pltpu.VMEM((blk, dv), F32)],
        sem=("parallel", "arbitrary"), jobs=jobs)


def _att_dq_call(q, k, v, lgt, o, lse, do, *, heads, softmax, scale, name, jobs=()):
    t_len = q.shape[0]
    dqk, dv = q.shape[1] // heads, v.shape[1] // heads
    blk = _pick(t_len, ATT_PREFS)
    nb = t_len // blk

    def body(ins, orefs, scr):
        q_ref, k_ref, v_ref, lg_ref, o_ref, do_ref, lse_ref = ins
        dq_ref, delta_ref = orefs
        (acc,) = scr
        i = pl.program_id(1)
        lg = lg_ref[0:1, 0:1]
        qb = q_ref[...]
        do = do_ref[...]
        dob = do.astype(BF)
        delta = jnp.sum(do * o_ref[...], axis=-1, keepdims=True)
        lse = lse_ref[...]
        delta_ref[...] = delta
        acc[...] = jnp.zeros_like(acc)

        def step(j, diag):
            rows = pl.ds(pl.multiple_of(j * blk, blk), blk)
            kb = k_ref[rows, :]
            s = lax.dot_general(qb, kb, NT, preferred_element_type=F32)
            w = _att_weights(s, (i - j) * blk, diag, lg, softmax, scale, 0)
            dp = lax.dot_general(dob, v_ref[rows, :], NT, preferred_element_type=F32)
            ds = jnp.exp(w - lse) * (dp - delta) * scale if softmax else dp * w
            acc[...] += jnp.dot(ds.astype(BF), kb, preferred_element_type=F32)

        lax.fori_loop(0, i, lambda j, c: (step(j, False), c)[1], 0)
        step(i, True)
        dq_ref[...] = acc[...]

    return _call(
        body, name=name + '_dq', grid=(heads, nb),
        in_specs=[
            pl.BlockSpec((blk, dqk), lambda h, i: (i, h)),
            pl.BlockSpec((t_len, dqk), lambda h, i: (0, h)),
            pl.BlockSpec((t_len, dv), lambda h, i: (0, h)),
            pl.BlockSpec((None, 1, 128), lambda h, i: (h, 0, 0)),
            pl.BlockSpec((blk, dv), lambda h, i: (i, h)),
            pl.BlockSpec((blk, dv), lambda h, i: (i, h)),
            pl.BlockSpec((None, blk, 1), lambda h, i: (h, i, 0)),
        ],
        out_specs=[pl.BlockSpec((blk, dqk), lambda h, i: (i, h)),
                   pl.BlockSpec((None, blk, 1), lambda h, i: (h, i, 0))],
        out_shape=[jax.ShapeDtypeStruct(q.shape, F32), jax.ShapeDtypeStruct((heads, t_len, 1), F32)],
        args=[q, k, v, lgt, o, do, lse],
        scratch=[pltpu.VMEM((blk, dqk), F32)], sem=("parallel", "arbitrary"), jobs=jobs)


def _att_dkv_call(q, k, v, lgt, lse_row, delta_row, do, *, heads, softmax, scale, name, jobs=()):
    t_len = q.shape[0]
    dqk, dv = q.shape[1] // heads, v.shape[1] // heads
    blk = _pick(t_len, ATT_PREFS)
    nb = t_len // blk

    def body(ins, orefs, scr):
        q_ref, k_ref, v_ref, lg_ref, do_ref, lse_ref, delta_ref = ins
        dk_acc, dv_acc = scr
        j = pl.program_id(1)
        lg = lg_ref[0:1, 0:1]
        kb, vb = k_ref[...], v_ref[...]
        dk_acc[...] = jnp.zeros_like(dk_acc)
        dv_acc[...] = jnp.zeros_like(dv_acc)

        def step(i, diag):
            rows = pl.ds(pl.multiple_of(i * blk, blk), blk)
            qb = q_ref[rows, :]
            dob = do_ref[rows, :].astype(BF)
            s = lax.dot_general(kb, qb, NT, preferred_element_type=F32)
            w = _att_weights(s, (i - j) * blk, diag, lg, softmax, scale, 1)
            dp = lax.dot_general(vb, dob, NT, preferred_element_type=F32)
            if softmax:
                p = jnp.exp(w - lse_ref[:, rows])
                ds = p * (dp - delta_ref[:, rows]) * scale
            else:
                p, ds = s * w, dp * w
            dv_acc[...] += jnp.dot(p.astype(BF), dob, preferred_element_type=F32)
            dk_acc[...] += jnp.dot(ds.astype(BF), qb, preferred_element_type=F32)

        step(j, True)
        lax.fori_loop(j + 1, nb, lambda i, c: (step(i, False), c)[1], 0)
        orefs[0][...] = dk_acc[...]
        orefs[1][...] = dv_acc[...]

    return _call(
        body, name=name + '_dkv', grid=(heads, nb),
        in_specs=[
            pl.BlockSpec((t_len, dqk), lambda h, j: (0, h)),
            pl.BlockSpec((blk, dqk), lambda h, j: (j, h)),
            pl.BlockSpec((blk, dv), lambda h, j: (j, h)),
            pl.BlockSpec((None, 1, 128), lambda h, j: (h, 0, 0)),
            pl.BlockSpec((t_len, dv), lambda h, j: (0, h)),
            pl.BlockSpec((None, 1, t_len), lambda h, j: (h, 0, 0)),
            pl.BlockSpec((None, 1, t_len), lambda h, j: (h, 0, 0)),
        ],
        out_specs=[
            pl.BlockSpec((blk, dqk), lambda h, j: (j, h)),
            pl.BlockSpec((blk, dv), lambda h, j: (j, h)),
        ],
        out_shape=[jax.ShapeDtypeStruct(k.shape, F32), jax.ShapeDtypeStruct(v.shape, F32)],
        args=[q, k, v, lgt, do, lse_row, delta_row],
        scratch=[pltpu.VMEM((blk, dqk), F32), pltpu.VMEM((blk, dv), F32)],
        sem=("parallel", "arbitrary"), jobs=jobs)


def _adamw_call(contribs, w, m, v, *, name, jobs=()):
    r_len, c_len = w.shape
    n_chunk = len(contribs)
    r_chunk = r_len // n_chunk
    cap = max(min(ADAM_BLOCK_ELEMS, 4 * ADAM_BLOCK_ELEMS // n_chunk) // c_len, 1)
    tr = r_chunk
    for cand in (512, 256, 128, 64, 32, 16):
        if cand <= cap and r_chunk % cand == 0:
            tr = cand
            break
    per = r_chunk // tr

    def body(ins, orefs, scr):
        w_ref, m_ref, v_ref = ins[n_chunk:]
        g_ref, d_ref, mo_ref, vo_ref = orefs
        i = pl.program_id(0)

        def update(c_ref):
            g = c_ref[0].astype(F32)
            for n in range(1, N_DEV):
                g = g + c_ref[n].astype(F32)
            m_new = ADAM_B1 * m_ref[...] + (1.0 - ADAM_B1) * g
            v_new = ADAM_B2 * v_ref[...] + (1.0 - ADAM_B2) * jnp.square(g)
            m_hat = m_new / (1.0 - ADAM_B1 ** ADAM_STEP)
            v_hat = v_new / (1.0 - ADAM_B2 ** ADAM_STEP)
            g_ref[...] = g
            d_ref[...] = -ADAM_LR * (m_hat / (jnp.sqrt(v_hat) + ADAM_EPS) + ADAM_WD * w_ref[...])
            mo_ref[...] = m_new
            vo_ref[...] = v_new

        if n_chunk == 1:
            update(ins[0])
        else:
            for n in range(n_chunk):
                @pl.when(i // per == n)
                def _():
                    update(ins[n])

    spec = pl.BlockSpec((tr, c_len), lambda i: (i, 0))
    c_specs = [pl.BlockSpec((N_DEV, tr, c_len), functools.partial(
        lambda i, n: (0, jnp.clip(i - n * per, 0, per - 1), 0), n=n)) for n in range(n_chunk)]
    return _call(body, name=name, grid=(r_len // tr,), in_specs=c_specs + [spec, spec, spec],
                 out_specs=[spec] * 4, out_shape=[jax.ShapeDtypeStruct((r_len, c_len), F32)] * 4,
                 args=list(contribs) + [w, m, v], sem=("arbitrary",), jobs=jobs)


def _rope_lanes(x, cc, s_lo, s_hi):
    return x * cc + _roll(x, 32, 1) * s_lo + _roll(x, 96, 1) * s_hi


def _f_lru(tt, branch):
    lb = branch // LRU_HEADS

    def f(rows, params):
        (xcat,) = rows
        cw, cb, wa, ba, wx, bx, lam = params
        conv = cb
        for j in range(CONV_W):
            sh = CONV_W - 1 - j
            xs = xcat if sh == 0 else _roll(xcat, sh, 0)
            conv = conv + cw[j:j + 1, :] * xs[tt:, :]
        rs, gs = [], []
        for h in range(LRU_HEADS):
            ub = conv[:, h * lb:(h + 1) * lb]
            rs.append(_bdot(ub, wa[h]))
            gs.append(_bdot(ub, wx[h]))
        r = jax.nn.sigmoid(jnp.concatenate(rs, axis=-1) + ba)
        gate = jax.nn.sigmoid(jnp.concatenate(gs, axis=-1) + bx)
        log_a = LRU_C * r * (-_softplus(-lam))
        a = jnp.exp(log_a)
        one_minus_a2 = -jnp.tanh(log_a) * (jnp.exp(2.0 * log_a) + 1.0)
        return a, (conv * gate) * jnp.sqrt(one_minus_a2)

    return f


def _f_gate(rows, params):
    hs, g = rows
    return (hs * _silu(g),)


def _f_ln(rows, params):
    h, br = rows
    g, b = params
    pre = ALPHA * h + br
    mu = jnp.mean(pre, axis=-1, keepdims=True)
    var = jnp.mean(jnp.square(pre - mu), axis=-1, keepdims=True)
    return ((pre - mu) * lax.rsqrt(var + LN_EPS) * g + b,)


def _f_pool(tt, branch):
    grp = branch // len(POOL_WINDOWS)

    def f(rows, params):
        xcat, tidx = rows
        sums, acc, w = [], xcat, 1
        while w < POOL_WINDOWS[-1]:
            acc = acc + _roll(acc, w, 0)
            w *= 2
            sums.append(acc[tt:, :])
        u = xcat[tt:, :]
        outs = []
        for gi, w in enumerate(POOL_WINDOWS):
            sl = slice(gi * grp, (gi + 1) * grp)
            outs.append(sums[gi][:, sl] / jnp.minimum(tidx + 1.0, float(w)) - u[:, sl])
        return tuple(outs)

    return f


def _f_gate_pool(rows, params):
    m0, m1, m2, m3, g = rows
    (scale,) = params
    return (jnp.concatenate([m0, m1, m2, m3], axis=-1) * scale * _silu(g),)


def _rms(x, g):
    return x * lax.rsqrt(jnp.mean(jnp.square(x), axis=-1, keepdims=True) + RMS_EPS) * g


def _f_mla_pre(rows, params):
    c, cc, s_lo, s_hi = rows
    qn, kvn = params
    cq = c[:, :Q_LORA]
    ckv = c[:, Q_LORA:Q_LORA + KV_LORA]
    kr = c[:, Q_LORA + KV_LORA:]
    return _rms(cq, qn), _rms(ckv, kvn), _rope_lanes(kr, cc, s_lo, s_hi)


def _f_rope_q(rows, params):
    qc, cc, s_lo, s_hi = rows
    out = []
    for h in range(MLA_HEADS):
        out.append(qc[:, h * MLA_QK:h * MLA_QK + MLA_NOPE])
        out.append(_rope_lanes(qc[:, h * MLA_QK + MLA_NOPE:(h + 1) * MLA_QK], cc, s_lo, s_hi))
    return (jnp.concatenate(out, axis=-1),)


def _f_kcat(dv):
    per = MLA_NOPE + dv

    def f(rows, params):
        kv, krr = rows
        ks, vs = [], []
        for h in range(MLA_HEADS):
            ks.append(kv[:, h * per:h * per + MLA_NOPE])
            ks.append(krr)
            vs.append(kv[:, h * per + MLA_NOPE:(h + 1) * per])
        return jnp.concatenate(ks, axis=-1), jnp.concatenate(vs, axis=-1)

    return f


def _f_rope_ret(dk):
    half = dk // 2

    def f(rows, params):
        q, k, cos, sin = rows
        qs, ks = [], []
        for h in range(RET_HEADS):
            for src, dst, mult in ((q, qs, 1.0), (k, ks, dk ** -0.5)):
                x1 = src[:, h * dk:h * dk + half]
                x2 = src[:, h * dk + half:(h + 1) * dk]
                dst.append((x1 * cos - x2 * sin) * mult)
                dst.append((x2 * cos + x1 * sin) * mult)
        return jnp.concatenate(qs, axis=-1), jnp.concatenate(ks, axis=-1)

    return f


def _f_gate_gn(dv):
    def f(rows, params):
        o, g = rows
        out = []
        for h in range(RET_HEADS):
            oh = o[:, h * dv:(h + 1) * dv]
            mu = jnp.mean(oh, axis=-1, keepdims=True)
            var = jnp.mean(jnp.square(oh - mu), axis=-1, keepdims=True)
            out.append((oh - mu) * lax.rsqrt(var + LN_EPS))
        return (jnp.concatenate(out, axis=-1) * _silu(g),)

    return f


def _f_loss(rows, params):
    h, tgt, mask = rows
    per_row = jnp.mean(jnp.square(h - tgt), axis=-1, keepdims=True) * mask
    return (0.5 * jnp.sum(per_row, axis=0, keepdims=True),)


def _cat(parts, axis=1):
    return parts[0] if len(parts) == 1 else jnp.concatenate(parts, axis=axis)


def _cols(g):
    return jnp.transpose(g, (1, 0, 2)).reshape(g.shape[1], -1)


def _uncols(w):
    k, n = w.shape
    return jnp.transpose(w.reshape(k, N_DEV, n // N_DEV), (1, 0, 2))


def _heads(g):
    return jnp.transpose(g, (1, 0, 2, 3)).reshape(g.shape[1], -1, g.shape[3])


def _unheads(w):
    h, r, c = w.shape
    return jnp.transpose(w.reshape(h, N_DEV, r // N_DEV, c), (1, 0, 2, 3))


def _rope_tables(t_pad, d):
    inv = ROPE_BASE ** (-jnp.arange(0, d, 2, dtype=F32) / d)
    ang = jnp.arange(t_pad, dtype=F32)[:, None] * inv[None, :]
    return jnp.cos(ang), jnp.sin(ang)


def _row2(v):
    return v.reshape(1, -1)


def _train_local(sch, x2d, tgt_pad, S, *, t_pad):
    seq, d_model = x2d.shape
    branch = d_model
    t_real = N_META + seq
    tt = _pick(t_pad, TT_PREFS)
    n_win = len(POOL_WINDOWS)
    grp = branch // n_win
    lb = branch // LRU_HEADS
    dv2 = branch // MLA_HEADS
    dk3 = branch // RET_HEADS
    gS = {}
    RW, RWB = 0.06, 0.1

    def ln_fwd(h, br, layer):
        h1, hb1 = rw_fwd(sch, RW, _f_ln, [h, br], [_row2(S[f'l{layer}_ln_g']), _row2(S[f'l{layer}_ln_b'])],
                         [(d_model, F32, 0), (d_model, BF, 0)], name=f'l{layer}_ln')
        return h1, hb1

    def ln_bwd(h, br, dh1, layer):
        dh, dbr, dg, db = rw_bwd(sch, RWB, _f_ln, [h, br],
                                 [_row2(S[f'l{layer}_ln_g']), _row2(S[f'l{layer}_ln_b'])], [dh1],
                                 name=f'l{layer}_ln')
        gS[f'l{layer}_ln_g'], gS[f'l{layer}_ln_b'] = dg.reshape(-1), db.reshape(-1)
        return dh, dbr

    tidx = jnp.arange(t_pad, dtype=F32)[:, None]
    rowmask = ((tidx >= N_META) & (tidx < t_real)).astype(F32)

    meta = _cols(_cat(sch.get('meta_tokens')))
    h0 = jnp.concatenate([meta, x2d, jnp.zeros((t_pad - t_real, d_model), F32)], axis=0)
    hb0 = h0.astype(BF)

    w0_in = _cols(_cat(sch.get('l0_w_in')))
    w0_u, w0_g = w0_in[:, :branch], w0_in[:, branch:]
    u0 = mm(sch, hb0, w0_u, 'l0_in_u')
    g0 = mm(sch, hb0, w0_g, 'l0_in_g')
    conv_w = jnp.transpose(_cat(sch.get('l0_conv_w')), (1, 2, 0, 3)).reshape(CONV_W, branch)
    w_a = _heads(_cat(sch.get('l0_w_a'), axis=2))
    w_x = _heads(_cat(sch.get('l0_w_x'), axis=2))
    p0 = [conv_w, _row2(S['l0_conv_b']), w_a, _row2(S['l0_b_a']), w_x, _row2(S['l0_b_x']), _row2(S['l0_lam'])]
    f_lru = _f_lru(tt, branch)
    a0, xin0 = rw_fwd(sch, 0.13, f_lru, [u0], p0, [(branch, F32, 0), (branch, F32, 1)], name='l0_lru',
                      halo=(0,), tt=tt)
    hs0 = _ride(sch, 0.22, _scan_call, a=a0, b=xin0, mul=None, reverse=False, name='l0_scan')[0]
    (z0,) = rw_fwd(sch, RW, _f_gate, [hs0, g0], [], [(branch, BF, 0)], name='l0_gate')
    w0_out = _cat(sch.get('l0_w_out')).reshape(branch, d_model)
    br0 = mm(sch, z0, w0_out, 'l0_out')
    h1, hb1 = ln_fwd(h0, br0, 0)

    w1_in = _cols(_cat(sch.get('l1_w_in')))
    w1_u, w1_g = w1_in[:, :branch], w1_in[:, branch:]
    u1 = mm(sch, hb1, w1_u, 'l1_in_u')
    g1 = mm(sch, hb1, w1_g, 'l1_in_g')
    f_pool = _f_pool(tt, branch)
    ps1 = rw_fwd(sch, RW, f_pool, [u1, tidx], [], [(grp, BF, gi) for gi in range(n_win)], name='l1_pool',
                 halo=(0,), tt=tt)
    w1_grp = _heads(_cat(sch.get('l1_w_grp'), axis=2))
    mixed1 = [mm(sch, ps1[gi], w1_grp[gi], f'l1_grp{gi}') for gi in range(n_win)]
    p1 = [_row2(S['l1_scale'])]
    (z1,) = rw_fwd(sch, RW, _f_gate_pool, mixed1 + [g1], p1, [(branch, BF, 0)], name='l1_gate')
    w1_out = _cat(sch.get('l1_w_out')).reshape(branch, d_model)
    br1 = mm(sch, z1, w1_out, 'l1_out')
    h2, hb2 = ln_fwd(h1, br1, 1)

    w2_in = _cols(_cat(sch.get('l2_w_in')))
    w2_g = w2_in[:, :branch]
    w2_lat = jnp.pad(w2_in[:, branch:], ((0, 0), (0, 128 - MLA_ROPE)))
    g2 = mm(sch, hb2, w2_g, 'l2_in_g')
    c2 = mm(sch, hb2, w2_lat, 'l2_in_c')
    cos, sin = _rope_tables(t_pad, MLA_ROPE)
    zz = jnp.zeros_like(cos)
    tabs = [jnp.concatenate([cos, cos, zz, zz], axis=-1), jnp.concatenate([zz, sin, zz, zz], axis=-1),
            jnp.concatenate([-sin, zz, zz, zz], axis=-1)]
    p2 = [_row2(S['l2_q_norm']), _row2(S['l2_kv_norm'])]
    cqn2, ckvn2, krr2 = rw_fwd(sch, 0.04, _f_mla_pre, [c2] + tabs, p2,
                               [(Q_LORA, BF, 0), (KV_LORA, BF, 1), (128, F32, 2)], name='l2_pre')
    w2_uq = _cols(_cat(sch.get('l2_w_uq'))).reshape(Q_LORA, MLA_HEADS, MLA_NOPE + MLA_ROPE)
    w2_uq = jnp.pad(w2_uq, ((0, 0), (0, 0), (0, MLA_QK - MLA_NOPE - MLA_ROPE))).reshape(Q_LORA, MLA_HEADS * MLA_QK)
    w2_ukv = _cols(_cat(sch.get('l2_w_ukv')))
    qc2 = mm(sch, cqn2, w2_uq, 'l2_uq')
    kv2 = mm(sch, ckvn2, w2_ukv, 'l2_ukv')
    (qcr2,) = rw_fwd(sch, 0.09, _f_rope_q, [qc2] + tabs, [], [(MLA_HEADS * MLA_QK, BF, 0)], name='l2_rope_q')
    f_kcat = _f_kcat(dv2)
    kcat2, v2 = rw_fwd(sch, 0.1, f_kcat, [kv2, krr2], [], [(MLA_HEADS * MLA_QK, BF, 0), (branch, BF, 1)],
                       name='l2_kcat')
    no_decay = jnp.zeros((MLA_HEADS, 1, 128), F32)
    att2 = dict(heads=MLA_HEADS, softmax=True, scale=(MLA_NOPE + MLA_ROPE) ** -0.5, name='l2_att')
    o2, lse2 = _ride(sch, 1.3, _att_fwd_call, q=qcr2, k=kcat2, v=v2, lgt=no_decay, **att2)
    (z2,) = rw_fwd(sch, RW, _f_gate, [o2, g2], [], [(branch, BF, 0)], name='l2_gate')
    w2_out = _cat(sch.get('l2_w_out')).reshape(branch, d_model)
    br2 = mm(sch, z2, w2_out, 'l2_out')
    h3, hb3 = ln_fwd(h2, br2, 2)

    w3_in = _cols(_cat(sch.get('l3_w_in')))
    w3 = [w3_in[:, n * branch:(n + 1) * branch] for n in range(4)]
    q3 = mm(sch, hb3, w3[0], 'l3_in_q')
    k3 = mm(sch, hb3, w3[1], 'l3_in_k')
    v3 = mm(sch, hb3, w3[2], 'l3_in_v', out_dtype=BF)
    g3 = mm(sch, hb3, w3[3], 'l3_in_g')
    cs3 = list(_rope_tables(t_pad, dk3))
    f_rope3 = _f_rope_ret(dk3)
    qr3, kr3 = rw_fwd(sch, 0.09, f_rope3, [q3, k3] + cs3, [], [(branch, BF, 0), (branch, BF, 1)], name='l3_rope')
    log_g = jnp.log(1.0 - 2.0 ** (-5.0 - jnp.arange(RET_HEADS, dtype=F32)))
    lgt = jnp.broadcast_to(log_g[:, None, None], (RET_HEADS, 1, 128))
    att3 = dict(heads=RET_HEADS, softmax=False, scale=1.0, name='l3_ret')
    o3, lse3 = _ride(sch, 0.6, _att_fwd_call, q=qr3, k=kr3, v=v3, lgt=lgt, **att3)
    f_gn = _f_gate_gn(dk3)
    (z3,) = rw_fwd(sch, 0.08, f_gn, [o3, g3], [], [(branch, BF, 0)], name='l3_gate')
    w3_out = _cat(sch.get('l3_w_out')).reshape(branch, d_model)
    br3 = mm(sch, z3, w3_out, 'l3_out')
    (h4,) = rw_fwd(sch, RW, _f_ln, [h3, br3], [_row2(S['l3_ln_g']), _row2(S['l3_ln_b'])], [(d_model, F32, 0)],
                   name='l3_ln')

    (loss,) = rw_fwd(sch, 0.05, _f_loss, [h4, tgt_pad, rowmask], [], [], name='loss', n_reduce=1)

    (dh4,) = rw_bwd(sch, 0.07, _f_loss, [h4, tgt_pad, rowmask], [], [jnp.ones((1, 1), F32)], name='loss',
                    n_reduce=1, nd_rows=(1, 2))

    dh3, dbr3 = ln_bwd(h3, br3, dh4, 3)
    dz3 = mm(sch, dbr3, w3_out, 'l3_out_dx', mode='nt')
    sch.push('l3_w_out', mm(sch, z3.T, dbr3, 'l3_out_dw', mode='nn', out_dtype=BF).reshape(N_DEV, -1, d_model),
             False, 1)
    do3, dg3 = rw_bwd(sch, 0.13, f_gn, [o3, g3], [], [dz3], name='l3_gate')
    dqr3, delta3 = _ride(sch, 0.7, _att_dq_call, q=qr3, k=kr3, v=v3, lgt=lgt, o=o3, lse=lse3, do=do3, **att3)
    dkr3, dv3 = _ride(sch, 0.8, _att_dkv_call, q=qr3, k=kr3, v=v3, lgt=lgt, lse_row=lse3.reshape(RET_HEADS, 1, -1),
                      delta_row=delta3.reshape(RET_HEADS, 1, -1), do=do3, **att3)
    dq3, dk3_ = rw_bwd(sch, 0.13, f_rope3, [q3, k3] + cs3, [], [dqr3, dkr3], name='l3_rope', nd_rows=(2, 3))
    d3 = [dq3, dk3_, dv3, dg3]
    for n in range(4):
        dh3 = mm(sch, d3[n], w3[n], f'l3_in_dx{n}', mode='nt', add=dh3)
    hb3_t = hb3.T
    dw3 = [mm(sch, hb3_t, d3[n], f'l3_in_dw{n}', mode='nn', out_dtype=BF) for n in range(4)]
    sch.push('l3_w_in', _uncols(jnp.concatenate(dw3, axis=1)), False, 1)

    dh2, dbr2 = ln_bwd(h2, br2, dh3, 2)
    dz2 = mm(sch, dbr2, w2_out, 'l2_out_dx', mode='nt')
    sch.push('l2_w_out', mm(sch, z2.T, dbr2, 'l2_out_dw', mode='nn', out_dtype=BF).reshape(N_DEV, -1, d_model),
             False, 1)
    do2, dg2 = rw_bwd(sch, 0.1, _f_gate, [o2, g2], [], [dz2], name='l2_gate')
    dqcr2, delta2 = _ride(sch, 1.3, _att_dq_call, q=qcr2, k=kcat2, v=v2, lgt=no_decay, o=o2, lse=lse2, do=do2,
                          **att2)
    dkcat2, dv2_ = _ride(sch, 1.5, _att_dkv_call, q=qcr2, k=kcat2, v=v2, lgt=no_decay,
                         lse_row=lse2.reshape(MLA_HEADS, 1, -1), delta_row=delta2.reshape(MLA_HEADS, 1, -1),
                         do=do2, **att2)
    dkv2, dkrr2 = rw_bwd(sch, 0.14, f_kcat, [kv2, krr2], [], [dkcat2, dv2_], name='l2_kcat')
    (dqc2,) = rw_bwd(sch, 0.13, _f_rope_q, [qc2] + tabs, [], [dqcr2], name='l2_rope_q', nd_rows=(1, 2, 3))
    dckvn2 = mm(sch, dkv2, w2_ukv, 'l2_ukv_dx', mode='nt')
    dcqn2 = mm(sch, dqc2, w2_uq, 'l2_uq_dx', mode='nt')
    sch.push('l2_w_ukv', _uncols(mm(sch, ckvn2, dkv2, 'l2_ukv_dw', mode='tn', out_dtype=BF)), False, 1)
    dw_uq = mm(sch, cqn2, dqc2, 'l2_uq_dw', mode='tn', out_dtype=BF)
    dw_uq = dw_uq.reshape(Q_LORA, MLA_HEADS, MLA_QK)[:, :, :MLA_NOPE + MLA_ROPE].reshape(Q_LORA, -1)
    sch.push('l2_w_uq', _uncols(dw_uq), False, 1)
    dc2, dqn, dkvn = rw_bwd(sch, 0.05, _f_mla_pre, [c2] + tabs, p2, [dcqn2, dckvn2, dkrr2], name='l2_pre',
                            nd_rows=(1, 2, 3))
    gS['l2_q_norm'], gS['l2_kv_norm'] = dqn.reshape(-1), dkvn.reshape(-1)
    dh2 = mm(sch, dg2, w2_g, 'l2_in_g_dx', mode='nt', add=dh2)
    dh2 = mm(sch, dc2, w2_lat, 'l2_in_c_dx', mode='nt', add=dh2)
    hb2_t = hb2.T
    dw2_g = mm(sch, hb2_t, dg2, 'l2_in_g_dw', mode='nn', out_dtype=BF)
    dw2_lat = mm(sch, hb2_t, dc2, 'l2_in_c_dw', mode='nn', out_dtype=BF)
    n_lat = Q_LORA + KV_LORA + MLA_ROPE
    sch.push('l2_w_in', _uncols(jnp.concatenate([dw2_g, dw2_lat[:, :n_lat]], axis=1)), False, 1)

    dh1, dbr1 = ln_bwd(h1, br1, dh2, 1)
    dz1 = mm(sch, dbr1, w1_out, 'l1_out_dx', mode='nt')
    sch.push('l1_w_out', mm(sch, z1.T, dbr1, 'l1_out_dw', mode='nn', out_dtype=BF).reshape(N_DEV, -1, d_model),
             False, 1)
    res = rw_bwd(sch, 0.11, _f_gate_pool, mixed1 + [g1], p1, [dz1], name='l1_gate')
    dmixed1, dg1, dscale = res[:n_win], res[n_win], res[n_win + 1]
    gS['l1_scale'] = dscale.reshape(-1)
    dps1 = [mm(sch, dmixed1[gi], w1_grp[gi], f'l1_grp{gi}_dx', mode='nt') for gi in range(n_win)]
    dw_grp = jnp.stack([mm(sch, ps1[gi], dmixed1[gi], f'l1_grp{gi}_dw', mode='tn', out_dtype=BF)
                        for gi in range(n_win)])
    sch.push('l1_w_grp', _unheads(dw_grp), False, 2)
    (du1,) = rw_bwd(sch, 0.09, f_pool, [u1, tidx], [], dps1, name='l1_pool', halo=(0,), nd_rows=(1,), tt=tt)
    dh1 = mm(sch, du1, w1_u, 'l1_in_u_dx', mode='nt', add=dh1)
    dh1 = mm(sch, dg1, w1_g, 'l1_in_g_dx', mode='nt', add=dh1)
    hb1_t = hb1.T
    dw1 = [mm(sch, hb1_t, du1, 'l1_in_u_dw', mode='nn', out_dtype=BF),
           mm(sch, hb1_t, dg1, 'l1_in_g_dw', mode='nn', out_dtype=BF)]
    sch.push('l1_w_in', _uncols(jnp.concatenate(dw1, axis=1)), False, 1)

    dh0, dbr0 = ln_bwd(h0, br0, dh1, 0)
    dz0 = mm(sch, dbr0, w0_out, 'l0_out_dx', mode='nt')
    sch.push('l0_w_out', mm(sch, z0.T, dbr0, 'l0_out_dw', mode='nn', out_dtype=BF).reshape(N_DEV, -1, d_model),
             False, 1)
    dhs0, dg0 = rw_bwd(sch, 0.1, _f_gate, [hs0, g0], [], [dz0], name='l0_gate')
    a_next = jnp.concatenate([a0[1:], jnp.ones_like(a0[:1])], axis=0)
    hs_prev = jnp.concatenate([jnp.zeros_like(hs0[:1]), hs0[:-1]], axis=0)
    dxin0, da0 = _ride(sch, 0.25, _scan_call, a=a_next, b=dhs0, mul=hs_prev, reverse=True, name='l0_scan_bwd')
    res = rw_bwd(sch, 0.3, f_lru, [u0], p0, [da0, dxin0], name='l0_lru', halo=(0,), tt=tt)
    du0 = res[0]
    gS['l0_conv_b'], gS['l0_b_a'], gS['l0_b_x'], gS['l0_lam'] = [res[k].reshape(-1) for k in (2, 4, 6, 7)]
    sch.push('l0_w_a', _unheads(res[3]), False, 2)
    sch.push('l0_w_x', _unheads(res[5]), False, 2)
    sch.push('l0_conv_w', jnp.transpose(res[1].reshape(CONV_W, 1, N_DEV, -1), (2, 0, 1, 3)), False, 3)
    hb0_t = hb0.T
    dw0 = [mm(sch, hb0_t, du0, 'l0_in_u_dw', mode='nn', out_dtype=BF),
           mm(sch, hb0_t, dg0, 'l0_in_g_dw', mode='nn', out_dtype=BF)]
    sch.push('l0_w_in', _uncols(jnp.concatenate(dw0, axis=1)), False, 1)
    dh0 = mm(sch, du0, w0_u, 'l0_in_u_dx', mode='nt', add=dh0)
    dh0 = mm(sch, dg0, w0_g, 'l0_in_g_dx', mode='nt', add=dh0)
    sch.push('meta_tokens', _uncols(dh0[:N_META]), False, 1)

    return loss[0, 0], dh0[N_META:t_real], gS


def _as2d(a):
    return a.reshape(-1, a.shape[-1])


def kernel(x, meta_tokens, l0_w_in, l0_conv_w, l0_conv_b, l0_w_a, l0_b_a, l0_w_x, l0_b_x, l0_lam, l0_w_out, l0_ln_g, l0_ln_b, l1_w_in, l1_w_grp, l1_scale, l1_w_out, l1_ln_g, l1_ln_b, l2_w_in, l2_q_norm, l2_w_uq, l2_kv_norm, l2_w_ukv, l2_w_out, l2_ln_g, l2_ln_b, l3_w_in, l3_w_out, l3_ln_g, l3_ln_b, loss_target, m_meta_tokens, m_l0_w_in, m_l0_conv_w, m_l0_conv_b, m_l0_w_a, m_l0_b_a, m_l0_w_x, m_l0_b_x, m_l0_lam, m_l0_w_out, m_l0_ln_g, m_l0_ln_b, m_l1_w_in, m_l1_w_grp, m_l1_scale, m_l1_w_out, m_l1_ln_g, m_l1_ln_b, m_l2_w_in, m_l2_q_norm, m_l2_w_uq, m_l2_kv_norm, m_l2_w_ukv, m_l2_w_out, m_l2_ln_g, m_l2_ln_b, m_l3_w_in, m_l3_w_out, m_l3_ln_g, m_l3_ln_b, v_meta_tokens, v_l0_w_in, v_l0_conv_w, v_l0_conv_b, v_l0_w_a, v_l0_b_a, v_l0_w_x, v_l0_b_x, v_l0_lam, v_l0_w_out, v_l0_ln_g, v_l0_ln_b, v_l1_w_in, v_l1_w_grp, v_l1_scale, v_l1_w_out, v_l1_ln_g, v_l1_ln_b, v_l2_w_in, v_l2_q_norm, v_l2_w_uq, v_l2_kv_norm, v_l2_w_ukv, v_l2_w_out, v_l2_ln_g, v_l2_ln_b, v_l3_w_in, v_l3_w_out, v_l3_ln_g, v_l3_ln_b):
    args = (meta_tokens, l0_w_in, l0_conv_w, l0_conv_b, l0_w_a, l0_b_a, l0_w_x, l0_b_x, l0_lam, l0_w_out, l0_ln_g, l0_ln_b, l1_w_in, l1_w_grp, l1_scale, l1_w_out, l1_ln_g, l1_ln_b, l2_w_in, l2_q_norm, l2_w_uq, l2_kv_norm, l2_w_ukv, l2_w_out, l2_ln_g, l2_ln_b, l3_w_in, l3_w_out, l3_ln_g, l3_ln_b)
    moms = (m_meta_tokens, m_l0_w_in, m_l0_conv_w, m_l0_conv_b, m_l0_w_a, m_l0_b_a, m_l0_w_x, m_l0_b_x, m_l0_lam, m_l0_w_out, m_l0_ln_g, m_l0_ln_b, m_l1_w_in, m_l1_w_grp, m_l1_scale, m_l1_w_out, m_l1_ln_g, m_l1_ln_b, m_l2_w_in, m_l2_q_norm, m_l2_w_uq, m_l2_kv_norm, m_l2_w_ukv, m_l2_w_out, m_l2_ln_g, m_l2_ln_b, m_l3_w_in, m_l3_w_out, m_l3_ln_g, m_l3_ln_b)
    vels = (v_meta_tokens, v_l0_w_in, v_l0_conv_w, v_l0_conv_b, v_l0_w_a, v_l0_b_a, v_l0_w_x, v_l0_b_x, v_l0_lam, v_l0_w_out, v_l0_ln_g, v_l0_ln_b, v_l1_w_in, v_l1_w_grp, v_l1_scale, v_l1_w_out, v_l1_ln_g, v_l1_ln_b, v_l2_w_in, v_l2_q_norm, v_l2_w_uq, v_l2_kv_norm, v_l2_w_ukv, v_l2_w_out, v_l2_ln_g, v_l2_ln_b, v_l3_w_in, v_l3_w_out, v_l3_ln_g, v_l3_ln_b)
    W = dict(zip(WEIGHTS, args))
    M = dict(zip(WEIGHTS, moms))
    V = dict(zip(WEIGHTS, vels))

    seq = x.shape[1]
    t_real = N_META + seq
    t_pad = -(-t_real // ROW_ALIGN) * ROW_ALIGN
    tgt_pad = jnp.pad(loss_target[0], ((N_META, t_pad - t_real), (0, 0)))

    sch = _Schedule()
    for n in GATHER_ORDER:
        shard = W[n].astype(BF) if n in BIG else W[n]
        sch.push(n, shard, True, shard.ndim - 2)
    S = {n: W[n] for n in REPLICATED}

    loss, gx, gS = _train_local(sch, x[0], tgt_pad, S, t_pad=t_pad)

    flat = jnp.concatenate([gS[n].reshape(-1) for n in REPLICATED]).reshape(-1, 128)
    sch.push('small_grads', flat, True, 0)

    out_g, out_d, out_m, out_v = {}, {}, {}, {}
    order = ['l3_w_out', 'l3_w_in', 'l2_w_out', 'l2_w_ukv', 'l2_w_uq', 'l2_w_in', 'l1_w_out', 'l1_w_grp',
             'l1_w_in', 'l0_w_out', 'l0_w_a', 'l0_w_x', 'l0_conv_w', 'l0_w_in', 'meta_tokens']
    for n in order:
        shp = W[n].shape
        w2, m2, v2 = _as2d(W[n]), _as2d(M[n]), _as2d(V[n])
        parts = [p.reshape((N_DEV, -1, w2.shape[1])) for p in sch.get(n)]
        res = _ride(sch, w2.size * 1.5e-8, _adamw_call, contribs=parts, w=w2, m=m2, v=v2, name='adamw_' + n)
        out_g[n], out_d[n], out_m[n], out_v[n] = [r.reshape(shp) for r in res]
    cat = lambda D: jnp.concatenate([D[n].reshape(-1) for n in REPLICATED]).reshape(-1, 128)
    res = _adamw_call(sch.get('small_grads'), cat(W), cat(M), cat(V), name='adamw_small')[0]
    off = 0
    for n in REPLICATED:
        size = W[n].size
        for dst, r in zip((out_g, out_d, out_m, out_v), res):
            dst[n] = r.reshape(-1)[off:off + size].reshape(W[n].shape)
        off += size
    sch.flush()

    loss = lax.psum(loss, ("x", "y", "c"))
    return (loss, gx[None], *[out_g[n] for n in WEIGHTS], *[out_d[n] for n in WEIGHTS],
            *[out_m[n] for n in WEIGHTS], *[out_v[n] for n in WEIGHTS])
```

```python
import functools

import jax
import jax.numpy as jnp
from jax import lax
from jax.experimental import pallas as pl
from jax.experimental.pallas import tpu as pltpu

F32 = jnp.float32
BF = jnp.bfloat16

N_DEV = 8
N_META = 16
ALPHA = (2.0 * 4) ** 0.25
LN_EPS = 1e-5
RMS_EPS = 1e-6
ROPE_BASE = 10000.0
LRU_HEADS = 16
CONV_W = 4
LRU_C = 8.0
POOL_WINDOWS = (2, 4, 8, 16)
MLA_HEADS = 32
MLA_NOPE = 128
MLA_ROPE = 64
MLA_QK = 256
Q_LORA = 1024
KV_LORA = 512
RET_HEADS = 16
ADAM_LR = 0.001
ADAM_B1 = 0.9
ADAM_B2 = 0.999
ADAM_EPS = 1e-08
ADAM_WD = 0.01
ADAM_STEP = 10

ROW_ALIGN = 128
VMEM_LIMIT_BYTES = 56 * 1024 * 1024
TT_PREFS = (128, 64, 32, 16, 8)
ATT_PREFS = (384, 256, 128)
MM_M_PREFS = (1408, 1024, 512, 384, 256, 128)
MM_N_PREFS = (1024, 640, 512, 384, 256, 128)
MM_K_PREFS = (1408, 1024, 512, 384, 256, 128)
MM_VMEM_BUDGET = 40 * 1024 * 1024
ADAM_BLOCK_ELEMS = 128 * 1024
EXCH_MS_PER_MB = 0.0857
EXCH_CHUNK_MS = 0.1
MXU_FLOPS_PER_MS = 6.0e11

WEIGHTS = ['meta_tokens', 'l0_w_in', 'l0_conv_w', 'l0_conv_b', 'l0_w_a', 'l0_b_a', 'l0_w_x', 'l0_b_x', 'l0_lam',
           'l0_w_out', 'l0_ln_g', 'l0_ln_b', 'l1_w_in', 'l1_w_grp', 'l1_scale', 'l1_w_out', 'l1_ln_g', 'l1_ln_b',
           'l2_w_in', 'l2_q_norm', 'l2_w_uq', 'l2_kv_norm', 'l2_w_ukv', 'l2_w_out', 'l2_ln_g', 'l2_ln_b',
           'l3_w_in', 'l3_w_out', 'l3_ln_g', 'l3_ln_b']
BIG = ['l0_w_in', 'l0_w_out', 'l1_w_in', 'l1_w_grp', 'l1_w_out', 'l2_w_in', 'l2_w_uq', 'l2_w_ukv', 'l2_w_out',
       'l3_w_in', 'l3_w_out']
SHARDED_F32 = ['meta_tokens', 'l0_conv_w', 'l0_w_a', 'l0_w_x']
REPLICATED = [n for n in WEIGHTS if n not in BIG and n not in SHARDED_F32]
GATHER_ORDER = ['meta_tokens', 'l0_w_in', 'l0_conv_w', 'l0_w_a', 'l0_w_x', 'l0_w_out', 'l1_w_in', 'l1_w_grp',
                'l1_w_out', 'l2_w_in', 'l2_w_uq', 'l2_w_ukv', 'l2_w_out', 'l3_w_in', 'l3_w_out']


def _pick(n, prefs):
    for p in prefs:
        if n % p == 0:
            return p
    return n


def _exchange_copies(jobs, in_refs, out_refs, send_sems, recv_sems, local_sems):
    x, y, c = lax.axis_index("x"), lax.axis_index("y"), lax.axis_index("c")
    me = 4 * x + 2 * y + c
    peers = [(x, y, 1 - c), (1 - x, y, c), (x, 1 - y, c), (1 - x, 1 - y, c),
             (1 - x, y, 1 - c), (x, 1 - y, 1 - c), (1 - x, 1 - y, 1 - c)]
    copies = []
    for n, (_, gather) in enumerate(jobs):
        for p, (px, py, pc) in enumerate(peers):
            src = in_refs[n] if gather else in_refs[n].at[4 * px + 2 * py + pc]
            k = n * (N_DEV - 1) + p
            copies.append(pltpu.make_async_remote_copy(
                src_ref=src, dst_ref=out_refs[n].at[me], send_sem=send_sems.at[k], recv_sem=recv_sems.at[k],
                device_id=(px, py, pc), device_id_type=pl.DeviceIdType.MESH))
        copies.append(pltpu.make_async_copy(in_refs[n] if gather else in_refs[n].at[me], out_refs[n].at[me],
                                            local_sems.at[n]))
    return copies


def _exchange_shapes(jobs):
    shapes = []
    for arr, gather in jobs:
        blk = arr.shape if gather else arr.shape[1:]
        shapes.append(jax.ShapeDtypeStruct((N_DEV,) + tuple(blk), arr.dtype))
    return shapes


def _exchange_scratch(jobs):
    n = len(jobs)
    return [pltpu.SemaphoreType.DMA((n * (N_DEV - 1),)), pltpu.SemaphoreType.DMA((n * (N_DEV - 1),)),
            pltpu.SemaphoreType.DMA((n,))]


def _call(body, *, name, grid, in_specs, out_specs, out_shape, args, scratch=(), sem=None, jobs=()):
    jobs = list(jobs)
    n_in, n_out, n_sc, n_job = len(in_specs), len(out_specs), len(scratch), len(jobs)
    hbm = pl.BlockSpec(memory_space=pl.ANY)

    def kern(*refs):
        ins = refs[:n_in]
        job_ins = refs[n_in:n_in + n_job]
        pos = n_in + n_job
        outs = refs[pos:pos + n_out]
        job_outs = refs[pos + n_out:pos + n_out + n_job]
        pos += n_out + n_job
        scr = refs[pos:pos + n_sc]
        if n_job:
            ids = [pl.program_id(d) for d in range(len(grid))]
            first = functools.reduce(jnp.logical_and, [i == 0 for i in ids])
            last = functools.reduce(jnp.logical_and, [i == g - 1 for i, g in zip(ids, grid)])
            copies = _exchange_copies(jobs, job_ins, job_outs, *refs[pos + n_sc:])

            @pl.when(first)
            def _():
                for cp in copies:
                    cp.start()

        body(ins, outs, scr)
        if n_job:
            @pl.when(last)
            def _():
                for cp in copies:
                    cp.wait()

    kw = dict(vmem_limit_bytes=VMEM_LIMIT_BYTES)
    if sem is not None:
        kw['dimension_semantics'] = tuple("arbitrary" for _ in grid) if n_job else sem
    res = pl.pallas_call(
        kern, name=name, grid=grid,
        in_specs=list(in_specs) + [hbm] * n_job,
        out_specs=list(out_specs) + [hbm] * n_job,
        out_shape=list(out_shape) + _exchange_shapes(jobs),
        scratch_shapes=list(scratch) + (_exchange_scratch(jobs) if n_job else []),
        compiler_params=pltpu.CompilerParams(**kw),
    )(*args, *[a for a, _ in jobs])
    return list(res[:n_out]), list(res[n_out:])


def _exchange_alone(jobs, name):
    def body(*refs):
        n = len(jobs)
        copies = _exchange_copies(jobs, refs[:n], refs[n:2 * n], *refs[2 * n:])
        for cp in copies:
            cp.start()
        for cp in copies:
            cp.wait()

    hbm = pl.BlockSpec(memory_space=pl.ANY)
    return list(pl.pallas_call(
        body, name=name, in_specs=[hbm] * len(jobs), out_specs=[hbm] * len(jobs),
        out_shape=_exchange_shapes(jobs), scratch_shapes=_exchange_scratch(jobs),
    )(*[a for a, _ in jobs]))


class _Schedule:
    def __init__(self):
        self.pending = []
        self.done = {}
        self.chunks = {}
        self.count = 0

    def push(self, name, arr, gather, row_axis):
        mb = arr.size * arr.dtype.itemsize / (1 if gather else N_DEV) / 1e6
        cost = mb * EXCH_MS_PER_MB
        rows = arr.shape[row_axis]
        n = 1
        leading = row_axis == (0 if gather else 1)
        while leading and cost / n > EXCH_CHUNK_MS and rows % (2 * n) == 0 and rows // (2 * n) >= 16:
            n *= 2
        self.chunks[name] = n
        step = rows // n
        for k in range(n):
            piece = lax.slice_in_dim(arr, k * step, (k + 1) * step, axis=row_axis) if n > 1 else arr
            self.pending.append(((name, k), piece, gather, cost / n))

    def take(self, budget):
        jobs, spent = [], 0.0
        while self.pending and spent + 0.5 * self.pending[0][3] <= budget:
            job = self.pending.pop(0)
            jobs.append(job)
            spent += job[3]
        return jobs

    def deliver(self, jobs, results):
        for (key, _, _, _), r in zip(jobs, results):
            self.done[key] = r

    def get(self, name):
        mine = [j for j in self.pending if j[0][0] == name]
        if mine:
            self.pending = [j for j in self.pending if j[0][0] != name]
            self.count += 1
            self.deliver(mine, _exchange_alone([(j[1], j[2]) for j in mine], f'exchange_{self.count}_{name}'))
        parts = [self.done.pop((name, k)) for k in range(self.chunks[name])]
        return parts

    def flush(self):
        if self.pending:
            jobs, self.pending = self.pending, []
            self.count += 1
            self.deliver(jobs, _exchange_alone([(j[1], j[2]) for j in jobs], f'exchange_{self.count}_rest'))


def _ride(sch, budget, fn, **kw):
    jobs = sch.take(budget) if sch is not None else []
    outs, exch = fn(jobs=[(j[1], j[2]) for j in jobs], **kw)
    if jobs:
        sch.deliver(jobs, exch)
    return outs


@functools.partial(jax.custom_vjp, nondiff_argnums=(1, 2))
def _roll(x, shift, axis):
    return pltpu.roll(x, shift, axis)


def _roll_fwd(x, shift, axis):
    return pltpu.roll(x, shift, axis), None


def _roll_bwd(shift, axis, _, g):
    n = g.shape[axis]
    return (pltpu.roll(g, (n - shift) % n, axis),)


_roll.defvjp(_roll_fwd, _roll_bwd)


@jax.custom_vjp
def _bdot(x, w):
    return jnp.dot(x.astype(BF), w.astype(BF), preferred_element_type=F32)


def _bdot_fwd(x, w):
    return _bdot(x, w), (x, w)


def _bdot_bwd(res, g):
    x, w = res
    gb = g.astype(BF)
    dx = lax.dot_general(gb, w.astype(BF), (((1,), (1,)), ((), ())), preferred_element_type=F32)
    dw = lax.dot_general(x.astype(BF), gb, (((0,), (0,)), ((), ())), preferred_element_type=F32)
    return dx, dw


_bdot.defvjp(_bdot_fwd, _bdot_bwd)


def _silu(g):
    return g * jax.nn.sigmoid(g)


def _softplus(x):
    return jnp.maximum(x, 0.0) + jnp.log1p(jnp.exp(-jnp.abs(x)))


def _mm_call(a, b, *, mode, out_dtype, name, add=None, jobs=()):
    if mode == 'nn':
        (mo, kc), (_, no) = a.shape, b.shape
    elif mode == 'nt':
        (mo, kc), (no, _) = a.shape, b.shape
    else:
        (kc, mo), (_, no) = a.shape, b.shape
    tm = _pick(mo, MM_M_PREFS)
    tn = no if no <= 2048 and no % 512 != 0 else _pick(no, MM_N_PREFS)
    out_bytes = tm * tn * (2 * jnp.dtype(out_dtype).itemsize + 4 + (8 if add is not None else 0))
    tk = None
    for cand in MM_K_PREFS:
        if kc % cand == 0:
            tk = cand
            if out_bytes + 2 * cand * (tm * a.dtype.itemsize + tn * b.dtype.itemsize) <= MM_VMEM_BUDGET:
                break
    tk = tk or kc
    nk = kc // tk
    if mode == 'nn':
        a_spec = pl.BlockSpec((tm, tk), lambda i, j, k: (i, k))
        b_spec = pl.BlockSpec((tk, tn), lambda i, j, k: (k, j))
        dims = (((1,), (0,)), ((), ()))
    elif mode == 'nt':
        a_spec = pl.BlockSpec((tm, tk), lambda i, j, k: (i, k))
        b_spec = pl.BlockSpec((tn, tk), lambda i, j, k: (j, k))
        dims = (((1,), (1,)), ((), ()))
    else:
        a_spec = pl.BlockSpec((tk, tm), lambda i, j, k: (k, i))
        b_spec = pl.BlockSpec((tk, tn), lambda i, j, k: (k, j))
        dims = (((0,), (0,)), ((), ()))
    o_spec = pl.BlockSpec((tm, tn), lambda i, j, k: (i, j))

    def body(ins, outs, scr):
        a_ref, b_ref = ins[0], ins[1]
        (o_ref,), (acc_ref,) = outs, scr
        k = pl.program_id(2)

        @pl.when(k == 0)
        def _():
            acc_ref[...] = jnp.zeros_like(acc_ref) if add is None else ins[2][...]

        acc_ref[...] += lax.dot_general(a_ref[...].astype(BF), b_ref[...].astype(BF), dims,
                                        preferred_element_type=F32)

        @pl.when(k == nk - 1)
        def _():
            o_ref[...] = acc_ref[...].astype(o_ref.dtype)

    outs, exch = _call(
        body, name=name, grid=(mo // tm, no // tn, nk),
        in_specs=[a_spec, b_spec] + ([o_spec] if add is not None else []),
        out_specs=[o_spec], out_shape=[jax.ShapeDtypeStruct((mo, no), out_dtype)],
        args=[a, b] + ([add] if add is not None else []),
        scratch=[pltpu.VMEM((tm, tn), F32)], sem=("parallel", "parallel", "arbitrary"), jobs=jobs)
    return outs, exch


def mm(sch, a, b, name, mode='nn', out_dtype=F32, add=None):
    if mode == 'nn':
        flops = 2.0 * a.shape[0] * a.shape[1] * b.shape[1]
    elif mode == 'nt':
        flops = 2.0 * a.shape[0] * a.shape[1] * b.shape[0]
    else:
        flops = 2.0 * a.shape[0] * a.shape[1] * b.shape[1]
    return _ride(sch, flops / MXU_FLOPS_PER_MS, _mm_call, a=a, b=b, mode=mode, out_dtype=out_dtype, name=name,
                 add=add)[0]


def _full_spec(p):
    nd = p.ndim
    return pl.BlockSpec(p.shape, lambda i: (0,) * nd)


def _load_rows(refs, n_rows, halo, step_is_first):
    cur, prev, pos = [], [], 0
    for r in range(n_rows):
        cur.append(refs[pos][...].astype(F32))
        pos += 1
        if r in halo:
            keep = jnp.where(step_is_first, 0.0, 1.0).astype(F32)
            prev.append(refs[pos][...].astype(F32) * keep)
            pos += 1
        else:
            prev.append(None)
    return cur, prev, pos


def _join(cur, prev):
    return [c if p is None else jnp.concatenate([p, c], axis=0) for c, p in zip(cur, prev)]


def _rw_fwd(f, rows, params, outs, *, name, n_reduce=0, halo=(), tt=None, jobs=()):
    t_len = rows[0].shape[0]
    tt = tt or _pick(t_len, TT_PREFS)
    nt = t_len // tt
    n_rows, n_par, n_out = len(rows), len(params), len(outs)

    def body(ins, orefs, scr):
        i = pl.program_id(0)
        cur, prev, pos = _load_rows(ins, n_rows, halo, i == 0)
        pvals = [ins[pos + k][...] for k in range(n_par)]
        res = f(_join(cur, prev), pvals)
        n_f = len(res) - n_reduce
        for k, (_, _, src) in enumerate(outs):
            orefs[k][...] = res[src].astype(orefs[k].dtype)
        for k in range(n_reduce):
            ref, val = orefs[n_out + k], res[n_f + k]

            @pl.when(i == 0)
            def _():
                ref[...] = val

            @pl.when(i > 0)
            def _():
                ref[...] += val

    in_specs, args = [], []
    for r, x in enumerate(rows):
        c = x.shape[1]
        in_specs.append(pl.BlockSpec((tt, c), lambda i: (i, 0)))
        args.append(x)
        if r in halo:
            in_specs.append(pl.BlockSpec((tt, c), lambda i: (jnp.maximum(i - 1, 0), 0)))
            args.append(x)
    for p in params:
        in_specs.append(_full_spec(p))
        args.append(p)
    out_specs = [pl.BlockSpec((tt, c), lambda i: (i, 0)) for c, _, _ in outs]
    out_shape = [jax.ShapeDtypeStruct((t_len, c), dt) for c, dt, _ in outs]
    for _ in range(n_reduce):
        out_specs.append(pl.BlockSpec((1, 1), lambda i: (0, 0)))
        out_shape.append(jax.ShapeDtypeStruct((1, 1), F32))
    return _call(body, name=name, grid=(nt,), in_specs=in_specs, out_specs=out_specs, out_shape=out_shape,
                 args=args, sem=("arbitrary",), jobs=jobs)


def _rw_bwd(f, rows, params, cts, *, name, n_reduce=0, halo=(), nd_rows=(), nd_params=(), tt=None,
            row_dtypes=None, jobs=()):
    t_len = rows[0].shape[0]
    tt = tt or _pick(t_len, TT_PREFS)
    nt = t_len // tt
    n_rows, n_par, n_ct = len(rows), len(params), len(cts)
    d_rows = [r for r in range(n_rows) if r not in nd_rows]
    d_pars = [k for k in range(n_par) if k not in nd_params]
    h_rows = [r for r in d_rows if r in halo]

    def blk(j):
        return nt - 1 - j

    def body(ins, orefs, carry_refs):
        j = pl.program_id(0)
        cur, prev, pos = _load_rows(ins, n_rows, halo, blk(j) == 0)
        pvals = [ins[pos + k][...] for k in range(n_par)]
        pos += n_par
        ct_vals = [ins[pos + k][...].astype(F32) for k in range(n_ct)]

        def g(dcur, dprev, dpar):
            c, p, q = list(cur), list(prev), list(pvals)
            for r, v in zip(d_rows, dcur):
                c[r] = v
            for r, v in zip(h_rows, dprev):
                p[r] = v
            for k, v in zip(d_pars, dpar):
                q[k] = v
            return tuple(f(_join(c, p), q))

        _, vjp = jax.vjp(g, [cur[r] for r in d_rows], [prev[r] for r in h_rows], [pvals[k] for k in d_pars])
        g_cur, g_prev, g_par = vjp(tuple(ct_vals))

        for n, r in enumerate(d_rows):
            if r in halo:
                cref = carry_refs[h_rows.index(r)]

                @pl.when(j == 0)
                def _():
                    cref[...] = jnp.zeros_like(cref)

                orefs[n][...] = (g_cur[n] + cref[...]).astype(orefs[n].dtype)
                cref[...] = g_prev[h_rows.index(r)]
            else:
                orefs[n][...] = g_cur[n].astype(orefs[n].dtype)
        for n in range(len(d_pars)):
            ref, val = orefs[len(d_rows) + n], g_par[n]

            @pl.when(j == 0)
            def _():
                ref[...] = val

            @pl.when(j > 0)
            def _():
                ref[...] += val

    in_specs, args = [], []
    for r, x in enumerate(rows):
        c = x.shape[1]
        in_specs.append(pl.BlockSpec((tt, c), lambda j: (blk(j), 0)))
        args.append(x)
        if r in halo:
            in_specs.append(pl.BlockSpec((tt, c), lambda j: (jnp.maximum(blk(j) - 1, 0), 0)))
            args.append(x)
    for p in params:
        in_specs.append(_full_spec(p))
        args.append(p)
    for ct in cts:
        if ct.shape == (1, 1):
            in_specs.append(pl.BlockSpec((1, 1), lambda j: (0, 0)))
        else:
            in_specs.append(pl.BlockSpec((tt, ct.shape[1]), lambda j: (blk(j), 0)))
        args.append(ct)
    out_specs, out_shape, scratch = [], [], []
    for n, r in enumerate(d_rows):
        c = rows[r].shape[1]
        out_specs.append(pl.BlockSpec((tt, c), lambda j: (blk(j), 0)))
        out_shape.append(jax.ShapeDtypeStruct((t_len, c), row_dtypes[n] if row_dtypes else F32))
        if r in halo:
            scratch.append(pltpu.VMEM((tt, c), F32))
    for k in d_pars:
        out_specs.append(_full_spec(params[k]))
        out_shape.append(jax.ShapeDtypeStruct(params[k].shape, F32))
    return _call(body, name=name + '_bwd', grid=(nt,), in_specs=in_specs, out_specs=out_specs,
                 out_shape=out_shape, args=args, scratch=scratch, sem=("arbitrary",), jobs=jobs)


def rw_fwd(sch, budget, f, rows, params, outs, **kw):
    return _ride(sch, budget, functools.partial(_rw_fwd, f, list(rows), list(params), outs), **kw)


def rw_bwd(sch, budget, f, rows, params, cts, **kw):
    return _ride(sch, budget, functools.partial(_rw_bwd, f, list(rows), list(params), list(cts)), **kw)


def _scan_call(a, b, mul, *, reverse, name, jobs=()):
    t_len, c_len = a.shape
    tt = _pick(t_len, TT_PREFS)
    tc = _pick(c_len, (512, 256, 128))
    nt = t_len // tt

    def body(ins, orefs, scr):
        (carry,) = scr
        t = pl.program_id(1)
        av, bv = ins[0][...], ins[1][...]
        row = lax.broadcasted_iota(jnp.int32, av.shape, 0)
        s = 1
        while s < tt:
            if reverse:
                ok = row < tt - s
                a_sh = jnp.where(ok, pltpu.roll(av, tt - s, 0), 1.0)
                b_sh = jnp.where(ok, pltpu.roll(bv, tt - s, 0), 0.0)
            else:
                ok = row >= s
                a_sh = jnp.where(ok, pltpu.roll(av, s, 0), 1.0)
                b_sh = jnp.where(ok, pltpu.roll(bv, s, 0), 0.0)
            bv = av * b_sh + bv
            av = av * a_sh
            s *= 2

        @pl.when(t == 0)
        def _():
            carry[...] = jnp.zeros_like(carry)

        hs = bv + av * carry[...]
        orefs[0][...] = hs
        edge = 0 if reverse else tt - 1
        carry[...] = orefs[0][edge:edge + 1, :]
        if mul is not None:
            orefs[1][...] = hs * ins[2][...]

    def idx(c, t):
        return ((nt - 1 - t) if reverse else t, c)

    spec = pl.BlockSpec((tt, tc), idx)
    n_in, n_out = (2, 1) if mul is None else (3, 2)
    return _call(body, name=name, grid=(c_len // tc, nt), in_specs=[spec] * n_in, out_specs=[spec] * n_out,
                 out_shape=[jax.ShapeDtypeStruct((t_len, c_len), F32)] * n_out,
                 args=[a, b] if mul is None else [a, b, mul],
                 scratch=[pltpu.VMEM((1, tc), F32)], sem=("parallel", "arbitrary"), jobs=jobs)


NT = (((1,), (1,)), ((), ()))


def _att_weights(s, rel, diag, lg, softmax, scale, t_axis):
    row = lax.broadcasted_iota(jnp.int32, s.shape, t_axis)
    col = lax.broadcasted_iota(jnp.int32, s.shape, 1 - t_axis)
    if softmax:
        s = s * scale
        return jnp.where(row >= col, s, -1e30) if diag else s
    diff = (rel + row - col).astype(F32)
    dec = jnp.exp(jnp.maximum(diff, 0.0) * lg)
    return jnp.where(diff >= 0.0, dec, 0.0) if diag else dec


def _att_fwd_call(q, k, v, lgt, *, heads, softmax, scale, name, jobs=()):
    t_len = q.shape[0]
    dqk, dv = q.shape[1] // heads, v.shape[1] // heads
    blk = _pick(t_len, ATT_PREFS)
    nb = t_len // blk

    lanes = 128
    n_fold = blk // lanes

    def body(ins, orefs, scr):
        q_ref, k_ref, v_ref, lg_ref = ins
        o_ref, lse_ref = orefs
        s_sc, m_sc, acc_sc = scr
        i = pl.program_id(1)
        lg = lg_ref[0:1, 0:1]
        qb = q_ref[...]
        acc_sc[...] = jnp.zeros_like(acc_sc)

        def block_rows(j):
            return pl.ds(pl.multiple_of(j * blk, blk), blk)

        if not softmax:
            def step(j, diag):
                s = lax.dot_general(qb, k_ref[block_rows(j), :], NT, preferred_element_type=F32)
                w = _att_weights(s, (i - j) * blk, diag, lg, softmax, scale, 0)
                acc_sc[...] += jnp.dot((s * w).astype(BF), v_ref[block_rows(j), :], preferred_element_type=F32)

            lax.fori_loop(0, i, lambda j, c: (step(j, False), c)[1], 0)
            step(i, True)
            o_ref[...] = acc_sc[...]
            lse_ref[...] = jnp.zeros_like(lse_ref)
            return

        m_sc[...] = jnp.full_like(m_sc, -1e30)

        def score(j, diag):
            s = lax.dot_general(qb, k_ref[block_rows(j), :], NT, preferred_element_type=F32)
            w = _att_weights(s, (i - j) * blk, diag, lg, softmax, scale, 0)
            s_sc[j] = w
            top = w[:, :lanes]
            for n in range(1, n_fold):
                top = jnp.maximum(top, w[:, n * lanes:(n + 1) * lanes])
            m_sc[...] = jnp.maximum(m_sc[...], top)

        lax.fori_loop(0, i, lambda j, c: (score(j, False), c)[1], 0)
        score(i, True)
        m = jnp.max(m_sc[...], axis=-1, keepdims=True)
        ones = jnp.ones((blk, lanes), BF)

        def accumulate(j, c):
            p = jnp.exp(s_sc[j] - m).astype(BF)
            v_aug = jnp.concatenate([v_ref[block_rows(j), :], ones], axis=1)
            acc_sc[...] += jnp.dot(p, v_aug, preferred_element_type=F32)
            return c

        lax.fori_loop(0, i + 1, accumulate, 0)
        acc = acc_sc[...]
        l = acc[:, dv:dv + 1]
        o_ref[...] = acc[:, :dv] / l
        lse_ref[...] = m + jnp.log(l)

    scratch = [pltpu.VMEM((nb, blk, blk), F32), pltpu.VMEM((blk, lanes), F32),
               pltpu.VMEM((blk, dv + lanes if softmax else dv), F32)]
    return _call(
        body, name=name, grid=(heads, nb),
        in_specs=[
            pl.BlockSpec((blk, dqk), lambda h, i: (i, h)),
            pl.BlockSpec((t_len, dqk), lambda h, i: (0, h)),
            pl.BlockSpec((t_len, dv), lambda h, i: (0, h)),
            pl.BlockSpec((None, 1, 128), lambda h, i: (h, 0, 0)),
        ],
        out_specs=[
            pl.BlockSpec((blk, dv), lambda h, i: (i, h)),
            pl.BlockSpec((None, blk, 1), lambda h, i: (h, i, 0)),
        ],
        out_shape=[jax.ShapeDtypeStruct((t_len, heads * dv), F32), jax.ShapeDtypeStruct((heads, t_len, 1), F32)],
        args=[q, k, v, lgt],
        scratch=scratch if softmax else [pltpu.VMEM((8, lanes), F32), scratch[1], scratch[2]],
        sem=("parallel", "arbitrary"), jobs=jobs)


def _att_dq_call(q, k, v, lgt, o, lse, do, *, heads, softmax, scale, name, jobs=()):
    t_len = q.shape[0]
    dqk, dv = q.shape[1] // heads, v.shape[1] // heads
    blk = _pick(t_len, ATT_PREFS)
    nb = t_len // blk

    def body(ins, orefs, scr):
        q_ref, k_ref, v_ref, lg_ref, o_ref, do_ref, lse_ref = ins
        dq_ref, delta_ref = orefs
        (acc,) = scr
        i = pl.program_id(1)
        lg = lg_ref[0:1, 0:1]
        qb = q_ref[...]
        do = do_ref[...]
        dob = do.astype(BF)
        delta = jnp.sum(do * o_ref[...], axis=-1, keepdims=True)
        lse = lse_ref[...]
        delta_ref[...] = delta
        acc[...] = jnp.zeros_like(acc)

        def step(j, diag):
            rows = pl.ds(pl.multiple_of(j * blk, blk), blk)
            kb = k_ref[rows, :]
            s = lax.dot_general(qb, kb, NT, preferred_element_type=F32)
            w = _att_weights(s, (i - j) * blk, diag, lg, softmax, scale, 0)
            dp = lax.dot_general(dob, v_ref[rows, :], NT, preferred_element_type=F32)
            ds = jnp.exp(w - lse) * (dp - delta) * scale if softmax else dp * w
            acc[...] += jnp.dot(ds.astype(BF), kb, preferred_element_type=F32)

        lax.fori_loop(0, i, lambda j, c: (step(j, False), c)[1], 0)
        step(i, True)
        dq_ref[...] = acc[...]

    return _call(
        body, name=name + '_dq', grid=(heads, nb),
        in_specs=[
            pl.BlockSpec((blk, dqk), lambda h, i: (i, h)),
            pl.BlockSpec((t_len, dqk), lambda h, i: (0, h)),
            pl.BlockSpec((t_len, dv), lambda h, i: (0, h)),
            pl.BlockSpec((None, 1, 128), lambda h, i: (h, 0, 0)),
            pl.BlockSpec((blk, dv), lambda h, i: (i, h)),
            pl.BlockSpec((blk, dv), lambda h, i: (i, h)),
            pl.BlockSpec((None, blk, 1), lambda h, i: (h, i, 0)),
        ],
        out_specs=[pl.BlockSpec((blk, dqk), lambda h, i: (i, h)),
                   pl.BlockSpec((None, blk, 1), lambda h, i: (h, i, 0))],
        out_shape=[jax.ShapeDtypeStruct(q.shape, F32), jax.ShapeDtypeStruct((heads, t_len, 1), F32)],
        args=[q, k, v, lgt, o, do, lse],
        scratch=[pltpu.VMEM((blk, dqk), F32)], sem=("parallel", "arbitrary"), jobs=jobs)


def _att_dkv_call(q, k, v, lgt, lse_row, delta_row, do, *, heads, softmax, scale, name, jobs=()):
    t_len = q.shape[0]
    dqk, dv = q.shape[1] // heads, v.shape[1] // heads
    blk = _pick(t_len, ATT_PREFS)
    nb = t_len // blk

    def body(ins, orefs, scr):
        q_ref, k_ref, v_ref, lg_ref, do_ref, lse_ref, delta_ref = ins
        dk_acc, dv_acc = scr
        j = pl.program_id(1)
        lg = lg_ref[0:1, 0:1]
        kb, vb = k_ref[...], v_ref[...]
        dk_acc[...] = jnp.zeros_like(dk_acc)
        dv_acc[...] = jnp.zeros_like(dv_acc)

        def step(i, diag):
            rows = pl.ds(pl.multiple_of(i * blk, blk), blk)
            qb = q_ref[rows, :]
            dob = do_ref[rows, :].astype(BF)
            s = lax.dot_general(kb, qb, NT, preferred_element_type=F32)
            w = _att_weights(s, (i - j) * blk, diag, lg, softmax, scale, 1)
            dp = lax.dot_general(vb, dob, NT, preferred_element_type=F32)
            if softmax:
                p = jnp.exp(w - lse_ref[:, rows])
                ds = p * (dp - delta_ref[:, rows]) * scale
            else:
                p, ds = s * w, dp * w
            dv_acc[...] += jnp.dot(p.astype(BF), dob, preferred_element_type=F32)
            dk_acc[...] += jnp.dot(ds.astype(BF), qb, preferred_element_type=F32)

        step(j, True)
        lax.fori_loop(j + 1, nb, lambda i, c: (step(i, False), c)[1], 0)
        orefs[0][...] = dk_acc[...]
        orefs[1][...] = dv_acc[...]

    return _call(
        body, name=name + '_dkv', grid=(heads, nb),
        in_specs=[
            pl.BlockSpec((t_len, dqk), lambda h, j: (0, h)),
            pl.BlockSpec((blk, dqk), lambda h, j: (j, h)),
            pl.BlockSpec((blk, dv), lambda h, j: (j, h)),
            pl.BlockSpec((None, 1, 128), lambda h, j: (h, 0, 0)),
            pl.BlockSpec((t_len, dv), lambda h, j: (0, h)),
            pl.BlockSpec((None, 1, t_len), lambda h, j: (h, 0, 0)),
            pl.BlockSpec((None, 1, t_len), lambda h, j: (h, 0, 0)),
        ],
        out_specs=[
            pl.BlockSpec((blk, dqk), lambda h, j: (j, h)),
            pl.BlockSpec((blk, dv), lambda h, j: (j, h)),
        ],
        out_shape=[jax.ShapeDtypeStruct(k.shape, F32), jax.ShapeDtypeStruct(v.shape, F32)],
        args=[q, k, v, lgt, do, lse_row, delta_row],
        scratch=[pltpu.VMEM((blk, dqk), F32), pltpu.VMEM((blk, dv), F32)],
        sem=("parallel", "arbitrary"), jobs=jobs)


def _adamw_call(contribs, w, m, v, *, name, jobs=()):
    r_len, c_len = w.shape
    n_chunk = len(contribs)
    r_chunk = r_len // n_chunk
    cap = max(min(ADAM_BLOCK_ELEMS, 4 * ADAM_BLOCK_ELEMS // n_chunk) // c_len, 1)
    tr = r_chunk
    for cand in (512, 256, 128, 64, 32, 16):
        if cand <= cap and r_chunk % cand == 0:
            tr = cand
            break
    per = r_chunk // tr

    def body(ins, orefs, scr):
        w_ref, m_ref, v_ref = ins[n_chunk:]
        g_ref, d_ref, mo_ref, vo_ref = orefs
        i = pl.program_id(0)

        def update(c_ref):
            g = c_ref[0].astype(F32)
            for n in range(1, N_DEV):
                g = g + c_ref[n].astype(F32)
            m_new = ADAM_B1 * m_ref[...] + (1.0 - ADAM_B1) * g
            v_new = ADAM_B2 * v_ref[...] + (1.0 - ADAM_B2) * jnp.square(g)
            m_hat = m_new / (1.0 - ADAM_B1 ** ADAM_STEP)
            v_hat = v_new / (1.0 - ADAM_B2 ** ADAM_STEP)
            g_ref[...] = g
            d_ref[...] = -ADAM_LR * (m_hat / (jnp.sqrt(v_hat) + ADAM_EPS) + ADAM_WD * w_ref[...])
            mo_ref[...] = m_new
            vo_ref[...] = v_new

        if n_chunk == 1:
            update(ins[0])
        else:
            for n in range(n_chunk):
                @pl.when(i // per == n)
                def _():
                    update(ins[n])

    spec = pl.BlockSpec((tr, c_len), lambda i: (i, 0))
    c_specs = [pl.BlockSpec((N_DEV, tr, c_len), functools.partial(
        lambda i, n: (0, jnp.clip(i - n * per, 0, per - 1), 0), n=n)) for n in range(n_chunk)]
    return _call(body, name=name, grid=(r_len // tr,), in_specs=c_specs + [spec, spec, spec],
                 out_specs=[spec] * 4, out_shape=[jax.ShapeDtypeStruct((r_len, c_len), F32)] * 4,
                 args=list(contribs) + [w, m, v], sem=("arbitrary",), jobs=jobs)


def _rope_lanes(x, cc, s_lo, s_hi):
    return x * cc + _roll(x, 32, 1) * s_lo + _roll(x, 96, 1) * s_hi


def _f_lru(tt, branch):
    lb = branch // LRU_HEADS

    def f(rows, params):
        (xcat,) = rows
        cw, cb, wa, ba, wx, bx, lam = params
        conv = cb
        for j in range(CONV_W):
            sh = CONV_W - 1 - j
            xs = xcat if sh == 0 else _roll(xcat, sh, 0)
            conv = conv + cw[j:j + 1, :] * xs[tt:, :]
        rs, gs = [], []
        for h in range(LRU_HEADS):
            ub = conv[:, h * lb:(h + 1) * lb]
            rs.append(_bdot(ub, wa[h]))
            gs.append(_bdot(ub, wx[h]))
        r = jax.nn.sigmoid(jnp.concatenate(rs, axis=-1) + ba)
        gate = jax.nn.sigmoid(jnp.concatenate(gs, axis=-1) + bx)
        log_a = LRU_C * r * (-_softplus(-lam))
        a = jnp.exp(log_a)
        one_minus_a2 = -jnp.tanh(log_a) * (jnp.exp(2.0 * log_a) + 1.0)
        return a, (conv * gate) * jnp.sqrt(one_minus_a2)

    return f


def _f_gate(rows, params):
    hs, g = rows
    return (hs * _silu(g),)


def _f_ln(rows, params):
    h, br = rows
    g, b = params
    pre = ALPHA * h + br
    mu = jnp.mean(pre, axis=-1, keepdims=True)
    var = jnp.mean(jnp.square(pre - mu), axis=-1, keepdims=True)
    return ((pre - mu) * lax.rsqrt(var + LN_EPS) * g + b,)


def _f_pool(tt, branch):
    grp = branch // len(POOL_WINDOWS)

    def f(rows, params):
        xcat, tidx = rows
        sums, acc, w = [], xcat, 1
        while w < POOL_WINDOWS[-1]:
            acc = acc + _roll(acc, w, 0)
            w *= 2
            sums.append(acc[tt:, :])
        u = xcat[tt:, :]
        outs = []
        for gi, w in enumerate(POOL_WINDOWS):
            sl = slice(gi * grp, (gi + 1) * grp)
            outs.append(sums[gi][:, sl] / jnp.minimum(tidx + 1.0, float(w)) - u[:, sl])
        return tuple(outs)

    return f


def _f_gate_pool(rows, params):
    m0, m1, m2, m3, g = rows
    (scale,) = params
    return (jnp.concatenate([m0, m1, m2, m3], axis=-1) * scale * _silu(g),)


def _rms(x, g):
    return x * lax.rsqrt(jnp.mean(jnp.square(x), axis=-1, keepdims=True) + RMS_EPS) * g


def _f_mla_pre(rows, params):
    c, cc, s_lo, s_hi = rows
    qn, kvn = params
    cq = c[:, :Q_LORA]
    ckv = c[:, Q_LORA:Q_LORA + KV_LORA]
    kr = c[:, Q_LORA + KV_LORA:]
    return _rms(cq, qn), _rms(ckv, kvn), _rope_lanes(kr, cc, s_lo, s_hi)


def _f_rope_q(rows, params):
    qc, cc, s_lo, s_hi = rows
    out = []
    for h in range(MLA_HEADS):
        out.append(qc[:, h * MLA_QK:h * MLA_QK + MLA_NOPE])
        out.append(_rope_lanes(qc[:, h * MLA_QK + MLA_NOPE:(h + 1) * MLA_QK], cc, s_lo, s_hi))
    return (jnp.concatenate(out, axis=-1),)


def _f_kcat(dv):
    per = MLA_NOPE + dv

    def f(rows, params):
        kv, krr = rows
        ks, vs = [], []
        for h in range(MLA_HEADS):
            ks.append(kv[:, h * per:h * per + MLA_NOPE])
            ks.append(krr)
            vs.append(kv[:, h * per + MLA_NOPE:(h + 1) * per])
        return jnp.concatenate(ks, axis=-1), jnp.concatenate(vs, axis=-1)

    return f


def _f_rope_ret(dk):
    half = dk // 2

    def f(rows, params):
        q, k, cos, sin = rows
        qs, ks = [], []
        for h in range(RET_HEADS):
            for src, dst, mult in ((q, qs, 1.0), (k, ks, dk ** -0.5)):
                x1 = src[:, h * dk:h * dk + half]
                x2 = src[:, h * dk + half:(h + 1) * dk]
                dst.append((x1 * cos - x2 * sin) * mult)
                dst.append((x2 * cos + x1 * sin) * mult)
        return jnp.concatenate(qs, axis=-1), jnp.concatenate(ks, axis=-1)

    return f


def _f_gate_gn(dv):
    def f(rows, params):
        o, g = rows
        out = []
        for h in range(RET_HEADS):
            oh = o[:, h * dv:(h + 1) * dv]
            mu = jnp.mean(oh, axis=-1, keepdims=True)
            var = jnp.mean(jnp.square(oh - mu), axis=-1, keepdims=True)
            out.append((oh - mu) * lax.rsqrt(var + LN_EPS))
        return (jnp.concatenate(out, axis=-1) * _silu(g),)

    return f


def _f_loss(rows, params):
    h, tgt, mask = rows
    per_row = jnp.mean(jnp.square(h - tgt), axis=-1, keepdims=True) * mask
    return (0.5 * jnp.sum(per_row, axis=0, keepdims=True),)


def _cat(parts, axis=1):
    return parts[0] if len(parts) == 1 else jnp.concatenate(parts, axis=axis)


def _cols(g):
    return jnp.transpose(g, (1, 0, 2)).reshape(g.shape[1], -1)


def _uncols(w):
    k, n = w.shape
    return jnp.transpose(w.reshape(k, N_DEV, n // N_DEV), (1, 0, 2))


def _heads(g):
    return jnp.transpose(g, (1, 0, 2, 3)).reshape(g.shape[1], -1, g.shape[3])


def _unheads(w):
    h, r, c = w.shape
    return jnp.transpose(w.reshape(h, N_DEV, r // N_DEV, c), (1, 0, 2, 3))


def _rope_tables(t_pad, d):
    inv = ROPE_BASE ** (-jnp.arange(0, d, 2, dtype=F32) / d)
    ang = jnp.arange(t_pad, dtype=F32)[:, None] * inv[None, :]
    return jnp.cos(ang), jnp.sin(ang)


def _row2(v):
    return v.reshape(1, -1)


def _train_local(sch, x2d, tgt_pad, S, *, t_pad):
    seq, d_model = x2d.shape
    branch = d_model
    t_real = N_META + seq
    tt = _pick(t_pad, TT_PREFS)
    n_win = len(POOL_WINDOWS)
    grp = branch // n_win
    lb = branch // LRU_HEADS
    dv2 = branch // MLA_HEADS
    dk3 = branch // RET_HEADS
    gS = {}
    RW, RWB = 0.06, 0.1

    def ln_fwd(h, br, layer):
        h1, hb1 = rw_fwd(sch, RW, _f_ln, [h, br], [_row2(S[f'l{layer}_ln_g']), _row2(S[f'l{layer}_ln_b'])],
                         [(d_model, F32, 0), (d_model, BF, 0)], name=f'l{layer}_ln')
        return h1, hb1

    def ln_bwd(h, br, dh1, layer):
        dh, dbr, dg, db = rw_bwd(sch, RWB, _f_ln, [h, br],
                                 [_row2(S[f'l{layer}_ln_g']), _row2(S[f'l{layer}_ln_b'])], [dh1],
                                 name=f'l{layer}_ln', row_dtypes=[F32, BF])
        gS[f'l{layer}_ln_g'], gS[f'l{layer}_ln_b'] = dg.reshape(-1), db.reshape(-1)
        return dh, dbr

    tidx = jnp.arange(t_pad, dtype=F32)[:, None]
    rowmask = ((tidx >= N_META) & (tidx < t_real)).astype(F32)

    meta = _cols(_cat(sch.get('meta_tokens')))
    h0 = jnp.concatenate([meta, x2d, jnp.zeros((t_pad - t_real, d_model), F32)], axis=0)
    hb0 = h0.astype(BF)

    w0_in = _cols(_cat(sch.get('l0_w_in')))
    w0_u, w0_g = w0_in[:, :branch], w0_in[:, branch:]
    u0 = mm(sch, hb0, w0_u, 'l0_in_u')
    g0 = mm(sch, hb0, w0_g, 'l0_in_g')
    conv_w = jnp.transpose(_cat(sch.get('l0_conv_w')), (1, 2, 0, 3)).reshape(CONV_W, branch)
    w_a = _heads(_cat(sch.get('l0_w_a'), axis=2))
    w_x = _heads(_cat(sch.get('l0_w_x'), axis=2))
    p0 = [conv_w, _row2(S['l0_conv_b']), w_a, _row2(S['l0_b_a']), w_x, _row2(S['l0_b_x']), _row2(S['l0_lam'])]
    f_lru = _f_lru(tt, branch)
    a0, xin0 = rw_fwd(sch, 0.13, f_lru, [u0], p0, [(branch, F32, 0), (branch, F32, 1)], name='l0_lru',
                      halo=(0,), tt=tt)
    hs0 = _ride(sch, 0.22, _scan_call, a=a0, b=xin0, mul=None, reverse=False, name='l0_scan')[0]
    (z0,) = rw_fwd(sch, RW, _f_gate, [hs0, g0], [], [(branch, BF, 0)], name='l0_gate')
    w0_out = _cat(sch.get('l0_w_out')).reshape(branch, d_model)
    br0 = mm(sch, z0, w0_out, 'l0_out')
    h1, hb1 = ln_fwd(h0, br0, 0)

    w1_in = _cols(_cat(sch.get('l1_w_in')))
    w1_u, w1_g = w1_in[:, :branch], w1_in[:, branch:]
    u1 = mm(sch, hb1, w1_u, 'l1_in_u')
    g1 = mm(sch, hb1, w1_g, 'l1_in_g')
    f_pool = _f_pool(tt, branch)
    ps1 = rw_fwd(sch, RW, f_pool, [u1, tidx], [], [(grp, BF, gi) for gi in range(n_win)], name='l1_pool',
                 halo=(0,), tt=tt)
    w1_grp = _heads(_cat(sch.get('l1_w_grp'), axis=2))
    mixed1 = [mm(sch, ps1[gi], w1_grp[gi], f'l1_grp{gi}') for gi in range(n_win)]
    p1 = [_row2(S['l1_scale'])]
    (z1,) = rw_fwd(sch, RW, _f_gate_pool, mixed1 + [g1], p1, [(branch, BF, 0)], name='l1_gate')
    w1_out = _cat(sch.get('l1_w_out')).reshape(branch, d_model)
    br1 = mm(sch, z1, w1_out, 'l1_out')
    h2, hb2 = ln_fwd(h1, br1, 1)

    w2_in = _cols(_cat(sch.get('l2_w_in')))
    w2_g = w2_in[:, :branch]
    w2_lat = jnp.pad(w2_in[:, branch:], ((0, 0), (0, 128 - MLA_ROPE)))
    g2 = mm(sch, hb2, w2_g, 'l2_in_g')
    c2 = mm(sch, hb2, w2_lat, 'l2_in_c')
    cos, sin = _rope_tables(t_pad, MLA_ROPE)
    zz = jnp.zeros_like(cos)
    tabs = [jnp.concatenate([cos, cos, zz, zz], axis=-1), jnp.concatenate([zz, sin, zz, zz], axis=-1),
            jnp.concatenate([-sin, zz, zz, zz], axis=-1)]
    p2 = [_row2(S['l2_q_norm']), _row2(S['l2_kv_norm'])]
    cqn2, ckvn2, krr2 = rw_fwd(sch, 0.04, _f_mla_pre, [c2] + tabs, p2,
                               [(Q_LORA, BF, 0), (KV_LORA, BF, 1), (128, F32, 2)], name='l2_pre')
    w2_uq = _cols(_cat(sch.get('l2_w_uq'))).reshape(Q_LORA, MLA_HEADS, MLA_NOPE + MLA_ROPE)
    w2_uq = jnp.pad(w2_uq, ((0, 0), (0, 0), (0, MLA_QK - MLA_NOPE - MLA_ROPE))).reshape(Q_LORA, MLA_HEADS * MLA_QK)
    w2_ukv = _cols(_cat(sch.get('l2_w_ukv')))
    qc2 = mm(sch, cqn2, w2_uq, 'l2_uq')
    kv2 = mm(sch, ckvn2, w2_ukv, 'l2_ukv')
    (qcr2,) = rw_fwd(sch, 0.09, _f_rope_q, [qc2] + tabs, [], [(MLA_HEADS * MLA_QK, BF, 0)], name='l2_rope_q')
    f_kcat = _f_kcat(dv2)
    kcat2, v2 = rw_fwd(sch, 0.1, f_kcat, [kv2, krr2], [], [(MLA_HEADS * MLA_QK, BF, 0), (branch, BF, 1)],
                       name='l2_kcat')
    no_decay = jnp.zeros((MLA_HEADS, 1, 128), F32)
    att2 = dict(heads=MLA_HEADS, softmax=True, scale=(MLA_NOPE + MLA_ROPE) ** -0.5, name='l2_att')
    o2, lse2 = _ride(sch, 1.3, _att_fwd_call, q=qcr2, k=kcat2, v=v2, lgt=no_decay, **att2)
    (z2,) = rw_fwd(sch, RW, _f_gate, [o2, g2], [], [(branch, BF, 0)], name='l2_gate')
    w2_out = _cat(sch.get('l2_w_out')).reshape(branch, d_model)
    br2 = mm(sch, z2, w2_out, 'l2_out')
    h3, hb3 = ln_fwd(h2, br2, 2)

    w3_in = _cols(_cat(sch.get('l3_w_in')))
    w3 = [w3_in[:, n * branch:(n + 1) * branch] for n in range(4)]
    q3 = mm(sch, hb3, w3[0], 'l3_in_q')
    k3 = mm(sch, hb3, w3[1], 'l3_in_k')
    v3 = mm(sch, hb3, w3[2], 'l3_in_v', out_dtype=BF)
    g3 = mm(sch, hb3, w3[3], 'l3_in_g')
    cs3 = list(_rope_tables(t_pad, dk3))
    f_rope3 = _f_rope_ret(dk3)
    qr3, kr3 = rw_fwd(sch, 0.09, f_rope3, [q3, k3] + cs3, [], [(branch, BF, 0), (branch, BF, 1)], name='l3_rope')
    log_g = jnp.log(1.0 - 2.0 ** (-5.0 - jnp.arange(RET_HEADS, dtype=F32)))
    lgt = jnp.broadcast_to(log_g[:, None, None], (RET_HEADS, 1, 128))
    att3 = dict(heads=RET_HEADS, softmax=False, scale=1.0, name='l3_ret')
    o3, lse3 = _ride(sch, 0.6, _att_fwd_call, q=qr3, k=kr3, v=v3, lgt=lgt, **att3)
    f_gn = _f_gate_gn(dk3)
    (z3,) = rw_fwd(sch, 0.08, f_gn, [o3, g3], [], [(branch, BF, 0)], name='l3_gate')
    w3_out = _cat(sch.get('l3_w_out')).reshape(branch, d_model)
    br3 = mm(sch, z3, w3_out, 'l3_out')
    (h4,) = rw_fwd(sch, RW, _f_ln, [h3, br3], [_row2(S['l3_ln_g']), _row2(S['l3_ln_b'])], [(d_model, F32, 0)],
                   name='l3_ln')

    (loss,) = rw_fwd(sch, 0.05, _f_loss, [h4, tgt_pad, rowmask], [], [], name='loss', n_reduce=1)

    (dh4,) = rw_bwd(sch, 0.07, _f_loss, [h4, tgt_pad, rowmask], [], [jnp.ones((1, 1), F32)], name='loss',
                    n_reduce=1, nd_rows=(1, 2))

    dh3, dbr3 = ln_bwd(h3, br3, dh4, 3)
    dz3 = mm(sch, dbr3, w3_out, 'l3_out_dx', mode='nt')
    sch.push('l3_w_out', mm(sch, z3.T, dbr3, 'l3_out_dw', mode='nn', out_dtype=BF).reshape(N_DEV, -1, d_model),
             False, 1)
    do3, dg3 = rw_bwd(sch, 0.13, f_gn, [o3, g3], [], [dz3], name='l3_gate', row_dtypes=[F32, BF])
    dqr3, delta3 = _ride(sch, 0.7, _att_dq_call, q=qr3, k=kr3, v=v3, lgt=lgt, o=o3, lse=lse3, do=do3, **att3)
    dkr3, dv3 = _ride(sch, 0.8, _att_dkv_call, q=qr3, k=kr3, v=v3, lgt=lgt, lse_row=lse3.reshape(RET_HEADS, 1, -1),
                      delta_row=delta3.reshape(RET_HEADS, 1, -1), do=do3, **att3)
    dq3, dk3_ = rw_bwd(sch, 0.13, f_rope3, [q3, k3] + cs3, [], [dqr3, dkr3], name='l3_rope', nd_rows=(2, 3),
                       row_dtypes=[BF, BF])
    d3 = [dq3, dk3_, dv3, dg3]
    for n in range(4):
        dh3 = mm(sch, d3[n], w3[n], f'l3_in_dx{n}', mode='nt', add=dh3)
    hb3_t = hb3.T
    dw3 = [mm(sch, hb3_t, d3[n], f'l3_in_dw{n}', mode='nn', out_dtype=BF) for n in range(4)]
    sch.push('l3_w_in', _uncols(jnp.concatenate(dw3, axis=1)), False, 1)

    dh2, dbr2 = ln_bwd(h2, br2, dh3, 2)
    dz2 = mm(sch, dbr2, w2_out, 'l2_out_dx', mode='nt')
    sch.push('l2_w_out', mm(sch, z2.T, dbr2, 'l2_out_dw', mode='nn', out_dtype=BF).reshape(N_DEV, -1, d_model),
             False, 1)
    do2, dg2 = rw_bwd(sch, 0.1, _f_gate, [o2, g2], [], [dz2], name='l2_gate', row_dtypes=[F32, BF])
    dqcr2, delta2 = _ride(sch, 1.3, _att_dq_call, q=qcr2, k=kcat2, v=v2, lgt=no_decay, o=o2, lse=lse2, do=do2,
                          **att2)
    dkcat2, dv2_ = _ride(sch, 1.5, _att_dkv_call, q=qcr2, k=kcat2, v=v2, lgt=no_decay,
                         lse_row=lse2.reshape(MLA_HEADS, 1, -1), delta_row=delta2.reshape(MLA_HEADS, 1, -1),
                         do=do2, **att2)
    dkv2, dkrr2 = rw_bwd(sch, 0.14, f_kcat, [kv2, krr2], [], [dkcat2, dv2_], name='l2_kcat', row_dtypes=[BF, F32])
    (dqc2,) = rw_bwd(sch, 0.13, _f_rope_q, [qc2] + tabs, [], [dqcr2], name='l2_rope_q', nd_rows=(1, 2, 3),
                     row_dtypes=[BF])
    dckvn2 = mm(sch, dkv2, w2_ukv, 'l2_ukv_dx', mode='nt')
    dcqn2 = mm(sch, dqc2, w2_uq, 'l2_uq_dx', mode='nt')
    sch.push('l2_w_ukv', _uncols(mm(sch, ckvn2, dkv2, 'l2_ukv_dw', mode='tn', out_dtype=BF)), False, 1)
    dw_uq = mm(sch, cqn2, dqc2, 'l2_uq_dw', mode='tn', out_dtype=BF)
    dw_uq = dw_uq.reshape(Q_LORA, MLA_HEADS, MLA_QK)[:, :, :MLA_NOPE + MLA_ROPE].reshape(Q_LORA, -1)
    sch.push('l2_w_uq', _uncols(dw_uq), False, 1)
    dc2, dqn, dkvn = rw_bwd(sch, 0.05, _f_mla_pre, [c2] + tabs, p2, [dcqn2, dckvn2, dkrr2], name='l2_pre',
                            nd_rows=(1, 2, 3), row_dtypes=[BF])
    gS['l2_q_norm'], gS['l2_kv_norm'] = dqn.reshape(-1), dkvn.reshape(-1)
    dh2 = mm(sch, dg2, w2_g, 'l2_in_g_dx', mode='nt', add=dh2)
    dh2 = mm(sch, dc2, w2_lat, 'l2_in_c_dx', mode='nt', add=dh2)
    hb2_t = hb2.T
    dw2_g = mm(sch, hb2_t, dg2, 'l2_in_g_dw', mode='nn', out_dtype=BF)
    dw2_lat = mm(sch, hb2_t, dc2, 'l2_in_c_dw', mode='nn', out_dtype=BF)
    n_lat = Q_LORA + KV_LORA + MLA_ROPE
    sch.push('l2_w_in', _uncols(jnp.concatenate([dw2_g, dw2_lat[:, :n_lat]], axis=1)), False, 1)

    dh1, dbr1 = ln_bwd(h1, br1, dh2, 1)
    dz1 = mm(sch, dbr1, w1_out, 'l1_out_dx', mode='nt')
    sch.push('l1_w_out', mm(sch, z1.T, dbr1, 'l1_out_dw', mode='nn', out_dtype=BF).reshape(N_DEV, -1, d_model),
             False, 1)
    res = rw_bwd(sch, 0.11, _f_gate_pool, mixed1 + [g1], p1, [dz1], name='l1_gate', row_dtypes=[BF] * (n_win + 1))
    dmixed1, dg1, dscale = res[:n_win], res[n_win], res[n_win + 1]
    gS['l1_scale'] = dscale.reshape(-1)
    dps1 = [mm(sch, dmixed1[gi], w1_grp[gi], f'l1_grp{gi}_dx', mode='nt') for gi in range(n_win)]
    dw_grp = jnp.stack([mm(sch, ps1[gi], dmixed1[gi], f'l1_grp{gi}_dw', mode='tn', out_dtype=BF)
                        for gi in range(n_win)])
    sch.push('l1_w_grp', _unheads(dw_grp), False, 2)
    (du1,) = rw_bwd(sch, 0.09, f_pool, [u1, tidx], [], dps1, name='l1_pool', halo=(0,), nd_rows=(1,), tt=tt,
                    row_dtypes=[BF])
    dh1 = mm(sch, du1, w1_u, 'l1_in_u_dx', mode='nt', add=dh1)
    dh1 = mm(sch, dg1, w1_g, 'l1_in_g_dx', mode='nt', add=dh1)
    hb1_t = hb1.T
    dw1 = [mm(sch, hb1_t, du1, 'l1_in_u_dw', mode='nn', out_dtype=BF),
           mm(sch, hb1_t, dg1, 'l1_in_g_dw', mode='nn', out_dtype=BF)]
    sch.push('l1_w_in', _uncols(jnp.concatenate(dw1, axis=1)), False, 1)

    dh0, dbr0 = ln_bwd(h0, br0, dh1, 0)
    dz0 = mm(sch, dbr0, w0_out, 'l0_out_dx', mode='nt')
    sch.push('l0_w_out', mm(sch, z0.T, dbr0, 'l0_out_dw', mode='nn', out_dtype=BF).reshape(N_DEV, -1, d_model),
             False, 1)
    dhs0, dg0 = rw_bwd(sch, 0.1, _f_gate, [hs0, g0], [], [dz0], name='l0_gate', row_dtypes=[F32, BF])
    a_next = jnp.concatenate([a0[1:], jnp.ones_like(a0[:1])], axis=0)
    hs_prev = jnp.concatenate([jnp.zeros_like(hs0[:1]), hs0[:-1]], axis=0)
    dxin0, da0 = _ride(sch, 0.25, _scan_call, a=a_next, b=dhs0, mul=hs_prev, reverse=True, name='l0_scan_bwd')
    res = rw_bwd(sch, 0.3, f_lru, [u0], p0, [da0, dxin0], name='l0_lru', halo=(0,), tt=tt, row_dtypes=[BF])
    du0 = res[0]
    gS['l0_conv_b'], gS['l0_b_a'], gS['l0_b_x'], gS['l0_lam'] = [res[k].reshape(-1) for k in (2, 4, 6, 7)]
    sch.push('l0_w_a', _unheads(res[3]), False, 2)
    sch.push('l0_w_x', _unheads(res[5]), False, 2)
    sch.push('l0_conv_w', jnp.transpose(res[1].reshape(CONV_W, 1, N_DEV, -1), (2, 0, 1, 3)), False, 3)
    hb0_t = hb0.T
    dw0 = [mm(sch, hb0_t, du0, 'l0_in_u_dw', mode='nn', out_dtype=BF),
           mm(sch, hb0_t, dg0, 'l0_in_g_dw', mode='nn', out_dtype=BF)]
    sch.push('l0_w_in', _uncols(jnp.concatenate(dw0, axis=1)), False, 1)
    dh0 = mm(sch, du0, w0_u, 'l0_in_u_dx', mode='nt', add=dh0)
    dh0 = mm(sch, dg0, w0_g, 'l0_in_g_dx', mode='nt', add=dh0)
    sch.push('meta_tokens', _uncols(dh0[:N_META]), False, 1)

    return loss[0, 0], dh0[N_META:t_real], gS


def _as2d(a):
    return a.reshape(-1, a.shape[-1])


def kernel(x, meta_tokens, l0_w_in, l0_conv_w, l0_conv_b, l0_w_a, l0_b_a, l0_w_x, l0_b_x, l0_lam, l0_w_out, l0_ln_g, l0_ln_b, l1_w_in, l1_w_grp, l1_scale, l1_w_out, l1_ln_g, l1_ln_b, l2_w_in, l2_q_norm, l2_w_uq, l2_kv_norm, l2_w_ukv, l2_w_out, l2_ln_g, l2_ln_b, l3_w_in, l3_w_out, l3_ln_g, l3_ln_b, loss_target, m_meta_tokens, m_l0_w_in, m_l0_conv_w, m_l0_conv_b, m_l0_w_a, m_l0_b_a, m_l0_w_x, m_l0_b_x, m_l0_lam, m_l0_w_out, m_l0_ln_g, m_l0_ln_b, m_l1_w_in, m_l1_w_grp, m_l1_scale, m_l1_w_out, m_l1_ln_g, m_l1_ln_b, m_l2_w_in, m_l2_q_norm, m_l2_w_uq, m_l2_kv_norm, m_l2_w_ukv, m_l2_w_out, m_l2_ln_g, m_l2_ln_b, m_l3_w_in, m_l3_w_out, m_l3_ln_g, m_l3_ln_b, v_meta_tokens, v_l0_w_in, v_l0_conv_w, v_l0_conv_b, v_l0_w_a, v_l0_b_a, v_l0_w_x, v_l0_b_x, v_l0_lam, v_l0_w_out, v_l0_ln_g, v_l0_ln_b, v_l1_w_in, v_l1_w_grp, v_l1_scale, v_l1_w_out, v_l1_ln_g, v_l1_ln_b, v_l2_w_in, v_l2_q_norm, v_l2_w_uq, v_l2_kv_norm, v_l2_w_ukv, v_l2_w_out, v_l2_ln_g, v_l2_ln_b, v_l3_w_in, v_l3_w_out, v_l3_ln_g, v_l3_ln_b):
    args = (meta_tokens, l0_w_in, l0_conv_w, l0_conv_b, l0_w_a, l0_b_a, l0_w_x, l0_b_x, l0_lam, l0_w_out, l0_ln_g, l0_ln_b, l1_w_in, l1_w_grp, l1_scale, l1_w_out, l1_ln_g, l1_ln_b, l2_w_in, l2_q_norm, l2_w_uq, l2_kv_norm, l2_w_ukv, l2_w_out, l2_ln_g, l2_ln_b, l3_w_in, l3_w_out, l3_ln_g, l3_ln_b)
    moms = (m_meta_tokens, m_l0_w_in, m_l0_conv_w, m_l0_conv_b, m_l0_w_a, m_l0_b_a, m_l0_w_x, m_l0_b_x, m_l0_lam, m_l0_w_out, m_l0_ln_g, m_l0_ln_b, m_l1_w_in, m_l1_w_grp, m_l1_scale, m_l1_w_out, m_l1_ln_g, m_l1_ln_b, m_l2_w_in, m_l2_q_norm, m_l2_w_uq, m_l2_kv_norm, m_l2_w_ukv, m_l2_w_out, m_l2_ln_g, m_l2_ln_b, m_l3_w_in, m_l3_w_out, m_l3_ln_g, m_l3_ln_b)
    vels = (v_meta_tokens, v_l0_w_in, v_l0_conv_w, v_l0_conv_b, v_l0_w_a, v_l0_b_a, v_l0_w_x, v_l0_b_x, v_l0_lam, v_l0_w_out, v_l0_ln_g, v_l0_ln_b, v_l1_w_in, v_l1_w_grp, v_l1_scale, v_l1_w_out, v_l1_ln_g, v_l1_ln_b, v_l2_w_in, v_l2_q_norm, v_l2_w_uq, v_l2_kv_norm, v_l2_w_ukv, v_l2_w_out, v_l2_ln_g, v_l2_ln_b, v_l3_w_in, v_l3_w_out, v_l3_ln_g, v_l3_ln_b)
    W = dict(zip(WEIGHTS, args))
    M = dict(zip(WEIGHTS, moms))
    V = dict(zip(WEIGHTS, vels))

    seq = x.shape[1]
    t_real = N_META + seq
    t_pad = -(-t_real // ROW_ALIGN) * ROW_ALIGN
    tgt_pad = jnp.pad(loss_target[0], ((N_META, t_pad - t_real), (0, 0)))

    sch = _Schedule()
    for n in GATHER_ORDER:
        shard = W[n].astype(BF) if n in BIG else W[n]
        sch.push(n, shard, True, shard.ndim - 2)
    S = {n: W[n] for n in REPLICATED}

    loss, gx, gS = _train_local(sch, x[0], tgt_pad, S, t_pad=t_pad)

    flat = jnp.concatenate([gS[n].reshape(-1) for n in REPLICATED]).reshape(-1, 128)
    sch.push('small_grads', flat, True, 0)

    out_g, out_d, out_m, out_v = {}, {}, {}, {}
    order = ['l3_w_out', 'l3_w_in', 'l2_w_out', 'l2_w_ukv', 'l2_w_uq', 'l2_w_in', 'l1_w_out', 'l1_w_grp',
             'l1_w_in', 'l0_w_out', 'l0_w_a', 'l0_w_x', 'l0_conv_w', 'l0_w_in', 'meta_tokens']
    for n in order:
        shp = W[n].shape
        w2, m2, v2 = _as2d(W[n]), _as2d(M[n]), _as2d(V[n])
        parts = [p.reshape((N_DEV, -1, w2.shape[1])) for p in sch.get(n)]
        res = _ride(sch, w2.size * 1.5e-8, _adamw_call, contribs=parts, w=w2, m=m2, v=v2, name='adamw_' + n)
        out_g[n], out_d[n], out_m[n], out_v[n] = [r.reshape(shp) for r in res]
    cat = lambda D: jnp.concatenate([D[n].reshape(-1) for n in REPLICATED]).reshape(-1, 128)
    res = _adamw_call(sch.get('small_grads'), cat(W), cat(M), cat(V), name='adamw_small')[0]
    off = 0
    for n in REPLICATED:
        size = W[n].size
        for dst, r in zip((out_g, out_d, out_m, out_v), res):
            dst[n] = r.reshape(-1)[off:off + size].reshape(W[n].shape)
        off += size
    sch.flush()

    loss = lax.psum(loss, ("x", "y", "c"))
    return (loss, gx[None], *[out_g[n] for n in WEIGHTS], *[out_d[n] for n in WEIGHTS],
            *[out_m[n] for n in WEIGHTS], *[out_v[n] for n in WEIGHTS])
```

```python
import functools

import jax
import jax.numpy as jnp
from jax import lax
from jax.experimental import pallas as pl
from jax.experimental.pallas import tpu as pltpu

F32 = jnp.float32
BF = jnp.bfloat16

N_DEV = 8
N_META = 16
ALPHA = (2.0 * 4) ** 0.25
LN_EPS = 1e-5
RMS_EPS = 1e-6
ROPE_BASE = 10000.0
LRU_HEADS = 16
CONV_W = 4
LRU_C = 8.0
POOL_WINDOWS = (2, 4, 8, 16)
MLA_HEADS = 32
MLA_NOPE = 128
MLA_ROPE = 64
MLA_QK = 256
Q_LORA = 1024
KV_LORA = 512
RET_HEADS = 16
ADAM_LR = 0.001
ADAM_B1 = 0.9
ADAM_B2 = 0.999
ADAM_EPS = 1e-08
ADAM_WD = 0.01
ADAM_STEP = 10

ROW_ALIGN = 128
VMEM_LIMIT_BYTES = 56 * 1024 * 1024
TT_PREFS = (128, 64, 32, 16, 8)
ATT_PREFS = (384, 256, 128)
MM_M_PREFS = (1408, 1024, 512, 384, 256, 128)
MM_N_PREFS = (1024, 640, 512, 384, 256, 128)
MM_K_PREFS = (1408, 1024, 512, 384, 256, 128)
MM_VMEM_BUDGET = 40 * 1024 * 1024
ADAM_BLOCK_ELEMS = 128 * 1024
EXCH_MS_PER_MB = 0.0857
EXCH_CHUNK_MS = 0.1
MXU_FLOPS_PER_MS = 7.8e11
BWD_RIDER_SHARE = 0.6

WEIGHTS = ['meta_tokens', 'l0_w_in', 'l0_conv_w', 'l0_conv_b', 'l0_w_a', 'l0_b_a', 'l0_w_x', 'l0_b_x', 'l0_lam',
           'l0_w_out', 'l0_ln_g', 'l0_ln_b', 'l1_w_in', 'l1_w_grp', 'l1_scale', 'l1_w_out', 'l1_ln_g', 'l1_ln_b',
           'l2_w_in', 'l2_q_norm', 'l2_w_uq', 'l2_kv_norm', 'l2_w_ukv', 'l2_w_out', 'l2_ln_g', 'l2_ln_b',
           'l3_w_in', 'l3_w_out', 'l3_ln_g', 'l3_ln_b']
BIG = ['l0_w_in', 'l0_w_out', 'l1_w_in', 'l1_w_grp', 'l1_w_out', 'l2_w_in', 'l2_w_uq', 'l2_w_ukv', 'l2_w_out',
       'l3_w_in', 'l3_w_out']
SHARDED_F32 = ['meta_tokens', 'l0_conv_w', 'l0_w_a', 'l0_w_x']
REPLICATED = [n for n in WEIGHTS if n not in BIG and n not in SHARDED_F32]
GATHER_ORDER = ['meta_tokens', 'l0_w_in', 'l0_conv_w', 'l0_w_a', 'l0_w_x', 'l0_w_out', 'l1_w_in', 'l1_w_grp',
                'l1_w_out', 'l2_w_in', 'l2_w_uq', 'l2_w_ukv', 'l2_w_out', 'l3_w_in', 'l3_w_out']


def _pick(n, prefs):
    for p in prefs:
        if n % p == 0:
            return p
    return n


def _exchange_copies(jobs, in_refs, out_refs, send_sems, recv_sems, local_sems):
    x, y, c = lax.axis_index("x"), lax.axis_index("y"), lax.axis_index("c")
    me = 4 * x + 2 * y + c
    sibling = (x, y, 1 - c)
    chips = [(1 - x, y), (x, 1 - y), (1 - x, 1 - y)]
    peers = [sibling] + [(px, py, c) for px, py in chips] + [(px, py, 1 - c) for px, py in chips]
    first, arrivals, forwards, rest = [], [], [], []

    def dev(px, py, pc):
        return 4 * px + 2 * py + pc

    for n, (_, gather) in enumerate(jobs):
        def remote(p, src, slot, to, n=n):
            k = n * (N_DEV - 1) + p
            return pltpu.make_async_remote_copy(
                src_ref=src, dst_ref=out_refs[n].at[slot], send_sem=send_sems.at[k], recv_sem=recv_sems.at[k],
                device_id=to, device_id_type=pl.DeviceIdType.MESH)

        if gather:
            first += [remote(p, in_refs[n], me, peers[p]) for p in range(4)]
            for j, (px, py) in enumerate(chips):
                landed = out_refs[n].at[dev(px, py, c)]
                arrivals.append(remote(1 + j, landed, dev(px, py, c), peers[1 + j]))
                forwards.append(remote(4 + j, landed, dev(px, py, c), sibling))
        else:
            first += [remote(p, in_refs[n].at[dev(*peers[p])], me, peers[p]) for p in range(N_DEV - 1)]
        rest.append(pltpu.make_async_copy(in_refs[n] if gather else in_refs[n].at[me], out_refs[n].at[me],
                                          local_sems.at[n]))

    def start():
        for cp in first + rest:
            cp.start()

    def finish():
        for arrived, forward in zip(arrivals, forwards):
            arrived.wait_recv()
            forward.start()
        for cp in first:
            cp.wait_send()
        for n, (_, gather) in enumerate(jobs):
            for p in range(N_DEV - 1):
                if not (gather and 1 <= p <= 3):
                    k = n * (N_DEV - 1) + p
                    pltpu.make_async_remote_copy(
                        src_ref=out_refs[n].at[me], dst_ref=out_refs[n].at[me], send_sem=send_sems.at[k],
                        recv_sem=recv_sems.at[k], device_id=sibling, device_id_type=pl.DeviceIdType.MESH).wait_recv()
        for cp in forwards:
            cp.wait_send()
        for cp in rest:
            cp.wait()

    return start, finish


def _exchange_shapes(jobs):
    shapes = []
    for arr, gather in jobs:
        blk = arr.shape if gather else arr.shape[1:]
        shapes.append(jax.ShapeDtypeStruct((N_DEV,) + tuple(blk), arr.dtype))
    return shapes


def _exchange_scratch(jobs):
    n = len(jobs)
    return [pltpu.SemaphoreType.DMA((n * (N_DEV - 1),)), pltpu.SemaphoreType.DMA((n * (N_DEV - 1),)),
            pltpu.SemaphoreType.DMA((n,))]


def _call(body, *, name, grid, in_specs, out_specs, out_shape, args, scratch=(), sem=None, jobs=()):
    jobs = list(jobs)
    n_in, n_out, n_sc, n_job = len(in_specs), len(out_specs), len(scratch), len(jobs)
    hbm = pl.BlockSpec(memory_space=pl.ANY)

    def kern(*refs):
        ins = refs[:n_in]
        job_ins = refs[n_in:n_in + n_job]
        pos = n_in + n_job
        outs = refs[pos:pos + n_out]
        job_outs = refs[pos + n_out:pos + n_out + n_job]
        pos += n_out + n_job
        scr = refs[pos:pos + n_sc]
        if n_job:
            ids = [pl.program_id(d) for d in range(len(grid))]
            first = functools.reduce(jnp.logical_and, [i == 0 for i in ids])
            last = functools.reduce(jnp.logical_and, [i == g - 1 for i, g in zip(ids, grid)])
            start, finish = _exchange_copies(jobs, job_ins, job_outs, *refs[pos + n_sc:])
            pl.when(first)(start)

        body(ins, outs, scr)
        if n_job:
            pl.when(last)(finish)

    kw = dict(vmem_limit_bytes=VMEM_LIMIT_BYTES)
    if sem is not None:
        kw['dimension_semantics'] = tuple("arbitrary" for _ in grid) if n_job else sem
    res = pl.pallas_call(
        kern, name=name, grid=grid,
        in_specs=list(in_specs) + [hbm] * n_job,
        out_specs=list(out_specs) + [hbm] * n_job,
        out_shape=list(out_shape) + _exchange_shapes(jobs),
        scratch_shapes=list(scratch) + (_exchange_scratch(jobs) if n_job else []),
        compiler_params=pltpu.CompilerParams(**kw),
    )(*args, *[a for a, _ in jobs])
    return list(res[:n_out]), list(res[n_out:])


def _exchange_alone(jobs, name):
    def body(*refs):
        n = len(jobs)
        start, finish = _exchange_copies(jobs, refs[:n], refs[n:2 * n], *refs[2 * n:])
        start()
        finish()

    hbm = pl.BlockSpec(memory_space=pl.ANY)
    return list(pl.pallas_call(
        body, name=name, in_specs=[hbm] * len(jobs), out_specs=[hbm] * len(jobs),
        out_shape=_exchange_shapes(jobs), scratch_shapes=_exchange_scratch(jobs),
    )(*[a for a, _ in jobs]))


class _Schedule:
    def __init__(self):
        self.pending = []
        self.done = {}
        self.chunks = {}
        self.count = 0
        self.scale = 1.0

    def push(self, name, arr, gather, row_axis):
        mb = arr.size * arr.dtype.itemsize / (1 if gather else N_DEV) / 1e6
        cost = mb * EXCH_MS_PER_MB * (0.5 if gather else 1.0)
        rows = arr.shape[row_axis]
        n = 1
        leading = row_axis == (0 if gather else 1)
        while leading and cost / n > EXCH_CHUNK_MS and rows % (2 * n) == 0 and rows // (2 * n) >= 16:
            n *= 2
        self.chunks[name] = n
        step = rows // n
        for k in range(n):
            piece = lax.slice_in_dim(arr, k * step, (k + 1) * step, axis=row_axis) if n > 1 else arr
            self.pending.append(((name, k), piece, gather, cost / n))

    def take(self, budget):
        jobs, spent = [], 0.0
        budget *= self.scale
        while self.pending and spent + 0.5 * self.pending[0][3] <= budget:
            job = self.pending.pop(0)
            jobs.append(job)
            spent += job[3]
        return jobs

    def deliver(self, jobs, results):
        for (key, _, _, _), r in zip(jobs, results):
            self.done[key] = r

    def get(self, name):
        mine = [j for j in self.pending if j[0][0] == name]
        if mine:
            self.pending = [j for j in self.pending if j[0][0] != name]
            self.count += 1
            self.deliver(mine, _exchange_alone([(j[1], j[2]) for j in mine], f'exchange_{self.count}_{name}'))
        parts = [self.done.pop((name, k)) for k in range(self.chunks[name])]
        return parts

    def flush(self):
        if self.pending:
            jobs, self.pending = self.pending, []
            self.count += 1
            self.deliver(jobs, _exchange_alone([(j[1], j[2]) for j in jobs], f'exchange_{self.count}_rest'))


def _ride(sch, budget, fn, **kw):
    jobs = sch.take(budget) if sch is not None else []
    outs, exch = fn(jobs=[(j[1], j[2]) for j in jobs], **kw)
    if jobs:
        sch.deliver(jobs, exch)
    return outs


@functools.partial(jax.custom_vjp, nondiff_argnums=(1, 2))
def _roll(x, shift, axis):
    return pltpu.roll(x, shift, axis)


def _roll_fwd(x, shift, axis):
    return pltpu.roll(x, shift, axis), None


def _roll_bwd(shift, axis, _, g):
    n = g.shape[axis]
    return (pltpu.roll(g, (n - shift) % n, axis),)


_roll.defvjp(_roll_fwd, _roll_bwd)


@jax.custom_vjp
def _bdot(x, w):
    return jnp.dot(x.astype(BF), w.astype(BF), preferred_element_type=F32)


def _bdot_fwd(x, w):
    return _bdot(x, w), (x, w)


def _bdot_bwd(res, g):
    x, w = res
    gb = g.astype(BF)
    dx = lax.dot_general(gb, w.astype(BF), (((1,), (1,)), ((), ())), preferred_element_type=F32)
    dw = lax.dot_general(x.astype(BF), gb, (((0,), (0,)), ((), ())), preferred_element_type=F32)
    return dx, dw


_bdot.defvjp(_bdot_fwd, _bdot_bwd)


def _silu(g):
    return g * jax.nn.sigmoid(g)


def _softplus(x):
    return jnp.maximum(x, 0.0) + jnp.log1p(jnp.exp(-jnp.abs(x)))


def _mm_call(a, b, *, mode, out_dtype, name, add=None, jobs=()):
    if mode == 'nn':
        (mo, kc), (_, no) = a.shape, b.shape
    elif mode == 'nt':
        (mo, kc), (no, _) = a.shape, b.shape
    else:
        (kc, mo), (_, no) = a.shape, b.shape
    tm = _pick(mo, MM_M_PREFS)
    tn = no if no <= 2048 and no % 512 != 0 else _pick(no, MM_N_PREFS)
    out_bytes = tm * tn * (2 * jnp.dtype(out_dtype).itemsize + 4 + (8 if add is not None else 0))
    tk = None
    for cand in MM_K_PREFS:
        if kc % cand == 0:
            tk = cand
            if out_bytes + 2 * cand * (tm * a.dtype.itemsize + tn * b.dtype.itemsize) <= MM_VMEM_BUDGET:
                break
    tk = tk or kc
    nk = kc // tk
    if mode == 'nn':
        a_spec = pl.BlockSpec((tm, tk), lambda i, j, k: (i, k))
        b_spec = pl.BlockSpec((tk, tn), lambda i, j, k: (k, j))
        dims = (((1,), (0,)), ((), ()))
    elif mode == 'nt':
        a_spec = pl.BlockSpec((tm, tk), lambda i, j, k: (i, k))
        b_spec = pl.BlockSpec((tn, tk), lambda i, j, k: (j, k))
        dims = (((1,), (1,)), ((), ()))
    else:
        a_spec = pl.BlockSpec((tk, tm), lambda i, j, k: (k, i))
        b_spec = pl.BlockSpec((tk, tn), lambda i, j, k: (k, j))
        dims = (((0,), (0,)), ((), ()))
    o_spec = pl.BlockSpec((tm, tn), lambda i, j, k: (i, j))

    def body(ins, outs, scr):
        a_ref, b_ref = ins[0], ins[1]
        (o_ref,), (acc_ref,) = outs, scr
        k = pl.program_id(2)

        @pl.when(k == 0)
        def _():
            acc_ref[...] = jnp.zeros_like(acc_ref) if add is None else ins[2][...]

        acc_ref[...] += lax.dot_general(a_ref[...].astype(BF), b_ref[...].astype(BF), dims,
                                        preferred_element_type=F32)

        @pl.when(k == nk - 1)
        def _():
            o_ref[...] = acc_ref[...].astype(o_ref.dtype)

    outs, exch = _call(
        body, name=name, grid=(mo // tm, no // tn, nk),
        in_specs=[a_spec, b_spec] + ([o_spec] if add is not None else []),
        out_specs=[o_spec], out_shape=[jax.ShapeDtypeStruct((mo, no), out_dtype)],
        args=[a, b] + ([add] if add is not None else []),
        scratch=[pltpu.VMEM((tm, tn), F32)], sem=("parallel", "parallel", "arbitrary"), jobs=jobs)
    return outs, exch


def mm(sch, a, b, name, mode='nn', out_dtype=F32, add=None):
    if mode == 'nn':
        flops = 2.0 * a.shape[0] * a.shape[1] * b.shape[1]
    elif mode == 'nt':
        flops = 2.0 * a.shape[0] * a.shape[1] * b.shape[0]
    else:
        flops = 2.0 * a.shape[0] * a.shape[1] * b.shape[1]
    return _ride(sch, flops / MXU_FLOPS_PER_MS, _mm_call, a=a, b=b, mode=mode, out_dtype=out_dtype, name=name,
                 add=add)[0]


def _full_spec(p):
    nd = p.ndim
    return pl.BlockSpec(p.shape, lambda i: (0,) * nd)


def _load_rows(refs, n_rows, halo, step_is_first):
    cur, prev, pos = [], [], 0
    for r in range(n_rows):
        cur.append(refs[pos][...].astype(F32))
        pos += 1
        if r in halo:
            keep = jnp.where(step_is_first, 0.0, 1.0).astype(F32)
            prev.append(refs[pos][...].astype(F32) * keep)
            pos += 1
        else:
            prev.append(None)
    return cur, prev, pos


def _join(cur, prev):
    return [c if p is None else jnp.concatenate([p, c], axis=0) for c, p in zip(cur, prev)]


def _rw_fwd(f, rows, params, outs, *, name, n_reduce=0, halo=(), tt=None, jobs=()):
    t_len = rows[0].shape[0]
    tt = tt or _pick(t_len, TT_PREFS)
    nt = t_len // tt
    n_rows, n_par, n_out = len(rows), len(params), len(outs)

    def body(ins, orefs, scr):
        i = pl.program_id(0)
        cur, prev, pos = _load_rows(ins, n_rows, halo, i == 0)
        pvals = [ins[pos + k][...] for k in range(n_par)]
        res = f(_join(cur, prev), pvals)
        n_f = len(res) - n_reduce
        for k, (_, _, src) in enumerate(outs):
            orefs[k][...] = res[src].astype(orefs[k].dtype)
        for k in range(n_reduce):
            ref, val = orefs[n_out + k], res[n_f + k]

            @pl.when(i == 0)
            def _():
                ref[...] = val

            @pl.when(i > 0)
            def _():
                ref[...] += val

    in_specs, args = [], []
    for r, x in enumerate(rows):
        c = x.shape[1]
        in_specs.append(pl.BlockSpec((tt, c), lambda i: (i, 0)))
        args.append(x)
        if r in halo:
            in_specs.append(pl.BlockSpec((tt, c), lambda i: (jnp.maximum(i - 1, 0), 0)))
            args.append(x)
    for p in params:
        in_specs.append(_full_spec(p))
        args.append(p)
    out_specs = [pl.BlockSpec((tt, c), lambda i: (i, 0)) for c, _, _ in outs]
    out_shape = [jax.ShapeDtypeStruct((t_len, c), dt) for c, dt, _ in outs]
    for _ in range(n_reduce):
        out_specs.append(pl.BlockSpec((1, 1), lambda i: (0, 0)))
        out_shape.append(jax.ShapeDtypeStruct((1, 1), F32))
    return _call(body, name=name, grid=(nt,), in_specs=in_specs, out_specs=out_specs, out_shape=out_shape,
                 args=args, sem=("arbitrary",), jobs=jobs)


def _rw_bwd(f, rows, params, cts, *, name, n_reduce=0, halo=(), nd_rows=(), nd_params=(), tt=None,
            row_dtypes=None, jobs=()):
    t_len = rows[0].shape[0]
    tt = tt or _pick(t_len, TT_PREFS)
    nt = t_len // tt
    n_rows, n_par, n_ct = len(rows), len(params), len(cts)
    d_rows = [r for r in range(n_rows) if r not in nd_rows]
    d_pars = [k for k in range(n_par) if k not in nd_params]
    h_rows = [r for r in d_rows if r in halo]

    def blk(j):
        return nt - 1 - j

    def body(ins, orefs, carry_refs):
        j = pl.program_id(0)
        cur, prev, pos = _load_rows(ins, n_rows, halo, blk(j) == 0)
        pvals = [ins[pos + k][...] for k in range(n_par)]
        pos += n_par
        ct_vals = [ins[pos + k][...].astype(F32) for k in range(n_ct)]

        def g(dcur, dprev, dpar):
            c, p, q = list(cur), list(prev), list(pvals)
            for r, v in zip(d_rows, dcur):
                c[r] = v
            for r, v in zip(h_rows, dprev):
                p[r] = v
            for k, v in zip(d_pars, dpar):
                q[k] = v
            return tuple(f(_join(c, p), q))

        _, vjp = jax.vjp(g, [cur[r] for r in d_rows], [prev[r] for r in h_rows], [pvals[k] for k in d_pars])
        g_cur, g_prev, g_par = vjp(tuple(ct_vals))

        for n, r in enumerate(d_rows):
            if r in halo:
                cref = carry_refs[h_rows.index(r)]

                @pl.when(j == 0)
                def _():
                    cref[...] = jnp.zeros_like(cref)

                orefs[n][...] = (g_cur[n] + cref[...]).astype(orefs[n].dtype)
                cref[...] = g_prev[h_rows.index(r)]
            else:
                orefs[n][...] = g_cur[n].astype(orefs[n].dtype)
        for n in range(len(d_pars)):
            ref, val = orefs[len(d_rows) + n], g_par[n]

            @pl.when(j == 0)
            def _():
                ref[...] = val

            @pl.when(j > 0)
            def _():
                ref[...] += val

    in_specs, args = [], []
    for r, x in enumerate(rows):
        c = x.shape[1]
        in_specs.append(pl.BlockSpec((tt, c), lambda j: (blk(j), 0)))
        args.append(x)
        if r in halo:
            in_specs.append(pl.BlockSpec((tt, c), lambda j: (jnp.maximum(blk(j) - 1, 0), 0)))
            args.append(x)
    for p in params:
        in_specs.append(_full_spec(p))
        args.append(p)
    for ct in cts:
        if ct.shape == (1, 1):
            in_specs.append(pl.BlockSpec((1, 1), lambda j: (0, 0)))
        else:
            in_specs.append(pl.BlockSpec((tt, ct.shape[1]), lambda j: (blk(j), 0)))
        args.append(ct)
    out_specs, out_shape, scratch = [], [], []
    for n, r in enumerate(d_rows):
        c = rows[r].shape[1]
        out_specs.append(pl.BlockSpec((tt, c), lambda j: (blk(j), 0)))
        out_shape.append(jax.ShapeDtypeStruct((t_len, c), row_dtypes[n] if row_dtypes else F32))
        if r in halo:
            scratch.append(pltpu.VMEM((tt, c), F32))
    for k in d_pars:
        out_specs.append(_full_spec(params[k]))
        out_shape.append(jax.ShapeDtypeStruct(params[k].shape, F32))
    return _call(body, name=name + '_bwd', grid=(nt,), in_specs=in_specs, out_specs=out_specs,
                 out_shape=out_shape, args=args, scratch=scratch, sem=("arbitrary",), jobs=jobs)


def rw_fwd(sch, budget, f, rows, params, outs, **kw):
    return _ride(sch, budget, functools.partial(_rw_fwd, f, list(rows), list(params), outs), **kw)


def rw_bwd(sch, budget, f, rows, params, cts, **kw):
    return _ride(sch, budget, functools.partial(_rw_bwd, f, list(rows), list(params), list(cts)), **kw)


def _scan_call(a, b, mul, *, reverse, name, jobs=()):
    t_len, c_len = a.shape
    tt = _pick(t_len, TT_PREFS)
    tc = _pick(c_len, (512, 256, 128))
    nt = t_len // tt

    def body(ins, orefs, scr):
        (carry,) = scr
        t = pl.program_id(1)
        av, bv = ins[0][...], ins[1][...]
        row = lax.broadcasted_iota(jnp.int32, av.shape, 0)
        s = 1
        while s < tt:
            if reverse:
                ok = row < tt - s
                a_sh = jnp.where(ok, pltpu.roll(av, tt - s, 0), 1.0)
                b_sh = jnp.where(ok, pltpu.roll(bv, tt - s, 0), 0.0)
            else:
                ok = row >= s
                a_sh = jnp.where(ok, pltpu.roll(av, s, 0), 1.0)
                b_sh = jnp.where(ok, pltpu.roll(bv, s, 0), 0.0)
            bv = av * b_sh + bv
            av = av * a_sh
            s *= 2

        @pl.when(t == 0)
        def _():
            carry[...] = jnp.zeros_like(carry)

        hs = bv + av * carry[...]
        orefs[0][...] = hs
        edge = 0 if reverse else tt - 1
        carry[...] = orefs[0][edge:edge + 1, :]
        if mul is not None:
            orefs[1][...] = hs * ins[2][...]

    def idx(c, t):
        return ((nt - 1 - t) if reverse else t, c)

    spec = pl.BlockSpec((tt, tc), idx)
    n_in, n_out = (2, 1) if mul is None else (3, 2)
    return _call(body, name=name, grid=(c_len // tc, nt), in_specs=[spec] * n_in, out_specs=[spec] * n_out,
                 out_shape=[jax.ShapeDtypeStruct((t_len, c_len), F32)] * n_out,
                 args=[a, b] if mul is None else [a, b, mul],
                 scratch=[pltpu.VMEM((1, tc), F32)], sem=("parallel", "arbitrary"), jobs=jobs)


NT = (((1,), (1,)), ((), ()))


def _att_weights(s, rel, diag, lg, softmax, scale, t_axis):
    row = lax.broadcasted_iota(jnp.int32, s.shape, t_axis)
    col = lax.broadcasted_iota(jnp.int32, s.shape, 1 - t_axis)
    if softmax:
        s = s * scale
        return jnp.where(row >= col, s, -1e30) if diag else s
    diff = (rel + row - col).astype(F32)
    dec = jnp.exp(jnp.maximum(diff, 0.0) * lg)
    return jnp.where(diff >= 0.0, dec, 0.0) if diag else dec


def _att_fwd_call(q, k, v, lgt, *, heads, softmax, scale, name, jobs=()):
    t_len = q.shape[0]
    dqk, dv = q.shape[1] // heads, v.shape[1] // heads
    blk = _pick(t_len, ATT_PREFS)
    nb = t_len // blk

    lanes = 128
    n_fold = blk // lanes

    def body(ins, orefs, scr):
        q_ref, k_ref, v_ref, lg_ref = ins
        o_ref, lse_ref = orefs
        s_sc, m_sc, acc_sc = scr
        i = pl.program_id(1)
        lg = lg_ref[0:1, 0:1]
        qb = q_ref[...]
        acc_sc[...] = jnp.zeros_like(acc_sc)

        def block_rows(j):
            return pl.ds(pl.multiple_of(j * blk, blk), blk)

        if not softmax:
            def step(j, diag):
                s = lax.dot_general(qb, k_ref[block_rows(j), :], NT, preferred_element_type=F32)
                w = _att_weights(s, (i - j) * blk, diag, lg, softmax, scale, 0)
                acc_sc[...] += jnp.dot((s * w).astype(BF), v_ref[block_rows(j), :], preferred_element_type=F32)

            lax.fori_loop(0, i, lambda j, c: (step(j, False), c)[1], 0)
            step(i, True)
            o_ref[...] = acc_sc[...]
            lse_ref[...] = jnp.zeros_like(lse_ref)
            return

        m_sc[...] = jnp.full_like(m_sc, -1e30)

        def score(j, diag):
            s = lax.dot_general(qb, k_ref[block_rows(j), :], NT, preferred_element_type=F32)
            w = _att_weights(s, (i - j) * blk, diag, lg, softmax, scale, 0)
            s_sc[j] = w
            top = w[:, :lanes]
            for n in range(1, n_fold):
                top = jnp.maximum(top, w[:, n * lanes:(n + 1) * lanes])
            m_sc[...] = jnp.maximum(m_sc[...], top)

        lax.fori_loop(0, i, lambda j, c: (score(j, False), c)[1], 0)
        score(i, True)
        m = jnp.max(m_sc[...], axis=-1, keepdims=True)
        ones = jnp.ones((blk, lanes), BF)

        def accumulate(j, c):
            p = jnp.exp(s_sc[j] - m).astype(BF)
            v_aug = jnp.concatenate([v_ref[block_rows(j), :], ones], axis=1)
            acc_sc[...] += jnp.dot(p, v_aug, preferred_element_type=F32)
            return c

        lax.fori_loop(0, i + 1, accumulate, 0)
        acc = acc_sc[...]
        l = acc[:, dv:dv + 1]
        o_ref[...] = acc[:, :dv] / l
        lse_ref[...] = m + jnp.log(l)

    scratch = [pltpu.VMEM((nb, blk, blk), F32), pltpu.VMEM((blk, lanes), F32),
               pltpu.VMEM((blk, dv + lanes if softmax else dv), F32)]
    return _call(
        body, name=name, grid=(heads, nb),
        in_specs=[
            pl.BlockSpec((blk, dqk), lambda h, i: (i, h)),
            pl.BlockSpec((t_len, dqk), lambda h, i: (0, h)),
            pl.BlockSpec((t_len, dv), lambda h, i: (0, h)),
            pl.BlockSpec((None, 1, 128), lambda h, i: (h, 0, 0)),
        ],
        out_specs=[
            pl.BlockSpec((blk, dv), lambda h, i: (i, h)),
            pl.BlockSpec((None, blk, 1), lambda h, i: (h, i, 0)),
        ],
        out_shape=[jax.ShapeDtypeStruct((t_len, heads * dv), F32), jax.ShapeDtypeStruct((heads, t_len, 1), F32)],
        args=[q, k, v, lgt],
        scratch=scratch if softmax else [pltpu.VMEM((8, lanes), F32), scratch[1], scratch[2]],
        sem=("parallel", "arbitrary"), jobs=jobs)


def _att_dq_call(q, k, v, lgt, o, lse, do, *, heads, softmax, scale, name, jobs=()):
    t_len = q.shape[0]
    dqk, dv = q.shape[1] // heads, v.shape[1] // heads
    blk = _pick(t_len, ATT_PREFS)
    nb = t_len // blk

    def body(ins, orefs, scr):
        q_ref, k_ref, v_ref, lg_ref, o_ref, do_ref, lse_ref = ins
        dq_ref, delta_ref = orefs
        (acc,) = scr
        i = pl.program_id(1)
        lg = lg_ref[0:1, 0:1]
        qb = q_ref[...]
        do = do_ref[...]
        dob = do.astype(BF)
        delta = jnp.sum(do * o_ref[...], axis=-1, keepdims=True)
        lse = lse_ref[...]
        delta_ref[...] = delta
        acc[...] = jnp.zeros_like(acc)

        def step(j, diag):
            rows = pl.ds(pl.multiple_of(j * blk, blk), blk)
            kb = k_ref[rows, :]
            s = lax.dot_general(qb, kb, NT, preferred_element_type=F32)
            w = _att_weights(s, (i - j) * blk, diag, lg, softmax, scale, 0)
            dp = lax.dot_general(dob, v_ref[rows, :], NT, preferred_element_type=F32)
            ds = jnp.exp(w - lse) * (dp - delta) * scale if softmax else dp * w
            acc[...] += jnp.dot(ds.astype(BF), kb, preferred_element_type=F32)

        lax.fori_loop(0, i, lambda j, c: (step(j, False), c)[1], 0)
        step(i, True)
        dq_ref[...] = acc[...]

    return _call(
        body, name=name + '_dq', grid=(heads, nb),
        in_specs=[
            pl.BlockSpec((blk, dqk), lambda h, i: (i, h)),
            pl.BlockSpec((t_len, dqk), lambda h, i: (0, h)),
            pl.BlockSpec((t_len, dv), lambda h, i: (0, h)),
            pl.BlockSpec((None, 1, 128), lambda h, i: (h, 0, 0)),
            pl.BlockSpec((blk, dv), lambda h, i: (i, h)),
            pl.BlockSpec((blk, dv), lambda h, i: (i, h)),
            pl.BlockSpec((None, blk, 1), lambda h, i: (h, i, 0)),
        ],
        out_specs=[pl.BlockSpec((blk, dqk), lambda h, i: (i, h)),
                   pl.BlockSpec((None, blk, 1), lambda h, i: (h, i, 0))],
        out_shape=[jax.ShapeDtypeStruct(q.shape, F32), jax.ShapeDtypeStruct((heads, t_len, 1), F32)],
        args=[q, k, v, lgt, o, do, lse],
        scratch=[pltpu.VMEM((blk, dqk), F32)], sem=("parallel", "arbitrary"), jobs=jobs)


def _att_dkv_call(q, k, v, lgt, lse_row, delta_row, do, *, heads, softmax, scale, name, jobs=()):
    t_len = q.shape[0]
    dqk, dv = q.shape[1] // heads, v.shape[1] // heads
    blk = _pick(t_len, ATT_PREFS)
    nb = t_len // blk

    def body(ins, orefs, scr):
        q_ref, k_ref, v_ref, lg_ref, do_ref, lse_ref, delta_ref = ins
        dk_acc, dv_acc = scr
        j = pl.program_id(1)
        lg = lg_ref[0:1, 0:1]
        kb, vb = k_ref[...], v_ref[...]
        dk_acc[...] = jnp.zeros_like(dk_acc)
        dv_acc[...] = jnp.zeros_like(dv_acc)

        def step(i, diag):
            rows = pl.ds(pl.multiple_of(i * blk, blk), blk)
            qb = q_ref[rows, :]
            dob = do_ref[rows, :].astype(BF)
            s = lax.dot_general(kb, qb, NT, preferred_element_type=F32)
            w = _att_weights(s, (i - j) * blk, diag, lg, softmax, scale, 1)
            dp = lax.dot_general(vb, dob, NT, preferred_element_type=F32)
            if softmax:
                p = jnp.exp(w - lse_ref[:, rows])
                ds = p * (dp - delta_ref[:, rows]) * scale
            else:
                p, ds = s * w, dp * w
            dv_acc[...] += jnp.dot(p.astype(BF), dob, preferred_element_type=F32)
            dk_acc[...] += jnp.dot(ds.astype(BF), qb, preferred_element_type=F32)

        step(j, True)
        lax.fori_loop(j + 1, nb, lambda i, c: (step(i, False), c)[1], 0)
        orefs[0][...] = dk_acc[...]
        orefs[1][...] = dv_acc[...]

    return _call(
        body, name=name + '_dkv', grid=(heads, nb),
        in_specs=[
            pl.BlockSpec((t_len, dqk), lambda h, j: (0, h)),
            pl.BlockSpec((blk, dqk), lambda h, j: (j, h)),
            pl.BlockSpec((blk, dv), lambda h, j: (j, h)),
            pl.BlockSpec((None, 1, 128), lambda h, j: (h, 0, 0)),
            pl.BlockSpec((t_len, dv), lambda h, j: (0, h)),
            pl.BlockSpec((None, 1, t_len), lambda h, j: (h, 0, 0)),
            pl.BlockSpec((None, 1, t_len), lambda h, j: (h, 0, 0)),
        ],
        out_specs=[
            pl.BlockSpec((blk, dqk), lambda h, j: (j, h)),
            pl.BlockSpec((blk, dv), lambda h, j: (j, h)),
        ],
        out_shape=[jax.ShapeDtypeStruct(k.shape, F32), jax.ShapeDtypeStruct(v.shape, F32)],
        args=[q, k, v, lgt, do, lse_row, delta_row],
        scratch=[pltpu.VMEM((blk, dqk), F32), pltpu.VMEM((blk, dv), F32)],
        sem=("parallel", "arbitrary"), jobs=jobs)


def _adamw_call(contribs, w, m, v, *, name, jobs=()):
    r_len, c_len = w.shape
    n_chunk = len(contribs)
    r_chunk = r_len // n_chunk
    cap = max(min(ADAM_BLOCK_ELEMS, 4 * ADAM_BLOCK_ELEMS // n_chunk) // c_len, 1)
    tr = r_chunk
    for cand in (512, 256, 128, 64, 32, 16):
        if cand <= cap and r_chunk % cand == 0:
            tr = cand
            break
    per = r_chunk // tr

    def body(ins, orefs, scr):
        w_ref, m_ref, v_ref = ins[n_chunk:]
        g_ref, d_ref, mo_ref, vo_ref = orefs
        i = pl.program_id(0)

        def update(c_ref):
            g = c_ref[0].astype(F32)
            for n in range(1, N_DEV):
                g = g + c_ref[n].astype(F32)
            m_new = ADAM_B1 * m_ref[...] + (1.0 - ADAM_B1) * g
            v_new = ADAM_B2 * v_ref[...] + (1.0 - ADAM_B2) * jnp.square(g)
            m_hat = m_new / (1.0 - ADAM_B1 ** ADAM_STEP)
            v_hat = v_new / (1.0 - ADAM_B2 ** ADAM_STEP)
            g_ref[...] = g
            d_ref[...] = -ADAM_LR * (m_hat / (jnp.sqrt(v_hat) + ADAM_EPS) + ADAM_WD * w_ref[...])
            mo_ref[...] = m_new
            vo_ref[...] = v_new

        if n_chunk == 1:
            update(ins[0])
        else:
            for n in range(n_chunk):
                @pl.when(i // per == n)
                def _():
                    update(ins[n])

    spec = pl.BlockSpec((tr, c_len), lambda i: (i, 0))
    c_specs = [pl.BlockSpec((N_DEV, tr, c_len), functools.partial(
        lambda i, n: (0, jnp.clip(i - n * per, 0, per - 1), 0), n=n)) for n in range(n_chunk)]
    return _call(body, name=name, grid=(r_len // tr,), in_specs=c_specs + [spec, spec, spec],
                 out_specs=[spec] * 4, out_shape=[jax.ShapeDtypeStruct((r_len, c_len), F32)] * 4,
                 args=list(contribs) + [w, m, v], sem=("arbitrary",), jobs=jobs)


def _rope_lanes(x, cc, s_lo, s_hi):
    return x * cc + _roll(x, 32, 1) * s_lo + _roll(x, 96, 1) * s_hi


def _f_lru(tt, branch):
    lb = branch // LRU_HEADS

    def f(rows, params):
        (xcat,) = rows
        cw, cb, wa, ba, wx, bx, lam = params
        conv = cb
        for j in range(CONV_W):
            sh = CONV_W - 1 - j
            xs = xcat if sh == 0 else _roll(xcat, sh, 0)
            conv = conv + cw[j:j + 1, :] * xs[tt:, :]
        rs, gs = [], []
        for h in range(LRU_HEADS):
            ub = conv[:, h * lb:(h + 1) * lb]
            rs.append(_bdot(ub, wa[h]))
            gs.append(_bdot(ub, wx[h]))
        r = jax.nn.sigmoid(jnp.concatenate(rs, axis=-1) + ba)
        gate = jax.nn.sigmoid(jnp.concatenate(gs, axis=-1) + bx)
        log_a = LRU_C * r * (-_softplus(-lam))
        a = jnp.exp(log_a)
        one_minus_a2 = -jnp.tanh(log_a) * (jnp.exp(2.0 * log_a) + 1.0)
        return a, (conv * gate) * jnp.sqrt(one_minus_a2)

    return f


def _f_gate(rows, params):
    hs, g = rows
    return (hs * _silu(g),)


def _f_ln(rows, params):
    h, br = rows
    g, b = params
    pre = ALPHA * h + br
    mu = jnp.mean(pre, axis=-1, keepdims=True)
    var = jnp.mean(jnp.square(pre - mu), axis=-1, keepdims=True)
    return ((pre - mu) * lax.rsqrt(var + LN_EPS) * g + b,)


def _f_pool(tt, branch):
    grp = branch // len(POOL_WINDOWS)

    def f(rows, params):
        xcat, tidx = rows
        sums, acc, w = [], xcat, 1
        while w < POOL_WINDOWS[-1]:
            acc = acc + _roll(acc, w, 0)
            w *= 2
            sums.append(acc[tt:, :])
        u = xcat[tt:, :]
        outs = []
        for gi, w in enumerate(POOL_WINDOWS):
            sl = slice(gi * grp, (gi + 1) * grp)
            outs.append(sums[gi][:, sl] / jnp.minimum(tidx + 1.0, float(w)) - u[:, sl])
        return tuple(outs)

    return f


def _f_gate_pool(rows, params):
    m0, m1, m2, m3, g = rows
    (scale,) = params
    return (jnp.concatenate([m0, m1, m2, m3], axis=-1) * scale * _silu(g),)


def _rms(x, g):
    return x * lax.rsqrt(jnp.mean(jnp.square(x), axis=-1, keepdims=True) + RMS_EPS) * g


def _f_mla_pre(rows, params):
    c, cc, s_lo, s_hi = rows
    qn, kvn = params
    cq = c[:, :Q_LORA]
    ckv = c[:, Q_LORA:Q_LORA + KV_LORA]
    kr = c[:, Q_LORA + KV_LORA:]
    return _rms(cq, qn), _rms(ckv, kvn), _rope_lanes(kr, cc, s_lo, s_hi)


def _f_rope_q(rows, params):
    qc, cc, s_lo, s_hi = rows
    out = []
    for h in range(MLA_HEADS):
        out.append(qc[:, h * MLA_QK:h * MLA_QK + MLA_NOPE])
        out.append(_rope_lanes(qc[:, h * MLA_QK + MLA_NOPE:(h + 1) * MLA_QK], cc, s_lo, s_hi))
    return (jnp.concatenate(out, axis=-1),)


def _f_kcat(dv):
    per = MLA_NOPE + dv

    def f(rows, params):
        kv, krr = rows
        ks, vs = [], []
        for h in range(MLA_HEADS):
            ks.append(kv[:, h * per:h * per + MLA_NOPE])
            ks.append(krr)
            vs.append(kv[:, h * per + MLA_NOPE:(h + 1) * per])
        return jnp.concatenate(ks, axis=-1), jnp.concatenate(vs, axis=-1)

    return f


def _f_rope_ret(dk):
    half = dk // 2

    def f(rows, params):
        q, k, cos, sin = rows
        qs, ks = [], []
        for h in range(RET_HEADS):
            for src, dst, mult in ((q, qs, 1.0), (k, ks, dk ** -0.5)):
                x1 = src[:, h * dk:h * dk + half]
                x2 = src[:, h * dk + half:(h + 1) * dk]
                dst.append((x1 * cos - x2 * sin) * mult)
                dst.append((x2 * cos + x1 * sin) * mult)
        return jnp.concatenate(qs, axis=-1), jnp.concatenate(ks, axis=-1)

    return f


def _f_gate_gn(dv):
    def f(rows, params):
        o, g = rows
        out = []
        for h in range(RET_HEADS):
            oh = o[:, h * dv:(h + 1) * dv]
            mu = jnp.mean(oh, axis=-1, keepdims=True)
            var = jnp.mean(jnp.square(oh - mu), axis=-1, keepdims=True)
            out.append((oh - mu) * lax.rsqrt(var + LN_EPS))
        return (jnp.concatenate(out, axis=-1) * _silu(g),)

    return f


def _f_loss(rows, params):
    h, tgt, mask = rows
    per_row = jnp.mean(jnp.square(h - tgt), axis=-1, keepdims=True) * mask
    return (0.5 * jnp.sum(per_row, axis=0, keepdims=True),)


def _cat(parts, axis=1):
    return parts[0] if len(parts) == 1 else jnp.concatenate(parts, axis=axis)


def _cols(g):
    return jnp.transpose(g, (1, 0, 2)).reshape(g.shape[1], -1)


def _uncols(w):
    k, n = w.shape
    return jnp.transpose(w.reshape(k, N_DEV, n // N_DEV), (1, 0, 2))


def _heads(g):
    return jnp.transpose(g, (1, 0, 2, 3)).reshape(g.shape[1], -1, g.shape[3])


def _unheads(w):
    h, r, c = w.shape
    return jnp.transpose(w.reshape(h, N_DEV, r // N_DEV, c), (1, 0, 2, 3))


def _rope_tables(t_pad, d):
    inv = ROPE_BASE ** (-jnp.arange(0, d, 2, dtype=F32) / d)
    ang = jnp.arange(t_pad, dtype=F32)[:, None] * inv[None, :]
    return jnp.cos(ang), jnp.sin(ang)


def _row2(v):
    return v.reshape(1, -1)


def _train_local(sch, x2d, tgt_pad, S, *, t_pad):
    seq, d_model = x2d.shape
    branch = d_model
    t_real = N_META + seq
    tt = _pick(t_pad, TT_PREFS)
    n_win = len(POOL_WINDOWS)
    grp = branch // n_win
    lb = branch // LRU_HEADS
    dv2 = branch // MLA_HEADS
    dk3 = branch // RET_HEADS
    gS = {}
    RW, RWB = 0.06, 0.1

    def ln_fwd(h, br, layer):
        h1, hb1 = rw_fwd(sch, RW, _f_ln, [h, br], [_row2(S[f'l{layer}_ln_g']), _row2(S[f'l{layer}_ln_b'])],
                         [(d_model, F32, 0), (d_model, BF, 0)], name=f'l{layer}_ln')
        return h1, hb1

    def ln_bwd(h, br, dh1, layer):
        dh, dbr, dg, db = rw_bwd(sch, RWB, _f_ln, [h, br],
                                 [_row2(S[f'l{layer}_ln_g']), _row2(S[f'l{layer}_ln_b'])], [dh1],
                                 name=f'l{layer}_ln', row_dtypes=[F32, BF])
        gS[f'l{layer}_ln_g'], gS[f'l{layer}_ln_b'] = dg.reshape(-1), db.reshape(-1)
        return dh, dbr

    tidx = jnp.arange(t_pad, dtype=F32)[:, None]
    rowmask = ((tidx >= N_META) & (tidx < t_real)).astype(F32)

    meta = _cols(_cat(sch.get('meta_tokens')))
    h0 = jnp.concatenate([meta, x2d, jnp.zeros((t_pad - t_real, d_model), F32)], axis=0)
    hb0 = h0.astype(BF)

    w0_in = _cols(_cat(sch.get('l0_w_in')))
    w0_u, w0_g = w0_in[:, :branch], w0_in[:, branch:]
    u0 = mm(sch, hb0, w0_u, 'l0_in_u')
    g0 = mm(sch, hb0, w0_g, 'l0_in_g')
    conv_w = jnp.transpose(_cat(sch.get('l0_conv_w')), (1, 2, 0, 3)).reshape(CONV_W, branch)
    w_a = _heads(_cat(sch.get('l0_w_a'), axis=2))
    w_x = _heads(_cat(sch.get('l0_w_x'), axis=2))
    p0 = [conv_w, _row2(S['l0_conv_b']), w_a, _row2(S['l0_b_a']), w_x, _row2(S['l0_b_x']), _row2(S['l0_lam'])]
    f_lru = _f_lru(tt, branch)
    a0, xin0 = rw_fwd(sch, 0.13, f_lru, [u0], p0, [(branch, F32, 0), (branch, F32, 1)], name='l0_lru',
                      halo=(0,), tt=tt)
    hs0 = _ride(sch, 0.22, _scan_call, a=a0, b=xin0, mul=None, reverse=False, name='l0_scan')[0]
    (z0,) = rw_fwd(sch, RW, _f_gate, [hs0, g0], [], [(branch, BF, 0)], name='l0_gate')
    w0_out = _cat(sch.get('l0_w_out')).reshape(branch, d_model)
    br0 = mm(sch, z0, w0_out, 'l0_out')
    h1, hb1 = ln_fwd(h0, br0, 0)

    w1_in = _cols(_cat(sch.get('l1_w_in')))
    w1_u, w1_g = w1_in[:, :branch], w1_in[:, branch:]
    u1 = mm(sch, hb1, w1_u, 'l1_in_u')
    g1 = mm(sch, hb1, w1_g, 'l1_in_g')
    f_pool = _f_pool(tt, branch)
    ps1 = rw_fwd(sch, RW, f_pool, [u1, tidx], [], [(grp, BF, gi) for gi in range(n_win)], name='l1_pool',
                 halo=(0,), tt=tt)
    w1_grp = _heads(_cat(sch.get('l1_w_grp'), axis=2))
    mixed1 = [mm(sch, ps1[gi], w1_grp[gi], f'l1_grp{gi}') for gi in range(n_win)]
    p1 = [_row2(S['l1_scale'])]
    (z1,) = rw_fwd(sch, RW, _f_gate_pool, mixed1 + [g1], p1, [(branch, BF, 0)], name='l1_gate')
    w1_out = _cat(sch.get('l1_w_out')).reshape(branch, d_model)
    br1 = mm(sch, z1, w1_out, 'l1_out')
    h2, hb2 = ln_fwd(h1, br1, 1)

    w2_in = _cols(_cat(sch.get('l2_w_in')))
    w2_g = w2_in[:, :branch]
    w2_lat = jnp.pad(w2_in[:, branch:], ((0, 0), (0, 128 - MLA_ROPE)))
    g2 = mm(sch, hb2, w2_g, 'l2_in_g')
    c2 = mm(sch, hb2, w2_lat, 'l2_in_c')
    cos, sin = _rope_tables(t_pad, MLA_ROPE)
    zz = jnp.zeros_like(cos)
    tabs = [jnp.concatenate([cos, cos, zz, zz], axis=-1), jnp.concatenate([zz, sin, zz, zz], axis=-1),
            jnp.concatenate([-sin, zz, zz, zz], axis=-1)]
    p2 = [_row2(S['l2_q_norm']), _row2(S['l2_kv_norm'])]
    cqn2, ckvn2, krr2 = rw_fwd(sch, 0.04, _f_mla_pre, [c2] + tabs, p2,
                               [(Q_LORA, BF, 0), (KV_LORA, BF, 1), (128, F32, 2)], name='l2_pre')
    w2_uq = _cols(_cat(sch.get('l2_w_uq'))).reshape(Q_LORA, MLA_HEADS, MLA_NOPE + MLA_ROPE)
    w2_uq = jnp.pad(w2_uq, ((0, 0), (0, 0), (0, MLA_QK - MLA_NOPE - MLA_ROPE))).reshape(Q_LORA, MLA_HEADS * MLA_QK)
    w2_ukv = _cols(_cat(sch.get('l2_w_ukv')))
    qc2 = mm(sch, cqn2, w2_uq, 'l2_uq')
    kv2 = mm(sch, ckvn2, w2_ukv, 'l2_ukv')
    (qcr2,) = rw_fwd(sch, 0.09, _f_rope_q, [qc2] + tabs, [], [(MLA_HEADS * MLA_QK, BF, 0)], name='l2_rope_q')
    f_kcat = _f_kcat(dv2)
    kcat2, v2 = rw_fwd(sch, 0.1, f_kcat, [kv2, krr2], [], [(MLA_HEADS * MLA_QK, BF, 0), (branch, BF, 1)],
                       name='l2_kcat')
    no_decay = jnp.zeros((MLA_HEADS, 1, 128), F32)
    att2 = dict(heads=MLA_HEADS, softmax=True, scale=(MLA_NOPE + MLA_ROPE) ** -0.5, name='l2_att')
    o2, lse2 = _ride(sch, 1.3, _att_fwd_call, q=qcr2, k=kcat2, v=v2, lgt=no_decay, **att2)
    (z2,) = rw_fwd(sch, RW, _f_gate, [o2, g2], [], [(branch, BF, 0)], name='l2_gate')
    w2_out = _cat(sch.get('l2_w_out')).reshape(branch, d_model)
    br2 = mm(sch, z2, w2_out, 'l2_out')
    h3, hb3 = ln_fwd(h2, br2, 2)

    w3_in = _cols(_cat(sch.get('l3_w_in')))
    w3 = [w3_in[:, n * branch:(n + 1) * branch] for n in range(4)]
    q3 = mm(sch, hb3, w3[0], 'l3_in_q')
    k3 = mm(sch, hb3, w3[1], 'l3_in_k')
    v3 = mm(sch, hb3, w3[2], 'l3_in_v', out_dtype=BF)
    g3 = mm(sch, hb3, w3[3], 'l3_in_g')
    cs3 = list(_rope_tables(t_pad, dk3))
    f_rope3 = _f_rope_ret(dk3)
    qr3, kr3 = rw_fwd(sch, 0.09, f_rope3, [q3, k3] + cs3, [], [(branch, BF, 0), (branch, BF, 1)], name='l3_rope')
    log_g = jnp.log(1.0 - 2.0 ** (-5.0 - jnp.arange(RET_HEADS, dtype=F32)))
    lgt = jnp.broadcast_to(log_g[:, None, None], (RET_HEADS, 1, 128))
    att3 = dict(heads=RET_HEADS, softmax=False, scale=1.0, name='l3_ret')
    o3, lse3 = _ride(sch, 0.6, _att_fwd_call, q=qr3, k=kr3, v=v3, lgt=lgt, **att3)
    f_gn = _f_gate_gn(dk3)
    (z3,) = rw_fwd(sch, 0.08, f_gn, [o3, g3], [], [(branch, BF, 0)], name='l3_gate')
    w3_out = _cat(sch.get('l3_w_out')).reshape(branch, d_model)
    br3 = mm(sch, z3, w3_out, 'l3_out')
    (h4,) = rw_fwd(sch, RW, _f_ln, [h3, br3], [_row2(S['l3_ln_g']), _row2(S['l3_ln_b'])], [(d_model, F32, 0)],
                   name='l3_ln')

    (loss,) = rw_fwd(sch, 0.05, _f_loss, [h4, tgt_pad, rowmask], [], [], name='loss', n_reduce=1)

    sch.scale = BWD_RIDER_SHARE
    (dh4,) = rw_bwd(sch, 0.07, _f_loss, [h4, tgt_pad, rowmask], [], [jnp.ones((1, 1), F32)], name='loss',
                    n_reduce=1, nd_rows=(1, 2))

    dh3, dbr3 = ln_bwd(h3, br3, dh4, 3)
    dz3 = mm(sch, dbr3, w3_out, 'l3_out_dx', mode='nt')
    sch.push('l3_w_out', mm(sch, z3.T, dbr3, 'l3_out_dw', mode='nn', out_dtype=BF).reshape(N_DEV, -1, d_model),
             False, 1)
    do3, dg3 = rw_bwd(sch, 0.13, f_gn, [o3, g3], [], [dz3], name='l3_gate', row_dtypes=[F32, BF])
    dqr3, delta3 = _ride(sch, 0.7, _att_dq_call, q=qr3, k=kr3, v=v3, lgt=lgt, o=o3, lse=lse3, do=do3, **att3)
    dkr3, dv3 = _ride(sch, 0.8, _att_dkv_call, q=qr3, k=kr3, v=v3, lgt=lgt, lse_row=lse3.reshape(RET_HEADS, 1, -1),
                      delta_row=delta3.reshape(RET_HEADS, 1, -1), do=do3, **att3)
    dq3, dk3_ = rw_bwd(sch, 0.13, f_rope3, [q3, k3] + cs3, [], [dqr3, dkr3], name='l3_rope', nd_rows=(2, 3),
                       row_dtypes=[BF, BF])
    d3 = [dq3, dk3_, dv3, dg3]
    for n in range(4):
        dh3 = mm(sch, d3[n], w3[n], f'l3_in_dx{n}', mode='nt', add=dh3)
    hb3_t = hb3.T
    dw3 = [mm(sch, hb3_t, d3[n], f'l3_in_dw{n}', mode='nn', out_dtype=BF) for n in range(4)]
    sch.push('l3_w_in', _uncols(jnp.concatenate(dw3, axis=1)), False, 1)

    dh2, dbr2 = ln_bwd(h2, br2, dh3, 2)
    dz2 = mm(sch, dbr2, w2_out, 'l2_out_dx', mode='nt')
    sch.push('l2_w_out', mm(sch, z2.T, dbr2, 'l2_out_dw', mode='nn', out_dtype=BF).reshape(N_DEV, -1, d_model),
             False, 1)
    do2, dg2 = rw_bwd(sch, 0.1, _f_gate, [o2, g2], [], [dz2], name='l2_gate', row_dtypes=[F32, BF])
    dqcr2, delta2 = _ride(sch, 1.3, _att_dq_call, q=qcr2, k=kcat2, v=v2, lgt=no_decay, o=o2, lse=lse2, do=do2,
                          **att2)
    dkcat2, dv2_ = _ride(sch, 1.5, _att_dkv_call, q=qcr2, k=kcat2, v=v2, lgt=no_decay,
                         lse_row=lse2.reshape(MLA_HEADS, 1, -1), delta_row=delta2.reshape(MLA_HEADS, 1, -1),
                         do=do2, **att2)
    dkv2, dkrr2 = rw_bwd(sch, 0.14, f_kcat, [kv2, krr2], [], [dkcat2, dv2_], name='l2_kcat', row_dtypes=[BF, F32])
    (dqc2,) = rw_bwd(sch, 0.13, _f_rope_q, [qc2] + tabs, [], [dqcr2], name='l2_rope_q', nd_rows=(1, 2, 3),
                     row_dtypes=[BF])
    dckvn2 = mm(sch, dkv2, w2_ukv, 'l2_ukv_dx', mode='nt')
    dcqn2 = mm(sch, dqc2, w2_uq, 'l2_uq_dx', mode='nt')
    sch.push('l2_w_ukv', _uncols(mm(sch, ckvn2, dkv2, 'l2_ukv_dw', mode='tn', out_dtype=BF)), False, 1)
    dw_uq = mm(sch, cqn2, dqc2, 'l2_uq_dw', mode='tn', out_dtype=BF)
    dw_uq = dw_uq.reshape(Q_LORA, MLA_HEADS, MLA_QK)[:, :, :MLA_NOPE + MLA_ROPE].reshape(Q_LORA, -1)
    sch.push('l2_w_uq', _uncols(dw_uq), False, 1)
    dc2, dqn, dkvn = rw_bwd(sch, 0.05, _f_mla_pre, [c2] + tabs, p2, [dcqn2, dckvn2, dkrr2], name='l2_pre',
                            nd_rows=(1, 2, 3), row_dtypes=[BF])
    gS['l2_q_norm'], gS['l2_kv_norm'] = dqn.reshape(-1), dkvn.reshape(-1)
    dh2 = mm(sch, dg2, w2_g, 'l2_in_g_dx', mode='nt', add=dh2)
    dh2 = mm(sch, dc2, w2_lat, 'l2_in_c_dx', mode='nt', add=dh2)
    hb2_t = hb2.T
    dw2_g = mm(sch, hb2_t, dg2, 'l2_in_g_dw', mode='nn', out_dtype=BF)
    dw2_lat = mm(sch, hb2_t, dc2, 'l2_in_c_dw', mode='nn', out_dtype=BF)
    n_lat = Q_LORA + KV_LORA + MLA_ROPE
    sch.push('l2_w_in', _uncols(jnp.concatenate([dw2_g, dw2_lat[:, :n_lat]], axis=1)), False, 1)

    dh1, dbr1 = ln_bwd(h1, br1, dh2, 1)
    dz1 = mm(sch, dbr1, w1_out, 'l1_out_dx', mode='nt')
    sch.push('l1_w_out', mm(sch, z1.T, dbr1, 'l1_out_dw', mode='nn', out_dtype=BF).reshape(N_DEV, -1, d_model),
             False, 1)
    res = rw_bwd(sch, 0.11, _f_gate_pool, mixed1 + [g1], p1, [dz1], name='l1_gate', row_dtypes=[BF] * (n_win + 1))
    dmixed1, dg1, dscale = res[:n_win], res[n_win], res[n_win + 1]
    gS['l1_scale'] = dscale.reshape(-1)
    dps1 = [mm(sch, dmixed1[gi], w1_grp[gi], f'l1_grp{gi}_dx', mode='nt') for gi in range(n_win)]
    dw_grp = jnp.stack([mm(sch, ps1[gi], dmixed1[gi], f'l1_grp{gi}_dw', mode='tn', out_dtype=BF)
                        for gi in range(n_win)])
    sch.push('l1_w_grp', _unheads(dw_grp), False, 2)
    (du1,) = rw_bwd(sch, 0.09, f_pool, [u1, tidx], [], dps1, name='l1_pool', halo=(0,), nd_rows=(1,), tt=tt,
                    row_dtypes=[BF])
    dh1 = mm(sch, du1, w1_u, 'l1_in_u_dx', mode='nt', add=dh1)
    dh1 = mm(sch, dg1, w1_g, 'l1_in_g_dx', mode='nt', add=dh1)
    hb1_t = hb1.T
    dw1 = [mm(sch, hb1_t, du1, 'l1_in_u_dw', mode='nn', out_dtype=BF),
           mm(sch, hb1_t, dg1, 'l1_in_g_dw', mode='nn', out_dtype=BF)]
    sch.push('l1_w_in', _uncols(jnp.concatenate(dw1, axis=1)), False, 1)

    dh0, dbr0 = ln_bwd(h0, br0, dh1, 0)
    dz0 = mm(sch, dbr0, w0_out, 'l0_out_dx', mode='nt')
    sch.push('l0_w_out', mm(sch, z0.T, dbr0, 'l0_out_dw', mode='nn', out_dtype=BF).reshape(N_DEV, -1, d_model),
             False, 1)
    dhs0, dg0 = rw_bwd(sch, 0.1, _f_gate, [hs0, g0], [], [dz0], name='l0_gate', row_dtypes=[F32, BF])
    a_next = jnp.concatenate([a0[1:], jnp.ones_like(a0[:1])], axis=0)
    hs_prev = jnp.concatenate([jnp.zeros_like(hs0[:1]), hs0[:-1]], axis=0)
    dxin0, da0 = _ride(sch, 0.25, _scan_call, a=a_next, b=dhs0, mul=hs_prev, reverse=True, name='l0_scan_bwd')
    res = rw_bwd(sch, 0.3, f_lru, [u0], p0, [da0, dxin0], name='l0_lru', halo=(0,), tt=tt, row_dtypes=[BF])
    du0 = res[0]
    gS['l0_conv_b'], gS['l0_b_a'], gS['l0_b_x'], gS['l0_lam'] = [res[k].reshape(-1) for k in (2, 4, 6, 7)]
    sch.push('l0_w_a', _unheads(res[3]), False, 2)
    sch.push('l0_w_x', _unheads(res[5]), False, 2)
    sch.push('l0_conv_w', jnp.transpose(res[1].reshape(CONV_W, 1, N_DEV, -1), (2, 0, 1, 3)), False, 3)
    hb0_t = hb0.T
    dw0 = [mm(sch, hb0_t, du0, 'l0_in_u_dw', mode='nn', out_dtype=BF),
           mm(sch, hb0_t, dg0, 'l0_in_g_dw', mode='nn', out_dtype=BF)]
    sch.push('l0_w_in', _uncols(jnp.concatenate(dw0, axis=1)), False, 1)
    sch.scale = 2.0
    dh0 = mm(sch, du0, w0_u, 'l0_in_u_dx', mode='nt', add=dh0)
    dh0 = mm(sch, dg0, w0_g, 'l0_in_g_dx', mode='nt', add=dh0)
    sch.push('meta_tokens', _uncols(dh0[:N_META]), False, 1)

    return loss[0, 0], dh0[N_META:t_real], gS


def _as2d(a):
    return a.reshape(-1, a.shape[-1])


def kernel(x, meta_tokens, l0_w_in, l0_conv_w, l0_conv_b, l0_w_a, l0_b_a, l0_w_x, l0_b_x, l0_lam, l0_w_out, l0_ln_g, l0_ln_b, l1_w_in, l1_w_grp, l1_scale, l1_w_out, l1_ln_g, l1_ln_b, l2_w_in, l2_q_norm, l2_w_uq, l2_kv_norm, l2_w_ukv, l2_w_out, l2_ln_g, l2_ln_b, l3_w_in, l3_w_out, l3_ln_g, l3_ln_b, loss_target, m_meta_tokens, m_l0_w_in, m_l0_conv_w, m_l0_conv_b, m_l0_w_a, m_l0_b_a, m_l0_w_x, m_l0_b_x, m_l0_lam, m_l0_w_out, m_l0_ln_g, m_l0_ln_b, m_l1_w_in, m_l1_w_grp, m_l1_scale, m_l1_w_out, m_l1_ln_g, m_l1_ln_b, m_l2_w_in, m_l2_q_norm, m_l2_w_uq, m_l2_kv_norm, m_l2_w_ukv, m_l2_w_out, m_l2_ln_g, m_l2_ln_b, m_l3_w_in, m_l3_w_out, m_l3_ln_g, m_l3_ln_b, v_meta_tokens, v_l0_w_in, v_l0_conv_w, v_l0_conv_b, v_l0_w_a, v_l0_b_a, v_l0_w_x, v_l0_b_x, v_l0_lam, v_l0_w_out, v_l0_ln_g, v_l0_ln_b, v_l1_w_in, v_l1_w_grp, v_l1_scale, v_l1_w_out, v_l1_ln_g, v_l1_ln_b, v_l2_w_in, v_l2_q_norm, v_l2_w_uq, v_l2_kv_norm, v_l2_w_ukv, v_l2_w_out, v_l2_ln_g, v_l2_ln_b, v_l3_w_in, v_l3_w_out, v_l3_ln_g, v_l3_ln_b):
    args = (meta_tokens, l0_w_in, l0_conv_w, l0_conv_b, l0_w_a, l0_b_a, l0_w_x, l0_b_x, l0_lam, l0_w_out, l0_ln_g, l0_ln_b, l1_w_in, l1_w_grp, l1_scale, l1_w_out, l1_ln_g, l1_ln_b, l2_w_in, l2_q_norm, l2_w_uq, l2_kv_norm, l2_w_ukv, l2_w_out, l2_ln_g, l2_ln_b, l3_w_in, l3_w_out, l3_ln_g, l3_ln_b)
    moms = (m_meta_tokens, m_l0_w_in, m_l0_conv_w, m_l0_conv_b, m_l0_w_a, m_l0_b_a, m_l0_w_x, m_l0_b_x, m_l0_lam, m_l0_w_out, m_l0_ln_g, m_l0_ln_b, m_l1_w_in, m_l1_w_grp, m_l1_scale, m_l1_w_out, m_l1_ln_g, m_l1_ln_b, m_l2_w_in, m_l2_q_norm, m_l2_w_uq, m_l2_kv_norm, m_l2_w_ukv, m_l2_w_out, m_l2_ln_g, m_l2_ln_b, m_l3_w_in, m_l3_w_out, m_l3_ln_g, m_l3_ln_b)
    vels = (v_meta_tokens, v_l0_w_in, v_l0_conv_w, v_l0_conv_b, v_l0_w_a, v_l0_b_a, v_l0_w_x, v_l0_b_x, v_l0_lam, v_l0_w_out, v_l0_ln_g, v_l0_ln_b, v_l1_w_in, v_l1_w_grp, v_l1_scale, v_l1_w_out, v_l1_ln_g, v_l1_ln_b, v_l2_w_in, v_l2_q_norm, v_l2_w_uq, v_l2_kv_norm, v_l2_w_ukv, v_l2_w_out, v_l2_ln_g, v_l2_ln_b, v_l3_w_in, v_l3_w_out, v_l3_ln_g, v_l3_ln_b)
    W = dict(zip(WEIGHTS, args))
    M = dict(zip(WEIGHTS, moms))
    V = dict(zip(WEIGHTS, vels))

    seq = x.shape[1]
    t_real = N_META + seq
    t_pad = -(-t_real // ROW_ALIGN) * ROW_ALIGN
    tgt_pad = jnp.pad(loss_target[0], ((N_META, t_pad - t_real), (0, 0)))

    sch = _Schedule()
    for n in GATHER_ORDER:
        shard = W[n].astype(BF) if n in BIG else W[n]
        sch.push(n, shard, True, shard.ndim - 2)
    S = {n: W[n] for n in REPLICATED}

    loss, gx, gS = _train_local(sch, x[0], tgt_pad, S, t_pad=t_pad)

    flat = jnp.concatenate([gS[n].reshape(-1) for n in REPLICATED]).reshape(-1, 128)
    sch.push('small_grads', flat, True, 0)

    out_g, out_d, out_m, out_v = {}, {}, {}, {}
    order = ['l3_w_out', 'l3_w_in', 'l2_w_out', 'l2_w_ukv', 'l2_w_uq', 'l2_w_in', 'l1_w_out', 'l1_w_grp',
             'l1_w_in', 'l0_w_out', 'l0_w_a', 'l0_w_x', 'l0_conv_w', 'l0_w_in', 'meta_tokens']
    for n in order:
        shp = W[n].shape
        w2, m2, v2 = _as2d(W[n]), _as2d(M[n]), _as2d(V[n])
        parts = [p.reshape((N_DEV, -1, w2.shape[1])) for p in sch.get(n)]
        res = _ride(sch, w2.size * 1.5e-8, _adamw_call, contribs=parts, w=w2, m=m2, v=v2, name='adamw_' + n)
        out_g[n], out_d[n], out_m[n], out_v[n] = [r.reshape(shp) for r in res]
    cat = lambda D: jnp.concatenate([D[n].reshape(-1) for n in REPLICATED]).reshape(-1, 128)
    res = _adamw_call(sch.get('small_grads'), cat(W), cat(M), cat(V), name='adamw_small')[0]
    off = 0
    for n in REPLICATED:
        size = W[n].size
        for dst, r in zip((out_g, out_d, out_m, out_v), res):
            dst[n] = r.reshape(-1)[off:off + size].reshape(W[n].shape)
        off += size
    sch.flush()

    loss = lax.psum(loss, ("x", "y", "c"))
    return (loss, gx[None], *[out_g[n] for n in WEIGHTS], *[out_d[n] for n in WEIGHTS],
            *[out_m[n] for n in WEIGHTS], *[out_v[n] for n in WEIGHTS])
```

```python
import functools

import jax
import jax.numpy as jnp
from jax import lax
from jax.experimental import pallas as pl
from jax.experimental.pallas import tpu as pltpu

F32 = jnp.float32
BF = jnp.bfloat16

N_DEV = 8
N_META = 16
ALPHA = (2.0 * 4) ** 0.25
LN_EPS = 1e-5
RMS_EPS = 1e-6
ROPE_BASE = 10000.0
LRU_HEADS = 16
CONV_W = 4
LRU_C = 8.0
POOL_WINDOWS = (2, 4, 8, 16)
MLA_HEADS = 32
MLA_NOPE = 128
MLA_ROPE = 64
MLA_QK = 256
Q_LORA = 1024
KV_LORA = 512
RET_HEADS = 16
ADAM_LR = 0.001
ADAM_B1 = 0.9
ADAM_B2 = 0.999
ADAM_EPS = 1e-08
ADAM_WD = 0.01
ADAM_STEP = 10

ROW_ALIGN = 128
VMEM_LIMIT_BYTES = 56 * 1024 * 1024
TT_PREFS = (128, 64, 32, 16, 8)
ATT_PREFS = (384, 256, 128)
MM_M_PREFS = (1408, 1024, 512, 384, 256, 128)
MM_N_PREFS = (1024, 640, 512, 384, 256, 128)
MM_K_PREFS = (1408, 1024, 512, 384, 256, 128)
MM_VMEM_BUDGET = 40 * 1024 * 1024
ADAM_BLOCK_ELEMS = 128 * 1024
EXCH_MS_PER_MB = 0.0857
EXCH_CHUNK_MS = 0.1
MXU_FLOPS_PER_MS = 7.8e11
BWD_RIDER_SHARE = 0.6

WEIGHTS = ['meta_tokens', 'l0_w_in', 'l0_conv_w', 'l0_conv_b', 'l0_w_a', 'l0_b_a', 'l0_w_x', 'l0_b_x', 'l0_lam',
           'l0_w_out', 'l0_ln_g', 'l0_ln_b', 'l1_w_in', 'l1_w_grp', 'l1_scale', 'l1_w_out', 'l1_ln_g', 'l1_ln_b',
           'l2_w_in', 'l2_q_norm', 'l2_w_uq', 'l2_kv_norm', 'l2_w_ukv', 'l2_w_out', 'l2_ln_g', 'l2_ln_b',
           'l3_w_in', 'l3_w_out', 'l3_ln_g', 'l3_ln_b']
BIG = ['l0_w_in', 'l0_w_out', 'l1_w_in', 'l1_w_grp', 'l1_w_out', 'l2_w_in', 'l2_w_uq', 'l2_w_ukv', 'l2_w_out',
       'l3_w_in', 'l3_w_out']
SHARDED_F32 = ['meta_tokens', 'l0_conv_w', 'l0_w_a', 'l0_w_x']
REPLICATED = [n for n in WEIGHTS if n not in BIG and n not in SHARDED_F32]
GATHER_ORDER = ['meta_tokens', 'l0_w_in', 'l0_conv_w', 'l0_w_a', 'l0_w_x', 'l0_w_out', 'l1_w_in', 'l1_w_grp',
                'l1_w_out', 'l2_w_in', 'l2_w_uq', 'l2_w_ukv', 'l2_w_out', 'l3_w_in', 'l3_w_out']


def _pick(n, prefs):
    for p in prefs:
        if n % p == 0:
            return p
    return n


def _exchange_copies(jobs, in_refs, out_refs, send_sems, recv_sems, local_sems):
    x, y, c = lax.axis_index("x"), lax.axis_index("y"), lax.axis_index("c")
    me = 4 * x + 2 * y + c
    sibling = (x, y, 1 - c)
    chips = [(1 - x, y), (x, 1 - y), (1 - x, 1 - y)]
    peers = [sibling] + [(px, py, c) for px, py in chips] + [(px, py, 1 - c) for px, py in chips]
    first, arrivals, forwards, rest = [], [], [], []

    def dev(px, py, pc):
        return 4 * px + 2 * py + pc

    for n, (_, gather) in enumerate(jobs):
        def remote(p, src, slot, to, n=n):
            k = n * (N_DEV - 1) + p
            return pltpu.make_async_remote_copy(
                src_ref=src, dst_ref=out_refs[n].at[slot], send_sem=send_sems.at[k], recv_sem=recv_sems.at[k],
                device_id=to, device_id_type=pl.DeviceIdType.MESH)

        if gather:
            first += [remote(p, in_refs[n], me, peers[p]) for p in range(4)]
            for j, (px, py) in enumerate(chips):
                landed = out_refs[n].at[dev(px, py, c)]
                arrivals.append(remote(1 + j, landed, dev(px, py, c), peers[1 + j]))
                forwards.append(remote(4 + j, landed, dev(px, py, c), sibling))
        else:
            first += [remote(p, in_refs[n].at[dev(*peers[p])], me, peers[p]) for p in range(N_DEV - 1)]
        rest.append(pltpu.make_async_copy(in_refs[n] if gather else in_refs[n].at[me], out_refs[n].at[me],
                                          local_sems.at[n]))

    def start():
        for cp in first + rest:
            cp.start()

    def finish():
        for arrived, forward in zip(arrivals, forwards):
            arrived.wait_recv()
            forward.start()
        for cp in first:
            cp.wait_send()
        for n, (_, gather) in enumerate(jobs):
            for p in range(N_DEV - 1):
                if not (gather and 1 <= p <= 3):
                    k = n * (N_DEV - 1) + p
                    pltpu.make_async_remote_copy(
                        src_ref=out_refs[n].at[me], dst_ref=out_refs[n].at[me], send_sem=send_sems.at[k],
                        recv_sem=recv_sems.at[k], device_id=sibling, device_id_type=pl.DeviceIdType.MESH).wait_recv()
        for cp in forwards:
            cp.wait_send()
        for cp in rest:
            cp.wait()

    return start, finish


def _exchange_shapes(jobs):
    shapes = []
    for arr, gather in jobs:
        blk = arr.shape if gather else arr.shape[1:]
        shapes.append(jax.ShapeDtypeStruct((N_DEV,) + tuple(blk), arr.dtype))
    return shapes


def _exchange_scratch(jobs):
    n = len(jobs)
    return [pltpu.SemaphoreType.DMA((n * (N_DEV - 1),)), pltpu.SemaphoreType.DMA((n * (N_DEV - 1),)),
            pltpu.SemaphoreType.DMA((n,))]


def _call(body, *, name, grid, in_specs, out_specs, out_shape, args, scratch=(), sem=None, jobs=()):
    jobs = list(jobs)
    n_in, n_out, n_sc, n_job = len(in_specs), len(out_specs), len(scratch), len(jobs)
    hbm = pl.BlockSpec(memory_space=pl.ANY)

    def kern(*refs):
        ins = refs[:n_in]
        job_ins = refs[n_in:n_in + n_job]
        pos = n_in + n_job
        outs = refs[pos:pos + n_out]
        job_outs = refs[pos + n_out:pos + n_out + n_job]
        pos += n_out + n_job
        scr = refs[pos:pos + n_sc]
        if n_job:
            ids = [pl.program_id(d) for d in range(len(grid))]
            first = functools.reduce(jnp.logical_and, [i == 0 for i in ids])
            last = functools.reduce(jnp.logical_and, [i == g - 1 for i, g in zip(ids, grid)])
            start, finish = _exchange_copies(jobs, job_ins, job_outs, *refs[pos + n_sc:])
            pl.when(first)(start)

        body(ins, outs, scr)
        if n_job:
            pl.when(last)(finish)

    kw = dict(vmem_limit_bytes=VMEM_LIMIT_BYTES)
    if sem is not None:
        kw['dimension_semantics'] = tuple("arbitrary" for _ in grid) if n_job else sem
    res = pl.pallas_call(
        kern, name=name, grid=grid,
        in_specs=list(in_specs) + [hbm] * n_job,
        out_specs=list(out_specs) + [hbm] * n_job,
        out_shape=list(out_shape) + _exchange_shapes(jobs),
        scratch_shapes=list(scratch) + (_exchange_scratch(jobs) if n_job else []),
        compiler_params=pltpu.CompilerParams(**kw),
    )(*args, *[a for a, _ in jobs])
    return list(res[:n_out]), list(res[n_out:])


def _exchange_alone(jobs, name):
    def body(*refs):
        n = len(jobs)
        start, finish = _exchange_copies(jobs, refs[:n], refs[n:2 * n], *refs[2 * n:])
        start()
        finish()

    hbm = pl.BlockSpec(memory_space=pl.ANY)
    return list(pl.pallas_call(
        body, name=name, in_specs=[hbm] * len(jobs), out_specs=[hbm] * len(jobs),
        out_shape=_exchange_shapes(jobs), scratch_shapes=_exchange_scratch(jobs),
    )(*[a for a, _ in jobs]))


class _Schedule:
    def __init__(self):
        self.pending = []
        self.done = {}
        self.chunks = {}
        self.count = 0
        self.scale = 1.0

    def push(self, name, arr, gather, row_axis):
        mb = arr.size * arr.dtype.itemsize / (1 if gather else N_DEV) / 1e6
        cost = mb * EXCH_MS_PER_MB * (0.5 if gather else 1.0)
        rows = arr.shape[row_axis]
        n = 1
        leading = row_axis == (0 if gather else 1)
        target = EXCH_CHUNK_MS if gather else 2 * EXCH_CHUNK_MS
        while leading and cost / n > target and rows % (2 * n) == 0 and rows // (2 * n) >= 16:
            n *= 2
        self.chunks[name] = n
        step = rows // n
        for k in range(n):
            piece = lax.slice_in_dim(arr, k * step, (k + 1) * step, axis=row_axis) if n > 1 else arr
            self.pending.append(((name, k), piece, gather, cost / n))

    def take(self, budget):
        jobs, spent = [], 0.0
        budget *= self.scale
        while self.pending and spent + 0.5 * self.pending[0][3] <= budget:
            job = self.pending.pop(0)
            jobs.append(job)
            spent += job[3]
        return jobs

    def deliver(self, jobs, results):
        for (key, _, _, _), r in zip(jobs, results):
            self.done[key] = r

    def get(self, name):
        mine = [j for j in self.pending if j[0][0] == name]
        if mine:
            self.pending = [j for j in self.pending if j[0][0] != name]
            self.count += 1
            self.deliver(mine, _exchange_alone([(j[1], j[2]) for j in mine], f'exchange_{self.count}_{name}'))
        parts = [self.done.pop((name, k)) for k in range(self.chunks[name])]
        return parts

    def flush(self):
        if self.pending:
            jobs, self.pending = self.pending, []
            self.count += 1
            self.deliver(jobs, _exchange_alone([(j[1], j[2]) for j in jobs], f'exchange_{self.count}_rest'))


def _ride(sch, budget, fn, **kw):
    jobs = sch.take(budget) if sch is not None else []
    outs, exch = fn(jobs=[(j[1], j[2]) for j in jobs], **kw)
    if jobs:
        sch.deliver(jobs, exch)
    return outs


@functools.partial(jax.custom_vjp, nondiff_argnums=(1, 2))
def _roll(x, shift, axis):
    return pltpu.roll(x, shift, axis)


def _roll_fwd(x, shift, axis):
    return pltpu.roll(x, shift, axis), None


def _roll_bwd(shift, axis, _, g):
    n = g.shape[axis]
    return (pltpu.roll(g, (n - shift) % n, axis),)


_roll.defvjp(_roll_fwd, _roll_bwd)


@jax.custom_vjp
def _bdot(x, w):
    return jnp.dot(x.astype(BF), w.astype(BF), preferred_element_type=F32)


def _bdot_fwd(x, w):
    return _bdot(x, w), (x, w)


def _bdot_bwd(res, g):
    x, w = res
    gb = g.astype(BF)
    dx = lax.dot_general(gb, w.astype(BF), (((1,), (1,)), ((), ())), preferred_element_type=F32)
    dw = lax.dot_general(x.astype(BF), gb, (((0,), (0,)), ((), ())), preferred_element_type=F32)
    return dx, dw


_bdot.defvjp(_bdot_fwd, _bdot_bwd)


def _silu(g):
    return g * jax.nn.sigmoid(g)


def _softplus(x):
    return jnp.maximum(x, 0.0) + jnp.log1p(jnp.exp(-jnp.abs(x)))


def _mm_call(a, b, *, mode, out_dtype, name, add=None, jobs=()):
    if mode == 'nn':
        (mo, kc), (_, no) = a.shape, b.shape
    elif mode == 'nt':
        (mo, kc), (no, _) = a.shape, b.shape
    else:
        (kc, mo), (_, no) = a.shape, b.shape
    tm = _pick(mo, MM_M_PREFS)
    tn = no if no <= 2048 and no % 512 != 0 else _pick(no, MM_N_PREFS)
    out_bytes = tm * tn * (2 * jnp.dtype(out_dtype).itemsize + 4 + (8 if add is not None else 0))
    tk = None
    for cand in MM_K_PREFS:
        if kc % cand == 0:
            tk = cand
            if out_bytes + 2 * cand * (tm * a.dtype.itemsize + tn * b.dtype.itemsize) <= MM_VMEM_BUDGET:
                break
    tk = tk or kc
    nk = kc // tk
    if mode == 'nn':
        a_spec = pl.BlockSpec((tm, tk), lambda i, j, k: (i, k))
        b_spec = pl.BlockSpec((tk, tn), lambda i, j, k: (k, j))
        dims = (((1,), (0,)), ((), ()))
    elif mode == 'nt':
        a_spec = pl.BlockSpec((tm, tk), lambda i, j, k: (i, k))
        b_spec = pl.BlockSpec((tn, tk), lambda i, j, k: (j, k))
        dims = (((1,), (1,)), ((), ()))
    else:
        a_spec = pl.BlockSpec((tk, tm), lambda i, j, k: (k, i))
        b_spec = pl.BlockSpec((tk, tn), lambda i, j, k: (k, j))
        dims = (((0,), (0,)), ((), ()))
    o_spec = pl.BlockSpec((tm, tn), lambda i, j, k: (i, j))

    def body(ins, outs, scr):
        a_ref, b_ref = ins[0], ins[1]
        (o_ref,), (acc_ref,) = outs, scr
        k = pl.program_id(2)

        @pl.when(k == 0)
        def _():
            acc_ref[...] = jnp.zeros_like(acc_ref) if add is None else ins[2][...]

        acc_ref[...] += lax.dot_general(a_ref[...].astype(BF), b_ref[...].astype(BF), dims,
                                        preferred_element_type=F32)

        @pl.when(k == nk - 1)
        def _():
            o_ref[...] = acc_ref[...].astype(o_ref.dtype)

    outs, exch = _call(
        body, name=name, grid=(mo // tm, no // tn, nk),
        in_specs=[a_spec, b_spec] + ([o_spec] if add is not None else []),
        out_specs=[o_spec], out_shape=[jax.ShapeDtypeStruct((mo, no), out_dtype)],
        args=[a, b] + ([add] if add is not None else []),
        scratch=[pltpu.VMEM((tm, tn), F32)], sem=("parallel", "parallel", "arbitrary"), jobs=jobs)
    return outs, exch


def mm(sch, a, b, name, mode='nn', out_dtype=F32, add=None):
    if mode == 'nn':
        flops = 2.0 * a.shape[0] * a.shape[1] * b.shape[1]
    elif mode == 'nt':
        flops = 2.0 * a.shape[0] * a.shape[1] * b.shape[0]
    else:
        flops = 2.0 * a.shape[0] * a.shape[1] * b.shape[1]
    return _ride(sch, flops / MXU_FLOPS_PER_MS, _mm_call, a=a, b=b, mode=mode, out_dtype=out_dtype, name=name,
                 add=add)[0]


def _full_spec(p):
    nd = p.ndim
    return pl.BlockSpec(p.shape, lambda i: (0,) * nd)


def _load_rows(refs, n_rows, halo, step_is_first):
    cur, prev, pos = [], [], 0
    for r in range(n_rows):
        cur.append(refs[pos][...].astype(F32))
        pos += 1
        if r in halo:
            keep = jnp.where(step_is_first, 0.0, 1.0).astype(F32)
            prev.append(refs[pos][...].astype(F32) * keep)
            pos += 1
        else:
            prev.append(None)
    return cur, prev, pos


def _join(cur, prev):
    return [c if p is None else jnp.concatenate([p, c], axis=0) for c, p in zip(cur, prev)]


def _rw_fwd(f, rows, params, outs, *, name, n_reduce=0, halo=(), tt=None, jobs=()):
    t_len = rows[0].shape[0]
    tt = tt or _pick(t_len, TT_PREFS)
    nt = t_len // tt
    n_rows, n_par, n_out = len(rows), len(params), len(outs)

    def body(ins, orefs, scr):
        i = pl.program_id(0)
        cur, prev, pos = _load_rows(ins, n_rows, halo, i == 0)
        pvals = [ins[pos + k][...] for k in range(n_par)]
        res = f(_join(cur, prev), pvals)
        n_f = len(res) - n_reduce
        for k, (_, _, src) in enumerate(outs):
            orefs[k][...] = res[src].astype(orefs[k].dtype)
        for k in range(n_reduce):
            ref, val = orefs[n_out + k], res[n_f + k]

            @pl.when(i == 0)
            def _():
                ref[...] = val

            @pl.when(i > 0)
            def _():
                ref[...] += val

    in_specs, args = [], []
    for r, x in enumerate(rows):
        c = x.shape[1]
        in_specs.append(pl.BlockSpec((tt, c), lambda i: (i, 0)))
        args.append(x)
        if r in halo:
            in_specs.append(pl.BlockSpec((tt, c), lambda i: (jnp.maximum(i - 1, 0), 0)))
            args.append(x)
    for p in params:
        in_specs.append(_full_spec(p))
        args.append(p)
    out_specs = [pl.BlockSpec((tt, c), lambda i: (i, 0)) for c, _, _ in outs]
    out_shape = [jax.ShapeDtypeStruct((t_len, c), dt) for c, dt, _ in outs]
    for _ in range(n_reduce):
        out_specs.append(pl.BlockSpec((1, 1), lambda i: (0, 0)))
        out_shape.append(jax.ShapeDtypeStruct((1, 1), F32))
    return _call(body, name=name, grid=(nt,), in_specs=in_specs, out_specs=out_specs, out_shape=out_shape,
                 args=args, sem=("arbitrary",), jobs=jobs)


def _rw_bwd(f, rows, params, cts, *, name, n_reduce=0, halo=(), nd_rows=(), nd_params=(), tt=None,
            row_dtypes=None, jobs=()):
    t_len = rows[0].shape[0]
    tt = tt or _pick(t_len, TT_PREFS)
    nt = t_len // tt
    n_rows, n_par, n_ct = len(rows), len(params), len(cts)
    d_rows = [r for r in range(n_rows) if r not in nd_rows]
    d_pars = [k for k in range(n_par) if k not in nd_params]
    h_rows = [r for r in d_rows if r in halo]

    def blk(j):
        return nt - 1 - j

    def body(ins, orefs, carry_refs):
        j = pl.program_id(0)
        cur, prev, pos = _load_rows(ins, n_rows, halo, blk(j) == 0)
        pvals = [ins[pos + k][...] for k in range(n_par)]
        pos += n_par
        ct_vals = [ins[pos + k][...].astype(F32) for k in range(n_ct)]

        def g(dcur, dprev, dpar):
            c, p, q = list(cur), list(prev), list(pvals)
            for r, v in zip(d_rows, dcur):
                c[r] = v
            for r, v in zip(h_rows, dprev):
                p[r] = v
            for k, v in zip(d_pars, dpar):
                q[k] = v
            return tuple(f(_join(c, p), q))

        _, vjp = jax.vjp(g, [cur[r] for r in d_rows], [prev[r] for r in h_rows], [pvals[k] for k in d_pars])
        g_cur, g_prev, g_par = vjp(tuple(ct_vals))

        for n, r in enumerate(d_rows):
            if r in halo:
                cref = carry_refs[h_rows.index(r)]

                @pl.when(j == 0)
                def _():
                    cref[...] = jnp.zeros_like(cref)

                orefs[n][...] = (g_cur[n] + cref[...]).astype(orefs[n].dtype)
                cref[...] = g_prev[h_rows.index(r)]
            else:
                orefs[n][...] = g_cur[n].astype(orefs[n].dtype)
        for n in range(len(d_pars)):
            ref, val = orefs[len(d_rows) + n], g_par[n]

            @pl.when(j == 0)
            def _():
                ref[...] = val

            @pl.when(j > 0)
            def _():
                ref[...] += val

    in_specs, args = [], []
    for r, x in enumerate(rows):
        c = x.shape[1]
        in_specs.append(pl.BlockSpec((tt, c), lambda j: (blk(j), 0)))
        args.append(x)
        if r in halo:
            in_specs.append(pl.BlockSpec((tt, c), lambda j: (jnp.maximum(blk(j) - 1, 0), 0)))
            args.append(x)
    for p in params:
        in_specs.append(_full_spec(p))
        args.append(p)
    for ct in cts:
        if ct.shape == (1, 1):
            in_specs.append(pl.BlockSpec((1, 1), lambda j: (0, 0)))
        else:
            in_specs.append(pl.BlockSpec((tt, ct.shape[1]), lambda j: (blk(j), 0)))
        args.append(ct)
    out_specs, out_shape, scratch = [], [], []
    for n, r in enumerate(d_rows):
        c = rows[r].shape[1]
        out_specs.append(pl.BlockSpec((tt, c), lambda j: (blk(j), 0)))
        out_shape.append(jax.ShapeDtypeStruct((t_len, c), row_dtypes[n] if row_dtypes else F32))
        if r in halo:
            scratch.append(pltpu.VMEM((tt, c), F32))
    for k in d_pars:
        out_specs.append(_full_spec(params[k]))
        out_shape.append(jax.ShapeDtypeStruct(params[k].shape, F32))
    return _call(body, name=name + '_bwd', grid=(nt,), in_specs=in_specs, out_specs=out_specs,
                 out_shape=out_shape, args=args, scratch=scratch, sem=("arbitrary",), jobs=jobs)


def rw_fwd(sch, budget, f, rows, params, outs, **kw):
    return _ride(sch, budget, functools.partial(_rw_fwd, f, list(rows), list(params), outs), **kw)


def rw_bwd(sch, budget, f, rows, params, cts, **kw):
    return _ride(sch, budget, functools.partial(_rw_bwd, f, list(rows), list(params), list(cts)), **kw)


def _scan_call(a, b, mul, *, reverse, name, jobs=()):
    t_len, c_len = a.shape
    tt = _pick(t_len, TT_PREFS)
    tc = _pick(c_len, (512, 256, 128))
    nt = t_len // tt

    def body(ins, orefs, scr):
        (carry,) = scr
        t = pl.program_id(1)
        av, bv = ins[0][...], ins[1][...]
        row = lax.broadcasted_iota(jnp.int32, av.shape, 0)
        s = 1
        while s < tt:
            if reverse:
                ok = row < tt - s
                a_sh = jnp.where(ok, pltpu.roll(av, tt - s, 0), 1.0)
                b_sh = jnp.where(ok, pltpu.roll(bv, tt - s, 0), 0.0)
            else:
                ok = row >= s
                a_sh = jnp.where(ok, pltpu.roll(av, s, 0), 1.0)
                b_sh = jnp.where(ok, pltpu.roll(bv, s, 0), 0.0)
            bv = av * b_sh + bv
            av = av * a_sh
            s *= 2

        @pl.when(t == 0)
        def _():
            carry[...] = jnp.zeros_like(carry)

        hs = bv + av * carry[...]
        orefs[0][...] = hs
        edge = 0 if reverse else tt - 1
        carry[...] = orefs[0][edge:edge + 1, :]
        if mul is not None:
            orefs[1][...] = hs * ins[2][...]

    def idx(c, t):
        return ((nt - 1 - t) if reverse else t, c)

    spec = pl.BlockSpec((tt, tc), idx)
    n_in, n_out = (2, 1) if mul is None else (3, 2)
    return _call(body, name=name, grid=(c_len // tc, nt), in_specs=[spec] * n_in, out_specs=[spec] * n_out,
                 out_shape=[jax.ShapeDtypeStruct((t_len, c_len), F32)] * n_out,
                 args=[a, b] if mul is None else [a, b, mul],
                 scratch=[pltpu.VMEM((1, tc), F32)], sem=("parallel", "arbitrary"), jobs=jobs)


NT = (((1,), (1,)), ((), ()))


def _for_blocks(lo, hi, fn):
    n = hi - lo

    def two(p, c):
        fn(lo + 2 * p)
        fn(lo + 2 * p + 1)
        return c

    lax.fori_loop(0, n // 2, two, 0)

    @pl.when(n % 2 == 1)
    def _():
        fn(hi - 1)


def _att_weights(s, rel, diag, lg, softmax, scale, t_axis):
    row = lax.broadcasted_iota(jnp.int32, s.shape, t_axis)
    col = lax.broadcasted_iota(jnp.int32, s.shape, 1 - t_axis)
    if softmax:
        s = s * scale
        return jnp.where(row >= col, s, -1e30) if diag else s
    diff = (rel + row - col).astype(F32)
    dec = jnp.exp(jnp.maximum(diff, 0.0) * lg)
    return jnp.where(diff >= 0.0, dec, 0.0) if diag else dec


def _att_fwd_call(q, k, v, lgt, *, heads, softmax, scale, name, jobs=()):
    t_len = q.shape[0]
    dqk, dv = q.shape[1] // heads, v.shape[1] // heads
    blk = _pick(t_len, ATT_PREFS)
    nb = t_len // blk

    lanes = 128
    n_fold = blk // lanes

    def body(ins, orefs, scr):
        q_ref, k_ref, v_ref, lg_ref = ins
        o_ref, lse_ref = orefs
        s_sc, m_sc, acc_sc = scr
        i = pl.program_id(1)
        lg = lg_ref[0:1, 0:1]
        qb = q_ref[...]
        acc_sc[...] = jnp.zeros_like(acc_sc)

        def block_rows(j):
            return pl.ds(pl.multiple_of(j * blk, blk), blk)

        if not softmax:
            def step(j, diag):
                s = lax.dot_general(qb, k_ref[block_rows(j), :], NT, preferred_element_type=F32)
                w = _att_weights(s, (i - j) * blk, diag, lg, softmax, scale, 0)
                acc_sc[...] += jnp.dot((s * w).astype(BF), v_ref[block_rows(j), :], preferred_element_type=F32)

            _for_blocks(0, i, lambda j: step(j, False))
            step(i, True)
            o_ref[...] = acc_sc[...]
            lse_ref[...] = jnp.zeros_like(lse_ref)
            return

        m_sc[...] = jnp.full_like(m_sc, -1e30)

        def score(j, diag):
            s = lax.dot_general(qb, k_ref[block_rows(j), :], NT, preferred_element_type=F32)
            w = _att_weights(s, (i - j) * blk, diag, lg, softmax, scale, 0)
            s_sc[j] = w
            top = w[:, :lanes]
            for n in range(1, n_fold):
                top = jnp.maximum(top, w[:, n * lanes:(n + 1) * lanes])
            m_sc[...] = jnp.maximum(m_sc[...], top)

        _for_blocks(0, i, lambda j: score(j, False))
        score(i, True)
        m = jnp.max(m_sc[...], axis=-1, keepdims=True)
        ones = jnp.ones((blk, lanes), BF)

        def accumulate(j):
            p = jnp.exp(s_sc[j] - m).astype(BF)
            v_aug = jnp.concatenate([v_ref[block_rows(j), :], ones], axis=1)
            acc_sc[...] += jnp.dot(p, v_aug, preferred_element_type=F32)

        _for_blocks(0, i + 1, accumulate)
        acc = acc_sc[...]
        l = acc[:, dv:dv + 1]
        o_ref[...] = acc[:, :dv] / l
        lse_ref[...] = m + jnp.log(l)

    scratch = [pltpu.VMEM((nb, blk, blk), F32), pltpu.VMEM((blk, lanes), F32),
               pltpu.VMEM((blk, dv + lanes if softmax else dv), F32)]
    return _call(
        body, name=name, grid=(heads, nb),
        in_specs=[
            pl.BlockSpec((blk, dqk), lambda h, i: (i, h)),
            pl.BlockSpec((t_len, dqk), lambda h, i: (0, h)),
            pl.BlockSpec((t_len, dv), lambda h, i: (0, h)),
            pl.BlockSpec((None, 1, 128), lambda h, i: (h, 0, 0)),
        ],
        out_specs=[
            pl.BlockSpec((blk, dv), lambda h, i: (i, h)),
            pl.BlockSpec((None, blk, 1), lambda h, i: (h, i, 0)),
        ],
        out_shape=[jax.ShapeDtypeStruct((t_len, heads * dv), F32), jax.ShapeDtypeStruct((heads, t_len, 1), F32)],
        args=[q, k, v, lgt],
        scratch=scratch if softmax else [pltpu.VMEM((8, lanes), F32), scratch[1], scratch[2]],
        sem=("parallel", "arbitrary"), jobs=jobs)


def _att_dq_call(q, k, v, lgt, o, lse, do, *, heads, softmax, scale, name, jobs=()):
    t_len = q.shape[0]
    dqk, dv = q.shape[1] // heads, v.shape[1] // heads
    blk = _pick(t_len, ATT_PREFS)
    nb = t_len // blk

    def body(ins, orefs, scr):
        q_ref, k_ref, v_ref, lg_ref, o_ref, do_ref, lse_ref = ins
        dq_ref, delta_ref = orefs
        (acc,) = scr
        i = pl.program_id(1)
        lg = lg_ref[0:1, 0:1]
        qb = q_ref[...]
        do = do_ref[...]
        dob = do.astype(BF)
        delta = jnp.sum(do * o_ref[...], axis=-1, keepdims=True)
        lse = lse_ref[...]
        delta_ref[...] = delta
        acc[...] = jnp.zeros_like(acc)

        def step(j, diag):
            rows = pl.ds(pl.multiple_of(j * blk, blk), blk)
            kb = k_ref[rows, :]
            s = lax.dot_general(qb, kb, NT, preferred_element_type=F32)
            w = _att_weights(s, (i - j) * blk, diag, lg, softmax, scale, 0)
            dp = lax.dot_general(dob, v_ref[rows, :], NT, preferred_element_type=F32)
            ds = jnp.exp(w - lse) * (dp - delta) * scale if softmax else dp * w
            acc[...] += jnp.dot(ds.astype(BF), kb, preferred_element_type=F32)

        _for_blocks(0, i, lambda j: step(j, False))
        step(i, True)
        dq_ref[...] = acc[...]

    return _call(
        body, name=name + '_dq', grid=(heads, nb),
        in_specs=[
            pl.BlockSpec((blk, dqk), lambda h, i: (i, h)),
            pl.BlockSpec((t_len, dqk), lambda h, i: (0, h)),
            pl.BlockSpec((t_len, dv), lambda h, i: (0, h)),
            pl.BlockSpec((None, 1, 128), lambda h, i: (h, 0, 0)),
            pl.BlockSpec((blk, dv), lambda h, i: (i, h)),
            pl.BlockSpec((blk, dv), lambda h, i: (i, h)),
            pl.BlockSpec((None, blk, 1), lambda h, i: (h, i, 0)),
        ],
        out_specs=[pl.BlockSpec((blk, dqk), lambda h, i: (i, h)),
                   pl.BlockSpec((None, blk, 1), lambda h, i: (h, i, 0))],
        out_shape=[jax.ShapeDtypeStruct(q.shape, F32), jax.ShapeDtypeStruct((heads, t_len, 1), F32)],
        args=[q, k, v, lgt, o, do, lse],
        scratch=[pltpu.VMEM((blk, dqk), F32)], sem=("parallel", "arbitrary"), jobs=jobs)


def _att_dkv_call(q, k, v, lgt, lse_row, delta_row, do, *, heads, softmax, scale, name, jobs=()):
    t_len = q.shape[0]
    dqk, dv = q.shape[1] // heads, v.shape[1] // heads
    blk = _pick(t_len, ATT_PREFS)
    nb = t_len // blk

    def body(ins, orefs, scr):
        q_ref, k_ref, v_ref, lg_ref, do_ref, lse_ref, delta_ref = ins
        dk_acc, dv_acc = scr
        j = pl.program_id(1)
        lg = lg_ref[0:1, 0:1]
        kb, vb = k_ref[...], v_ref[...]
        dk_acc[...] = jnp.zeros_like(dk_acc)
        dv_acc[...] = jnp.zeros_like(dv_acc)

        def step(i, diag):
            rows = pl.ds(pl.multiple_of(i * blk, blk), blk)
            qb = q_ref[rows, :]
            dob = do_ref[rows, :].astype(BF)
            s = lax.dot_general(kb, qb, NT, preferred_element_type=F32)
            w = _att_weights(s, (i - j) * blk, diag, lg, softmax, scale, 1)
            dp = lax.dot_general(vb, dob, NT, preferred_element_type=F32)
            if softmax:
                p = jnp.exp(w - lse_ref[:, rows])
                ds = p * (dp - delta_ref[:, rows]) * scale
            else:
                p, ds = s * w, dp * w
            dv_acc[...] += jnp.dot(p.astype(BF), dob, preferred_element_type=F32)
            dk_acc[...] += jnp.dot(ds.astype(BF), qb, preferred_element_type=F32)

        step(j, True)
        _for_blocks(j + 1, nb, lambda i: step(i, False))
        orefs[0][...] = dk_acc[...]
        orefs[1][...] = dv_acc[...]

    return _call(
        body, name=name + '_dkv', grid=(heads, nb),
        in_specs=[
            pl.BlockSpec((t_len, dqk), lambda h, j: (0, h)),
            pl.BlockSpec((blk, dqk), lambda h, j: (j, h)),
            pl.BlockSpec((blk, dv), lambda h, j: (j, h)),
            pl.BlockSpec((None, 1, 128), lambda h, j: (h, 0, 0)),
            pl.BlockSpec((t_len, dv), lambda h, j: (0, h)),
            pl.BlockSpec((None, 1, t_len), lambda h, j: (h, 0, 0)),
            pl.BlockSpec((None, 1, t_len), lambda h, j: (h, 0, 0)),
        ],
        out_specs=[
            pl.BlockSpec((blk, dqk), lambda h, j: (j, h)),
            pl.BlockSpec((blk, dv), lambda h, j: (j, h)),
        ],
        out_shape=[jax.ShapeDtypeStruct(k.shape, F32), jax.ShapeDtypeStruct(v.shape, F32)],
        args=[q, k, v, lgt, do, lse_row, delta_row],
        scratch=[pltpu.VMEM((blk, dqk), F32), pltpu.VMEM((blk, dv), F32)],
        sem=("parallel", "arbitrary"), jobs=jobs)


def _adamw_call(contribs, w, m, v, *, name, jobs=()):
    r_len, c_len = w.shape
    n_chunk = len(contribs)
    r_chunk = r_len // n_chunk
    cap = max(min(ADAM_BLOCK_ELEMS, 6 * ADAM_BLOCK_ELEMS // n_chunk) // c_len, 1)
    tr = r_chunk
    for cand in (512, 256, 128, 64, 32, 16):
        if cand <= cap and r_chunk % cand == 0:
            tr = cand
            break
    per = r_chunk // tr

    def body(ins, orefs, scr):
        w_ref, m_ref, v_ref = ins[n_chunk:]
        g_ref, d_ref, mo_ref, vo_ref = orefs
        i = pl.program_id(0)

        def update(c_ref):
            g = c_ref[0].astype(F32)
            for n in range(1, N_DEV):
                g = g + c_ref[n].astype(F32)
            m_new = ADAM_B1 * m_ref[...] + (1.0 - ADAM_B1) * g
            v_new = ADAM_B2 * v_ref[...] + (1.0 - ADAM_B2) * jnp.square(g)
            m_hat = m_new / (1.0 - ADAM_B1 ** ADAM_STEP)
            v_hat = v_new / (1.0 - ADAM_B2 ** ADAM_STEP)
            g_ref[...] = g
            d_ref[...] = -ADAM_LR * (m_hat / (jnp.sqrt(v_hat) + ADAM_EPS) + ADAM_WD * w_ref[...])
            mo_ref[...] = m_new
            vo_ref[...] = v_new

        if n_chunk == 1:
            update(ins[0])
        else:
            for n in range(n_chunk):
                @pl.when(i // per == n)
                def _():
                    update(ins[n])

    spec = pl.BlockSpec((tr, c_len), lambda i: (i, 0))
    c_specs = [pl.BlockSpec((N_DEV, tr, c_len), functools.partial(
        lambda i, n: (0, jnp.clip(i - n * per, 0, per - 1), 0), n=n)) for n in range(n_chunk)]
    return _call(body, name=name, grid=(r_len // tr,), in_specs=c_specs + [spec, spec, spec],
                 out_specs=[spec] * 4, out_shape=[jax.ShapeDtypeStruct((r_len, c_len), F32)] * 4,
                 args=list(contribs) + [w, m, v], sem=("arbitrary",), jobs=jobs)


def _rope_lanes(x, cc, s_lo, s_hi):
    return x * cc + _roll(x, 32, 1) * s_lo + _roll(x, 96, 1) * s_hi


def _f_lru(tt, branch):
    lb = branch // LRU_HEADS

    def f(rows, params):
        (xcat,) = rows
        cw, cb, wa, ba, wx, bx, lam = params
        conv = cb
        for j in range(CONV_W):
            sh = CONV_W - 1 - j
            xs = xcat if sh == 0 else _roll(xcat, sh, 0)
            conv = conv + cw[j:j + 1, :] * xs[tt:, :]
        rs, gs = [], []
        for h in range(LRU_HEADS):
            ub = conv[:, h * lb:(h + 1) * lb]
            rs.append(_bdot(ub, wa[h]))
            gs.append(_bdot(ub, wx[h]))
        r = jax.nn.sigmoid(jnp.concatenate(rs, axis=-1) + ba)
        gate = jax.nn.sigmoid(jnp.concatenate(gs, axis=-1) + bx)
        log_a = LRU_C * r * (-_softplus(-lam))
        a = jnp.exp(log_a)
        one_minus_a2 = -jnp.tanh(log_a) * (jnp.exp(2.0 * log_a) + 1.0)
        return a, (conv * gate) * jnp.sqrt(one_minus_a2)

    return f


def _f_gate(rows, params):
    hs, g = rows
    return (hs * _silu(g),)


def _f_ln(rows, params):
    h, br = rows
    g, b = params
    pre = ALPHA * h + br
    mu = jnp.mean(pre, axis=-1, keepdims=True)
    var = jnp.mean(jnp.square(pre - mu), axis=-1, keepdims=True)
    return ((pre - mu) * lax.rsqrt(var + LN_EPS) * g + b,)


def _f_pool(tt, branch):
    grp = branch // len(POOL_WINDOWS)

    def f(rows, params):
        xcat, tidx = rows
        sums, acc, w = [], xcat, 1
        while w < POOL_WINDOWS[-1]:
            acc = acc + _roll(acc, w, 0)
            w *= 2
            sums.append(acc[tt:, :])
        u = xcat[tt:, :]
        outs = []
        for gi, w in enumerate(POOL_WINDOWS):
            sl = slice(gi * grp, (gi + 1) * grp)
            outs.append(sums[gi][:, sl] / jnp.minimum(tidx + 1.0, float(w)) - u[:, sl])
        return tuple(outs)

    return f


def _f_gate_pool(rows, params):
    m0, m1, m2, m3, g = rows
    (scale,) = params
    return (jnp.concatenate([m0, m1, m2, m3], axis=-1) * scale * _silu(g),)


def _rms(x, g):
    return x * lax.rsqrt(jnp.mean(jnp.square(x), axis=-1, keepdims=True) + RMS_EPS) * g


def _f_mla_pre(rows, params):
    c, cc, s_lo, s_hi = rows
    qn, kvn = params
    cq = c[:, :Q_LORA]
    ckv = c[:, Q_LORA:Q_LORA + KV_LORA]
    kr = c[:, Q_LORA + KV_LORA:]
    return _rms(cq, qn), _rms(ckv, kvn), _rope_lanes(kr, cc, s_lo, s_hi)


def _f_rope_q(rows, params):
    qc, cc, s_lo, s_hi = rows
    out = []
    for h in range(MLA_HEADS):
        out.append(qc[:, h * MLA_QK:h * MLA_QK + MLA_NOPE])
        out.append(_rope_lanes(qc[:, h * MLA_QK + MLA_NOPE:(h + 1) * MLA_QK], cc, s_lo, s_hi))
    return (jnp.concatenate(out, axis=-1),)


def _f_kcat(dv):
    per = MLA_NOPE + dv

    def f(rows, params):
        kv, krr = rows
        ks, vs = [], []
        for h in range(MLA_HEADS):
            ks.append(kv[:, h * per:h * per + MLA_NOPE])
            ks.append(krr)
            vs.append(kv[:, h * per + MLA_NOPE:(h + 1) * per])
        return jnp.concatenate(ks, axis=-1), jnp.concatenate(vs, axis=-1)

    return f


def _f_rope_ret(dk):
    half = dk // 2

    def f(rows, params):
        q, k, cos, sin = rows
        qs, ks = [], []
        for h in range(RET_HEADS):
            for src, dst, mult in ((q, qs, 1.0), (k, ks, dk ** -0.5)):
                x1 = src[:, h * dk:h * dk + half]
                x2 = src[:, h * dk + half:(h + 1) * dk]
                dst.append((x1 * cos - x2 * sin) * mult)
                dst.append((x2 * cos + x1 * sin) * mult)
        return jnp.concatenate(qs, axis=-1), jnp.concatenate(ks, axis=-1)

    return f


def _f_gate_gn(dv):
    def f(rows, params):
        o, g = rows
        out = []
        for h in range(RET_HEADS):
            oh = o[:, h * dv:(h + 1) * dv]
            mu = jnp.mean(oh, axis=-1, keepdims=True)
            var = jnp.mean(jnp.square(oh - mu), axis=-1, keepdims=True)
            out.append((oh - mu) * lax.rsqrt(var + LN_EPS))
        return (jnp.concatenate(out, axis=-1) * _silu(g),)

    return f


def _f_loss(rows, params):
    h, tgt, mask = rows
    per_row = jnp.mean(jnp.square(h - tgt), axis=-1, keepdims=True) * mask
    return (0.5 * jnp.sum(per_row, axis=0, keepdims=True),)


def _cat(parts, axis=1):
    return parts[0] if len(parts) == 1 else jnp.concatenate(parts, axis=axis)


def _cols(g):
    return jnp.transpose(g, (1, 0, 2)).reshape(g.shape[1], -1)


def _uncols(w):
    k, n = w.shape
    return jnp.transpose(w.reshape(k, N_DEV, n // N_DEV), (1, 0, 2))


def _heads(g):
    return jnp.transpose(g, (1, 0, 2, 3)).reshape(g.shape[1], -1, g.shape[3])


def _unheads(w):
    h, r, c = w.shape
    return jnp.transpose(w.reshape(h, N_DEV, r // N_DEV, c), (1, 0, 2, 3))


def _rope_tables(t_pad, d):
    inv = ROPE_BASE ** (-jnp.arange(0, d, 2, dtype=F32) / d)
    ang = jnp.arange(t_pad, dtype=F32)[:, None] * inv[None, :]
    return jnp.cos(ang), jnp.sin(ang)


def _row2(v):
    return v.reshape(1, -1)


def _train_local(sch, x2d, tgt_pad, S, *, t_pad):
    seq, d_model = x2d.shape
    branch = d_model
    t_real = N_META + seq
    tt = _pick(t_pad, TT_PREFS)
    n_win = len(POOL_WINDOWS)
    grp = branch // n_win
    lb = branch // LRU_HEADS
    dv2 = branch // MLA_HEADS
    dk3 = branch // RET_HEADS
    gS = {}
    RW, RWB = 0.06, 0.1

    def ln_fwd(h, br, layer):
        h1, hb1 = rw_fwd(sch, RW, _f_ln, [h, br], [_row2(S[f'l{layer}_ln_g']), _row2(S[f'l{layer}_ln_b'])],
                         [(d_model, F32, 0), (d_model, BF, 0)], name=f'l{layer}_ln')
        return h1, hb1

    def ln_bwd(h, br, dh1, layer):
        dh, dbr, dg, db = rw_bwd(sch, RWB, _f_ln, [h, br],
                                 [_row2(S[f'l{layer}_ln_g']), _row2(S[f'l{layer}_ln_b'])], [dh1],
                                 name=f'l{layer}_ln', row_dtypes=[F32, BF])
        gS[f'l{layer}_ln_g'], gS[f'l{layer}_ln_b'] = dg.reshape(-1), db.reshape(-1)
        return dh, dbr

    tidx = jnp.arange(t_pad, dtype=F32)[:, None]
    rowmask = ((tidx >= N_META) & (tidx < t_real)).astype(F32)

    meta = _cols(_cat(sch.get('meta_tokens')))
    h0 = jnp.concatenate([meta, x2d, jnp.zeros((t_pad - t_real, d_model), F32)], axis=0)
    hb0 = h0.astype(BF)

    w0_in = _cols(_cat(sch.get('l0_w_in')))
    w0_u, w0_g = w0_in[:, :branch], w0_in[:, branch:]
    u0 = mm(sch, hb0, w0_u, 'l0_in_u')
    g0 = mm(sch, hb0, w0_g, 'l0_in_g')
    conv_w = jnp.transpose(_cat(sch.get('l0_conv_w')), (1, 2, 0, 3)).reshape(CONV_W, branch)
    w_a = _heads(_cat(sch.get('l0_w_a'), axis=2))
    w_x = _heads(_cat(sch.get('l0_w_x'), axis=2))
    p0 = [conv_w, _row2(S['l0_conv_b']), w_a, _row2(S['l0_b_a']), w_x, _row2(S['l0_b_x']), _row2(S['l0_lam'])]
    f_lru = _f_lru(tt, branch)
    a0, xin0 = rw_fwd(sch, 0.13, f_lru, [u0], p0, [(branch, F32, 0), (branch, F32, 1)], name='l0_lru',
                      halo=(0,), tt=tt)
    hs0 = _ride(sch, 0.22, _scan_call, a=a0, b=xin0, mul=None, reverse=False, name='l0_scan')[0]
    (z0,) = rw_fwd(sch, RW, _f_gate, [hs0, g0], [], [(branch, BF, 0)], name='l0_gate')
    w0_out = _cat(sch.get('l0_w_out')).reshape(branch, d_model)
    br0 = mm(sch, z0, w0_out, 'l0_out')
    h1, hb1 = ln_fwd(h0, br0, 0)

    w1_in = _cols(_cat(sch.get('l1_w_in')))
    w1_u, w1_g = w1_in[:, :branch], w1_in[:, branch:]
    u1 = mm(sch, hb1, w1_u, 'l1_in_u')
    g1 = mm(sch, hb1, w1_g, 'l1_in_g')
    f_pool = _f_pool(tt, branch)
    ps1 = rw_fwd(sch, RW, f_pool, [u1, tidx], [], [(grp, BF, gi) for gi in range(n_win)], name='l1_pool',
                 halo=(0,), tt=tt)
    w1_grp = _heads(_cat(sch.get('l1_w_grp'), axis=2))
    mixed1 = [mm(sch, ps1[gi], w1_grp[gi], f'l1_grp{gi}') for gi in range(n_win)]
    p1 = [_row2(S['l1_scale'])]
    (z1,) = rw_fwd(sch, RW, _f_gate_pool, mixed1 + [g1], p1, [(branch, BF, 0)], name='l1_gate')
    w1_out = _cat(sch.get('l1_w_out')).reshape(branch, d_model)
    br1 = mm(sch, z1, w1_out, 'l1_out')
    h2, hb2 = ln_fwd(h1, br1, 1)

    w2_in = _cols(_cat(sch.get('l2_w_in')))
    w2_g = w2_in[:, :branch]
    w2_lat = jnp.pad(w2_in[:, branch:], ((0, 0), (0, 128 - MLA_ROPE)))
    g2 = mm(sch, hb2, w2_g, 'l2_in_g')
    c2 = mm(sch, hb2, w2_lat, 'l2_in_c')
    cos, sin = _rope_tables(t_pad, MLA_ROPE)
    zz = jnp.zeros_like(cos)
    tabs = [jnp.concatenate([cos, cos, zz, zz], axis=-1), jnp.concatenate([zz, sin, zz, zz], axis=-1),
            jnp.concatenate([-sin, zz, zz, zz], axis=-1)]
    p2 = [_row2(S['l2_q_norm']), _row2(S['l2_kv_norm'])]
    cqn2, ckvn2, krr2 = rw_fwd(sch, 0.04, _f_mla_pre, [c2] + tabs, p2,
                               [(Q_LORA, BF, 0), (KV_LORA, BF, 1), (128, F32, 2)], name='l2_pre')
    w2_uq = _cols(_cat(sch.get('l2_w_uq'))).reshape(Q_LORA, MLA_HEADS, MLA_NOPE + MLA_ROPE)
    w2_uq = jnp.pad(w2_uq, ((0, 0), (0, 0), (0, MLA_QK - MLA_NOPE - MLA_ROPE))).reshape(Q_LORA, MLA_HEADS * MLA_QK)
    w2_ukv = _cols(_cat(sch.get('l2_w_ukv')))
    qc2 = mm(sch, cqn2, w2_uq, 'l2_uq')
    kv2 = mm(sch, ckvn2, w2_ukv, 'l2_ukv')
    (qcr2,) = rw_fwd(sch, 0.09, _f_rope_q, [qc2] + tabs, [], [(MLA_HEADS * MLA_QK, BF, 0)], name='l2_rope_q')
    f_kcat = _f_kcat(dv2)
    kcat2, v2 = rw_fwd(sch, 0.1, f_kcat, [kv2, krr2], [], [(MLA_HEADS * MLA_QK, BF, 0), (branch, BF, 1)],
                       name='l2_kcat')
    no_decay = jnp.zeros((MLA_HEADS, 1, 128), F32)
    att2 = dict(heads=MLA_HEADS, softmax=True, scale=(MLA_NOPE + MLA_ROPE) ** -0.5, name='l2_att')
    o2, lse2 = _ride(sch, 1.3, _att_fwd_call, q=qcr2, k=kcat2, v=v2, lgt=no_decay, **att2)
    (z2,) = rw_fwd(sch, RW, _f_gate, [o2, g2], [], [(branch, BF, 0)], name='l2_gate')
    w2_out = _cat(sch.get('l2_w_out')).reshape(branch, d_model)
    br2 = mm(sch, z2, w2_out, 'l2_out')
    h3, hb3 = ln_fwd(h2, br2, 2)

    w3_in = _cols(_cat(sch.get('l3_w_in')))
    w3 = [w3_in[:, n * branch:(n + 1) * branch] for n in range(4)]
    q3 = mm(sch, hb3, w3[0], 'l3_in_q')
    k3 = mm(sch, hb3, w3[1], 'l3_in_k')
    v3 = mm(sch, hb3, w3[2], 'l3_in_v', out_dtype=BF)
    g3 = mm(sch, hb3, w3[3], 'l3_in_g')
    cs3 = list(_rope_tables(t_pad, dk3))
    f_rope3 = _f_rope_ret(dk3)
    qr3, kr3 = rw_fwd(sch, 0.09, f_rope3, [q3, k3] + cs3, [], [(branch, BF, 0), (branch, BF, 1)], name='l3_rope')
    log_g = jnp.log(1.0 - 2.0 ** (-5.0 - jnp.arange(RET_HEADS, dtype=F32)))
    lgt = jnp.broadcast_to(log_g[:, None, None], (RET_HEADS, 1, 128))
    att3 = dict(heads=RET_HEADS, softmax=False, scale=1.0, name='l3_ret')
    o3, lse3 = _ride(sch, 0.6, _att_fwd_call, q=qr3, k=kr3, v=v3, lgt=lgt, **att3)
    f_gn = _f_gate_gn(dk3)
    (z3,) = rw_fwd(sch, 0.08, f_gn, [o3, g3], [], [(branch, BF, 0)], name='l3_gate')
    w3_out = _cat(sch.get('l3_w_out')).reshape(branch, d_model)
    br3 = mm(sch, z3, w3_out, 'l3_out')
    (h4,) = rw_fwd(sch, RW, _f_ln, [h3, br3], [_row2(S['l3_ln_g']), _row2(S['l3_ln_b'])], [(d_model, F32, 0)],
                   name='l3_ln')

    (loss,) = rw_fwd(sch, 0.05, _f_loss, [h4, tgt_pad, rowmask], [], [], name='loss', n_reduce=1)

    sch.scale = BWD_RIDER_SHARE
    (dh4,) = rw_bwd(sch, 0.07, _f_loss, [h4, tgt_pad, rowmask], [], [jnp.ones((1, 1), F32)], name='loss',
                    n_reduce=1, nd_rows=(1, 2))

    dh3, dbr3 = ln_bwd(h3, br3, dh4, 3)
    dz3 = mm(sch, dbr3, w3_out, 'l3_out_dx', mode='nt')
    sch.push('l3_w_out', mm(sch, z3.T, dbr3, 'l3_out_dw', mode='nn', out_dtype=BF).reshape(N_DEV, -1, d_model),
             False, 1)
    do3, dg3 = rw_bwd(sch, 0.13, f_gn, [o3, g3], [], [dz3], name='l3_gate', row_dtypes=[F32, BF])
    dqr3, delta3 = _ride(sch, 0.7, _att_dq_call, q=qr3, k=kr3, v=v3, lgt=lgt, o=o3, lse=lse3, do=do3, **att3)
    dkr3, dv3 = _ride(sch, 0.8, _att_dkv_call, q=qr3, k=kr3, v=v3, lgt=lgt, lse_row=lse3.reshape(RET_HEADS, 1, -1),
                      delta_row=delta3.reshape(RET_HEADS, 1, -1), do=do3, **att3)
    dq3, dk3_ = rw_bwd(sch, 0.13, f_rope3, [q3, k3] + cs3, [], [dqr3, dkr3], name='l3_rope', nd_rows=(2, 3),
                       row_dtypes=[BF, BF])
    d3 = [dq3, dk3_, dv3, dg3]
    for n in range(4):
        dh3 = mm(sch, d3[n], w3[n], f'l3_in_dx{n}', mode='nt', add=dh3)
    hb3_t = hb3.T
    dw3 = [mm(sch, hb3_t, d3[n], f'l3_in_dw{n}', mode='nn', out_dtype=BF) for n in range(4)]
    sch.push('l3_w_in', _uncols(jnp.concatenate(dw3, axis=1)), False, 1)

    dh2, dbr2 = ln_bwd(h2, br2, dh3, 2)
    dz2 = mm(sch, dbr2, w2_out, 'l2_out_dx', mode='nt')
    sch.push('l2_w_out', mm(sch, z2.T, dbr2, 'l2_out_dw', mode='nn', out_dtype=BF).reshape(N_DEV, -1, d_model),
             False, 1)
    do2, dg2 = rw_bwd(sch, 0.1, _f_gate, [o2, g2], [], [dz2], name='l2_gate', row_dtypes=[F32, BF])
    dqcr2, delta2 = _ride(sch, 1.3, _att_dq_call, q=qcr2, k=kcat2, v=v2, lgt=no_decay, o=o2, lse=lse2, do=do2,
                          **att2)
    dkcat2, dv2_ = _ride(sch, 1.5, _att_dkv_call, q=qcr2, k=kcat2, v=v2, lgt=no_decay,
                         lse_row=lse2.reshape(MLA_HEADS, 1, -1), delta_row=delta2.reshape(MLA_HEADS, 1, -1),
                         do=do2, **att2)
    dkv2, dkrr2 = rw_bwd(sch, 0.14, f_kcat, [kv2, krr2], [], [dkcat2, dv2_], name='l2_kcat', row_dtypes=[BF, F32])
    (dqc2,) = rw_bwd(sch, 0.13, _f_rope_q, [qc2] + tabs, [], [dqcr2], name='l2_rope_q', nd_rows=(1, 2, 3),
                     row_dtypes=[BF])
    dckvn2 = mm(sch, dkv2, w2_ukv, 'l2_ukv_dx', mode='nt')
    dcqn2 = mm(sch, dqc2, w2_uq, 'l2_uq_dx', mode='nt')
    sch.push('l2_w_ukv', _uncols(mm(sch, ckvn2, dkv2, 'l2_ukv_dw', mode='tn', out_dtype=BF)), False, 1)
    dw_uq = mm(sch, cqn2, dqc2, 'l2_uq_dw', mode='tn', out_dtype=BF)
    dw_uq = dw_uq.reshape(Q_LORA, MLA_HEADS, MLA_QK)[:, :, :MLA_NOPE + MLA_ROPE].reshape(Q_LORA, -1)
    sch.push('l2_w_uq', _uncols(dw_uq), False, 1)
    dc2, dqn, dkvn = rw_bwd(sch, 0.05, _f_mla_pre, [c2] + tabs, p2, [dcqn2, dckvn2, dkrr2], name='l2_pre',
                            nd_rows=(1, 2, 3), row_dtypes=[BF])
    gS['l2_q_norm'], gS['l2_kv_norm'] = dqn.reshape(-1), dkvn.reshape(-1)
    dh2 = mm(sch, dg2, w2_g, 'l2_in_g_dx', mode='nt', add=dh2)
    dh2 = mm(sch, dc2, w2_lat, 'l2_in_c_dx', mode='nt', add=dh2)
    hb2_t = hb2.T
    dw2_g = mm(sch, hb2_t, dg2, 'l2_in_g_dw', mode='nn', out_dtype=BF)
    dw2_lat = mm(sch, hb2_t, dc2, 'l2_in_c_dw', mode='nn', out_dtype=BF)
    n_lat = Q_LORA + KV_LORA + MLA_ROPE
    sch.push('l2_w_in', _uncols(jnp.concatenate([dw2_g, dw2_lat[:, :n_lat]], axis=1)), False, 1)

    dh1, dbr1 = ln_bwd(h1, br1, dh2, 1)
    dz1 = mm(sch, dbr1, w1_out, 'l1_out_dx', mode='nt')
    sch.push('l1_w_out', mm(sch, z1.T, dbr1, 'l1_out_dw', mode='nn', out_dtype=BF).reshape(N_DEV, -1, d_model),
             False, 1)
    res = rw_bwd(sch, 0.11, _f_gate_pool, mixed1 + [g1], p1, [dz1], name='l1_gate', row_dtypes=[BF] * (n_win + 1))
    dmixed1, dg1, dscale = res[:n_win], res[n_win], res[n_win + 1]
    gS['l1_scale'] = dscale.reshape(-1)
    dps1 = [mm(sch, dmixed1[gi], w1_grp[gi], f'l1_grp{gi}_dx', mode='nt') for gi in range(n_win)]
    dw_grp = jnp.stack([mm(sch, ps1[gi], dmixed1[gi], f'l1_grp{gi}_dw', mode='tn', out_dtype=BF)
                        for gi in range(n_win)])
    sch.push('l1_w_grp', _unheads(dw_grp), False, 2)
    (du1,) = rw_bwd(sch, 0.09, f_pool, [u1, tidx], [], dps1, name='l1_pool', halo=(0,), nd_rows=(1,), tt=tt,
                    row_dtypes=[BF])
    dh1 = mm(sch, du1, w1_u, 'l1_in_u_dx', mode='nt', add=dh1)
    dh1 = mm(sch, dg1, w1_g, 'l1_in_g_dx', mode='nt', add=dh1)
    hb1_t = hb1.T
    dw1 = [mm(sch, hb1_t, du1, 'l1_in_u_dw', mode='nn', out_dtype=BF),
           mm(sch, hb1_t, dg1, 'l1_in_g_dw', mode='nn', out_dtype=BF)]
    sch.push('l1_w_in', _uncols(jnp.concatenate(dw1, axis=1)), False, 1)

    dh0, dbr0 = ln_bwd(h0, br0, dh1, 0)
    dz0 = mm(sch, dbr0, w0_out, 'l0_out_dx', mode='nt')
    sch.push('l0_w_out', mm(sch, z0.T, dbr0, 'l0_out_dw', mode='nn', out_dtype=BF).reshape(N_DEV, -1, d_model),
             False, 1)
    dhs0, dg0 = rw_bwd(sch, 0.1, _f_gate, [hs0, g0], [], [dz0], name='l0_gate', row_dtypes=[F32, BF])
    a_next = jnp.concatenate([a0[1:], jnp.ones_like(a0[:1])], axis=0)
    hs_prev = jnp.concatenate([jnp.zeros_like(hs0[:1]), hs0[:-1]], axis=0)
    dxin0, da0 = _ride(sch, 0.25, _scan_call, a=a_next, b=dhs0, mul=hs_prev, reverse=True, name='l0_scan_bwd')
    res = rw_bwd(sch, 0.3, f_lru, [u0], p0, [da0, dxin0], name='l0_lru', halo=(0,), tt=tt, row_dtypes=[BF])
    du0 = res[0]
    gS['l0_conv_b'], gS['l0_b_a'], gS['l0_b_x'], gS['l0_lam'] = [res[k].reshape(-1) for k in (2, 4, 6, 7)]
    sch.push('l0_w_a', _unheads(res[3]), False, 2)
    sch.push('l0_w_x', _unheads(res[5]), False, 2)
    sch.push('l0_conv_w', jnp.transpose(res[1].reshape(CONV_W, 1, N_DEV, -1), (2, 0, 1, 3)), False, 3)
    hb0_t = hb0.T
    dw0 = [mm(sch, hb0_t, du0, 'l0_in_u_dw', mode='nn', out_dtype=BF),
           mm(sch, hb0_t, dg0, 'l0_in_g_dw', mode='nn', out_dtype=BF)]
    sch.push('l0_w_in', _uncols(jnp.concatenate(dw0, axis=1)), False, 1)
    sch.scale = 2.0
    dh0 = mm(sch, du0, w0_u, 'l0_in_u_dx', mode='nt', add=dh0)
    dh0 = mm(sch, dg0, w0_g, 'l0_in_g_dx', mode='nt', add=dh0)
    sch.push('meta_tokens', _uncols(dh0[:N_META]), False, 1)

    return loss[0, 0], dh0[N_META:t_real], gS


def _as2d(a):
    return a.reshape(-1, a.shape[-1])


def kernel(x, meta_tokens, l0_w_in, l0_conv_w, l0_conv_b, l0_w_a, l0_b_a, l0_w_x, l0_b_x, l0_lam, l0_w_out, l0_ln_g, l0_ln_b, l1_w_in, l1_w_grp, l1_scale, l1_w_out, l1_ln_g, l1_ln_b, l2_w_in, l2_q_norm, l2_w_uq, l2_kv_norm, l2_w_ukv, l2_w_out, l2_ln_g, l2_ln_b, l3_w_in, l3_w_out, l3_ln_g, l3_ln_b, loss_target, m_meta_tokens, m_l0_w_in, m_l0_conv_w, m_l0_conv_b, m_l0_w_a, m_l0_b_a, m_l0_w_x, m_l0_b_x, m_l0_lam, m_l0_w_out, m_l0_ln_g, m_l0_ln_b, m_l1_w_in, m_l1_w_grp, m_l1_scale, m_l1_w_out, m_l1_ln_g, m_l1_ln_b, m_l2_w_in, m_l2_q_norm, m_l2_w_uq, m_l2_kv_norm, m_l2_w_ukv, m_l2_w_out, m_l2_ln_g, m_l2_ln_b, m_l3_w_in, m_l3_w_out, m_l3_ln_g, m_l3_ln_b, v_meta_tokens, v_l0_w_in, v_l0_conv_w, v_l0_conv_b, v_l0_w_a, v_l0_b_a, v_l0_w_x, v_l0_b_x, v_l0_lam, v_l0_w_out, v_l0_ln_g, v_l0_ln_b, v_l1_w_in, v_l1_w_grp, v_l1_scale, v_l1_w_out, v_l1_ln_g, v_l1_ln_b, v_l2_w_in, v_l2_q_norm, v_l2_w_uq, v_l2_kv_norm, v_l2_w_ukv, v_l2_w_out, v_l2_ln_g, v_l2_ln_b, v_l3_w_in, v_l3_w_out, v_l3_ln_g, v_l3_ln_b):
    args = (meta_tokens, l0_w_in, l0_conv_w, l0_conv_b, l0_w_a, l0_b_a, l0_w_x, l0_b_x, l0_lam, l0_w_out, l0_ln_g, l0_ln_b, l1_w_in, l1_w_grp, l1_scale, l1_w_out, l1_ln_g, l1_ln_b, l2_w_in, l2_q_norm, l2_w_uq, l2_kv_norm, l2_w_ukv, l2_w_out, l2_ln_g, l2_ln_b, l3_w_in, l3_w_out, l3_ln_g, l3_ln_b)
    moms = (m_meta_tokens, m_l0_w_in, m_l0_conv_w, m_l0_conv_b, m_l0_w_a, m_l0_b_a, m_l0_w_x, m_l0_b_x, m_l0_lam, m_l0_w_out, m_l0_ln_g, m_l0_ln_b, m_l1_w_in, m_l1_w_grp, m_l1_scale, m_l1_w_out, m_l1_ln_g, m_l1_ln_b, m_l2_w_in, m_l2_q_norm, m_l2_w_uq, m_l2_kv_norm, m_l2_w_ukv, m_l2_w_out, m_l2_ln_g, m_l2_ln_b, m_l3_w_in, m_l3_w_out, m_l3_ln_g, m_l3_ln_b)
    vels = (v_meta_tokens, v_l0_w_in, v_l0_conv_w, v_l0_conv_b, v_l0_w_a, v_l0_b_a, v_l0_w_x, v_l0_b_x, v_l0_lam, v_l0_w_out, v_l0_ln_g, v_l0_ln_b, v_l1_w_in, v_l1_w_grp, v_l1_scale, v_l1_w_out, v_l1_ln_g, v_l1_ln_b, v_l2_w_in, v_l2_q_norm, v_l2_w_uq, v_l2_kv_norm, v_l2_w_ukv, v_l2_w_out, v_l2_ln_g, v_l2_ln_b, v_l3_w_in, v_l3_w_out, v_l3_ln_g, v_l3_ln_b)
    W = dict(zip(WEIGHTS, args))
    M = dict(zip(WEIGHTS, moms))
    V = dict(zip(WEIGHTS, vels))

    seq = x.shape[1]
    t_real = N_META + seq
    t_pad = -(-t_real // ROW_ALIGN) * ROW_ALIGN
    tgt_pad = jnp.pad(loss_target[0], ((N_META, t_pad - t_real), (0, 0)))

    sch = _Schedule()
    for n in GATHER_ORDER:
        shard = W[n].astype(BF) if n in BIG else W[n]
        sch.push(n, shard, True, shard.ndim - 2)
    S = {n: W[n] for n in REPLICATED}

    loss, gx, gS = _train_local(sch, x[0], tgt_pad, S, t_pad=t_pad)

    flat = jnp.concatenate([gS[n].reshape(-1) for n in REPLICATED]).reshape(-1, 128)
    sch.push('small_grads', flat, True, 0)

    out_g, out_d, out_m, out_v = {}, {}, {}, {}
    order = ['l3_w_out', 'l3_w_in', 'l2_w_out', 'l2_w_ukv', 'l2_w_uq', 'l2_w_in', 'l1_w_out', 'l1_w_grp',
             'l1_w_in', 'l0_w_out', 'l0_w_a', 'l0_w_x', 'l0_conv_w', 'l0_w_in', 'meta_tokens']
    for n in order:
        shp = W[n].shape
        w2, m2, v2 = _as2d(W[n]), _as2d(M[n]), _as2d(V[n])
        parts = [p.reshape((N_DEV, -1, w2.shape[1])) for p in sch.get(n)]
        res = _ride(sch, w2.size * 1.5e-8, _adamw_call, contribs=parts, w=w2, m=m2, v=v2, name='adamw_' + n)
        out_g[n], out_d[n], out_m[n], out_v[n] = [r.reshape(shp) for r in res]
    cat = lambda D: jnp.concatenate([D[n].reshape(-1) for n in REPLICATED]).reshape(-1, 128)
    res = _adamw_call(sch.get('small_grads'), cat(W), cat(M), cat(V), name='adamw_small')[0]
    off = 0
    for n in REPLICATED:
        size = W[n].size
        for dst, r in zip((out_g, out_d, out_m, out_v), res):
            dst[n] = r.reshape(-1)[off:off + size].reshape(W[n].shape)
        off += size
    sch.flush()

    loss = lax.psum(loss, ("x", "y", "c"))
    return (loss, gx[None], *[out_g[n] for n in WEIGHTS], *[out_d[n] for n in WEIGHTS],
            *[out_m[n] for n in WEIGHTS], *[out_v[n] for n in WEIGHTS])
```

```python
import functools

import jax
import jax.numpy as jnp
from jax import lax
from jax.experimental import pallas as pl
from jax.experimental.pallas import tpu as pltpu

F32 = jnp.float32
BF = jnp.bfloat16

N_DEV = 8
N_META = 16
ALPHA = (2.0 * 4) ** 0.25
LN_EPS = 1e-5
RMS_EPS = 1e-6
ROPE_BASE = 10000.0
LRU_HEADS = 16
CONV_W = 4
LRU_C = 8.0
POOL_WINDOWS = (2, 4, 8, 16)
MLA_HEADS = 32
MLA_NOPE = 128
MLA_ROPE = 64
MLA_QK = 256
Q_LORA = 1024
KV_LORA = 512
RET_HEADS = 16
ADAM_LR = 0.001
ADAM_B1 = 0.9
ADAM_B2 = 0.999
ADAM_EPS = 1e-08
ADAM_WD = 0.01
ADAM_STEP = 10

ROW_ALIGN = 128
VMEM_LIMIT_BYTES = 56 * 1024 * 1024
TT_PREFS = (128, 64, 32, 16, 8)
ATT_PREFS = (384, 256, 128)
MM_M_PREFS = (1408, 1024, 512, 384, 256, 128)
MM_N_PREFS = (1024, 640, 512, 384, 256, 128)
MM_K_PREFS = (1408, 1024, 512, 384, 256, 128)
MM_VMEM_BUDGET = 40 * 1024 * 1024
ADAM_BLOCK_ELEMS = 128 * 1024
EXCH_MS_PER_MB = 0.0857
EXCH_CHUNK_MS = 0.1
MXU_FLOPS_PER_MS = 7.8e11
BWD_RIDER_SHARE = 0.6

WEIGHTS = ['meta_tokens', 'l0_w_in', 'l0_conv_w', 'l0_conv_b', 'l0_w_a', 'l0_b_a', 'l0_w_x', 'l0_b_x', 'l0_lam',
           'l0_w_out', 'l0_ln_g', 'l0_ln_b', 'l1_w_in', 'l1_w_grp', 'l1_scale', 'l1_w_out', 'l1_ln_g', 'l1_ln_b',
           'l2_w_in', 'l2_q_norm', 'l2_w_uq', 'l2_kv_norm', 'l2_w_ukv', 'l2_w_out', 'l2_ln_g', 'l2_ln_b',
           'l3_w_in', 'l3_w_out', 'l3_ln_g', 'l3_ln_b']
BIG = ['l0_w_in', 'l0_w_out', 'l1_w_in', 'l1_w_grp', 'l1_w_out', 'l2_w_in', 'l2_w_uq', 'l2_w_ukv', 'l2_w_out',
       'l3_w_in', 'l3_w_out']
SHARDED_F32 = ['meta_tokens', 'l0_conv_w', 'l0_w_a', 'l0_w_x']
REPLICATED = [n for n in WEIGHTS if n not in BIG and n not in SHARDED_F32]
GATHER_ORDER = ['meta_tokens', 'l0_w_in', 'l0_conv_w', 'l0_w_a', 'l0_w_x', 'l0_w_out', 'l1_w_in', 'l1_w_grp',
                'l1_w_out', 'l2_w_in', 'l2_w_uq', 'l2_w_ukv', 'l2_w_out', 'l3_w_in', 'l3_w_out']


def _pick(n, prefs):
    for p in prefs:
        if n % p == 0:
            return p
    return n


def _exchange_copies(jobs, in_refs, out_refs, send_sems, recv_sems, local_sems):
    x, y, c = lax.axis_index("x"), lax.axis_index("y"), lax.axis_index("c")
    me = 4 * x + 2 * y + c
    sibling = (x, y, 1 - c)
    chips = [(1 - x, y), (x, 1 - y), (1 - x, 1 - y)]
    peers = [sibling] + [(px, py, c) for px, py in chips] + [(px, py, 1 - c) for px, py in chips]
    first, arrivals, forwards, rest = [], [], [], []

    def dev(px, py, pc):
        return 4 * px + 2 * py + pc

    for n, (_, gather) in enumerate(jobs):
        def remote(p, src, slot, to, n=n):
            k = n * (N_DEV - 1) + p
            return pltpu.make_async_remote_copy(
                src_ref=src, dst_ref=out_refs[n].at[slot], send_sem=send_sems.at[k], recv_sem=recv_sems.at[k],
                device_id=to, device_id_type=pl.DeviceIdType.MESH)

        if gather:
            first += [remote(p, in_refs[n], me, peers[p]) for p in range(4)]
            for j, (px, py) in enumerate(chips):
                landed = out_refs[n].at[dev(px, py, c)]
                arrivals.append(remote(1 + j, landed, dev(px, py, c), peers[1 + j]))
                forwards.append(remote(4 + j, landed, dev(px, py, c), sibling))
        else:
            first += [remote(p, in_refs[n].at[dev(*peers[p])], me, peers[p]) for p in range(N_DEV - 1)]
        rest.append(pltpu.make_async_copy(in_refs[n] if gather else in_refs[n].at[me], out_refs[n].at[me],
                                          local_sems.at[n]))

    def start():
        for cp in first + rest:
            cp.start()

    def finish():
        for arrived, forward in zip(arrivals, forwards):
            arrived.wait_recv()
            forward.start()
        for cp in first:
            cp.wait_send()
        for n, (_, gather) in enumerate(jobs):
            for p in range(N_DEV - 1):
                if not (gather and 1 <= p <= 3):
                    k = n * (N_DEV - 1) + p
                    pltpu.make_async_remote_copy(
                        src_ref=out_refs[n].at[me], dst_ref=out_refs[n].at[me], send_sem=send_sems.at[k],
                        recv_sem=recv_sems.at[k], device_id=sibling, device_id_type=pl.DeviceIdType.MESH).wait_recv()
        for cp in forwards:
            cp.wait_send()
        for cp in rest:
            cp.wait()

    return start, finish


def _exchange_shapes(jobs):
    shapes = []
    for arr, gather in jobs:
        blk = arr.shape if gather else arr.shape[1:]
        shapes.append(jax.ShapeDtypeStruct((N_DEV,) + tuple(blk), arr.dtype))
    return shapes


def _exchange_scratch(jobs):
    n = len(jobs)
    return [pltpu.SemaphoreType.DMA((n * (N_DEV - 1),)), pltpu.SemaphoreType.DMA((n * (N_DEV - 1),)),
            pltpu.SemaphoreType.DMA((n,))]


def _call(body, *, name, grid, in_specs, out_specs, out_shape, args, scratch=(), sem=None, jobs=()):
    jobs = list(jobs)
    n_in, n_out, n_sc, n_job = len(in_specs), len(out_specs), len(scratch), len(jobs)
    hbm = pl.BlockSpec(memory_space=pl.ANY)

    def kern(*refs):
        ins = refs[:n_in]
        job_ins = refs[n_in:n_in + n_job]
        pos = n_in + n_job
        outs = refs[pos:pos + n_out]
        job_outs = refs[pos + n_out:pos + n_out + n_job]
        pos += n_out + n_job
        scr = refs[pos:pos + n_sc]
        if n_job:
            ids = [pl.program_id(d) for d in range(len(grid))]
            first = functools.reduce(jnp.logical_and, [i == 0 for i in ids])
            last = functools.reduce(jnp.logical_and, [i == g - 1 for i, g in zip(ids, grid)])
            start, finish = _exchange_copies(jobs, job_ins, job_outs, *refs[pos + n_sc:])
            pl.when(first)(start)

        body(ins, outs, scr)
        if n_job:
            pl.when(last)(finish)

    kw = dict(vmem_limit_bytes=VMEM_LIMIT_BYTES)
    if sem is not None:
        kw['dimension_semantics'] = tuple("arbitrary" for _ in grid) if n_job else sem
    res = pl.pallas_call(
        kern, name=name, grid=grid,
        in_specs=list(in_specs) + [hbm] * n_job,
        out_specs=list(out_specs) + [hbm] * n_job,
        out_shape=list(out_shape) + _exchange_shapes(jobs),
        scratch_shapes=list(scratch) + (_exchange_scratch(jobs) if n_job else []),
        compiler_params=pltpu.CompilerParams(**kw),
    )(*args, *[a for a, _ in jobs])
    return list(res[:n_out]), list(res[n_out:])


def _exchange_alone(jobs, name):
    def body(*refs):
        n = len(jobs)
        start, finish = _exchange_copies(jobs, refs[:n], refs[n:2 * n], *refs[2 * n:])
        start()
        finish()

    hbm = pl.BlockSpec(memory_space=pl.ANY)
    return list(pl.pallas_call(
        body, name=name, in_specs=[hbm] * len(jobs), out_specs=[hbm] * len(jobs),
        out_shape=_exchange_shapes(jobs), scratch_shapes=_exchange_scratch(jobs),
    )(*[a for a, _ in jobs]))


class _Schedule:
    def __init__(self):
        self.pending = []
        self.done = {}
        self.chunks = {}
        self.count = 0
        self.scale = 1.0

    def push(self, name, arr, gather, row_axis):
        mb = arr.size * arr.dtype.itemsize / (1 if gather else N_DEV) / 1e6
        cost = mb * EXCH_MS_PER_MB * (0.5 if gather else 1.0)
        rows = arr.shape[row_axis]
        n = 1
        leading = row_axis == (0 if gather else 1)
        target = EXCH_CHUNK_MS if gather else 2 * EXCH_CHUNK_MS
        while leading and cost / n > target and rows % (2 * n) == 0 and rows // (2 * n) >= 16:
            n *= 2
        self.chunks[name] = n
        step = rows // n
        for k in range(n):
            piece = lax.slice_in_dim(arr, k * step, (k + 1) * step, axis=row_axis) if n > 1 else arr
            self.pending.append(((name, k), piece, gather, cost / n))

    def take(self, budget):
        jobs, spent = [], 0.0
        budget *= self.scale
        while self.pending and spent + 0.5 * self.pending[0][3] <= budget:
            job = self.pending.pop(0)
            jobs.append(job)
            spent += job[3]
        return jobs

    def deliver(self, jobs, results):
        for (key, _, _, _), r in zip(jobs, results):
            self.done[key] = r

    def get(self, name):
        mine = [j for j in self.pending if j[0][0] == name]
        if mine:
            self.pending = [j for j in self.pending if j[0][0] != name]
            self.count += 1
            self.deliver(mine, _exchange_alone([(j[1], j[2]) for j in mine], f'exchange_{self.count}_{name}'))
        parts = [self.done.pop((name, k)) for k in range(self.chunks[name])]
        return parts

    def flush(self):
        if self.pending:
            jobs, self.pending = self.pending, []
            self.count += 1
            self.deliver(jobs, _exchange_alone([(j[1], j[2]) for j in jobs], f'exchange_{self.count}_rest'))


def _ride(sch, budget, fn, **kw):
    jobs = sch.take(budget) if sch is not None else []
    outs, exch = fn(jobs=[(j[1], j[2]) for j in jobs], **kw)
    if jobs:
        sch.deliver(jobs, exch)
    return outs


@functools.partial(jax.custom_vjp, nondiff_argnums=(1, 2))
def _roll(x, shift, axis):
    return pltpu.roll(x, shift, axis)


def _roll_fwd(x, shift, axis):
    return pltpu.roll(x, shift, axis), None


def _roll_bwd(shift, axis, _, g):
    n = g.shape[axis]
    return (pltpu.roll(g, (n - shift) % n, axis),)


_roll.defvjp(_roll_fwd, _roll_bwd)


@jax.custom_vjp
def _bdot(x, w):
    return jnp.dot(x.astype(BF), w.astype(BF), preferred_element_type=F32)


def _bdot_fwd(x, w):
    return _bdot(x, w), (x, w)


def _bdot_bwd(res, g):
    x, w = res
    gb = g.astype(BF)
    dx = lax.dot_general(gb, w.astype(BF), (((1,), (1,)), ((), ())), preferred_element_type=F32)
    dw = lax.dot_general(x.astype(BF), gb, (((0,), (0,)), ((), ())), preferred_element_type=F32)
    return dx, dw


_bdot.defvjp(_bdot_fwd, _bdot_bwd)


def _silu(g):
    return g * jax.nn.sigmoid(g)


def _softplus(x):
    return jnp.maximum(x, 0.0) + jnp.log1p(jnp.exp(-jnp.abs(x)))


def _mm_call(a, b, *, mode, out_dtype, name, add=None, jobs=()):
    if mode == 'nn':
        (mo, kc), (_, no) = a.shape, b.shape
    elif mode == 'nt':
        (mo, kc), (no, _) = a.shape, b.shape
    else:
        (kc, mo), (_, no) = a.shape, b.shape
    tm = _pick(mo, MM_M_PREFS)
    tn = no if no <= 2048 and no % 512 != 0 else _pick(no, MM_N_PREFS)
    out_bytes = tm * tn * (2 * jnp.dtype(out_dtype).itemsize + 4 + (8 if add is not None else 0))
    tk = None
    for cand in MM_K_PREFS:
        if kc % cand == 0:
            tk = cand
            if out_bytes + 2 * cand * (tm * a.dtype.itemsize + tn * b.dtype.itemsize) <= MM_VMEM_BUDGET:
                break
    tk = tk or kc
    nk = kc // tk
    if mode == 'nn':
        a_spec = pl.BlockSpec((tm, tk), lambda i, j, k: (i, k))
        b_spec = pl.BlockSpec((tk, tn), lambda i, j, k: (k, j))
        dims = (((1,), (0,)), ((), ()))
    elif mode == 'nt':
        a_spec = pl.BlockSpec((tm, tk), lambda i, j, k: (i, k))
        b_spec = pl.BlockSpec((tn, tk), lambda i, j, k: (j, k))
        dims = (((1,), (1,)), ((), ()))
    else:
        a_spec = pl.BlockSpec((tk, tm), lambda i, j, k: (k, i))
        b_spec = pl.BlockSpec((tk, tn), lambda i, j, k: (k, j))
        dims = (((0,), (0,)), ((), ()))
    o_spec = pl.BlockSpec((tm, tn), lambda i, j, k: (i, j))

    def body(ins, outs, scr):
        a_ref, b_ref = ins[0], ins[1]
        (o_ref,), (acc_ref,) = outs, scr
        k = pl.program_id(2)

        @pl.when(k == 0)
        def _():
            acc_ref[...] = jnp.zeros_like(acc_ref) if add is None else ins[2][...]

        acc_ref[...] += lax.dot_general(a_ref[...].astype(BF), b_ref[...].astype(BF), dims,
                                        preferred_element_type=F32)

        @pl.when(k == nk - 1)
        def _():
            o_ref[...] = acc_ref[...].astype(o_ref.dtype)

    outs, exch = _call(
        body, name=name, grid=(mo // tm, no // tn, nk),
        in_specs=[a_spec, b_spec] + ([o_spec] if add is not None else []),
        out_specs=[o_spec], out_shape=[jax.ShapeDtypeStruct((mo, no), out_dtype)],
        args=[a, b] + ([add] if add is not None else []),
        scratch=[pltpu.VMEM((tm, tn), F32)], sem=("parallel", "parallel", "arbitrary"), jobs=jobs)
    return outs, exch


def mm(sch, a, b, name, mode='nn', out_dtype=F32, add=None):
    if mode == 'nn':
        flops = 2.0 * a.shape[0] * a.shape[1] * b.shape[1]
    elif mode == 'nt':
        flops = 2.0 * a.shape[0] * a.shape[1] * b.shape[0]
    else:
        flops = 2.0 * a.shape[0] * a.shape[1] * b.shape[1]
    return _ride(sch, flops / MXU_FLOPS_PER_MS, _mm_call, a=a, b=b, mode=mode, out_dtype=out_dtype, name=name,
                 add=add)[0]


def _full_spec(p):
    nd = p.ndim
    return pl.BlockSpec(p.shape, lambda i: (0,) * nd)


def _load_rows(refs, n_rows, halo, step_is_first):
    cur, prev, pos = [], [], 0
    for r in range(n_rows):
        cur.append(refs[pos][...].astype(F32))
        pos += 1
        if r in halo:
            keep = jnp.where(step_is_first, 0.0, 1.0).astype(F32)
            prev.append(refs[pos][...].astype(F32) * keep)
            pos += 1
        else:
            prev.append(None)
    return cur, prev, pos


def _join(cur, prev):
    return [c if p is None else jnp.concatenate([p, c], axis=0) for c, p in zip(cur, prev)]


def _rw_fwd(f, rows, params, outs, *, name, n_reduce=0, halo=(), tt=None, jobs=()):
    t_len = rows[0].shape[0]
    tt = tt or _pick(t_len, TT_PREFS)
    nt = t_len // tt
    n_rows, n_par, n_out = len(rows), len(params), len(outs)

    def body(ins, orefs, scr):
        i = pl.program_id(0)
        cur, prev, pos = _load_rows(ins, n_rows, halo, i == 0)
        pvals = [ins[pos + k][...] for k in range(n_par)]
        res = f(_join(cur, prev), pvals)
        n_f = len(res) - n_reduce
        for k, (_, _, src) in enumerate(outs):
            orefs[k][...] = res[src].astype(orefs[k].dtype)
        for k in range(n_reduce):
            ref, val = orefs[n_out + k], res[n_f + k]

            @pl.when(i == 0)
            def _():
                ref[...] = val

            @pl.when(i > 0)
            def _():
                ref[...] += val

    in_specs, args = [], []
    for r, x in enumerate(rows):
        c = x.shape[1]
        in_specs.append(pl.BlockSpec((tt, c), lambda i: (i, 0)))
        args.append(x)
        if r in halo:
            in_specs.append(pl.BlockSpec((tt, c), lambda i: (jnp.maximum(i - 1, 0), 0)))
            args.append(x)
    for p in params:
        in_specs.append(_full_spec(p))
        args.append(p)
    out_specs = [pl.BlockSpec((tt, c), lambda i: (i, 0)) for c, _, _ in outs]
    out_shape = [jax.ShapeDtypeStruct((t_len, c), dt) for c, dt, _ in outs]
    for _ in range(n_reduce):
        out_specs.append(pl.BlockSpec((1, 1), lambda i: (0, 0)))
        out_shape.append(jax.ShapeDtypeStruct((1, 1), F32))
    return _call(body, name=name, grid=(nt,), in_specs=in_specs, out_specs=out_specs, out_shape=out_shape,
                 args=args, sem=("arbitrary",), jobs=jobs)


def _rw_bwd(f, rows, params, cts, *, name, n_reduce=0, halo=(), nd_rows=(), nd_params=(), tt=None,
            row_dtypes=None, jobs=()):
    t_len = rows[0].shape[0]
    tt = tt or _pick(t_len, TT_PREFS)
    nt = t_len // tt
    n_rows, n_par, n_ct = len(rows), len(params), len(cts)
    d_rows = [r for r in range(n_rows) if r not in nd_rows]
    d_pars = [k for k in range(n_par) if k not in nd_params]
    h_rows = [r for r in d_rows if r in halo]

    def blk(j):
        return nt - 1 - j

    def body(ins, orefs, carry_refs):
        j = pl.program_id(0)
        cur, prev, pos = _load_rows(ins, n_rows, halo, blk(j) == 0)
        pvals = [ins[pos + k][...] for k in range(n_par)]
        pos += n_par
        ct_vals = [ins[pos + k][...].astype(F32) for k in range(n_ct)]

        def g(dcur, dprev, dpar):
            c, p, q = list(cur), list(prev), list(pvals)
            for r, v in zip(d_rows, dcur):
                c[r] = v
            for r, v in zip(h_rows, dprev):
                p[r] = v
            for k, v in zip(d_pars, dpar):
                q[k] = v
            return tuple(f(_join(c, p), q))

        _, vjp = jax.vjp(g, [cur[r] for r in d_rows], [prev[r] for r in h_rows], [pvals[k] for k in d_pars])
        g_cur, g_prev, g_par = vjp(tuple(ct_vals))

        for n, r in enumerate(d_rows):
            if r in halo:
                cref = carry_refs[h_rows.index(r)]

                @pl.when(j == 0)
                def _():
                    cref[...] = jnp.zeros_like(cref)

                orefs[n][...] = (g_cur[n] + cref[...]).astype(orefs[n].dtype)
                cref[...] = g_prev[h_rows.index(r)]
            else:
                orefs[n][...] = g_cur[n].astype(orefs[n].dtype)
        for n in range(len(d_pars)):
            ref, val = orefs[len(d_rows) + n], g_par[n]

            @pl.when(j == 0)
            def _():
                ref[...] = val

            @pl.when(j > 0)
            def _():
                ref[...] += val

    in_specs, args = [], []
    for r, x in enumerate(rows):
        c = x.shape[1]
        in_specs.append(pl.BlockSpec((tt, c), lambda j: (blk(j), 0)))
        args.append(x)
        if r in halo:
            in_specs.append(pl.BlockSpec((tt, c), lambda j: (jnp.maximum(blk(j) - 1, 0), 0)))
            args.append(x)
    for p in params:
        in_specs.append(_full_spec(p))
        args.append(p)
    for ct in cts:
        if ct.shape == (1, 1):
            in_specs.append(pl.BlockSpec((1, 1), lambda j: (0, 0)))
        else:
            in_specs.append(pl.BlockSpec((tt, ct.shape[1]), lambda j: (blk(j), 0)))
        args.append(ct)
    out_specs, out_shape, scratch = [], [], []
    for n, r in enumerate(d_rows):
        c = rows[r].shape[1]
        out_specs.append(pl.BlockSpec((tt, c), lambda j: (blk(j), 0)))
        out_shape.append(jax.ShapeDtypeStruct((t_len, c), row_dtypes[n] if row_dtypes else F32))
        if r in halo:
            scratch.append(pltpu.VMEM((tt, c), F32))
    for k in d_pars:
        out_specs.append(_full_spec(params[k]))
        out_shape.append(jax.ShapeDtypeStruct(params[k].shape, F32))
    return _call(body, name=name + '_bwd', grid=(nt,), in_specs=in_specs, out_specs=out_specs,
                 out_shape=out_shape, args=args, scratch=scratch, sem=("arbitrary",), jobs=jobs)


def rw_fwd(sch, budget, f, rows, params, outs, **kw):
    return _ride(sch, budget, functools.partial(_rw_fwd, f, list(rows), list(params), outs), **kw)


def rw_bwd(sch, budget, f, rows, params, cts, **kw):
    return _ride(sch, budget, functools.partial(_rw_bwd, f, list(rows), list(params), list(cts)), **kw)


def _scan_call(a, b, mul, *, reverse, name, jobs=()):
    t_len, c_len = a.shape
    tt = _pick(t_len, TT_PREFS)
    tc = _pick(c_len, (512, 256, 128))
    nt = t_len // tt

    def body(ins, orefs, scr):
        (carry,) = scr
        t = pl.program_id(1)
        av, bv = ins[0][...], ins[1][...]
        row = lax.broadcasted_iota(jnp.int32, av.shape, 0)
        s = 1
        while s < tt:
            if reverse:
                ok = row < tt - s
                a_sh = jnp.where(ok, pltpu.roll(av, tt - s, 0), 1.0)
                b_sh = jnp.where(ok, pltpu.roll(bv, tt - s, 0), 0.0)
            else:
                ok = row >= s
                a_sh = jnp.where(ok, pltpu.roll(av, s, 0), 1.0)
                b_sh = jnp.where(ok, pltpu.roll(bv, s, 0), 0.0)
            bv = av * b_sh + bv
            av = av * a_sh
            s *= 2

        @pl.when(t == 0)
        def _():
            carry[...] = jnp.zeros_like(carry)

        hs = bv + av * carry[...]
        orefs[0][...] = hs
        edge = 0 if reverse else tt - 1
        carry[...] = orefs[0][edge:edge + 1, :]
        if mul is not None:
            orefs[1][...] = hs * ins[2][...]

    def idx(c, t):
        return ((nt - 1 - t) if reverse else t, c)

    spec = pl.BlockSpec((tt, tc), idx)
    n_in, n_out = (2, 1) if mul is None else (3, 2)
    return _call(body, name=name, grid=(c_len // tc, nt), in_specs=[spec] * n_in, out_specs=[spec] * n_out,
                 out_shape=[jax.ShapeDtypeStruct((t_len, c_len), F32)] * n_out,
                 args=[a, b] if mul is None else [a, b, mul],
                 scratch=[pltpu.VMEM((1, tc), F32)], sem=("parallel", "arbitrary"), jobs=jobs)


NT = (((1,), (1,)), ((), ()))


def _for_blocks(lo, hi, fn, group=2):
    n = hi - lo

    def trip(p, c):
        for g in range(group):
            fn(lo + group * p + g)
        return c

    lax.fori_loop(0, n // group, trip, 0)
    for g in range(1, group):
        @pl.when(n % group >= g)
        def _():
            fn(hi - (n % group) + g - 1)


def _att_weights(s, rel, diag, lg, softmax, scale, t_axis):
    row = lax.broadcasted_iota(jnp.int32, s.shape, t_axis)
    col = lax.broadcasted_iota(jnp.int32, s.shape, 1 - t_axis)
    if softmax:
        s = s * scale
        return jnp.where(row >= col, s, -1e30) if diag else s
    diff = (rel + row - col).astype(F32)
    dec = jnp.exp(jnp.maximum(diff, 0.0) * lg)
    return jnp.where(diff >= 0.0, dec, 0.0) if diag else dec


def _att_fwd_call(q, k, v, lgt, *, heads, softmax, scale, name, jobs=()):
    t_len = q.shape[0]
    dqk, dv = q.shape[1] // heads, v.shape[1] // heads
    blk = _pick(t_len, ATT_PREFS)
    nb = t_len // blk

    lanes = 128
    n_fold = blk // lanes

    def body(ins, orefs, scr):
        q_ref, k_ref, v_ref, lg_ref = ins
        o_ref, lse_ref = orefs
        s_sc, m_sc, acc_sc = scr
        i = pl.program_id(1)
        lg = lg_ref[0:1, 0:1]
        qb = q_ref[...]
        acc_sc[...] = jnp.zeros_like(acc_sc)

        def block_rows(j):
            return pl.ds(pl.multiple_of(j * blk, blk), blk)

        if not softmax:
            def step(j, diag):
                s = lax.dot_general(qb, k_ref[block_rows(j), :], NT, preferred_element_type=F32)
                w = _att_weights(s, (i - j) * blk, diag, lg, softmax, scale, 0)
                acc_sc[...] += jnp.dot((s * w).astype(BF), v_ref[block_rows(j), :], preferred_element_type=F32)

            _for_blocks(0, i, lambda j: step(j, False), group=3)
            step(i, True)
            o_ref[...] = acc_sc[...]
            lse_ref[...] = jnp.zeros_like(lse_ref)
            return

        m_sc[...] = jnp.full_like(m_sc, -1e30)

        def score(j, diag):
            s = lax.dot_general(qb, k_ref[block_rows(j), :], NT, preferred_element_type=F32)
            w = _att_weights(s, (i - j) * blk, diag, lg, softmax, scale, 0)
            s_sc[j] = w
            top = w[:, :lanes]
            for n in range(1, n_fold):
                top = jnp.maximum(top, w[:, n * lanes:(n + 1) * lanes])
            m_sc[...] = jnp.maximum(m_sc[...], top)

        _for_blocks(0, i, lambda j: score(j, False), group=3)
        score(i, True)
        m = jnp.max(m_sc[...], axis=-1, keepdims=True)
        ones = jnp.ones((blk, lanes), BF)

        def accumulate(j):
            p = jnp.exp(s_sc[j] - m).astype(BF)
            v_aug = jnp.concatenate([v_ref[block_rows(j), :], ones], axis=1)
            acc_sc[...] += jnp.dot(p, v_aug, preferred_element_type=F32)

        _for_blocks(0, i + 1, accumulate, group=3)
        acc = acc_sc[...]
        l = acc[:, dv:dv + 1]
        o_ref[...] = acc[:, :dv] / l
        lse_ref[...] = m + jnp.log(l)

    scratch = [pltpu.VMEM((nb, blk, blk), F32), pltpu.VMEM((blk, lanes), F32),
               pltpu.VMEM((blk, dv + lanes if softmax else dv), F32)]
    return _call(
        body, name=name, grid=(heads, nb),
        in_specs=[
            pl.BlockSpec((blk, dqk), lambda h, i: (i, h)),
            pl.BlockSpec((t_len, dqk), lambda h, i: (0, h)),
            pl.BlockSpec((t_len, dv), lambda h, i: (0, h)),
            pl.BlockSpec((None, 1, 128), lambda h, i: (h, 0, 0)),
        ],
        out_specs=[
            pl.BlockSpec((blk, dv), lambda h, i: (i, h)),
            pl.BlockSpec((None, blk, 1), lambda h, i: (h, i, 0)),
        ],
        out_shape=[jax.ShapeDtypeStruct((t_len, heads * dv), F32), jax.ShapeDtypeStruct((heads, t_len, 1), F32)],
        args=[q, k, v, lgt],
        scratch=scratch if softmax else [pltpu.VMEM((8, lanes), F32), scratch[1], scratch[2]],
        sem=("parallel", "arbitrary"), jobs=jobs)


def _att_dq_call(q, k, v, lgt, o, lse, do, *, heads, softmax, scale, name, jobs=()):
    t_len = q.shape[0]
    dqk, dv = q.shape[1] // heads, v.shape[1] // heads
    blk = _pick(t_len, ATT_PREFS)
    nb = t_len // blk

    def body(ins, orefs, scr):
        q_ref, k_ref, v_ref, lg_ref, o_ref, do_ref, lse_ref = ins
        dq_ref, delta_ref = orefs
        (acc,) = scr
        i = pl.program_id(1)
        lg = lg_ref[0:1, 0:1]
        qb = q_ref[...]
        do = do_ref[...]
        dob = do.astype(BF)
        delta = jnp.sum(do * o_ref[...], axis=-1, keepdims=True)
        lse = lse_ref[...]
        delta_ref[...] = delta
        acc[...] = jnp.zeros_like(acc)

        def step(j, diag):
            rows = pl.ds(pl.multiple_of(j * blk, blk), blk)
            kb = k_ref[rows, :]
            s = lax.dot_general(qb, kb, NT, preferred_element_type=F32)
            w = _att_weights(s, (i - j) * blk, diag, lg, softmax, scale, 0)
            dp = lax.dot_general(dob, v_ref[rows, :], NT, preferred_element_type=F32)
            ds = jnp.exp(w - lse) * (dp - delta) * scale if softmax else dp * w
            acc[...] += jnp.dot(ds.astype(BF), kb, preferred_element_type=F32)

        _for_blocks(0, i, lambda j: step(j, False))
        step(i, True)
        dq_ref[...] = acc[...]

    return _call(
        body, name=name + '_dq', grid=(heads, nb),
        in_specs=[
            pl.BlockSpec((blk, dqk), lambda h, i: (i, h)),
            pl.BlockSpec((t_len, dqk), lambda h, i: (0, h)),
            pl.BlockSpec((t_len, dv), lambda h, i: (0, h)),
            pl.BlockSpec((None, 1, 128), lambda h, i: (h, 0, 0)),
            pl.BlockSpec((blk, dv), lambda h, i: (i, h)),
            pl.BlockSpec((blk, dv), lambda h, i: (i, h)),
            pl.BlockSpec((None, blk, 1), lambda h, i: (h, i, 0)),
        ],
        out_specs=[pl.BlockSpec((blk, dqk), lambda h, i: (i, h)),
                   pl.BlockSpec((None, blk, 1), lambda h, i: (h, i, 0))],
        out_shape=[jax.ShapeDtypeStruct(q.shape, F32), jax.ShapeDtypeStruct((heads, t_len, 1), F32)],
        args=[q, k, v, lgt, o, do, lse],
        scratch=[pltpu.VMEM((blk, dqk), F32)], sem=("parallel", "arbitrary"), jobs=jobs)


def _att_dkv_call(q, k, v, lgt, lse_row, delta_row, do, *, heads, softmax, scale, name, jobs=()):
    t_len = q.shape[0]
    dqk, dv = q.shape[1] // heads, v.shape[1] // heads
    blk = _pick(t_len, ATT_PREFS)
    nb = t_len // blk

    def body(ins, orefs, scr):
        q_ref, k_ref, v_ref, lg_ref, do_ref, lse_ref, delta_ref = ins
        dk_acc, dv_acc = scr
        j = pl.program_id(1)
        lg = lg_ref[0:1, 0:1]
        kb, vb = k_ref[...], v_ref[...]
        dk_acc[...] = jnp.zeros_like(dk_acc)
        dv_acc[...] = jnp.zeros_like(dv_acc)

        def step(i, diag):
            rows = pl.ds(pl.multiple_of(i * blk, blk), blk)
            qb = q_ref[rows, :]
            dob = do_ref[rows, :].astype(BF)
            s = lax.dot_general(kb, qb, NT, preferred_element_type=F32)
            w = _att_weights(s, (i - j) * blk, diag, lg, softmax, scale, 1)
            dp = lax.dot_general(vb, dob, NT, preferred_element_type=F32)
            if softmax:
                p = jnp.exp(w - lse_ref[:, rows])
                ds = p * (dp - delta_ref[:, rows]) * scale
            else:
                p, ds = s * w, dp * w
            dv_acc[...] += jnp.dot(p.astype(BF), dob, preferred_element_type=F32)
            dk_acc[...] += jnp.dot(ds.astype(BF), qb, preferred_element_type=F32)

        step(j, True)
        _for_blocks(j + 1, nb, lambda i: step(i, False), group=3)
        orefs[0][...] = dk_acc[...]
        orefs[1][...] = dv_acc[...]

    return _call(
        body, name=name + '_dkv', grid=(heads, nb),
        in_specs=[
            pl.BlockSpec((t_len, dqk), lambda h, j: (0, h)),
            pl.BlockSpec((blk, dqk), lambda h, j: (j, h)),
            pl.BlockSpec((blk, dv), lambda h, j: (j, h)),
            pl.BlockSpec((None, 1, 128), lambda h, j: (h, 0, 0)),
            pl.BlockSpec((t_len, dv), lambda h, j: (0, h)),
            pl.BlockSpec((None, 1, t_len), lambda h, j: (h, 0, 0)),
            pl.BlockSpec((None, 1, t_len), lambda h, j: (h, 0, 0)),
        ],
        out_specs=[
            pl.BlockSpec((blk, dqk), lambda h, j: (j, h)),
            pl.BlockSpec((blk, dv), lambda h, j: (j, h)),
        ],
        out_shape=[jax.ShapeDtypeStruct(k.shape, F32), jax.ShapeDtypeStruct(v.shape, F32)],
        args=[q, k, v, lgt, do, lse_row, delta_row],
        scratch=[pltpu.VMEM((blk, dqk), F32), pltpu.VMEM((blk, dv), F32)],
        sem=("parallel", "arbitrary"), jobs=jobs)


def _adamw_call(contribs, w, m, v, *, name, jobs=()):
    r_len, c_len = w.shape
    n_chunk = len(contribs)
    r_chunk = r_len // n_chunk
    cap = max(min(ADAM_BLOCK_ELEMS, 6 * ADAM_BLOCK_ELEMS // n_chunk) // c_len, 1)
    tr = r_chunk
    for cand in (512, 256, 128, 64, 32, 16):
        if cand <= cap and r_chunk % cand == 0:
            tr = cand
            break
    per = r_chunk // tr

    def body(ins, orefs, scr):
        w_ref, m_ref, v_ref = ins[n_chunk:]
        g_ref, d_ref, mo_ref, vo_ref = orefs
        i = pl.program_id(0)

        def update(c_ref):
            g = c_ref[0].astype(F32)
            for n in range(1, N_DEV):
                g = g + c_ref[n].astype(F32)
            m_new = ADAM_B1 * m_ref[...] + (1.0 - ADAM_B1) * g
            v_new = ADAM_B2 * v_ref[...] + (1.0 - ADAM_B2) * jnp.square(g)
            m_hat = m_new / (1.0 - ADAM_B1 ** ADAM_STEP)
            v_hat = v_new / (1.0 - ADAM_B2 ** ADAM_STEP)
            g_ref[...] = g
            d_ref[...] = -ADAM_LR * (m_hat / (jnp.sqrt(v_hat) + ADAM_EPS) + ADAM_WD * w_ref[...])
            mo_ref[...] = m_new
            vo_ref[...] = v_new

        if n_chunk == 1:
            update(ins[0])
        else:
            for n in range(n_chunk):
                @pl.when(i // per == n)
                def _():
                    update(ins[n])

    spec = pl.BlockSpec((tr, c_len), lambda i: (i, 0))
    c_specs = [pl.BlockSpec((N_DEV, tr, c_len), functools.partial(
        lambda i, n: (0, jnp.clip(i - n * per, 0, per - 1), 0), n=n)) for n in range(n_chunk)]
    return _call(body, name=name, grid=(r_len // tr,), in_specs=c_specs + [spec, spec, spec],
                 out_specs=[spec] * 4, out_shape=[jax.ShapeDtypeStruct((r_len, c_len), F32)] * 4,
                 args=list(contribs) + [w, m, v], sem=("arbitrary",), jobs=jobs)


def _rope_lanes(x, cc, s_lo, s_hi):
    return x * cc + _roll(x, 32, 1) * s_lo + _roll(x, 96, 1) * s_hi


def _f_lru(tt, branch):
    lb = branch // LRU_HEADS

    def f(rows, params):
        (xcat,) = rows
        cw, cb, wa, ba, wx, bx, lam = params
        conv = cb
        for j in range(CONV_W):
            sh = CONV_W - 1 - j
            xs = xcat if sh == 0 else _roll(xcat, sh, 0)
            conv = conv + cw[j:j + 1, :] * xs[tt:, :]
        rs, gs = [], []
        for h in range(LRU_HEADS):
            ub = conv[:, h * lb:(h + 1) * lb]
            rs.append(_bdot(ub, wa[h]))
            gs.append(_bdot(ub, wx[h]))
        r = jax.nn.sigmoid(jnp.concatenate(rs, axis=-1) + ba)
        gate = jax.nn.sigmoid(jnp.concatenate(gs, axis=-1) + bx)
        log_a = LRU_C * r * (-_softplus(-lam))
        a = jnp.exp(log_a)
        one_minus_a2 = -jnp.tanh(log_a) * (jnp.exp(2.0 * log_a) + 1.0)
        return a, (conv * gate) * jnp.sqrt(one_minus_a2)

    return f


def _f_gate(rows, params):
    hs, g = rows
    return (hs * _silu(g),)


def _f_ln(rows, params):
    h, br = rows
    g, b = params
    pre = ALPHA * h + br
    mu = jnp.mean(pre, axis=-1, keepdims=True)
    var = jnp.mean(jnp.square(pre - mu), axis=-1, keepdims=True)
    return ((pre - mu) * lax.rsqrt(var + LN_EPS) * g + b,)


def _f_pool(tt, branch):
    grp = branch // len(POOL_WINDOWS)

    def f(rows, params):
        xcat, tidx = rows
        sums, acc, w = [], xcat, 1
        while w < POOL_WINDOWS[-1]:
            acc = acc + _roll(acc, w, 0)
            w *= 2
            sums.append(acc[tt:, :])
        u = xcat[tt:, :]
        outs = []
        for gi, w in enumerate(POOL_WINDOWS):
            sl = slice(gi * grp, (gi + 1) * grp)
            outs.append(sums[gi][:, sl] / jnp.minimum(tidx + 1.0, float(w)) - u[:, sl])
        return tuple(outs)

    return f


def _f_gate_pool(rows, params):
    m0, m1, m2, m3, g = rows
    (scale,) = params
    return (jnp.concatenate([m0, m1, m2, m3], axis=-1) * scale * _silu(g),)


def _rms(x, g):
    return x * lax.rsqrt(jnp.mean(jnp.square(x), axis=-1, keepdims=True) + RMS_EPS) * g


def _f_mla_pre(rows, params):
    c, cc, s_lo, s_hi = rows
    qn, kvn = params
    cq = c[:, :Q_LORA]
    ckv = c[:, Q_LORA:Q_LORA + KV_LORA]
    kr = c[:, Q_LORA + KV_LORA:]
    return _rms(cq, qn), _rms(ckv, kvn), _rope_lanes(kr, cc, s_lo, s_hi)


def _f_rope_q(rows, params):
    qc, cc, s_lo, s_hi = rows
    out = []
    for h in range(MLA_HEADS):
        out.append(qc[:, h * MLA_QK:h * MLA_QK + MLA_NOPE])
        out.append(_rope_lanes(qc[:, h * MLA_QK + MLA_NOPE:(h + 1) * MLA_QK], cc, s_lo, s_hi))
    return (jnp.concatenate(out, axis=-1),)


def _f_kcat(dv):
    per = MLA_NOPE + dv

    def f(rows, params):
        kv, krr = rows
        ks, vs = [], []
        for h in range(MLA_HEADS):
            ks.append(kv[:, h * per:h * per + MLA_NOPE])
            ks.append(krr)
            vs.append(kv[:, h * per + MLA_NOPE:(h + 1) * per])
        return jnp.concatenate(ks, axis=-1), jnp.concatenate(vs, axis=-1)

    return f


def _f_rope_ret(dk):
    half = dk // 2

    def f(rows, params):
        q, k, cos, sin = rows
        qs, ks = [], []
        for h in range(RET_HEADS):
            for src, dst, mult in ((q, qs, 1.0), (k, ks, dk ** -0.5)):
                x1 = src[:, h * dk:h * dk + half]
                x2 = src[:, h * dk + half:(h + 1) * dk]
                dst.append((x1 * cos - x2 * sin) * mult)
                dst.append((x2 * cos + x1 * sin) * mult)
        return jnp.concatenate(qs, axis=-1), jnp.concatenate(ks, axis=-1)

    return f


def _f_gate_gn(dv):
    def f(rows, params):
        o, g = rows
        out = []
        for h in range(RET_HEADS):
            oh = o[:, h * dv:(h + 1) * dv]
            mu = jnp.mean(oh, axis=-1, keepdims=True)
            var = jnp.mean(jnp.square(oh - mu), axis=-1, keepdims=True)
            out.append((oh - mu) * lax.rsqrt(var + LN_EPS))
        return (jnp.concatenate(out, axis=-1) * _silu(g),)

    return f


def _f_loss(rows, params):
    h, tgt, mask = rows
    per_row = jnp.mean(jnp.square(h - tgt), axis=-1, keepdims=True) * mask
    return (0.5 * jnp.sum(per_row, axis=0, keepdims=True),)


def _cat(parts, axis=1):
    return parts[0] if len(parts) == 1 else jnp.concatenate(parts, axis=axis)


def _cols(g):
    return jnp.transpose(g, (1, 0, 2)).reshape(g.shape[1], -1)


def _uncols(w):
    k, n = w.shape
    return jnp.transpose(w.reshape(k, N_DEV, n // N_DEV), (1, 0, 2))


def _heads(g):
    return jnp.transpose(g, (1, 0, 2, 3)).reshape(g.shape[1], -1, g.shape[3])


def _unheads(w):
    h, r, c = w.shape
    return jnp.transpose(w.reshape(h, N_DEV, r // N_DEV, c), (1, 0, 2, 3))


def _rope_tables(t_pad, d):
    inv = ROPE_BASE ** (-jnp.arange(0, d, 2, dtype=F32) / d)
    ang = jnp.arange(t_pad, dtype=F32)[:, None] * inv[None, :]
    return jnp.cos(ang), jnp.sin(ang)


def _row2(v):
    return v.reshape(1, -1)


def _train_local(sch, x2d, tgt_pad, S, *, t_pad):
    seq, d_model = x2d.shape
    branch = d_model
    t_real = N_META + seq
    tt = _pick(t_pad, TT_PREFS)
    n_win = len(POOL_WINDOWS)
    grp = branch // n_win
    lb = branch // LRU_HEADS
    dv2 = branch // MLA_HEADS
    dk3 = branch // RET_HEADS
    gS = {}
    RW, RWB = 0.06, 0.1

    def ln_fwd(h, br, layer):
        h1, hb1 = rw_fwd(sch, RW, _f_ln, [h, br], [_row2(S[f'l{layer}_ln_g']), _row2(S[f'l{layer}_ln_b'])],
                         [(d_model, F32, 0), (d_model, BF, 0)], name=f'l{layer}_ln')
        return h1, hb1

    def ln_bwd(h, br, dh1, layer):
        dh, dbr, dg, db = rw_bwd(sch, RWB, _f_ln, [h, br],
                                 [_row2(S[f'l{layer}_ln_g']), _row2(S[f'l{layer}_ln_b'])], [dh1],
                                 name=f'l{layer}_ln', row_dtypes=[F32, BF])
        gS[f'l{layer}_ln_g'], gS[f'l{layer}_ln_b'] = dg.reshape(-1), db.reshape(-1)
        return dh, dbr

    tidx = jnp.arange(t_pad, dtype=F32)[:, None]
    rowmask = ((tidx >= N_META) & (tidx < t_real)).astype(F32)

    meta = _cols(_cat(sch.get('meta_tokens')))
    h0 = jnp.concatenate([meta, x2d, jnp.zeros((t_pad - t_real, d_model), F32)], axis=0)
    hb0 = h0.astype(BF)

    w0_in = _cols(_cat(sch.get('l0_w_in')))
    w0_u, w0_g = w0_in[:, :branch], w0_in[:, branch:]
    u0 = mm(sch, hb0, w0_u, 'l0_in_u')
    g0 = mm(sch, hb0, w0_g, 'l0_in_g')
    conv_w = jnp.transpose(_cat(sch.get('l0_conv_w')), (1, 2, 0, 3)).reshape(CONV_W, branch)
    w_a = _heads(_cat(sch.get('l0_w_a'), axis=2))
    w_x = _heads(_cat(sch.get('l0_w_x'), axis=2))
    p0 = [conv_w, _row2(S['l0_conv_b']), w_a, _row2(S['l0_b_a']), w_x, _row2(S['l0_b_x']), _row2(S['l0_lam'])]
    f_lru = _f_lru(tt, branch)
    a0, xin0 = rw_fwd(sch, 0.13, f_lru, [u0], p0, [(branch, F32, 0), (branch, F32, 1)], name='l0_lru',
                      halo=(0,), tt=tt)
    hs0 = _ride(sch, 0.22, _scan_call, a=a0, b=xin0, mul=None, reverse=False, name='l0_scan')[0]
    (z0,) = rw_fwd(sch, RW, _f_gate, [hs0, g0], [], [(branch, BF, 0)], name='l0_gate')
    w0_out = _cat(sch.get('l0_w_out')).reshape(branch, d_model)
    br0 = mm(sch, z0, w0_out, 'l0_out')
    h1, hb1 = ln_fwd(h0, br0, 0)

    w1_in = _cols(_cat(sch.get('l1_w_in')))
    w1_u, w1_g = w1_in[:, :branch], w1_in[:, branch:]
    u1 = mm(sch, hb1, w1_u, 'l1_in_u')
    g1 = mm(sch, hb1, w1_g, 'l1_in_g')
    f_pool = _f_pool(tt, branch)
    ps1 = rw_fwd(sch, RW, f_pool, [u1, tidx], [], [(grp, BF, gi) for gi in range(n_win)], name='l1_pool',
                 halo=(0,), tt=tt)
    w1_grp = _heads(_cat(sch.get('l1_w_grp'), axis=2))
    mixed1 = [mm(sch, ps1[gi], w1_grp[gi], f'l1_grp{gi}') for gi in range(n_win)]
    p1 = [_row2(S['l1_scale'])]
    (z1,) = rw_fwd(sch, RW, _f_gate_pool, mixed1 + [g1], p1, [(branch, BF, 0)], name='l1_gate')
    w1_out = _cat(sch.get('l1_w_out')).reshape(branch, d_model)
    br1 = mm(sch, z1, w1_out, 'l1_out')
    h2, hb2 = ln_fwd(h1, br1, 1)

    w2_in = _cols(_cat(sch.get('l2_w_in')))
    w2_g = w2_in[:, :branch]
    w2_lat = jnp.pad(w2_in[:, branch:], ((0, 0), (0, 128 - MLA_ROPE)))
    g2 = mm(sch, hb2, w2_g, 'l2_in_g')
    c2 = mm(sch, hb2, w2_lat, 'l2_in_c')
    cos, sin = _rope_tables(t_pad, MLA_ROPE)
    zz = jnp.zeros_like(cos)
    tabs = [jnp.concatenate([cos, cos, zz, zz], axis=-1), jnp.concatenate([zz, sin, zz, zz], axis=-1),
            jnp.concatenate([-sin, zz, zz, zz], axis=-1)]
    p2 = [_row2(S['l2_q_norm']), _row2(S['l2_kv_norm'])]
    cqn2, ckvn2, krr2 = rw_fwd(sch, 0.04, _f_mla_pre, [c2] + tabs, p2,
                               [(Q_LORA, BF, 0), (KV_LORA, BF, 1), (128, F32, 2)], name='l2_pre')
    w2_uq = _cols(_cat(sch.get('l2_w_uq'))).reshape(Q_LORA, MLA_HEADS, MLA_NOPE + MLA_ROPE)
    w2_uq = jnp.pad(w2_uq, ((0, 0), (0, 0), (0, MLA_QK - MLA_NOPE - MLA_ROPE))).reshape(Q_LORA, MLA_HEADS * MLA_QK)
    w2_ukv = _cols(_cat(sch.get('l2_w_ukv')))
    qc2 = mm(sch, cqn2, w2_uq, 'l2_uq')
    kv2 = mm(sch, ckvn2, w2_ukv, 'l2_ukv')
    (qcr2,) = rw_fwd(sch, 0.09, _f_rope_q, [qc2] + tabs, [], [(MLA_HEADS * MLA_QK, BF, 0)], name='l2_rope_q')
    f_kcat = _f_kcat(dv2)
    kcat2, v2 = rw_fwd(sch, 0.1, f_kcat, [kv2, krr2], [], [(MLA_HEADS * MLA_QK, BF, 0), (branch, BF, 1)],
                       name='l2_kcat')
    no_decay = jnp.zeros((MLA_HEADS, 1, 128), F32)
    att2 = dict(heads=MLA_HEADS, softmax=True, scale=(MLA_NOPE + MLA_ROPE) ** -0.5, name='l2_att')
    o2, lse2 = _ride(sch, 1.3, _att_fwd_call, q=qcr2, k=kcat2, v=v2, lgt=no_decay, **att2)
    (z2,) = rw_fwd(sch, RW, _f_gate, [o2, g2], [], [(branch, BF, 0)], name='l2_gate')
    w2_out = _cat(sch.get('l2_w_out')).reshape(branch, d_model)
    br2 = mm(sch, z2, w2_out, 'l2_out')
    h3, hb3 = ln_fwd(h2, br2, 2)

    w3_in = _cols(_cat(sch.get('l3_w_in')))
    w3 = [w3_in[:, n * branch:(n + 1) * branch] for n in range(4)]
    q3 = mm(sch, hb3, w3[0], 'l3_in_q')
    k3 = mm(sch, hb3, w3[1], 'l3_in_k')
    v3 = mm(sch, hb3, w3[2], 'l3_in_v', out_dtype=BF)
    g3 = mm(sch, hb3, w3[3], 'l3_in_g')
    cs3 = list(_rope_tables(t_pad, dk3))
    f_rope3 = _f_rope_ret(dk3)
    qr3, kr3 = rw_fwd(sch, 0.09, f_rope3, [q3, k3] + cs3, [], [(branch, BF, 0), (branch, BF, 1)], name='l3_rope')
    log_g = jnp.log(1.0 - 2.0 ** (-5.0 - jnp.arange(RET_HEADS, dtype=F32)))
    lgt = jnp.broadcast_to(log_g[:, None, None], (RET_HEADS, 1, 128))
    att3 = dict(heads=RET_HEADS, softmax=False, scale=1.0, name='l3_ret')
    o3, lse3 = _ride(sch, 0.6, _att_fwd_call, q=qr3, k=kr3, v=v3, lgt=lgt, **att3)
    f_gn = _f_gate_gn(dk3)
    (z3,) = rw_fwd(sch, 0.08, f_gn, [o3, g3], [], [(branch, BF, 0)], name='l3_gate')
    w3_out = _cat(sch.get('l3_w_out')).reshape(branch, d_model)
    br3 = mm(sch, z3, w3_out, 'l3_out')
    (h4,) = rw_fwd(sch, RW, _f_ln, [h3, br3], [_row2(S['l3_ln_g']), _row2(S['l3_ln_b'])], [(d_model, F32, 0)],
                   name='l3_ln')

    (loss,) = rw_fwd(sch, 0.05, _f_loss, [h4, tgt_pad, rowmask], [], [], name='loss', n_reduce=1)

    sch.scale = BWD_RIDER_SHARE
    (dh4,) = rw_bwd(sch, 0.07, _f_loss, [h4, tgt_pad, rowmask], [], [jnp.ones((1, 1), F32)], name='loss',
                    n_reduce=1, nd_rows=(1, 2))

    dh3, dbr3 = ln_bwd(h3, br3, dh4, 3)
    dz3 = mm(sch, dbr3, w3_out, 'l3_out_dx', mode='nt')
    sch.push('l3_w_out', mm(sch, z3.T, dbr3, 'l3_out_dw', mode='nn', out_dtype=BF).reshape(N_DEV, -1, d_model),
             False, 1)
    do3, dg3 = rw_bwd(sch, 0.13, f_gn, [o3, g3], [], [dz3], name='l3_gate', row_dtypes=[F32, BF])
    dqr3, delta3 = _ride(sch, 0.7, _att_dq_call, q=qr3, k=kr3, v=v3, lgt=lgt, o=o3, lse=lse3, do=do3, **att3)
    dkr3, dv3 = _ride(sch, 0.8, _att_dkv_call, q=qr3, k=kr3, v=v3, lgt=lgt, lse_row=lse3.reshape(RET_HEADS, 1, -1),
                      delta_row=delta3.reshape(RET_HEADS, 1, -1), do=do3, **att3)
    dq3, dk3_ = rw_bwd(sch, 0.13, f_rope3, [q3, k3] + cs3, [], [dqr3, dkr3], name='l3_rope', nd_rows=(2, 3),
                       row_dtypes=[BF, BF])
    d3 = [dq3, dk3_, dv3, dg3]
    for n in range(4):
        dh3 = mm(sch, d3[n], w3[n], f'l3_in_dx{n}', mode='nt', add=dh3)
    hb3_t = hb3.T
    dw3 = [mm(sch, hb3_t, d3[n], f'l3_in_dw{n}', mode='nn', out_dtype=BF) for n in range(4)]
    sch.push('l3_w_in', _uncols(jnp.concatenate(dw3, axis=1)), False, 1)

    dh2, dbr2 = ln_bwd(h2, br2, dh3, 2)
    dz2 = mm(sch, dbr2, w2_out, 'l2_out_dx', mode='nt')
    sch.push('l2_w_out', mm(sch, z2.T, dbr2, 'l2_out_dw', mode='nn', out_dtype=BF).reshape(N_DEV, -1, d_model),
             False, 1)
    do2, dg2 = rw_bwd(sch, 0.1, _f_gate, [o2, g2], [], [dz2], name='l2_gate', row_dtypes=[F32, BF])
    dqcr2, delta2 = _ride(sch, 1.3, _att_dq_call, q=qcr2, k=kcat2, v=v2, lgt=no_decay, o=o2, lse=lse2, do=do2,
                          **att2)
    dkcat2, dv2_ = _ride(sch, 1.5, _att_dkv_call, q=qcr2, k=kcat2, v=v2, lgt=no_decay,
                         lse_row=lse2.reshape(MLA_HEADS, 1, -1), delta_row=delta2.reshape(MLA_HEADS, 1, -1),
                         do=do2, **att2)
    dkv2, dkrr2 = rw_bwd(sch, 0.14, f_kcat, [kv2, krr2], [], [dkcat2, dv2_], name='l2_kcat', row_dtypes=[BF, F32])
    (dqc2,) = rw_bwd(sch, 0.13, _f_rope_q, [qc2] + tabs, [], [dqcr2], name='l2_rope_q', nd_rows=(1, 2, 3),
                     row_dtypes=[BF])
    dckvn2 = mm(sch, dkv2, w2_ukv, 'l2_ukv_dx', mode='nt')
    dcqn2 = mm(sch, dqc2, w2_uq, 'l2_uq_dx', mode='nt')
    sch.push('l2_w_ukv', _uncols(mm(sch, ckvn2, dkv2, 'l2_ukv_dw', mode='tn', out_dtype=BF)), False, 1)
    dw_uq = mm(sch, cqn2, dqc2, 'l2_uq_dw', mode='tn', out_dtype=BF)
    dw_uq = dw_uq.reshape(Q_LORA, MLA_HEADS, MLA_QK)[:, :, :MLA_NOPE + MLA_ROPE].reshape(Q_LORA, -1)
    sch.push('l2_w_uq', _uncols(dw_uq), False, 1)
    dc2, dqn, dkvn = rw_bwd(sch, 0.05, _f_mla_pre, [c2] + tabs, p2, [dcqn2, dckvn2, dkrr2], name='l2_pre',
                            nd_rows=(1, 2, 3), row_dtypes=[BF])
    gS['l2_q_norm'], gS['l2_kv_norm'] = dqn.reshape(-1), dkvn.reshape(-1)
    dh2 = mm(sch, dg2, w2_g, 'l2_in_g_dx', mode='nt', add=dh2)
    dh2 = mm(sch, dc2, w2_lat, 'l2_in_c_dx', mode='nt', add=dh2)
    hb2_t = hb2.T
    dw2_g = mm(sch, hb2_t, dg2, 'l2_in_g_dw', mode='nn', out_dtype=BF)
    dw2_lat = mm(sch, hb2_t, dc2, 'l2_in_c_dw', mode='nn', out_dtype=BF)
    n_lat = Q_LORA + KV_LORA + MLA_ROPE
    sch.push('l2_w_in', _uncols(jnp.concatenate([dw2_g, dw2_lat[:, :n_lat]], axis=1)), False, 1)

    dh1, dbr1 = ln_bwd(h1, br1, dh2, 1)
    dz1 = mm(sch, dbr1, w1_out, 'l1_out_dx', mode='nt')
    sch.push('l1_w_out', mm(sch, z1.T, dbr1, 'l1_out_dw', mode='nn', out_dtype=BF).reshape(N_DEV, -1, d_model),
             False, 1)
    res = rw_bwd(sch, 0.11, _f_gate_pool, mixed1 + [g1], p1, [dz1], name='l1_gate', row_dtypes=[BF] * (n_win + 1))
    dmixed1, dg1, dscale = res[:n_win], res[n_win], res[n_win + 1]
    gS['l1_scale'] = dscale.reshape(-1)
    dps1 = [mm(sch, dmixed1[gi], w1_grp[gi], f'l1_grp{gi}_dx', mode='nt') for gi in range(n_win)]
    dw_grp = jnp.stack([mm(sch, ps1[gi], dmixed1[gi], f'l1_grp{gi}_dw', mode='tn', out_dtype=BF)
                        for gi in range(n_win)])
    sch.push('l1_w_grp', _unheads(dw_grp), False, 2)
    (du1,) = rw_bwd(sch, 0.09, f_pool, [u1, tidx], [], dps1, name='l1_pool', halo=(0,), nd_rows=(1,), tt=tt,
                    row_dtypes=[BF])
    dh1 = mm(sch, du1, w1_u, 'l1_in_u_dx', mode='nt', add=dh1)
    dh1 = mm(sch, dg1, w1_g, 'l1_in_g_dx', mode='nt', add=dh1)
    hb1_t = hb1.T
    dw1 = [mm(sch, hb1_t, du1, 'l1_in_u_dw', mode='nn', out_dtype=BF),
           mm(sch, hb1_t, dg1, 'l1_in_g_dw', mode='nn', out_dtype=BF)]
    sch.push('l1_w_in', _uncols(jnp.concatenate(dw1, axis=1)), False, 1)

    dh0, dbr0 = ln_bwd(h0, br0, dh1, 0)
    dz0 = mm(sch, dbr0, w0_out, 'l0_out_dx', mode='nt')
    sch.push('l0_w_out', mm(sch, z0.T, dbr0, 'l0_out_dw', mode='nn', out_dtype=BF).reshape(N_DEV, -1, d_model),
             False, 1)
    dhs0, dg0 = rw_bwd(sch, 0.1, _f_gate, [hs0, g0], [], [dz0], name='l0_gate', row_dtypes=[F32, BF])
    a_next = jnp.concatenate([a0[1:], jnp.ones_like(a0[:1])], axis=0)
    hs_prev = jnp.concatenate([jnp.zeros_like(hs0[:1]), hs0[:-1]], axis=0)
    dxin0, da0 = _ride(sch, 0.25, _scan_call, a=a_next, b=dhs0, mul=hs_prev, reverse=True, name='l0_scan_bwd')
    res = rw_bwd(sch, 0.3, f_lru, [u0], p0, [da0, dxin0], name='l0_lru', halo=(0,), tt=tt, row_dtypes=[BF])
    du0 = res[0]
    gS['l0_conv_b'], gS['l0_b_a'], gS['l0_b_x'], gS['l0_lam'] = [res[k].reshape(-1) for k in (2, 4, 6, 7)]
    sch.push('l0_w_a', _unheads(res[3]), False, 2)
    sch.push('l0_w_x', _unheads(res[5]), False, 2)
    sch.push('l0_conv_w', jnp.transpose(res[1].reshape(CONV_W, 1, N_DEV, -1), (2, 0, 1, 3)), False, 3)
    hb0_t = hb0.T
    dw0 = [mm(sch, hb0_t, du0, 'l0_in_u_dw', mode='nn', out_dtype=BF),
           mm(sch, hb0_t, dg0, 'l0_in_g_dw', mode='nn', out_dtype=BF)]
    sch.push('l0_w_in', _uncols(jnp.concatenate(dw0, axis=1)), False, 1)
    sch.scale = 1.0
    dh0 = mm(sch, du0, w0_u, 'l0_in_u_dx', mode='nt', add=dh0)
    dh0 = mm(sch, dg0, w0_g, 'l0_in_g_dx', mode='nt', add=dh0)
    sch.push('meta_tokens', _uncols(dh0[:N_META]), False, 1)

    return loss[0, 0], dh0[N_META:t_real], gS


def _as2d(a):
    return a.reshape(-1, a.shape[-1])


def kernel(x, meta_tokens, l0_w_in, l0_conv_w, l0_conv_b, l0_w_a, l0_b_a, l0_w_x, l0_b_x, l0_lam, l0_w_out, l0_ln_g, l0_ln_b, l1_w_in, l1_w_grp, l1_scale, l1_w_out, l1_ln_g, l1_ln_b, l2_w_in, l2_q_norm, l2_w_uq, l2_kv_norm, l2_w_ukv, l2_w_out, l2_ln_g, l2_ln_b, l3_w_in, l3_w_out, l3_ln_g, l3_ln_b, loss_target, m_meta_tokens, m_l0_w_in, m_l0_conv_w, m_l0_conv_b, m_l0_w_a, m_l0_b_a, m_l0_w_x, m_l0_b_x, m_l0_lam, m_l0_w_out, m_l0_ln_g, m_l0_ln_b, m_l1_w_in, m_l1_w_grp, m_l1_scale, m_l1_w_out, m_l1_ln_g, m_l1_ln_b, m_l2_w_in, m_l2_q_norm, m_l2_w_uq, m_l2_kv_norm, m_l2_w_ukv, m_l2_w_out, m_l2_ln_g, m_l2_ln_b, m_l3_w_in, m_l3_w_out, m_l3_ln_g, m_l3_ln_b, v_meta_tokens, v_l0_w_in, v_l0_conv_w, v_l0_conv_b, v_l0_w_a, v_l0_b_a, v_l0_w_x, v_l0_b_x, v_l0_lam, v_l0_w_out, v_l0_ln_g, v_l0_ln_b, v_l1_w_in, v_l1_w_grp, v_l1_scale, v_l1_w_out, v_l1_ln_g, v_l1_ln_b, v_l2_w_in, v_l2_q_norm, v_l2_w_uq, v_l2_kv_norm, v_l2_w_ukv, v_l2_w_out, v_l2_ln_g, v_l2_ln_b, v_l3_w_in, v_l3_w_out, v_l3_ln_g, v_l3_ln_b):
    args = (meta_tokens, l0_w_in, l0_conv_w, l0_conv_b, l0_w_a, l0_b_a, l0_w_x, l0_b_x, l0_lam, l0_w_out, l0_ln_g, l0_ln_b, l1_w_in, l1_w_grp, l1_scale, l1_w_out, l1_ln_g, l1_ln_b, l2_w_in, l2_q_norm, l2_w_uq, l2_kv_norm, l2_w_ukv, l2_w_out, l2_ln_g, l2_ln_b, l3_w_in, l3_w_out, l3_ln_g, l3_ln_b)
    moms = (m_meta_tokens, m_l0_w_in, m_l0_conv_w, m_l0_conv_b, m_l0_w_a, m_l0_b_a, m_l0_w_x, m_l0_b_x, m_l0_lam, m_l0_w_out, m_l0_ln_g, m_l0_ln_b, m_l1_w_in, m_l1_w_grp, m_l1_scale, m_l1_w_out, m_l1_ln_g, m_l1_ln_b, m_l2_w_in, m_l2_q_norm, m_l2_w_uq, m_l2_kv_norm, m_l2_w_ukv, m_l2_w_out, m_l2_ln_g, m_l2_ln_b, m_l3_w_in, m_l3_w_out, m_l3_ln_g, m_l3_ln_b)
    vels = (v_meta_tokens, v_l0_w_in, v_l0_conv_w, v_l0_conv_b, v_l0_w_a, v_l0_b_a, v_l0_w_x, v_l0_b_x, v_l0_lam, v_l0_w_out, v_l0_ln_g, v_l0_ln_b, v_l1_w_in, v_l1_w_grp, v_l1_scale, v_l1_w_out, v_l1_ln_g, v_l1_ln_b, v_l2_w_in, v_l2_q_norm, v_l2_w_uq, v_l2_kv_norm, v_l2_w_ukv, v_l2_w_out, v_l2_ln_g, v_l2_ln_b, v_l3_w_in, v_l3_w_out, v_l3_ln_g, v_l3_ln_b)
    W = dict(zip(WEIGHTS, args))
    M = dict(zip(WEIGHTS, moms))
    V = dict(zip(WEIGHTS, vels))

    seq = x.shape[1]
    t_real = N_META + seq
    t_pad = -(-t_real // ROW_ALIGN) * ROW_ALIGN
    tgt_pad = jnp.pad(loss_target[0], ((N_META, t_pad - t_real), (0, 0)))

    sch = _Schedule()
    for n in GATHER_ORDER:
        shard = W[n].astype(BF) if n in BIG else W[n]
        sch.push(n, shard, True, shard.ndim - 2)
    S = {n: W[n] for n in REPLICATED}

    loss, gx, gS = _train_local(sch, x[0], tgt_pad, S, t_pad=t_pad)

    flat = jnp.concatenate([gS[n].reshape(-1) for n in REPLICATED]).reshape(-1, 128)
    sch.push('small_grads', flat, True, 0)

    out_g, out_d, out_m, out_v = {}, {}, {}, {}
    order = ['l3_w_out', 'l3_w_in', 'l2_w_out', 'l2_w_ukv', 'l2_w_uq', 'l2_w_in', 'l1_w_out', 'l1_w_grp',
             'l1_w_in', 'l0_w_out', 'l0_w_a', 'l0_w_x', 'l0_conv_w', 'l0_w_in', 'meta_tokens']
    for n in order:
        shp = W[n].shape
        w2, m2, v2 = _as2d(W[n]), _as2d(M[n]), _as2d(V[n])
        parts = [p.reshape((N_DEV, -1, w2.shape[1])) for p in sch.get(n)]
        res = _ride(sch, w2.size * 1.5e-8, _adamw_call, contribs=parts, w=w2, m=m2, v=v2, name='adamw_' + n)
        out_g[n], out_d[n], out_m[n], out_v[n] = [r.reshape(shp) for r in res]
    cat = lambda D: jnp.concatenate([D[n].reshape(-1) for n in REPLICATED]).reshape(-1, 128)
    res = _adamw_call(sch.get('small_grads'), cat(W), cat(M), cat(V), name='adamw_small')[0]
    off = 0
    for n in REPLICATED:
        size = W[n].size
        for dst, r in zip((out_g, out_d, out_m, out_v), res):
            dst[n] = r.reshape(-1)[off:off + size].reshape(W[n].shape)
        off += size
    sch.flush()

    loss = lax.psum(loss, ("x", "y", "c"))
    return (loss, gx[None], *[out_g[n] for n in WEIGHTS], *[out_d[n] for n in WEIGHTS],
            *[out_m[n] for n in WEIGHTS], *[out_v[n] for n in WEIGHTS])
```

```python
import functools

import jax
import jax.numpy as jnp
from jax import lax
from jax.experimental import pallas as pl
from jax.experimental.pallas import tpu as pltpu

F32 = jnp.float32
BF = jnp.bfloat16

N_DEV = 8
N_META = 16
ALPHA = (2.0 * 4) ** 0.25
LN_EPS = 1e-5
RMS_EPS = 1e-6
ROPE_BASE = 10000.0
LRU_HEADS = 16
CONV_W = 4
LRU_C = 8.0
POOL_WINDOWS = (2, 4, 8, 16)
MLA_HEADS = 32
MLA_NOPE = 128
MLA_ROPE = 64
MLA_QK = 256
Q_LORA = 1024
KV_LORA = 512
RET_HEADS = 16
ADAM_LR = 0.001
ADAM_B1 = 0.9
ADAM_B2 = 0.999
ADAM_EPS = 1e-08
ADAM_WD = 0.01
ADAM_STEP = 10

ROW_ALIGN = 128
VMEM_LIMIT_BYTES = 56 * 1024 * 1024
TT_PREFS = (128, 64, 32, 16, 8)
ATT_PREFS = (384, 256, 128)
MM_M_PREFS = (1408, 1024, 512, 384, 256, 128)
MM_N_PREFS = (1024, 640, 512, 384, 256, 128)
MM_K_PREFS = (1408, 1024, 512, 384, 256, 128)
MM_VMEM_BUDGET = 40 * 1024 * 1024
ADAM_BLOCK_ELEMS = 128 * 1024
EXCH_MS_PER_MB = 0.0857
EXCH_CHUNK_MS = 0.1
MXU_FLOPS_PER_MS = 7.8e11
BWD_RIDER_SHARE = 0.6
ADAM_MS_PER_ELEM = 2.3e-8

WEIGHTS = ['meta_tokens', 'l0_w_in', 'l0_conv_w', 'l0_conv_b', 'l0_w_a', 'l0_b_a', 'l0_w_x', 'l0_b_x', 'l0_lam',
           'l0_w_out', 'l0_ln_g', 'l0_ln_b', 'l1_w_in', 'l1_w_grp', 'l1_scale', 'l1_w_out', 'l1_ln_g', 'l1_ln_b',
           'l2_w_in', 'l2_q_norm', 'l2_w_uq', 'l2_kv_norm', 'l2_w_ukv', 'l2_w_out', 'l2_ln_g', 'l2_ln_b',
           'l3_w_in', 'l3_w_out', 'l3_ln_g', 'l3_ln_b']
BIG = ['l0_w_in', 'l0_w_out', 'l1_w_in', 'l1_w_grp', 'l1_w_out', 'l2_w_in', 'l2_w_uq', 'l2_w_ukv', 'l2_w_out',
       'l3_w_in', 'l3_w_out']
SHARDED_F32 = ['meta_tokens', 'l0_conv_w', 'l0_w_a', 'l0_w_x']
REPLICATED = [n for n in WEIGHTS if n not in BIG and n not in SHARDED_F32]
GATHER_ORDER = ['meta_tokens', 'l0_w_in', 'l0_conv_w', 'l0_w_a', 'l0_w_x', 'l0_w_out', 'l1_w_in', 'l1_w_grp',
                'l1_w_out', 'l2_w_in', 'l2_w_uq', 'l2_w_ukv', 'l2_w_out', 'l3_w_in', 'l3_w_out']


def _pick(n, prefs):
    for p in prefs:
        if n % p == 0:
            return p
    return n


def _exchange_copies(jobs, in_refs, out_refs, send_sems, recv_sems, local_sems):
    x, y, c = lax.axis_index("x"), lax.axis_index("y"), lax.axis_index("c")
    me = 4 * x + 2 * y + c
    sibling = (x, y, 1 - c)
    chips = [(1 - x, y), (x, 1 - y), (1 - x, 1 - y)]
    peers = [sibling] + [(px, py, c) for px, py in chips] + [(px, py, 1 - c) for px, py in chips]
    first, arrivals, forwards, rest = [], [], [], []

    def dev(px, py, pc):
        return 4 * px + 2 * py + pc

    for n, (_, gather) in enumerate(jobs):
        def remote(p, src, slot, to, n=n):
            k = n * (N_DEV - 1) + p
            return pltpu.make_async_remote_copy(
                src_ref=src, dst_ref=out_refs[n].at[slot], send_sem=send_sems.at[k], recv_sem=recv_sems.at[k],
                device_id=to, device_id_type=pl.DeviceIdType.MESH)

        if gather:
            first += [remote(p, in_refs[n], me, peers[p]) for p in range(4)]
            for j, (px, py) in enumerate(chips):
                landed = out_refs[n].at[dev(px, py, c)]
                arrivals.append(remote(1 + j, landed, dev(px, py, c), peers[1 + j]))
                forwards.append(remote(4 + j, landed, dev(px, py, c), sibling))
        else:
            first += [remote(p, in_refs[n].at[dev(*peers[p])], me, peers[p]) for p in range(N_DEV - 1)]
        rest.append(pltpu.make_async_copy(in_refs[n] if gather else in_refs[n].at[me], out_refs[n].at[me],
                                          local_sems.at[n]))

    def start():
        for cp in first + rest:
            cp.start()

    def finish():
        for arrived, forward in zip(arrivals, forwards):
            arrived.wait_recv()
            forward.start()
        for cp in first:
            cp.wait_send()
        for n, (_, gather) in enumerate(jobs):
            for p in range(N_DEV - 1):
                if not (gather and 1 <= p <= 3):
                    k = n * (N_DEV - 1) + p
                    pltpu.make_async_remote_copy(
                        src_ref=out_refs[n].at[me], dst_ref=out_refs[n].at[me], send_sem=send_sems.at[k],
                        recv_sem=recv_sems.at[k], device_id=sibling, device_id_type=pl.DeviceIdType.MESH).wait_recv()
        for cp in forwards:
            cp.wait_send()
        for cp in rest:
            cp.wait()

    return start, finish


def _exchange_shapes(jobs):
    shapes = []
    for arr, gather in jobs:
        blk = arr.shape if gather else arr.shape[1:]
        shapes.append(jax.ShapeDtypeStruct((N_DEV,) + tuple(blk), arr.dtype))
    return shapes


def _exchange_scratch(jobs):
    n = len(jobs)
    return [pltpu.SemaphoreType.DMA((n * (N_DEV - 1),)), pltpu.SemaphoreType.DMA((n * (N_DEV - 1),)),
            pltpu.SemaphoreType.DMA((n,))]


def _call(body, *, name, grid, in_specs, out_specs, out_shape, args, scratch=(), sem=None, jobs=()):
    jobs = list(jobs)
    n_in, n_out, n_sc, n_job = len(in_specs), len(out_specs), len(scratch), len(jobs)
    hbm = pl.BlockSpec(memory_space=pl.ANY)

    def kern(*refs):
        ins = refs[:n_in]
        job_ins = refs[n_in:n_in + n_job]
        pos = n_in + n_job
        outs = refs[pos:pos + n_out]
        job_outs = refs[pos + n_out:pos + n_out + n_job]
        pos += n_out + n_job
        scr = refs[pos:pos + n_sc]
        if n_job:
            ids = [pl.program_id(d) for d in range(len(grid))]
            first = functools.reduce(jnp.logical_and, [i == 0 for i in ids])
            last = functools.reduce(jnp.logical_and, [i == g - 1 for i, g in zip(ids, grid)])
            start, finish = _exchange_copies(jobs, job_ins, job_outs, *refs[pos + n_sc:])
            pl.when(first)(start)

        body(ins, outs, scr)
        if n_job:
            pl.when(last)(finish)

    kw = dict(vmem_limit_bytes=VMEM_LIMIT_BYTES)
    if sem is not None:
        kw['dimension_semantics'] = tuple("arbitrary" for _ in grid) if n_job else sem
    res = pl.pallas_call(
        kern, name=name, grid=grid,
        in_specs=list(in_specs) + [hbm] * n_job,
        out_specs=list(out_specs) + [hbm] * n_job,
        out_shape=list(out_shape) + _exchange_shapes(jobs),
        scratch_shapes=list(scratch) + (_exchange_scratch(jobs) if n_job else []),
        compiler_params=pltpu.CompilerParams(**kw),
    )(*args, *[a for a, _ in jobs])
    return list(res[:n_out]), list(res[n_out:])


def _exchange_alone(jobs, name):
    def body(*refs):
        n = len(jobs)
        start, finish = _exchange_copies(jobs, refs[:n], refs[n:2 * n], *refs[2 * n:])
        start()
        finish()

    hbm = pl.BlockSpec(memory_space=pl.ANY)
    return list(pl.pallas_call(
        body, name=name, in_specs=[hbm] * len(jobs), out_specs=[hbm] * len(jobs),
        out_shape=_exchange_shapes(jobs), scratch_shapes=_exchange_scratch(jobs),
    )(*[a for a, _ in jobs]))


class _Schedule:
    def __init__(self):
        self.pending = []
        self.done = {}
        self.chunks = {}
        self.count = 0
        self.scale = 1.0

    def push(self, name, arr, gather, row_axis, chunk_ms=None):
        mb = arr.size * arr.dtype.itemsize / (1 if gather else N_DEV) / 1e6
        cost = mb * EXCH_MS_PER_MB * (0.5 if gather else 1.0)
        rows = arr.shape[row_axis]
        n = 1
        leading = row_axis == (0 if gather else 1)
        target = chunk_ms or (EXCH_CHUNK_MS if gather else 2 * EXCH_CHUNK_MS)
        while leading and cost / n > target and rows % (2 * n) == 0 and rows // (2 * n) >= 16:
            n *= 2
        self.chunks[name] = n
        step = rows // n
        for k in range(n):
            piece = lax.slice_in_dim(arr, k * step, (k + 1) * step, axis=row_axis) if n > 1 else arr
            self.pending.append(((name, k), piece, gather, cost / n))

    def take(self, budget):
        jobs, spent = [], 0.0
        budget *= self.scale
        while self.pending and spent + 0.5 * self.pending[0][3] <= budget:
            job = self.pending.pop(0)
            jobs.append(job)
            spent += job[3]
        return jobs

    def deliver(self, jobs, results):
        for (key, _, _, _), r in zip(jobs, results):
            self.done[key] = r

    def get(self, name):
        mine = [j for j in self.pending if j[0][0] == name]
        if mine:
            self.pending = [j for j in self.pending if j[0][0] != name]
            self.count += 1
            self.deliver(mine, _exchange_alone([(j[1], j[2]) for j in mine], f'exchange_{self.count}_{name}'))
        parts = [self.done.pop((name, k)) for k in range(self.chunks[name])]
        return parts

    def flush(self):
        if self.pending:
            jobs, self.pending = self.pending, []
            self.count += 1
            self.deliver(jobs, _exchange_alone([(j[1], j[2]) for j in jobs], f'exchange_{self.count}_rest'))


def _ride(sch, budget, fn, **kw):
    jobs = sch.take(budget) if sch is not None else []
    outs, exch = fn(jobs=[(j[1], j[2]) for j in jobs], **kw)
    if jobs:
        sch.deliver(jobs, exch)
    return outs


@functools.partial(jax.custom_vjp, nondiff_argnums=(1, 2))
def _roll(x, shift, axis):
    return pltpu.roll(x, shift, axis)


def _roll_fwd(x, shift, axis):
    return pltpu.roll(x, shift, axis), None


def _roll_bwd(shift, axis, _, g):
    n = g.shape[axis]
    return (pltpu.roll(g, (n - shift) % n, axis),)


_roll.defvjp(_roll_fwd, _roll_bwd)


@jax.custom_vjp
def _bdot(x, w):
    return jnp.dot(x.astype(BF), w.astype(BF), preferred_element_type=F32)


def _bdot_fwd(x, w):
    return _bdot(x, w), (x, w)


def _bdot_bwd(res, g):
    x, w = res
    gb = g.astype(BF)
    dx = lax.dot_general(gb, w.astype(BF), (((1,), (1,)), ((), ())), preferred_element_type=F32)
    dw = lax.dot_general(x.astype(BF), gb, (((0,), (0,)), ((), ())), preferred_element_type=F32)
    return dx, dw


_bdot.defvjp(_bdot_fwd, _bdot_bwd)


def _silu(g):
    return g * jax.nn.sigmoid(g)


def _softplus(x):
    return jnp.maximum(x, 0.0) + jnp.log1p(jnp.exp(-jnp.abs(x)))


def _mm_call(a, b, *, mode, out_dtype, name, add=None, jobs=()):
    if mode == 'nn':
        (mo, kc), (_, no) = a.shape, b.shape
    elif mode == 'nt':
        (mo, kc), (no, _) = a.shape, b.shape
    else:
        (kc, mo), (_, no) = a.shape, b.shape
    tm = _pick(mo, MM_M_PREFS)
    tn = no if no <= 2048 and no % 512 != 0 else _pick(no, MM_N_PREFS)
    out_bytes = tm * tn * (2 * jnp.dtype(out_dtype).itemsize + 4 + (8 if add is not None else 0))
    tk = None
    for cand in MM_K_PREFS:
        if kc % cand == 0:
            tk = cand
            if out_bytes + 2 * cand * (tm * a.dtype.itemsize + tn * b.dtype.itemsize) <= MM_VMEM_BUDGET:
                break
    tk = tk or kc
    nk = kc // tk
    if mode == 'nn':
        a_spec = pl.BlockSpec((tm, tk), lambda i, j, k: (i, k))
        b_spec = pl.BlockSpec((tk, tn), lambda i, j, k: (k, j))
        dims = (((1,), (0,)), ((), ()))
    elif mode == 'nt':
        a_spec = pl.BlockSpec((tm, tk), lambda i, j, k: (i, k))
        b_spec = pl.BlockSpec((tn, tk), lambda i, j, k: (j, k))
        dims = (((1,), (1,)), ((), ()))
    else:
        a_spec = pl.BlockSpec((tk, tm), lambda i, j, k: (k, i))
        b_spec = pl.BlockSpec((tk, tn), lambda i, j, k: (k, j))
        dims = (((0,), (0,)), ((), ()))
    o_spec = pl.BlockSpec((tm, tn), lambda i, j, k: (i, j))

    def body(ins, outs, scr):
        a_ref, b_ref = ins[0], ins[1]
        (o_ref,), (acc_ref,) = outs, scr
        k = pl.program_id(2)

        @pl.when(k == 0)
        def _():
            acc_ref[...] = jnp.zeros_like(acc_ref) if add is None else ins[2][...]

        acc_ref[...] += lax.dot_general(a_ref[...].astype(BF), b_ref[...].astype(BF), dims,
                                        preferred_element_type=F32)

        @pl.when(k == nk - 1)
        def _():
            o_ref[...] = acc_ref[...].astype(o_ref.dtype)

    outs, exch = _call(
        body, name=name, grid=(mo // tm, no // tn, nk),
        in_specs=[a_spec, b_spec] + ([o_spec] if add is not None else []),
        out_specs=[o_spec], out_shape=[jax.ShapeDtypeStruct((mo, no), out_dtype)],
        args=[a, b] + ([add] if add is not None else []),
        scratch=[pltpu.VMEM((tm, tn), F32)], sem=("parallel", "parallel", "arbitrary"), jobs=jobs)
    return outs, exch


def mm(sch, a, b, name, mode='nn', out_dtype=F32, add=None):
    if mode == 'nn':
        flops = 2.0 * a.shape[0] * a.shape[1] * b.shape[1]
    elif mode == 'nt':
        flops = 2.0 * a.shape[0] * a.shape[1] * b.shape[0]
    else:
        flops = 2.0 * a.shape[0] * a.shape[1] * b.shape[1]
    return _ride(sch, flops / MXU_FLOPS_PER_MS, _mm_call, a=a, b=b, mode=mode, out_dtype=out_dtype, name=name,
                 add=add)[0]


def _full_spec(p):
    nd = p.ndim
    return pl.BlockSpec(p.shape, lambda i: (0,) * nd)


def _load_rows(refs, n_rows, halo, step_is_first):
    cur, prev, pos = [], [], 0
    for r in range(n_rows):
        cur.append(refs[pos][...].astype(F32))
        pos += 1
        if r in halo:
            keep = jnp.where(step_is_first, 0.0, 1.0).astype(F32)
            prev.append(refs[pos][...].astype(F32) * keep)
            pos += 1
        else:
            prev.append(None)
    return cur, prev, pos


def _join(cur, prev):
    return [c if p is None else jnp.concatenate([p, c], axis=0) for c, p in zip(cur, prev)]


def _rw_fwd(f, rows, params, outs, *, name, n_reduce=0, halo=(), tt=None, jobs=()):
    t_len = rows[0].shape[0]
    tt = tt or _pick(t_len, TT_PREFS)
    nt = t_len // tt
    n_rows, n_par, n_out = len(rows), len(params), len(outs)

    def body(ins, orefs, scr):
        i = pl.program_id(0)
        cur, prev, pos = _load_rows(ins, n_rows, halo, i == 0)
        pvals = [ins[pos + k][...] for k in range(n_par)]
        res = f(_join(cur, prev), pvals)
        n_f = len(res) - n_reduce
        for k, (_, _, src) in enumerate(outs):
            orefs[k][...] = res[src].astype(orefs[k].dtype)
        for k in range(n_reduce):
            ref, val = orefs[n_out + k], res[n_f + k]

            @pl.when(i == 0)
            def _():
                ref[...] = val

            @pl.when(i > 0)
            def _():
                ref[...] += val

    in_specs, args = [], []
    for r, x in enumerate(rows):
        c = x.shape[1]
        in_specs.append(pl.BlockSpec((tt, c), lambda i: (i, 0)))
        args.append(x)
        if r in halo:
            in_specs.append(pl.BlockSpec((tt, c), lambda i: (jnp.maximum(i - 1, 0), 0)))
            args.append(x)
    for p in params:
        in_specs.append(_full_spec(p))
        args.append(p)
    out_specs = [pl.BlockSpec((tt, c), lambda i: (i, 0)) for c, _, _ in outs]
    out_shape = [jax.ShapeDtypeStruct((t_len, c), dt) for c, dt, _ in outs]
    for _ in range(n_reduce):
        out_specs.append(pl.BlockSpec((1, 1), lambda i: (0, 0)))
        out_shape.append(jax.ShapeDtypeStruct((1, 1), F32))
    return _call(body, name=name, grid=(nt,), in_specs=in_specs, out_specs=out_specs, out_shape=out_shape,
                 args=args, sem=("arbitrary",), jobs=jobs)


def _rw_bwd(f, rows, params, cts, *, name, n_reduce=0, halo=(), nd_rows=(), nd_params=(), tt=None,
            row_dtypes=None, jobs=()):
    t_len = rows[0].shape[0]
    tt = tt or _pick(t_len, TT_PREFS)
    nt = t_len // tt
    n_rows, n_par, n_ct = len(rows), len(params), len(cts)
    d_rows = [r for r in range(n_rows) if r not in nd_rows]
    d_pars = [k for k in range(n_par) if k not in nd_params]
    h_rows = [r for r in d_rows if r in halo]

    def blk(j):
        return nt - 1 - j

    def body(ins, orefs, carry_refs):
        j = pl.program_id(0)
        cur, prev, pos = _load_rows(ins, n_rows, halo, blk(j) == 0)
        pvals = [ins[pos + k][...] for k in range(n_par)]
        pos += n_par
        ct_vals = [ins[pos + k][...].astype(F32) for k in range(n_ct)]

        def g(dcur, dprev, dpar):
            c, p, q = list(cur), list(prev), list(pvals)
            for r, v in zip(d_rows, dcur):
                c[r] = v
            for r, v in zip(h_rows, dprev):
                p[r] = v
            for k, v in zip(d_pars, dpar):
                q[k] = v
            return tuple(f(_join(c, p), q))

        _, vjp = jax.vjp(g, [cur[r] for r in d_rows], [prev[r] for r in h_rows], [pvals[k] for k in d_pars])
        g_cur, g_prev, g_par = vjp(tuple(ct_vals))

        for n, r in enumerate(d_rows):
            if r in halo:
                cref = carry_refs[h_rows.index(r)]

                @pl.when(j == 0)
                def _():
                    cref[...] = jnp.zeros_like(cref)

                orefs[n][...] = (g_cur[n] + cref[...]).astype(orefs[n].dtype)
                cref[...] = g_prev[h_rows.index(r)]
            else:
                orefs[n][...] = g_cur[n].astype(orefs[n].dtype)
        for n in range(len(d_pars)):
            ref, val = orefs[len(d_rows) + n], g_par[n]

            @pl.when(j == 0)
            def _():
                ref[...] = val

            @pl.when(j > 0)
            def _():
                ref[...] += val

    in_specs, args = [], []
    for r, x in enumerate(rows):
        c = x.shape[1]
        in_specs.append(pl.BlockSpec((tt, c), lambda j: (blk(j), 0)))
        args.append(x)
        if r in halo:
            in_specs.append(pl.BlockSpec((tt, c), lambda j: (jnp.maximum(blk(j) - 1, 0), 0)))
            args.append(x)
    for p in params:
        in_specs.append(_full_spec(p))
        args.append(p)
    for ct in cts:
        if ct.shape == (1, 1):
            in_specs.append(pl.BlockSpec((1, 1), lambda j: (0, 0)))
        else:
            in_specs.append(pl.BlockSpec((tt, ct.shape[1]), lambda j: (blk(j), 0)))
        args.append(ct)
    out_specs, out_shape, scratch = [], [], []
    for n, r in enumerate(d_rows):
        c = rows[r].shape[1]
        out_specs.append(pl.BlockSpec((tt, c), lambda j: (blk(j), 0)))
        out_shape.append(jax.ShapeDtypeStruct((t_len, c), row_dtypes[n] if row_dtypes else F32))
        if r in halo:
            scratch.append(pltpu.VMEM((tt, c), F32))
    for k in d_pars:
        out_specs.append(_full_spec(params[k]))
        out_shape.append(jax.ShapeDtypeStruct(params[k].shape, F32))
    return _call(body, name=name + '_bwd', grid=(nt,), in_specs=in_specs, out_specs=out_specs,
                 out_shape=out_shape, args=args, scratch=scratch, sem=("arbitrary",), jobs=jobs)


def rw_fwd(sch, budget, f, rows, params, outs, **kw):
    return _ride(sch, budget, functools.partial(_rw_fwd, f, list(rows), list(params), outs), **kw)


def rw_bwd(sch, budget, f, rows, params, cts, **kw):
    return _ride(sch, budget, functools.partial(_rw_bwd, f, list(rows), list(params), list(cts)), **kw)


def _scan_call(a, b, mul, *, reverse, name, jobs=()):
    t_len, c_len = a.shape
    tt = _pick(t_len, TT_PREFS)
    tc = _pick(c_len, (512, 256, 128))
    nt = t_len // tt

    def body(ins, orefs, scr):
        (carry,) = scr
        t = pl.program_id(1)
        av, bv = ins[0][...], ins[1][...]
        row = lax.broadcasted_iota(jnp.int32, av.shape, 0)
        s = 1
        while s < tt:
            if reverse:
                ok = row < tt - s
                a_sh = jnp.where(ok, pltpu.roll(av, tt - s, 0), 1.0)
                b_sh = jnp.where(ok, pltpu.roll(bv, tt - s, 0), 0.0)
            else:
                ok = row >= s
                a_sh = jnp.where(ok, pltpu.roll(av, s, 0), 1.0)
                b_sh = jnp.where(ok, pltpu.roll(bv, s, 0), 0.0)
            bv = av * b_sh + bv
            av = av * a_sh
            s *= 2

        @pl.when(t == 0)
        def _():
            carry[...] = jnp.zeros_like(carry)

        hs = bv + av * carry[...]
        orefs[0][...] = hs
        edge = 0 if reverse else tt - 1
        carry[...] = orefs[0][edge:edge + 1, :]
        if mul is not None:
            orefs[1][...] = hs * ins[2][...]

    def idx(c, t):
        return ((nt - 1 - t) if reverse else t, c)

    spec = pl.BlockSpec((tt, tc), idx)
    n_in, n_out = (2, 1) if mul is None else (3, 2)
    return _call(body, name=name, grid=(c_len // tc, nt), in_specs=[spec] * n_in, out_specs=[spec] * n_out,
                 out_shape=[jax.ShapeDtypeStruct((t_len, c_len), F32)] * n_out,
                 args=[a, b] if mul is None else [a, b, mul],
                 scratch=[pltpu.VMEM((1, tc), F32)], sem=("parallel", "arbitrary"), jobs=jobs)


NT = (((1,), (1,)), ((), ()))


def _for_blocks(lo, hi, fn, group=2):
    n = hi - lo

    def trip(p, c):
        for g in range(group):
            fn(lo + group * p + g)
        return c

    lax.fori_loop(0, n // group, trip, 0)
    for g in range(1, group):
        @pl.when(n % group >= g)
        def _():
            fn(hi - (n % group) + g - 1)


def _att_weights(s, rel, diag, lg, softmax, scale, t_axis):
    row = lax.broadcasted_iota(jnp.int32, s.shape, t_axis)
    col = lax.broadcasted_iota(jnp.int32, s.shape, 1 - t_axis)
    if softmax:
        s = s * scale
        return jnp.where(row >= col, s, -1e30) if diag else s
    diff = (rel + row - col).astype(F32)
    dec = jnp.exp(jnp.maximum(diff, 0.0) * lg)
    return jnp.where(diff >= 0.0, dec, 0.0) if diag else dec


def _att_fwd_call(q, k, v, lgt, *, heads, softmax, scale, name, jobs=()):
    t_len = q.shape[0]
    dqk, dv = q.shape[1] // heads, v.shape[1] // heads
    blk = _pick(t_len, ATT_PREFS)
    nb = t_len // blk

    lanes = 128
    n_fold = blk // lanes

    def body(ins, orefs, scr):
        q_ref, k_ref, v_ref, lg_ref = ins
        o_ref, lse_ref = orefs
        s_sc, m_sc, acc_sc = scr
        i = pl.program_id(1)
        lg = lg_ref[0:1, 0:1]
        qb = q_ref[...]
        acc_sc[...] = jnp.zeros_like(acc_sc)

        def block_rows(j):
            return pl.ds(pl.multiple_of(j * blk, blk), blk)

        if not softmax:
            def step(j, diag):
                s = lax.dot_general(qb, k_ref[block_rows(j), :], NT, preferred_element_type=F32)
                w = _att_weights(s, (i - j) * blk, diag, lg, softmax, scale, 0)
                acc_sc[...] += jnp.dot((s * w).astype(BF), v_ref[block_rows(j), :], preferred_element_type=F32)

            _for_blocks(0, i, lambda j: step(j, False), group=3)
            step(i, True)
            o_ref[...] = acc_sc[...]
            lse_ref[...] = jnp.zeros_like(lse_ref)
            return

        m_sc[...] = jnp.full_like(m_sc, -1e30)

        def score(j, diag):
            s = lax.dot_general(qb, k_ref[block_rows(j), :], NT, preferred_element_type=F32)
            w = _att_weights(s, (i - j) * blk, diag, lg, softmax, scale, 0)
            s_sc[j] = w
            top = w[:, :lanes]
            for n in range(1, n_fold):
                top = jnp.maximum(top, w[:, n * lanes:(n + 1) * lanes])
            m_sc[...] = jnp.maximum(m_sc[...], top)

        _for_blocks(0, i, lambda j: score(j, False), group=3)
        score(i, True)
        m = jnp.max(m_sc[...], axis=-1, keepdims=True)
        ones = jnp.ones((blk, lanes), BF)

        def accumulate(j):
            p = jnp.exp(s_sc[j] - m).astype(BF)
            v_aug = jnp.concatenate([v_ref[block_rows(j), :], ones], axis=1)
            acc_sc[...] += jnp.dot(p, v_aug, preferred_element_type=F32)

        _for_blocks(0, i + 1, accumulate, group=3)
        acc = acc_sc[...]
        l = acc[:, dv:dv + 1]
        o_ref[...] = acc[:, :dv] / l
        lse_ref[...] = m + jnp.log(l)

    scratch = [pltpu.VMEM((nb, blk, blk), F32), pltpu.VMEM((blk, lanes), F32),
               pltpu.VMEM((blk, dv + lanes if softmax else dv), F32)]
    return _call(
        body, name=name, grid=(heads, nb),
        in_specs=[
            pl.BlockSpec((blk, dqk), lambda h, i: (i, h)),
            pl.BlockSpec((t_len, dqk), lambda h, i: (0, h)),
            pl.BlockSpec((t_len, dv), lambda h, i: (0, h)),
            pl.BlockSpec((None, 1, 128), lambda h, i: (h, 0, 0)),
        ],
        out_specs=[
            pl.BlockSpec((blk, dv), lambda h, i: (i, h)),
            pl.BlockSpec((None, blk, 1), lambda h, i: (h, i, 0)),
        ],
        out_shape=[jax.ShapeDtypeStruct((t_len, heads * dv), F32), jax.ShapeDtypeStruct((heads, t_len, 1), F32)],
        args=[q, k, v, lgt],
        scratch=scratch if softmax else [pltpu.VMEM((8, lanes), F32), scratch[1], scratch[2]],
        sem=("parallel", "arbitrary"), jobs=jobs)


def _att_dq_call(q, k, v, lgt, o, lse, do, *, heads, softmax, scale, name, jobs=()):
    t_len = q.shape[0]
    dqk, dv = q.shape[1] // heads, v.shape[1] // heads
    blk = _pick(t_len, ATT_PREFS)
    nb = t_len // blk

    def body(ins, orefs, scr):
        q_ref, k_ref, v_ref, lg_ref, o_ref, do_ref, lse_ref = ins
        dq_ref, delta_ref = orefs
        (acc,) = scr
        i = pl.program_id(1)
        lg = lg_ref[0:1, 0:1]
        qb = q_ref[...]
        do = do_ref[...]
        dob = do.astype(BF)
        delta = jnp.sum(do * o_ref[...], axis=-1, keepdims=True)
        lse = lse_ref[...]
        delta_ref[...] = delta
        acc[...] = jnp.zeros_like(acc)

        def step(j, diag):
            rows = pl.ds(pl.multiple_of(j * blk, blk), blk)
            kb = k_ref[rows, :]
            s = lax.dot_general(qb, kb, NT, preferred_element_type=F32)
            w = _att_weights(s, (i - j) * blk, diag, lg, softmax, scale, 0)
            dp = lax.dot_general(dob, v_ref[rows, :], NT, preferred_element_type=F32)
            ds = jnp.exp(w - lse) * (dp - delta) * scale if softmax else dp * w
            acc[...] += jnp.dot(ds.astype(BF), kb, preferred_element_type=F32)

        _for_blocks(0, i, lambda j: step(j, False))
        step(i, True)
        dq_ref[...] = acc[...]

    return _call(
        body, name=name + '_dq', grid=(heads, nb),
        in_specs=[
            pl.BlockSpec((blk, dqk), lambda h, i: (i, h)),
            pl.BlockSpec((t_len, dqk), lambda h, i: (0, h)),
            pl.BlockSpec((t_len, dv), lambda h, i: (0, h)),
            pl.BlockSpec((None, 1, 128), lambda h, i: (h, 0, 0)),
            pl.BlockSpec((blk, dv), lambda h, i: (i, h)),
            pl.BlockSpec((blk, dv), lambda h, i: (i, h)),
            pl.BlockSpec((None, blk, 1), lambda h, i: (h, i, 0)),
        ],
        out_specs=[pl.BlockSpec((blk, dqk), lambda h, i: (i, h)),
                   pl.BlockSpec((None, blk, 1), lambda h, i: (h, i, 0))],
        out_shape=[jax.ShapeDtypeStruct(q.shape, F32), jax.ShapeDtypeStruct((heads, t_len, 1), F32)],
        args=[q, k, v, lgt, o, do, lse],
        scratch=[pltpu.VMEM((blk, dqk), F32)], sem=("parallel", "arbitrary"), jobs=jobs)


def _att_dkv_call(q, k, v, lgt, lse_row, delta_row, do, *, heads, softmax, scale, name, jobs=()):
    t_len = q.shape[0]
    dqk, dv = q.shape[1] // heads, v.shape[1] // heads
    blk = _pick(t_len, ATT_PREFS)
    nb = t_len // blk

    def body(ins, orefs, scr):
        q_ref, k_ref, v_ref, lg_ref, do_ref, lse_ref, delta_ref = ins
        dk_acc, dv_acc = scr
        j = pl.program_id(1)
        lg = lg_ref[0:1, 0:1]
        kb, vb = k_ref[...], v_ref[...]
        dk_acc[...] = jnp.zeros_like(dk_acc)
        dv_acc[...] = jnp.zeros_like(dv_acc)

        def step(i, diag):
            rows = pl.ds(pl.multiple_of(i * blk, blk), blk)
            qb = q_ref[rows, :]
            dob = do_ref[rows, :].astype(BF)
            s = lax.dot_general(kb, qb, NT, preferred_element_type=F32)
            w = _att_weights(s, (i - j) * blk, diag, lg, softmax, scale, 1)
            dp = lax.dot_general(vb, dob, NT, preferred_element_type=F32)
            if softmax:
                p = jnp.exp(w - lse_ref[:, rows])
                ds = p * (dp - delta_ref[:, rows]) * scale
            else:
                p, ds = s * w, dp * w
            dv_acc[...] += jnp.dot(p.astype(BF), dob, preferred_element_type=F32)
            dk_acc[...] += jnp.dot(ds.astype(BF), qb, preferred_element_type=F32)

        step(j, True)
        _for_blocks(j + 1, nb, lambda i: step(i, False), group=3)
        orefs[0][...] = dk_acc[...]
        orefs[1][...] = dv_acc[...]

    return _call(
        body, name=name + '_dkv', grid=(heads, nb),
        in_specs=[
            pl.BlockSpec((t_len, dqk), lambda h, j: (0, h)),
            pl.BlockSpec((blk, dqk), lambda h, j: (j, h)),
            pl.BlockSpec((blk, dv), lambda h, j: (j, h)),
            pl.BlockSpec((None, 1, 128), lambda h, j: (h, 0, 0)),
            pl.BlockSpec((t_len, dv), lambda h, j: (0, h)),
            pl.BlockSpec((None, 1, t_len), lambda h, j: (h, 0, 0)),
            pl.BlockSpec((None, 1, t_len), lambda h, j: (h, 0, 0)),
        ],
        out_specs=[
            pl.BlockSpec((blk, dqk), lambda h, j: (j, h)),
            pl.BlockSpec((blk, dv), lambda h, j: (j, h)),
        ],
        out_shape=[jax.ShapeDtypeStruct(k.shape, F32), jax.ShapeDtypeStruct(v.shape, F32)],
        args=[q, k, v, lgt, do, lse_row, delta_row],
        scratch=[pltpu.VMEM((blk, dqk), F32), pltpu.VMEM((blk, dv), F32)],
        sem=("parallel", "arbitrary"), jobs=jobs)


def _adamw_call(contribs, w, m, v, *, name, jobs=()):
    r_len, c_len = w.shape
    n_chunk = len(contribs)
    r_chunk = r_len // n_chunk
    cap = max(min(ADAM_BLOCK_ELEMS, 6 * ADAM_BLOCK_ELEMS // n_chunk) // c_len, 1)
    tr = r_chunk
    for cand in (512, 256, 128, 64, 32, 16):
        if cand <= cap and r_chunk % cand == 0:
            tr = cand
            break
    per = r_chunk // tr

    def body(ins, orefs, scr):
        w_ref, m_ref, v_ref = ins[n_chunk:]
        g_ref, d_ref, mo_ref, vo_ref = orefs
        i = pl.program_id(0)

        def update(c_ref):
            g = c_ref[0].astype(F32)
            for n in range(1, N_DEV):
                g = g + c_ref[n].astype(F32)
            m_new = ADAM_B1 * m_ref[...] + (1.0 - ADAM_B1) * g
            v_new = ADAM_B2 * v_ref[...] + (1.0 - ADAM_B2) * jnp.square(g)
            m_hat = m_new / (1.0 - ADAM_B1 ** ADAM_STEP)
            v_hat = v_new / (1.0 - ADAM_B2 ** ADAM_STEP)
            g_ref[...] = g
            d_ref[...] = -ADAM_LR * (m_hat / (jnp.sqrt(v_hat) + ADAM_EPS) + ADAM_WD * w_ref[...])
            mo_ref[...] = m_new
            vo_ref[...] = v_new

        if n_chunk == 1:
            update(ins[0])
        else:
            for n in range(n_chunk):
                @pl.when(i // per == n)
                def _():
                    update(ins[n])

    spec = pl.BlockSpec((tr, c_len), lambda i: (i, 0))
    c_specs = [pl.BlockSpec((N_DEV, tr, c_len), functools.partial(
        lambda i, n: (0, jnp.clip(i - n * per, 0, per - 1), 0), n=n)) for n in range(n_chunk)]
    return _call(body, name=name, grid=(r_len // tr,), in_specs=c_specs + [spec, spec, spec],
                 out_specs=[spec] * 4, out_shape=[jax.ShapeDtypeStruct((r_len, c_len), F32)] * 4,
                 args=list(contribs) + [w, m, v], sem=("arbitrary",), jobs=jobs)


def _rope_lanes(x, cc, s_lo, s_hi):
    return x * cc + _roll(x, 32, 1) * s_lo + _roll(x, 96, 1) * s_hi


def _f_lru(tt, branch):
    lb = branch // LRU_HEADS

    def f(rows, params):
        (xcat,) = rows
        cw, cb, wa, ba, wx, bx, lam = params
        conv = cb
        for j in range(CONV_W):
            sh = CONV_W - 1 - j
            xs = xcat if sh == 0 else _roll(xcat, sh, 0)
            conv = conv + cw[j:j + 1, :] * xs[tt:, :]
        rs, gs = [], []
        for h in range(LRU_HEADS):
            ub = conv[:, h * lb:(h + 1) * lb]
            rs.append(_bdot(ub, wa[h]))
            gs.append(_bdot(ub, wx[h]))
        r = jax.nn.sigmoid(jnp.concatenate(rs, axis=-1) + ba)
        gate = jax.nn.sigmoid(jnp.concatenate(gs, axis=-1) + bx)
        log_a = LRU_C * r * (-_softplus(-lam))
        a = jnp.exp(log_a)
        one_minus_a2 = -jnp.tanh(log_a) * (jnp.exp(2.0 * log_a) + 1.0)
        return a, (conv * gate) * jnp.sqrt(one_minus_a2)

    return f


def _f_gate(rows, params):
    hs, g = rows
    return (hs * _silu(g),)


def _f_ln(rows, params):
    h, br = rows
    g, b = params
    pre = ALPHA * h + br
    mu = jnp.mean(pre, axis=-1, keepdims=True)
    var = jnp.mean(jnp.square(pre - mu), axis=-1, keepdims=True)
    return ((pre - mu) * lax.rsqrt(var + LN_EPS) * g + b,)


def _f_pool(tt, branch):
    grp = branch // len(POOL_WINDOWS)

    def f(rows, params):
        xcat, tidx = rows
        sums, acc, w = [], xcat, 1
        while w < POOL_WINDOWS[-1]:
            acc = acc + _roll(acc, w, 0)
            w *= 2
            sums.append(acc[tt:, :])
        u = xcat[tt:, :]
        outs = []
        for gi, w in enumerate(POOL_WINDOWS):
            sl = slice(gi * grp, (gi + 1) * grp)
            outs.append(sums[gi][:, sl] / jnp.minimum(tidx + 1.0, float(w)) - u[:, sl])
        return tuple(outs)

    return f


def _f_gate_pool(rows, params):
    m0, m1, m2, m3, g = rows
    (scale,) = params
    return (jnp.concatenate([m0, m1, m2, m3], axis=-1) * scale * _silu(g),)


def _rms(x, g):
    return x * lax.rsqrt(jnp.mean(jnp.square(x), axis=-1, keepdims=True) + RMS_EPS) * g


def _f_mla_pre(rows, params):
    c, cc, s_lo, s_hi = rows
    qn, kvn = params
    cq = c[:, :Q_LORA]
    ckv = c[:, Q_LORA:Q_LORA + KV_LORA]
    kr = c[:, Q_LORA + KV_LORA:]
    return _rms(cq, qn), _rms(ckv, kvn), _rope_lanes(kr, cc, s_lo, s_hi)


def _f_rope_q(rows, params):
    qc, cc, s_lo, s_hi = rows
    out = []
    for h in range(MLA_HEADS):
        out.append(qc[:, h * MLA_QK:h * MLA_QK + MLA_NOPE])
        out.append(_rope_lanes(qc[:, h * MLA_QK + MLA_NOPE:(h + 1) * MLA_QK], cc, s_lo, s_hi))
    return (jnp.concatenate(out, axis=-1),)


def _f_kcat(dv):
    per = MLA_NOPE + dv

    def f(rows, params):
        kv, krr = rows
        ks, vs = [], []
        for h in range(MLA_HEADS):
            ks.append(kv[:, h * per:h * per + MLA_NOPE])
            ks.append(krr)
            vs.append(kv[:, h * per + MLA_NOPE:(h + 1) * per])
        return jnp.concatenate(ks, axis=-1), jnp.concatenate(vs, axis=-1)

    return f


def _f_rope_ret(dk):
    half = dk // 2

    def f(rows, params):
        q, k, cos, sin = rows
        qs, ks = [], []
        for h in range(RET_HEADS):
            for src, dst, mult in ((q, qs, 1.0), (k, ks, dk ** -0.5)):
                x1 = src[:, h * dk:h * dk + half]
                x2 = src[:, h * dk + half:(h + 1) * dk]
                dst.append((x1 * cos - x2 * sin) * mult)
                dst.append((x2 * cos + x1 * sin) * mult)
        return jnp.concatenate(qs, axis=-1), jnp.concatenate(ks, axis=-1)

    return f


def _f_gate_gn(dv):
    def f(rows, params):
        o, g = rows
        out = []
        for h in range(RET_HEADS):
            oh = o[:, h * dv:(h + 1) * dv]
            mu = jnp.mean(oh, axis=-1, keepdims=True)
            var = jnp.mean(jnp.square(oh - mu), axis=-1, keepdims=True)
            out.append((oh - mu) * lax.rsqrt(var + LN_EPS))
        return (jnp.concatenate(out, axis=-1) * _silu(g),)

    return f


def _f_loss(rows, params):
    h, tgt, mask = rows
    per_row = jnp.mean(jnp.square(h - tgt), axis=-1, keepdims=True) * mask
    return (0.5 * jnp.sum(per_row, axis=0, keepdims=True),)


def _cat(parts, axis=1):
    return parts[0] if len(parts) == 1 else jnp.concatenate(parts, axis=axis)


def _cols(g):
    return jnp.transpose(g, (1, 0, 2)).reshape(g.shape[1], -1)


def _uncols(w):
    k, n = w.shape
    return jnp.transpose(w.reshape(k, N_DEV, n // N_DEV), (1, 0, 2))


def _heads(g):
    return jnp.transpose(g, (1, 0, 2, 3)).reshape(g.shape[1], -1, g.shape[3])


def _unheads(w):
    h, r, c = w.shape
    return jnp.transpose(w.reshape(h, N_DEV, r // N_DEV, c), (1, 0, 2, 3))


def _rope_tables(t_pad, d):
    inv = ROPE_BASE ** (-jnp.arange(0, d, 2, dtype=F32) / d)
    ang = jnp.arange(t_pad, dtype=F32)[:, None] * inv[None, :]
    return jnp.cos(ang), jnp.sin(ang)


def _row2(v):
    return v.reshape(1, -1)


def _train_local(sch, x2d, tgt_pad, S, *, t_pad):
    seq, d_model = x2d.shape
    branch = d_model
    t_real = N_META + seq
    tt = _pick(t_pad, TT_PREFS)
    n_win = len(POOL_WINDOWS)
    grp = branch // n_win
    lb = branch // LRU_HEADS
    dv2 = branch // MLA_HEADS
    dk3 = branch // RET_HEADS
    gS = {}
    RW, RWB = 0.06, 0.1

    def ln_fwd(h, br, layer):
        h1, hb1 = rw_fwd(sch, RW, _f_ln, [h, br], [_row2(S[f'l{layer}_ln_g']), _row2(S[f'l{layer}_ln_b'])],
                         [(d_model, F32, 0), (d_model, BF, 0)], name=f'l{layer}_ln')
        return h1, hb1

    def ln_bwd(h, br, dh1, layer):
        dh, dbr, dg, db = rw_bwd(sch, RWB, _f_ln, [h, br],
                                 [_row2(S[f'l{layer}_ln_g']), _row2(S[f'l{layer}_ln_b'])], [dh1],
                                 name=f'l{layer}_ln', row_dtypes=[F32, BF])
        gS[f'l{layer}_ln_g'], gS[f'l{layer}_ln_b'] = dg.reshape(-1), db.reshape(-1)
        return dh, dbr

    tidx = jnp.arange(t_pad, dtype=F32)[:, None]
    rowmask = ((tidx >= N_META) & (tidx < t_real)).astype(F32)

    meta = _cols(_cat(sch.get('meta_tokens')))
    h0 = jnp.concatenate([meta, x2d, jnp.zeros((t_pad - t_real, d_model), F32)], axis=0)
    hb0 = h0.astype(BF)

    w0_in = _cols(_cat(sch.get('l0_w_in')))
    w0_u, w0_g = w0_in[:, :branch], w0_in[:, branch:]
    u0 = mm(sch, hb0, w0_u, 'l0_in_u')
    g0 = mm(sch, hb0, w0_g, 'l0_in_g')
    conv_w = jnp.transpose(_cat(sch.get('l0_conv_w')), (1, 2, 0, 3)).reshape(CONV_W, branch)
    w_a = _heads(_cat(sch.get('l0_w_a'), axis=2))
    w_x = _heads(_cat(sch.get('l0_w_x'), axis=2))
    p0 = [conv_w, _row2(S['l0_conv_b']), w_a, _row2(S['l0_b_a']), w_x, _row2(S['l0_b_x']), _row2(S['l0_lam'])]
    f_lru = _f_lru(tt, branch)
    a0, xin0 = rw_fwd(sch, 0.13, f_lru, [u0], p0, [(branch, F32, 0), (branch, F32, 1)], name='l0_lru',
                      halo=(0,), tt=tt)
    hs0 = _ride(sch, 0.22, _scan_call, a=a0, b=xin0, mul=None, reverse=False, name='l0_scan')[0]
    (z0,) = rw_fwd(sch, RW, _f_gate, [hs0, g0], [], [(branch, BF, 0)], name='l0_gate')
    w0_out = _cat(sch.get('l0_w_out')).reshape(branch, d_model)
    br0 = mm(sch, z0, w0_out, 'l0_out')
    h1, hb1 = ln_fwd(h0, br0, 0)

    w1_in = _cols(_cat(sch.get('l1_w_in')))
    w1_u, w1_g = w1_in[:, :branch], w1_in[:, branch:]
    u1 = mm(sch, hb1, w1_u, 'l1_in_u')
    g1 = mm(sch, hb1, w1_g, 'l1_in_g')
    f_pool = _f_pool(tt, branch)
    ps1 = rw_fwd(sch, RW, f_pool, [u1, tidx], [], [(grp, BF, gi) for gi in range(n_win)], name='l1_pool',
                 halo=(0,), tt=tt)
    w1_grp = _heads(_cat(sch.get('l1_w_grp'), axis=2))
    mixed1 = [mm(sch, ps1[gi], w1_grp[gi], f'l1_grp{gi}') for gi in range(n_win)]
    p1 = [_row2(S['l1_scale'])]
    (z1,) = rw_fwd(sch, RW, _f_gate_pool, mixed1 + [g1], p1, [(branch, BF, 0)], name='l1_gate')
    w1_out = _cat(sch.get('l1_w_out')).reshape(branch, d_model)
    br1 = mm(sch, z1, w1_out, 'l1_out')
    h2, hb2 = ln_fwd(h1, br1, 1)

    w2_in = _cols(_cat(sch.get('l2_w_in')))
    w2_g = w2_in[:, :branch]
    w2_lat = jnp.pad(w2_in[:, branch:], ((0, 0), (0, 128 - MLA_ROPE)))
    g2 = mm(sch, hb2, w2_g, 'l2_in_g')
    c2 = mm(sch, hb2, w2_lat, 'l2_in_c')
    cos, sin = _rope_tables(t_pad, MLA_ROPE)
    zz = jnp.zeros_like(cos)
    tabs = [jnp.concatenate([cos, cos, zz, zz], axis=-1), jnp.concatenate([zz, sin, zz, zz], axis=-1),
            jnp.concatenate([-sin, zz, zz, zz], axis=-1)]
    p2 = [_row2(S['l2_q_norm']), _row2(S['l2_kv_norm'])]
    cqn2, ckvn2, krr2 = rw_fwd(sch, 0.04, _f_mla_pre, [c2] + tabs, p2,
                               [(Q_LORA, BF, 0), (KV_LORA, BF, 1), (128, F32, 2)], name='l2_pre')
    w2_uq = _cols(_cat(sch.get('l2_w_uq'))).reshape(Q_LORA, MLA_HEADS, MLA_NOPE + MLA_ROPE)
    w2_uq = jnp.pad(w2_uq, ((0, 0), (0, 0), (0, MLA_QK - MLA_NOPE - MLA_ROPE))).reshape(Q_LORA, MLA_HEADS * MLA_QK)
    w2_ukv = _cols(_cat(sch.get('l2_w_ukv')))
    qc2 = mm(sch, cqn2, w2_uq, 'l2_uq')
    kv2 = mm(sch, ckvn2, w2_ukv, 'l2_ukv')
    (qcr2,) = rw_fwd(sch, 0.09, _f_rope_q, [qc2] + tabs, [], [(MLA_HEADS * MLA_QK, BF, 0)], name='l2_rope_q')
    f_kcat = _f_kcat(dv2)
    kcat2, v2 = rw_fwd(sch, 0.1, f_kcat, [kv2, krr2], [], [(MLA_HEADS * MLA_QK, BF, 0), (branch, BF, 1)],
                       name='l2_kcat')
    no_decay = jnp.zeros((MLA_HEADS, 1, 128), F32)
    att2 = dict(heads=MLA_HEADS, softmax=True, scale=(MLA_NOPE + MLA_ROPE) ** -0.5, name='l2_att')
    o2, lse2 = _ride(sch, 1.3, _att_fwd_call, q=qcr2, k=kcat2, v=v2, lgt=no_decay, **att2)
    (z2,) = rw_fwd(sch, RW, _f_gate, [o2, g2], [], [(branch, BF, 0)], name='l2_gate')
    w2_out = _cat(sch.get('l2_w_out')).reshape(branch, d_model)
    br2 = mm(sch, z2, w2_out, 'l2_out')
    h3, hb3 = ln_fwd(h2, br2, 2)

    w3_in = _cols(_cat(sch.get('l3_w_in')))
    w3 = [w3_in[:, n * branch:(n + 1) * branch] for n in range(4)]
    q3 = mm(sch, hb3, w3[0], 'l3_in_q')
    k3 = mm(sch, hb3, w3[1], 'l3_in_k')
    v3 = mm(sch, hb3, w3[2], 'l3_in_v', out_dtype=BF)
    g3 = mm(sch, hb3, w3[3], 'l3_in_g')
    cs3 = list(_rope_tables(t_pad, dk3))
    f_rope3 = _f_rope_ret(dk3)
    qr3, kr3 = rw_fwd(sch, 0.09, f_rope3, [q3, k3] + cs3, [], [(branch, BF, 0), (branch, BF, 1)], name='l3_rope')
    log_g = jnp.log(1.0 - 2.0 ** (-5.0 - jnp.arange(RET_HEADS, dtype=F32)))
    lgt = jnp.broadcast_to(log_g[:, None, None], (RET_HEADS, 1, 128))
    att3 = dict(heads=RET_HEADS, softmax=False, scale=1.0, name='l3_ret')
    o3, lse3 = _ride(sch, 0.6, _att_fwd_call, q=qr3, k=kr3, v=v3, lgt=lgt, **att3)
    f_gn = _f_gate_gn(dk3)
    (z3,) = rw_fwd(sch, 0.08, f_gn, [o3, g3], [], [(branch, BF, 0)], name='l3_gate')
    w3_out = _cat(sch.get('l3_w_out')).reshape(branch, d_model)
    br3 = mm(sch, z3, w3_out, 'l3_out')
    (h4,) = rw_fwd(sch, RW, _f_ln, [h3, br3], [_row2(S['l3_ln_g']), _row2(S['l3_ln_b'])], [(d_model, F32, 0)],
                   name='l3_ln')

    (loss,) = rw_fwd(sch, 0.05, _f_loss, [h4, tgt_pad, rowmask], [], [], name='loss', n_reduce=1)

    sch.scale = BWD_RIDER_SHARE
    (dh4,) = rw_bwd(sch, 0.07, _f_loss, [h4, tgt_pad, rowmask], [], [jnp.ones((1, 1), F32)], name='loss',
                    n_reduce=1, nd_rows=(1, 2))

    dh3, dbr3 = ln_bwd(h3, br3, dh4, 3)
    dz3 = mm(sch, dbr3, w3_out, 'l3_out_dx', mode='nt')
    sch.push('l3_w_out', mm(sch, z3.T, dbr3, 'l3_out_dw', mode='nn', out_dtype=BF).reshape(N_DEV, -1, d_model),
             False, 1)
    do3, dg3 = rw_bwd(sch, 0.13, f_gn, [o3, g3], [], [dz3], name='l3_gate', row_dtypes=[F32, BF])
    dqr3, delta3 = _ride(sch, 0.7, _att_dq_call, q=qr3, k=kr3, v=v3, lgt=lgt, o=o3, lse=lse3, do=do3, **att3)
    dkr3, dv3 = _ride(sch, 0.8, _att_dkv_call, q=qr3, k=kr3, v=v3, lgt=lgt, lse_row=lse3.reshape(RET_HEADS, 1, -1),
                      delta_row=delta3.reshape(RET_HEADS, 1, -1), do=do3, **att3)
    dq3, dk3_ = rw_bwd(sch, 0.13, f_rope3, [q3, k3] + cs3, [], [dqr3, dkr3], name='l3_rope', nd_rows=(2, 3),
                       row_dtypes=[BF, BF])
    d3 = [dq3, dk3_, dv3, dg3]
    for n in range(4):
        dh3 = mm(sch, d3[n], w3[n], f'l3_in_dx{n}', mode='nt', add=dh3)
    hb3_t = hb3.T
    dw3 = [mm(sch, hb3_t, d3[n], f'l3_in_dw{n}', mode='nn', out_dtype=BF) for n in range(4)]
    sch.push('l3_w_in', _uncols(jnp.concatenate(dw3, axis=1)), False, 1)

    dh2, dbr2 = ln_bwd(h2, br2, dh3, 2)
    dz2 = mm(sch, dbr2, w2_out, 'l2_out_dx', mode='nt')
    sch.push('l2_w_out', mm(sch, z2.T, dbr2, 'l2_out_dw', mode='nn', out_dtype=BF).reshape(N_DEV, -1, d_model),
             False, 1)
    do2, dg2 = rw_bwd(sch, 0.1, _f_gate, [o2, g2], [], [dz2], name='l2_gate', row_dtypes=[F32, BF])
    dqcr2, delta2 = _ride(sch, 1.3, _att_dq_call, q=qcr2, k=kcat2, v=v2, lgt=no_decay, o=o2, lse=lse2, do=do2,
                          **att2)
    dkcat2, dv2_ = _ride(sch, 1.5, _att_dkv_call, q=qcr2, k=kcat2, v=v2, lgt=no_decay,
                         lse_row=lse2.reshape(MLA_HEADS, 1, -1), delta_row=delta2.reshape(MLA_HEADS, 1, -1),
                         do=do2, **att2)
    dkv2, dkrr2 = rw_bwd(sch, 0.14, f_kcat, [kv2, krr2], [], [dkcat2, dv2_], name='l2_kcat', row_dtypes=[BF, F32])
    (dqc2,) = rw_bwd(sch, 0.13, _f_rope_q, [qc2] + tabs, [], [dqcr2], name='l2_rope_q', nd_rows=(1, 2, 3),
                     row_dtypes=[BF])
    dckvn2 = mm(sch, dkv2, w2_ukv, 'l2_ukv_dx', mode='nt')
    dcqn2 = mm(sch, dqc2, w2_uq, 'l2_uq_dx', mode='nt')
    sch.push('l2_w_ukv', _uncols(mm(sch, ckvn2, dkv2, 'l2_ukv_dw', mode='tn', out_dtype=BF)), False, 1)
    dw_uq = mm(sch, cqn2, dqc2, 'l2_uq_dw', mode='tn', out_dtype=BF)
    dw_uq = dw_uq.reshape(Q_LORA, MLA_HEADS, MLA_QK)[:, :, :MLA_NOPE + MLA_ROPE].reshape(Q_LORA, -1)
    sch.push('l2_w_uq', _uncols(dw_uq), False, 1)
    dc2, dqn, dkvn = rw_bwd(sch, 0.05, _f_mla_pre, [c2] + tabs, p2, [dcqn2, dckvn2, dkrr2], name='l2_pre',
                            nd_rows=(1, 2, 3), row_dtypes=[BF])
    gS['l2_q_norm'], gS['l2_kv_norm'] = dqn.reshape(-1), dkvn.reshape(-1)
    dh2 = mm(sch, dg2, w2_g, 'l2_in_g_dx', mode='nt', add=dh2)
    dh2 = mm(sch, dc2, w2_lat, 'l2_in_c_dx', mode='nt', add=dh2)
    hb2_t = hb2.T
    dw2_g = mm(sch, hb2_t, dg2, 'l2_in_g_dw', mode='nn', out_dtype=BF)
    dw2_lat = mm(sch, hb2_t, dc2, 'l2_in_c_dw', mode='nn', out_dtype=BF)
    n_lat = Q_LORA + KV_LORA + MLA_ROPE
    sch.push('l2_w_in', _uncols(jnp.concatenate([dw2_g, dw2_lat[:, :n_lat]], axis=1)), False, 1)

    dh1, dbr1 = ln_bwd(h1, br1, dh2, 1)
    dz1 = mm(sch, dbr1, w1_out, 'l1_out_dx', mode='nt')
    sch.push('l1_w_out', mm(sch, z1.T, dbr1, 'l1_out_dw', mode='nn', out_dtype=BF).reshape(N_DEV, -1, d_model),
             False, 1)
    res = rw_bwd(sch, 0.11, _f_gate_pool, mixed1 + [g1], p1, [dz1], name='l1_gate', row_dtypes=[BF] * (n_win + 1))
    dmixed1, dg1, dscale = res[:n_win], res[n_win], res[n_win + 1]
    gS['l1_scale'] = dscale.reshape(-1)
    dps1 = [mm(sch, dmixed1[gi], w1_grp[gi], f'l1_grp{gi}_dx', mode='nt') for gi in range(n_win)]
    dw_grp = jnp.stack([mm(sch, ps1[gi], dmixed1[gi], f'l1_grp{gi}_dw', mode='tn', out_dtype=BF)
                        for gi in range(n_win)])
    sch.push('l1_w_grp', _unheads(dw_grp), False, 2)
    (du1,) = rw_bwd(sch, 0.09, f_pool, [u1, tidx], [], dps1, name='l1_pool', halo=(0,), nd_rows=(1,), tt=tt,
                    row_dtypes=[BF])
    dh1 = mm(sch, du1, w1_u, 'l1_in_u_dx', mode='nt', add=dh1)
    dh1 = mm(sch, dg1, w1_g, 'l1_in_g_dx', mode='nt', add=dh1)
    hb1_t = hb1.T
    dw1 = [mm(sch, hb1_t, du1, 'l1_in_u_dw', mode='nn', out_dtype=BF),
           mm(sch, hb1_t, dg1, 'l1_in_g_dw', mode='nn', out_dtype=BF)]
    sch.push('l1_w_in', _uncols(jnp.concatenate(dw1, axis=1)), False, 1)

    dh0, dbr0 = ln_bwd(h0, br0, dh1, 0)
    dz0 = mm(sch, dbr0, w0_out, 'l0_out_dx', mode='nt')
    sch.push('l0_w_out', mm(sch, z0.T, dbr0, 'l0_out_dw', mode='nn', out_dtype=BF).reshape(N_DEV, -1, d_model),
             False, 1)
    dhs0, dg0 = rw_bwd(sch, 0.1, _f_gate, [hs0, g0], [], [dz0], name='l0_gate', row_dtypes=[F32, BF])
    a_next = jnp.concatenate([a0[1:], jnp.ones_like(a0[:1])], axis=0)
    hs_prev = jnp.concatenate([jnp.zeros_like(hs0[:1]), hs0[:-1]], axis=0)
    dxin0, da0 = _ride(sch, 0.25, _scan_call, a=a_next, b=dhs0, mul=hs_prev, reverse=True, name='l0_scan_bwd')
    res = rw_bwd(sch, 0.3, f_lru, [u0], p0, [da0, dxin0], name='l0_lru', halo=(0,), tt=tt, row_dtypes=[BF])
    du0 = res[0]
    gS['l0_conv_b'], gS['l0_b_a'], gS['l0_b_x'], gS['l0_lam'] = [res[k].reshape(-1) for k in (2, 4, 6, 7)]
    sch.push('l0_w_a', _unheads(res[3]), False, 2)
    sch.push('l0_w_x', _unheads(res[5]), False, 2)
    sch.push('l0_conv_w', jnp.transpose(res[1].reshape(CONV_W, 1, N_DEV, -1), (2, 0, 1, 3)), False, 3)
    hb0_t = hb0.T
    dw0 = [mm(sch, hb0_t, du0, 'l0_in_u_dw', mode='nn', out_dtype=BF),
           mm(sch, hb0_t, dg0, 'l0_in_g_dw', mode='nn', out_dtype=BF)]
    sch.push('l0_w_in', _uncols(jnp.concatenate(dw0, axis=1)), False, 1, chunk_ms=EXCH_CHUNK_MS)
    sch.scale = 1.0
    dh0 = mm(sch, du0, w0_u, 'l0_in_u_dx', mode='nt', add=dh0)
    dh0 = mm(sch, dg0, w0_g, 'l0_in_g_dx', mode='nt', add=dh0)
    sch.push('meta_tokens', _uncols(dh0[:N_META]), False, 1)

    return loss[0, 0], dh0[N_META:t_real], gS


def _as2d(a):
    return a.reshape(-1, a.shape[-1])


def kernel(x, meta_tokens, l0_w_in, l0_conv_w, l0_conv_b, l0_w_a, l0_b_a, l0_w_x, l0_b_x, l0_lam, l0_w_out, l0_ln_g, l0_ln_b, l1_w_in, l1_w_grp, l1_scale, l1_w_out, l1_ln_g, l1_ln_b, l2_w_in, l2_q_norm, l2_w_uq, l2_kv_norm, l2_w_ukv, l2_w_out, l2_ln_g, l2_ln_b, l3_w_in, l3_w_out, l3_ln_g, l3_ln_b, loss_target, m_meta_tokens, m_l0_w_in, m_l0_conv_w, m_l0_conv_b, m_l0_w_a, m_l0_b_a, m_l0_w_x, m_l0_b_x, m_l0_lam, m_l0_w_out, m_l0_ln_g, m_l0_ln_b, m_l1_w_in, m_l1_w_grp, m_l1_scale, m_l1_w_out, m_l1_ln_g, m_l1_ln_b, m_l2_w_in, m_l2_q_norm, m_l2_w_uq, m_l2_kv_norm, m_l2_w_ukv, m_l2_w_out, m_l2_ln_g, m_l2_ln_b, m_l3_w_in, m_l3_w_out, m_l3_ln_g, m_l3_ln_b, v_meta_tokens, v_l0_w_in, v_l0_conv_w, v_l0_conv_b, v_l0_w_a, v_l0_b_a, v_l0_w_x, v_l0_b_x, v_l0_lam, v_l0_w_out, v_l0_ln_g, v_l0_ln_b, v_l1_w_in, v_l1_w_grp, v_l1_scale, v_l1_w_out, v_l1_ln_g, v_l1_ln_b, v_l2_w_in, v_l2_q_norm, v_l2_w_uq, v_l2_kv_norm, v_l2_w_ukv, v_l2_w_out, v_l2_ln_g, v_l2_ln_b, v_l3_w_in, v_l3_w_out, v_l3_ln_g, v_l3_ln_b):
    args = (meta_tokens, l0_w_in, l0_conv_w, l0_conv_b, l0_w_a, l0_b_a, l0_w_x, l0_b_x, l0_lam, l0_w_out, l0_ln_g, l0_ln_b, l1_w_in, l1_w_grp, l1_scale, l1_w_out, l1_ln_g, l1_ln_b, l2_w_in, l2_q_norm, l2_w_uq, l2_kv_norm, l2_w_ukv, l2_w_out, l2_ln_g, l2_ln_b, l3_w_in, l3_w_out, l3_ln_g, l3_ln_b)
    moms = (m_meta_tokens, m_l0_w_in, m_l0_conv_w, m_l0_conv_b, m_l0_w_a, m_l0_b_a, m_l0_w_x, m_l0_b_x, m_l0_lam, m_l0_w_out, m_l0_ln_g, m_l0_ln_b, m_l1_w_in, m_l1_w_grp, m_l1_scale, m_l1_w_out, m_l1_ln_g, m_l1_ln_b, m_l2_w_in, m_l2_q_norm, m_l2_w_uq, m_l2_kv_norm, m_l2_w_ukv, m_l2_w_out, m_l2_ln_g, m_l2_ln_b, m_l3_w_in, m_l3_w_out, m_l3_ln_g, m_l3_ln_b)
    vels = (v_meta_tokens, v_l0_w_in, v_l0_conv_w, v_l0_conv_b, v_l0_w_a, v_l0_b_a, v_l0_w_x, v_l0_b_x, v_l0_lam, v_l0_w_out, v_l0_ln_g, v_l0_ln_b, v_l1_w_in, v_l1_w_grp, v_l1_scale, v_l1_w_out, v_l1_ln_g, v_l1_ln_b, v_l2_w_in, v_l2_q_norm, v_l2_w_uq, v_l2_kv_norm, v_l2_w_ukv, v_l2_w_out, v_l2_ln_g, v_l2_ln_b, v_l3_w_in, v_l3_w_out, v_l3_ln_g, v_l3_ln_b)
    W = dict(zip(WEIGHTS, args))
    M = dict(zip(WEIGHTS, moms))
    V = dict(zip(WEIGHTS, vels))

    seq = x.shape[1]
    t_real = N_META + seq
    t_pad = -(-t_real // ROW_ALIGN) * ROW_ALIGN
    tgt_pad = jnp.pad(loss_target[0], ((N_META, t_pad - t_real), (0, 0)))

    sch = _Schedule()
    for n in GATHER_ORDER:
        shard = W[n].astype(BF) if n in BIG else W[n]
        sch.push(n, shard, True, shard.ndim - 2)
    S = {n: W[n] for n in REPLICATED}

    loss, gx, gS = _train_local(sch, x[0], tgt_pad, S, t_pad=t_pad)

    flat = jnp.concatenate([gS[n].reshape(-1) for n in REPLICATED]).reshape(-1, 128)
    sch.push('small_grads', flat, True, 0)

    out_g, out_d, out_m, out_v = {}, {}, {}, {}
    order = ['l3_w_out', 'l3_w_in', 'l2_w_out', 'l2_w_ukv', 'l2_w_uq', 'l2_w_in', 'l1_w_out', 'l1_w_grp',
             'l1_w_in', 'l0_w_out', 'l0_w_a', 'l0_w_x', 'l0_conv_w', 'l0_w_in', 'meta_tokens']
    for n in order:
        shp = W[n].shape
        w2, m2, v2 = _as2d(W[n]), _as2d(M[n]), _as2d(V[n])
        parts = [p.reshape((N_DEV, -1, w2.shape[1])) for p in sch.get(n)]
        res = _ride(sch, w2.size * ADAM_MS_PER_ELEM, _adamw_call, contribs=parts, w=w2, m=m2, v=v2, name='adamw_' + n)
        out_g[n], out_d[n], out_m[n], out_v[n] = [r.reshape(shp) for r in res]
    cat = lambda D: jnp.concatenate([D[n].reshape(-1) for n in REPLICATED]).reshape(-1, 128)
    res = _adamw_call(sch.get('small_grads'), cat(W), cat(M), cat(V), name='adamw_small')[0]
    off = 0
    for n in REPLICATED:
        size = W[n].size
        for dst, r in zip((out_g, out_d, out_m, out_v), res):
            dst[n] = r.reshape(-1)[off:off + size].reshape(W[n].shape)
        off += size
    sch.flush()

    loss = lax.psum(loss, ("x", "y", "c"))
    return (loss, gx[None], *[out_g[n] for n in WEIGHTS], *[out_d[n] for n in WEIGHTS],
            *[out_m[n] for n in WEIGHTS], *[out_v[n] for n in WEIGHTS])
```

```python
import functools

import jax
import jax.numpy as jnp
from jax import lax
from jax.experimental import pallas as pl
from jax.experimental.pallas import tpu as pltpu

F32 = jnp.float32
BF = jnp.bfloat16

N_DEV = 8
N_META = 16
ALPHA = (2.0 * 4) ** 0.25
LN_EPS = 1e-5
RMS_EPS = 1e-6
ROPE_BASE = 10000.0
LRU_HEADS = 16
CONV_W = 4
LRU_C = 8.0
POOL_WINDOWS = (2, 4, 8, 16)
MLA_HEADS = 32
MLA_NOPE = 128
MLA_ROPE = 64
MLA_QK = 256
Q_LORA = 1024
KV_LORA = 512
RET_HEADS = 16
ADAM_LR = 0.001
ADAM_B1 = 0.9
ADAM_B2 = 0.999
ADAM_EPS = 1e-08
ADAM_WD = 0.01
ADAM_STEP = 10

ROW_ALIGN = 128
VMEM_LIMIT_BYTES = 56 * 1024 * 1024
TT_PREFS = (128, 64, 32, 16, 8)
ATT_PREFS = (384, 256, 128)
MM_M_PREFS = (2048, 1408, 1024, 512, 384, 256, 128)
MM_N_PREFS = (1024, 640, 512, 384, 256, 128)
MM_K_PREFS = (1408, 1024, 512, 384, 256, 128)
MM_VMEM_BUDGET = 40 * 1024 * 1024
ADAM_BLOCK_ELEMS = 128 * 1024
EXCH_MS_PER_MB = 0.0857
EXCH_CHUNK_MS = 0.1
MXU_FLOPS_PER_MS = 7.8e11
BWD_RIDER_SHARE = 0.6
ADAM_MS_PER_ELEM = 2.3e-8

WEIGHTS = ['meta_tokens', 'l0_w_in', 'l0_conv_w', 'l0_conv_b', 'l0_w_a', 'l0_b_a', 'l0_w_x', 'l0_b_x', 'l0_lam',
           'l0_w_out', 'l0_ln_g', 'l0_ln_b', 'l1_w_in', 'l1_w_grp', 'l1_scale', 'l1_w_out', 'l1_ln_g', 'l1_ln_b',
           'l2_w_in', 'l2_q_norm', 'l2_w_uq', 'l2_kv_norm', 'l2_w_ukv', 'l2_w_out', 'l2_ln_g', 'l2_ln_b',
           'l3_w_in', 'l3_w_out', 'l3_ln_g', 'l3_ln_b']
BIG = ['l0_w_in', 'l0_w_out', 'l1_w_in', 'l1_w_grp', 'l1_w_out', 'l2_w_in', 'l2_w_uq', 'l2_w_ukv', 'l2_w_out',
       'l3_w_in', 'l3_w_out']
SHARDED_F32 = ['meta_tokens', 'l0_conv_w', 'l0_w_a', 'l0_w_x']
REPLICATED = [n for n in WEIGHTS if n not in BIG and n not in SHARDED_F32]
GATHER_ORDER = ['meta_tokens', 'l0_w_in', 'l0_conv_w', 'l0_w_a', 'l0_w_x', 'l0_w_out', 'l1_w_in', 'l1_w_grp',
                'l1_w_out', 'l2_w_in', 'l2_w_uq', 'l2_w_ukv', 'l2_w_out', 'l3_w_in', 'l3_w_out']


def _pick(n, prefs):
    for p in prefs:
        if n % p == 0:
            return p
    return n


def _exchange_copies(jobs, in_refs, out_refs, send_sems, recv_sems, local_sems):
    x, y, c = lax.axis_index("x"), lax.axis_index("y"), lax.axis_index("c")
    me = 4 * x + 2 * y + c
    sibling = (x, y, 1 - c)
    chips = [(1 - x, y), (x, 1 - y), (1 - x, 1 - y)]
    peers = [sibling] + [(px, py, c) for px, py in chips] + [(px, py, 1 - c) for px, py in chips]
    first, arrivals, forwards, rest = [], [], [], []

    def dev(px, py, pc):
        return 4 * px + 2 * py + pc

    for n, (_, gather) in enumerate(jobs):
        def remote(p, src, slot, to, n=n):
            k = n * (N_DEV - 1) + p
            return pltpu.make_async_remote_copy(
                src_ref=src, dst_ref=out_refs[n].at[slot], send_sem=send_sems.at[k], recv_sem=recv_sems.at[k],
                device_id=to, device_id_type=pl.DeviceIdType.MESH)

        if gather:
            first += [remote(p, in_refs[n], me, peers[p]) for p in range(4)]
            for j, (px, py) in enumerate(chips):
                landed = out_refs[n].at[dev(px, py, c)]
                arrivals.append(remote(1 + j, landed, dev(px, py, c), peers[1 + j]))
                forwards.append(remote(4 + j, landed, dev(px, py, c), sibling))
        else:
            first += [remote(p, in_refs[n].at[dev(*peers[p])], me, peers[p]) for p in range(N_DEV - 1)]
        rest.append(pltpu.make_async_copy(in_refs[n] if gather else in_refs[n].at[me], out_refs[n].at[me],
                                          local_sems.at[n]))

    def start():
        for cp in first + rest:
            cp.start()

    def finish():
        for arrived, forward in zip(arrivals, forwards):
            arrived.wait_recv()
            forward.start()
        for cp in first:
            cp.wait_send()
        for n, (_, gather) in enumerate(jobs):
            for p in range(N_DEV - 1):
                if not (gather and 1 <= p <= 3):
                    k = n * (N_DEV - 1) + p
                    pltpu.make_async_remote_copy(
                        src_ref=out_refs[n].at[me], dst_ref=out_refs[n].at[me], send_sem=send_sems.at[k],
                        recv_sem=recv_sems.at[k], device_id=sibling, device_id_type=pl.DeviceIdType.MESH).wait_recv()
        for cp in forwards:
            cp.wait_send()
        for cp in rest:
            cp.wait()

    return start, finish


def _exchange_shapes(jobs):
    shapes = []
    for arr, gather in jobs:
        blk = arr.shape if gather else arr.shape[1:]
        shapes.append(jax.ShapeDtypeStruct((N_DEV,) + tuple(blk), arr.dtype))
    return shapes


def _exchange_scratch(jobs):
    n = len(jobs)
    return [pltpu.SemaphoreType.DMA((n * (N_DEV - 1),)), pltpu.SemaphoreType.DMA((n * (N_DEV - 1),)),
            pltpu.SemaphoreType.DMA((n,))]


def _call(body, *, name, grid, in_specs, out_specs, out_shape, args, scratch=(), sem=None, jobs=()):
    jobs = list(jobs)
    n_in, n_out, n_sc, n_job = len(in_specs), len(out_specs), len(scratch), len(jobs)
    hbm = pl.BlockSpec(memory_space=pl.ANY)

    def kern(*refs):
        ins = refs[:n_in]
        job_ins = refs[n_in:n_in + n_job]
        pos = n_in + n_job
        outs = refs[pos:pos + n_out]
        job_outs = refs[pos + n_out:pos + n_out + n_job]
        pos += n_out + n_job
        scr = refs[pos:pos + n_sc]
        if n_job:
            ids = [pl.program_id(d) for d in range(len(grid))]
            first = functools.reduce(jnp.logical_and, [i == 0 for i in ids])
            last = functools.reduce(jnp.logical_and, [i == g - 1 for i, g in zip(ids, grid)])
            start, finish = _exchange_copies(jobs, job_ins, job_outs, *refs[pos + n_sc:])
            pl.when(first)(start)

        body(ins, outs, scr)
        if n_job:
            pl.when(last)(finish)

    kw = dict(vmem_limit_bytes=VMEM_LIMIT_BYTES)
    if sem is not None:
        kw['dimension_semantics'] = tuple("arbitrary" for _ in grid) if n_job else sem
    res = pl.pallas_call(
        kern, name=name, grid=grid,
        in_specs=list(in_specs) + [hbm] * n_job,
        out_specs=list(out_specs) + [hbm] * n_job,
        out_shape=list(out_shape) + _exchange_shapes(jobs),
        scratch_shapes=list(scratch) + (_exchange_scratch(jobs) if n_job else []),
        compiler_params=pltpu.CompilerParams(**kw),
    )(*args, *[a for a, _ in jobs])
    return list(res[:n_out]), list(res[n_out:])


def _exchange_alone(jobs, name):
    def body(*refs):
        n = len(jobs)
        start, finish = _exchange_copies(jobs, refs[:n], refs[n:2 * n], *refs[2 * n:])
        start()
        finish()

    hbm = pl.BlockSpec(memory_space=pl.ANY)
    return list(pl.pallas_call(
        body, name=name, in_specs=[hbm] * len(jobs), out_specs=[hbm] * len(jobs),
        out_shape=_exchange_shapes(jobs), scratch_shapes=_exchange_scratch(jobs),
    )(*[a for a, _ in jobs]))


class _Schedule:
    def __init__(self):
        self.pending = []
        self.done = {}
        self.chunks = {}
        self.count = 0
        self.scale = 1.0

    def push(self, name, arr, gather, row_axis, chunk_ms=None):
        mb = arr.size * arr.dtype.itemsize / (1 if gather else N_DEV) / 1e6
        cost = mb * EXCH_MS_PER_MB * (0.5 if gather else 1.0)
        rows = arr.shape[row_axis]
        n = 1
        leading = row_axis == (0 if gather else 1)
        target = chunk_ms or (EXCH_CHUNK_MS if gather else 2 * EXCH_CHUNK_MS)
        while leading and cost / n > target and rows % (2 * n) == 0 and rows // (2 * n) >= 16:
            n *= 2
        self.chunks[name] = n
        step = rows // n
        for k in range(n):
            piece = lax.slice_in_dim(arr, k * step, (k + 1) * step, axis=row_axis) if n > 1 else arr
            self.pending.append(((name, k), piece, gather, cost / n))

    def take(self, budget):
        jobs, spent = [], 0.0
        budget *= self.scale
        while self.pending and spent + 0.5 * self.pending[0][3] <= budget:
            job = self.pending.pop(0)
            jobs.append(job)
            spent += job[3]
        return jobs

    def deliver(self, jobs, results):
        for (key, _, _, _), r in zip(jobs, results):
            self.done[key] = r

    def get(self, name):
        mine = [j for j in self.pending if j[0][0] == name]
        if mine:
            self.pending = [j for j in self.pending if j[0][0] != name]
            self.count += 1
            self.deliver(mine, _exchange_alone([(j[1], j[2]) for j in mine], f'exchange_{self.count}_{name}'))
        parts = [self.done.pop((name, k)) for k in range(self.chunks[name])]
        return parts

    def flush(self):
        if self.pending:
            jobs, self.pending = self.pending, []
            self.count += 1
            self.deliver(jobs, _exchange_alone([(j[1], j[2]) for j in jobs], f'exchange_{self.count}_rest'))


def _ride(sch, budget, fn, **kw):
    jobs = sch.take(budget) if sch is not None else []
    outs, exch = fn(jobs=[(j[1], j[2]) for j in jobs], **kw)
    if jobs:
        sch.deliver(jobs, exch)
    return outs


@functools.partial(jax.custom_vjp, nondiff_argnums=(1, 2))
def _roll(x, shift, axis):
    return pltpu.roll(x, shift, axis)


def _roll_fwd(x, shift, axis):
    return pltpu.roll(x, shift, axis), None


def _roll_bwd(shift, axis, _, g):
    n = g.shape[axis]
    return (pltpu.roll(g, (n - shift) % n, axis),)


_roll.defvjp(_roll_fwd, _roll_bwd)


@jax.custom_vjp
def _bdot(x, w):
    return jnp.dot(x.astype(BF), w.astype(BF), preferred_element_type=F32)


def _bdot_fwd(x, w):
    return _bdot(x, w), (x, w)


def _bdot_bwd(res, g):
    x, w = res
    gb = g.astype(BF)
    dx = lax.dot_general(gb, w.astype(BF), (((1,), (1,)), ((), ())), preferred_element_type=F32)
    dw = lax.dot_general(x.astype(BF), gb, (((0,), (0,)), ((), ())), preferred_element_type=F32)
    return dx, dw


_bdot.defvjp(_bdot_fwd, _bdot_bwd)


def _silu(g):
    return g * jax.nn.sigmoid(g)


def _softplus(x):
    return jnp.maximum(x, 0.0) + jnp.log1p(jnp.exp(-jnp.abs(x)))


def _mm_call(a, b, *, mode, out_dtype, name, add=None, jobs=()):
    if mode == 'nn':
        (mo, kc), (_, no) = a.shape, b.shape
    elif mode == 'nt':
        (mo, kc), (no, _) = a.shape, b.shape
    else:
        (kc, mo), (_, no) = a.shape, b.shape
    tm = _pick(mo, MM_M_PREFS)
    tn = no if no <= 2048 and no % 512 != 0 else _pick(no, MM_N_PREFS)
    out_bytes = tm * tn * (2 * jnp.dtype(out_dtype).itemsize + 4 + (8 if add is not None else 0))
    tk = None
    for cand in MM_K_PREFS:
        if kc % cand == 0:
            tk = cand
            if out_bytes + 2 * cand * (tm * a.dtype.itemsize + tn * b.dtype.itemsize) <= MM_VMEM_BUDGET:
                break
    tk = tk or kc
    nk = kc // tk
    if mode == 'nn':
        a_spec = pl.BlockSpec((tm, tk), lambda i, j, k: (i, k))
        b_spec = pl.BlockSpec((tk, tn), lambda i, j, k: (k, j))
        dims = (((1,), (0,)), ((), ()))
    elif mode == 'nt':
        a_spec = pl.BlockSpec((tm, tk), lambda i, j, k: (i, k))
        b_spec = pl.BlockSpec((tn, tk), lambda i, j, k: (j, k))
        dims = (((1,), (1,)), ((), ()))
    else:
        a_spec = pl.BlockSpec((tk, tm), lambda i, j, k: (k, i))
        b_spec = pl.BlockSpec((tk, tn), lambda i, j, k: (k, j))
        dims = (((0,), (0,)), ((), ()))
    o_spec = pl.BlockSpec((tm, tn), lambda i, j, k: (i, j))

    def body(ins, outs, scr):
        a_ref, b_ref = ins[0], ins[1]
        (o_ref,), (acc_ref,) = outs, scr
        k = pl.program_id(2)

        @pl.when(k == 0)
        def _():
            acc_ref[...] = jnp.zeros_like(acc_ref) if add is None else ins[2][...]

        acc_ref[...] += lax.dot_general(a_ref[...].astype(BF), b_ref[...].astype(BF), dims,
                                        preferred_element_type=F32)

        @pl.when(k == nk - 1)
        def _():
            o_ref[...] = acc_ref[...].astype(o_ref.dtype)

    outs, exch = _call(
        body, name=name, grid=(mo // tm, no // tn, nk),
        in_specs=[a_spec, b_spec] + ([o_spec] if add is not None else []),
        out_specs=[o_spec], out_shape=[jax.ShapeDtypeStruct((mo, no), out_dtype)],
        args=[a, b] + ([add] if add is not None else []),
        scratch=[pltpu.VMEM((tm, tn), F32)], sem=("parallel", "parallel", "arbitrary"), jobs=jobs)
    return outs, exch


def mm(sch, a, b, name, mode='nn', out_dtype=F32, add=None):
    if mode == 'nn':
        flops = 2.0 * a.shape[0] * a.shape[1] * b.shape[1]
    elif mode == 'nt':
        flops = 2.0 * a.shape[0] * a.shape[1] * b.shape[0]
    else:
        flops = 2.0 * a.shape[0] * a.shape[1] * b.shape[1]
    return _ride(sch, flops / MXU_FLOPS_PER_MS, _mm_call, a=a, b=b, mode=mode, out_dtype=out_dtype, name=name,
                 add=add)[0]


def _full_spec(p):
    nd = p.ndim
    return pl.BlockSpec(p.shape, lambda i: (0,) * nd)


def _load_rows(refs, n_rows, halo, step_is_first):
    cur, prev, pos = [], [], 0
    for r in range(n_rows):
        cur.append(refs[pos][...].astype(F32))
        pos += 1
        if r in halo:
            keep = jnp.where(step_is_first, 0.0, 1.0).astype(F32)
            prev.append(refs[pos][...].astype(F32) * keep)
            pos += 1
        else:
            prev.append(None)
    return cur, prev, pos


def _join(cur, prev):
    return [c if p is None else jnp.concatenate([p, c], axis=0) for c, p in zip(cur, prev)]


def _rw_fwd(f, rows, params, outs, *, name, n_reduce=0, halo=(), tt=None, jobs=()):
    t_len = rows[0].shape[0]
    tt = tt or _pick(t_len, TT_PREFS)
    nt = t_len // tt
    n_rows, n_par, n_out = len(rows), len(params), len(outs)

    def body(ins, orefs, scr):
        i = pl.program_id(0)
        cur, prev, pos = _load_rows(ins, n_rows, halo, i == 0)
        pvals = [ins[pos + k][...] for k in range(n_par)]
        res = f(_join(cur, prev), pvals)
        n_f = len(res) - n_reduce
        for k, (_, _, src) in enumerate(outs):
            orefs[k][...] = res[src].astype(orefs[k].dtype)
        for k in range(n_reduce):
            ref, val = orefs[n_out + k], res[n_f + k]

            @pl.when(i == 0)
            def _():
                ref[...] = val

            @pl.when(i > 0)
            def _():
                ref[...] += val

    in_specs, args = [], []
    for r, x in enumerate(rows):
        c = x.shape[1]
        in_specs.append(pl.BlockSpec((tt, c), lambda i: (i, 0)))
        args.append(x)
        if r in halo:
            in_specs.append(pl.BlockSpec((tt, c), lambda i: (jnp.maximum(i - 1, 0), 0)))
            args.append(x)
    for p in params:
        in_specs.append(_full_spec(p))
        args.append(p)
    out_specs = [pl.BlockSpec((tt, c), lambda i: (i, 0)) for c, _, _ in outs]
    out_shape = [jax.ShapeDtypeStruct((t_len, c), dt) for c, dt, _ in outs]
    for _ in range(n_reduce):
        out_specs.append(pl.BlockSpec((1, 1), lambda i: (0, 0)))
        out_shape.append(jax.ShapeDtypeStruct((1, 1), F32))
    return _call(body, name=name, grid=(nt,), in_specs=in_specs, out_specs=out_specs, out_shape=out_shape,
                 args=args, sem=("arbitrary",), jobs=jobs)


def _rw_bwd(f, rows, params, cts, *, name, n_reduce=0, halo=(), nd_rows=(), nd_params=(), tt=None,
            row_dtypes=None, jobs=()):
    t_len = rows[0].shape[0]
    tt = tt or _pick(t_len, TT_PREFS)
    nt = t_len // tt
    n_rows, n_par, n_ct = len(rows), len(params), len(cts)
    d_rows = [r for r in range(n_rows) if r not in nd_rows]
    d_pars = [k for k in range(n_par) if k not in nd_params]
    h_rows = [r for r in d_rows if r in halo]

    def blk(j):
        return nt - 1 - j

    def body(ins, orefs, carry_refs):
        j = pl.program_id(0)
        cur, prev, pos = _load_rows(ins, n_rows, halo, blk(j) == 0)
        pvals = [ins[pos + k][...] for k in range(n_par)]
        pos += n_par
        ct_vals = [ins[pos + k][...].astype(F32) for k in range(n_ct)]

        def g(dcur, dprev, dpar):
            c, p, q = list(cur), list(prev), list(pvals)
            for r, v in zip(d_rows, dcur):
                c[r] = v
            for r, v in zip(h_rows, dprev):
                p[r] = v
            for k, v in zip(d_pars, dpar):
                q[k] = v
            return tuple(f(_join(c, p), q))

        _, vjp = jax.vjp(g, [cur[r] for r in d_rows], [prev[r] for r in h_rows], [pvals[k] for k in d_pars])
        g_cur, g_prev, g_par = vjp(tuple(ct_vals))

        for n, r in enumerate(d_rows):
            if r in halo:
                cref = carry_refs[h_rows.index(r)]

                @pl.when(j == 0)
                def _():
                    cref[...] = jnp.zeros_like(cref)

                orefs[n][...] = (g_cur[n] + cref[...]).astype(orefs[n].dtype)
                cref[...] = g_prev[h_rows.index(r)]
            else:
                orefs[n][...] = g_cur[n].astype(orefs[n].dtype)
        for n in range(len(d_pars)):
            ref, val = orefs[len(d_rows) + n], g_par[n]

            @pl.when(j == 0)
            def _():
                ref[...] = val

            @pl.when(j > 0)
            def _():
                ref[...] += val

    in_specs, args = [], []
    for r, x in enumerate(rows):
        c = x.shape[1]
        in_specs.append(pl.BlockSpec((tt, c), lambda j: (blk(j), 0)))
        args.append(x)
        if r in halo:
            in_specs.append(pl.BlockSpec((tt, c), lambda j: (jnp.maximum(blk(j) - 1, 0), 0)))
            args.append(x)
    for p in params:
        in_specs.append(_full_spec(p))
        args.append(p)
    for ct in cts:
        if ct.shape == (1, 1):
            in_specs.append(pl.BlockSpec((1, 1), lambda j: (0, 0)))
        else:
            in_specs.append(pl.BlockSpec((tt, ct.shape[1]), lambda j: (blk(j), 0)))
        args.append(ct)
    out_specs, out_shape, scratch = [], [], []
    for n, r in enumerate(d_rows):
        c = rows[r].shape[1]
        out_specs.append(pl.BlockSpec((tt, c), lambda j: (blk(j), 0)))
        out_shape.append(jax.ShapeDtypeStruct((t_len, c), row_dtypes[n] if row_dtypes else F32))
        if r in halo:
            scratch.append(pltpu.VMEM((tt, c), F32))
    for k in d_pars:
        out_specs.append(_full_spec(params[k]))
        out_shape.append(jax.ShapeDtypeStruct(params[k].shape, F32))
    return _call(body, name=name + '_bwd', grid=(nt,), in_specs=in_specs, out_specs=out_specs,
                 out_shape=out_shape, args=args, scratch=scratch, sem=("arbitrary",), jobs=jobs)


def rw_fwd(sch, budget, f, rows, params, outs, **kw):
    return _ride(sch, budget, functools.partial(_rw_fwd, f, list(rows), list(params), outs), **kw)


def rw_bwd(sch, budget, f, rows, params, cts, **kw):
    return _ride(sch, budget, functools.partial(_rw_bwd, f, list(rows), list(params), list(cts)), **kw)


def _scan_call(a, b, mul, *, reverse, name, jobs=()):
    t_len, c_len = a.shape
    tt = _pick(t_len, TT_PREFS)
    tc = _pick(c_len, (512, 256, 128))
    nt = t_len // tt

    def body(ins, orefs, scr):
        (carry,) = scr
        t = pl.program_id(1)
        av, bv = ins[0][...], ins[1][...]
        row = lax.broadcasted_iota(jnp.int32, av.shape, 0)
        s = 1
        while s < tt:
            if reverse:
                ok = row < tt - s
                a_sh = jnp.where(ok, pltpu.roll(av, tt - s, 0), 1.0)
                b_sh = jnp.where(ok, pltpu.roll(bv, tt - s, 0), 0.0)
            else:
                ok = row >= s
                a_sh = jnp.where(ok, pltpu.roll(av, s, 0), 1.0)
                b_sh = jnp.where(ok, pltpu.roll(bv, s, 0), 0.0)
            bv = av * b_sh + bv
            av = av * a_sh
            s *= 2

        @pl.when(t == 0)
        def _():
            carry[...] = jnp.zeros_like(carry)

        hs = bv + av * carry[...]
        orefs[0][...] = hs
        edge = 0 if reverse else tt - 1
        carry[...] = orefs[0][edge:edge + 1, :]
        if mul is not None:
            orefs[1][...] = hs * ins[2][...]

    def idx(c, t):
        return ((nt - 1 - t) if reverse else t, c)

    spec = pl.BlockSpec((tt, tc), idx)
    n_in, n_out = (2, 1) if mul is None else (3, 2)
    return _call(body, name=name, grid=(c_len // tc, nt), in_specs=[spec] * n_in, out_specs=[spec] * n_out,
                 out_shape=[jax.ShapeDtypeStruct((t_len, c_len), F32)] * n_out,
                 args=[a, b] if mul is None else [a, b, mul],
                 scratch=[pltpu.VMEM((1, tc), F32)], sem=("parallel", "arbitrary"), jobs=jobs)


NT = (((1,), (1,)), ((), ()))


def _for_blocks(lo, hi, fn, group=2):
    n = hi - lo

    def trip(p, c):
        for g in range(group):
            fn(lo + group * p + g)
        return c

    lax.fori_loop(0, n // group, trip, 0)
    for g in range(1, group):
        @pl.when(n % group >= g)
        def _():
            fn(hi - (n % group) + g - 1)


def _att_weights(s, rel, diag, lg, softmax, scale, t_axis):
    row = lax.broadcasted_iota(jnp.int32, s.shape, t_axis)
    col = lax.broadcasted_iota(jnp.int32, s.shape, 1 - t_axis)
    if softmax:
        s = s * scale
        return jnp.where(row >= col, s, -1e30) if diag else s
    diff = (rel + row - col).astype(F32)
    dec = jnp.exp(jnp.maximum(diff, 0.0) * lg)
    return jnp.where(diff >= 0.0, dec, 0.0) if diag else dec


def _att_fwd_call(q, k, v, lgt, *, heads, softmax, scale, name, jobs=()):
    t_len = q.shape[0]
    dqk, dv = q.shape[1] // heads, v.shape[1] // heads
    blk = _pick(t_len, ATT_PREFS)
    nb = t_len // blk

    lanes = 128
    n_fold = blk // lanes

    def body(ins, orefs, scr):
        q_ref, k_ref, v_ref, lg_ref = ins
        o_ref, lse_ref = orefs
        s_sc, m_sc, acc_sc = scr
        i = pl.program_id(1)
        lg = lg_ref[0:1, 0:1]
        qb = q_ref[...]
        acc_sc[...] = jnp.zeros_like(acc_sc)

        def block_rows(j):
            return pl.ds(pl.multiple_of(j * blk, blk), blk)

        if not softmax:
            def step(j, diag):
                s = lax.dot_general(qb, k_ref[block_rows(j), :], NT, preferred_element_type=F32)
                w = _att_weights(s, (i - j) * blk, diag, lg, softmax, scale, 0)
                acc_sc[...] += jnp.dot((s * w).astype(BF), v_ref[block_rows(j), :], preferred_element_type=F32)

            _for_blocks(0, i, lambda j: step(j, False), group=3)
            step(i, True)
            o_ref[...] = acc_sc[...]
            lse_ref[...] = jnp.zeros_like(lse_ref)
            return

        m_sc[...] = jnp.full_like(m_sc, -1e30)

        def score(j, diag):
            s = lax.dot_general(qb, k_ref[block_rows(j), :], NT, preferred_element_type=F32)
            w = _att_weights(s, (i - j) * blk, diag, lg, softmax, scale, 0)
            s_sc[j] = w
            top = w[:, :lanes]
            for n in range(1, n_fold):
                top = jnp.maximum(top, w[:, n * lanes:(n + 1) * lanes])
            m_sc[...] = jnp.maximum(m_sc[...], top)

        _for_blocks(0, i, lambda j: score(j, False), group=3)
        score(i, True)
        m = jnp.max(m_sc[...], axis=-1, keepdims=True)
        ones = jnp.ones((blk, lanes), BF)

        def accumulate(j):
            p = jnp.exp(s_sc[j] - m).astype(BF)
            v_aug = jnp.concatenate([v_ref[block_rows(j), :], ones], axis=1)
            acc_sc[...] += jnp.dot(p, v_aug, preferred_element_type=F32)

        _for_blocks(0, i + 1, accumulate, group=3)
        acc = acc_sc[...]
        l = acc[:, dv:dv + 1]
        o_ref[...] = acc[:, :dv] / l
        lse_ref[...] = m + jnp.log(l)

    scratch = [pltpu.VMEM((nb, blk, blk), F32), pltpu.VMEM((blk, lanes), F32),
               pltpu.VMEM((blk, dv + lanes if softmax else dv), F32)]
    return _call(
        body, name=name, grid=(heads, nb),
        in_specs=[
            pl.BlockSpec((blk, dqk), lambda h, i: (i, h)),
            pl.BlockSpec((t_len, dqk), lambda h, i: (0, h)),
            pl.BlockSpec((t_len, dv), lambda h, i: (0, h)),
            pl.BlockSpec((None, 1, 128), lambda h, i: (h, 0, 0)),
        ],
        out_specs=[
            pl.BlockSpec((blk, dv), lambda h, i: (i, h)),
            pl.BlockSpec((None, blk, 1), lambda h, i: (h, i, 0)),
        ],
        out_shape=[jax.ShapeDtypeStruct((t_len, heads * dv), F32), jax.ShapeDtypeStruct((heads, t_len, 1), F32)],
        args=[q, k, v, lgt],
        scratch=scratch if softmax else [pltpu.VMEM((8, lanes), F32), scratch[1], scratch[2]],
        sem=("parallel", "arbitrary"), jobs=jobs)


def _att_dq_call(q, k, v, lgt, o, lse, do, *, heads, softmax, scale, name, jobs=()):
    t_len = q.shape[0]
    dqk, dv = q.shape[1] // heads, v.shape[1] // heads
    blk = _pick(t_len, ATT_PREFS)
    nb = t_len // blk

    def body(ins, orefs, scr):
        q_ref, k_ref, v_ref, lg_ref, o_ref, do_ref, lse_ref = ins
        dq_ref, delta_ref = orefs
        (acc,) = scr
        i = pl.program_id(1)
        lg = lg_ref[0:1, 0:1]
        qb = q_ref[...]
        do = do_ref[...]
        dob = do.astype(BF)
        delta = jnp.sum(do * o_ref[...], axis=-1, keepdims=True)
        lse = lse_ref[...]
        delta_ref[...] = delta
        acc[...] = jnp.zeros_like(acc)

        def step(j, diag):
            rows = pl.ds(pl.multiple_of(j * blk, blk), blk)
            kb = k_ref[rows, :]
            s = lax.dot_general(qb, kb, NT, preferred_element_type=F32)
            w = _att_weights(s, (i - j) * blk, diag, lg, softmax, scale, 0)
            dp = lax.dot_general(dob, v_ref[rows, :], NT, preferred_element_type=F32)
            ds = jnp.exp(w - lse) * (dp - delta) * scale if softmax else dp * w
            acc[...] += jnp.dot(ds.astype(BF), kb, preferred_element_type=F32)

        _for_blocks(0, i, lambda j: step(j, False))
        step(i, True)
        dq_ref[...] = acc[...]

    return _call(
        body, name=name + '_dq', grid=(heads, nb),
        in_specs=[
            pl.BlockSpec((blk, dqk), lambda h, i: (i, h)),
            pl.BlockSpec((t_len, dqk), lambda h, i: (0, h)),
            pl.BlockSpec((t_len, dv), lambda h, i: (0, h)),
            pl.BlockSpec((None, 1, 128), lambda h, i: (h, 0, 0)),
            pl.BlockSpec((blk, dv), lambda h, i: (i, h)),
            pl.BlockSpec((blk, dv), lambda h, i: (i, h)),
            pl.BlockSpec((None, blk, 1), lambda h, i: (h, i, 0)),
        ],
        out_specs=[pl.BlockSpec((blk, dqk), lambda h, i: (i, h)),
                   pl.BlockSpec((None, blk, 1), lambda h, i: (h, i, 0))],
        out_shape=[jax.ShapeDtypeStruct(q.shape, F32), jax.ShapeDtypeStruct((heads, t_len, 1), F32)],
        args=[q, k, v, lgt, o, do, lse],
        scratch=[pltpu.VMEM((blk, dqk), F32)], sem=("parallel", "arbitrary"), jobs=jobs)


def _att_dkv_call(q, k, v, lgt, lse_row, delta_row, do, *, heads, softmax, scale, name, jobs=()):
    t_len = q.shape[0]
    dqk, dv = q.shape[1] // heads, v.shape[1] // heads
    blk = _pick(t_len, ATT_PREFS)
    nb = t_len // blk

    def body(ins, orefs, scr):
        q_ref, k_ref, v_ref, lg_ref, do_ref, lse_ref, delta_ref = ins
        dk_acc, dv_acc = scr
        j = pl.program_id(1)
        lg = lg_ref[0:1, 0:1]
        kb, vb = k_ref[...], v_ref[...]
        dk_acc[...] = jnp.zeros_like(dk_acc)
        dv_acc[...] = jnp.zeros_like(dv_acc)

        def step(i, diag):
            rows = pl.ds(pl.multiple_of(i * blk, blk), blk)
            qb = q_ref[rows, :]
            dob = do_ref[rows, :].astype(BF)
            s = lax.dot_general(kb, qb, NT, preferred_element_type=F32)
            w = _att_weights(s, (i - j) * blk, diag, lg, softmax, scale, 1)
            dp = lax.dot_general(vb, dob, NT, preferred_element_type=F32)
            if softmax:
                p = jnp.exp(w - lse_ref[:, rows])
                ds = p * (dp - delta_ref[:, rows]) * scale
            else:
                p, ds = s * w, dp * w
            dv_acc[...] += jnp.dot(p.astype(BF), dob, preferred_element_type=F32)
            dk_acc[...] += jnp.dot(ds.astype(BF), qb, preferred_element_type=F32)

        step(j, True)
        _for_blocks(j + 1, nb, lambda i: step(i, False), group=3)
        orefs[0][...] = dk_acc[...]
        orefs[1][...] = dv_acc[...]

    return _call(
        body, name=name + '_dkv', grid=(heads, nb),
        in_specs=[
            pl.BlockSpec((t_len, dqk), lambda h, j: (0, h)),
            pl.BlockSpec((blk, dqk), lambda h, j: (j, h)),
            pl.BlockSpec((blk, dv), lambda h, j: (j, h)),
            pl.BlockSpec((None, 1, 128), lambda h, j: (h, 0, 0)),
            pl.BlockSpec((t_len, dv), lambda h, j: (0, h)),
            pl.BlockSpec((None, 1, t_len), lambda h, j: (h, 0, 0)),
            pl.BlockSpec((None, 1, t_len), lambda h, j: (h, 0, 0)),
        ],
        out_specs=[
            pl.BlockSpec((blk, dqk), lambda h, j: (j, h)),
            pl.BlockSpec((blk, dv), lambda h, j: (j, h)),
        ],
        out_shape=[jax.ShapeDtypeStruct(k.shape, F32), jax.ShapeDtypeStruct(v.shape, F32)],
        args=[q, k, v, lgt, do, lse_row, delta_row],
        scratch=[pltpu.VMEM((blk, dqk), F32), pltpu.VMEM((blk, dv), F32)],
        sem=("parallel", "arbitrary"), jobs=jobs)


def _adamw_call(contribs, w, m, v, *, name, jobs=()):
    r_len, c_len = w.shape
    n_chunk = len(contribs)
    r_chunk = r_len // n_chunk
    cap = max(min(ADAM_BLOCK_ELEMS, 6 * ADAM_BLOCK_ELEMS // n_chunk) // c_len, 1)
    tr = r_chunk
    for cand in (512, 256, 128, 64, 32, 16):
        if cand <= cap and r_chunk % cand == 0:
            tr = cand
            break
    per = r_chunk // tr

    def body(ins, orefs, scr):
        w_ref, m_ref, v_ref = ins[n_chunk:]
        g_ref, d_ref, mo_ref, vo_ref = orefs
        i = pl.program_id(0)

        def update(c_ref):
            g = c_ref[0].astype(F32)
            for n in range(1, N_DEV):
                g = g + c_ref[n].astype(F32)
            m_new = ADAM_B1 * m_ref[...] + (1.0 - ADAM_B1) * g
            v_new = ADAM_B2 * v_ref[...] + (1.0 - ADAM_B2) * jnp.square(g)
            m_hat = m_new / (1.0 - ADAM_B1 ** ADAM_STEP)
            v_hat = v_new / (1.0 - ADAM_B2 ** ADAM_STEP)
            g_ref[...] = g
            d_ref[...] = -ADAM_LR * (m_hat / (jnp.sqrt(v_hat) + ADAM_EPS) + ADAM_WD * w_ref[...])
            mo_ref[...] = m_new
            vo_ref[...] = v_new

        if n_chunk == 1:
            update(ins[0])
        else:
            for n in range(n_chunk):
                @pl.when(i // per == n)
                def _():
                    update(ins[n])

    spec = pl.BlockSpec((tr, c_len), lambda i: (i, 0))
    c_specs = [pl.BlockSpec((N_DEV, tr, c_len), functools.partial(
        lambda i, n: (0, jnp.clip(i - n * per, 0, per - 1), 0), n=n)) for n in range(n_chunk)]
    return _call(body, name=name, grid=(r_len // tr,), in_specs=c_specs + [spec, spec, spec],
                 out_specs=[spec] * 4, out_shape=[jax.ShapeDtypeStruct((r_len, c_len), F32)] * 4,
                 args=list(contribs) + [w, m, v], sem=("arbitrary",), jobs=jobs)


def _rope_lanes(x, cc, s_lo, s_hi):
    return x * cc + _roll(x, 32, 1) * s_lo + _roll(x, 96, 1) * s_hi


def _f_lru(tt, branch):
    lb = branch // LRU_HEADS

    def f(rows, params):
        (xcat,) = rows
        cw, cb, wa, ba, wx, bx, lam = params
        conv = cb
        for j in range(CONV_W):
            sh = CONV_W - 1 - j
            xs = xcat if sh == 0 else _roll(xcat, sh, 0)
            conv = conv + cw[j:j + 1, :] * xs[tt:, :]
        rs, gs = [], []
        for h in range(LRU_HEADS):
            ub = conv[:, h * lb:(h + 1) * lb]
            rs.append(_bdot(ub, wa[h]))
            gs.append(_bdot(ub, wx[h]))
        r = jax.nn.sigmoid(jnp.concatenate(rs, axis=-1) + ba)
        gate = jax.nn.sigmoid(jnp.concatenate(gs, axis=-1) + bx)
        log_a = LRU_C * r * (-_softplus(-lam))
        a = jnp.exp(log_a)
        one_minus_a2 = -jnp.tanh(log_a) * (jnp.exp(2.0 * log_a) + 1.0)
        return a, (conv * gate) * jnp.sqrt(one_minus_a2)

    return f


def _f_gate(rows, params):
    hs, g = rows
    return (hs * _silu(g),)


def _f_ln(rows, params):
    h, br = rows
    g, b = params
    pre = ALPHA * h + br
    mu = jnp.mean(pre, axis=-1, keepdims=True)
    var = jnp.mean(jnp.square(pre - mu), axis=-1, keepdims=True)
    return ((pre - mu) * lax.rsqrt(var + LN_EPS) * g + b,)


def _f_pool(tt, branch):
    grp = branch // len(POOL_WINDOWS)

    def f(rows, params):
        xcat, tidx = rows
        sums, acc, w = [], xcat, 1
        while w < POOL_WINDOWS[-1]:
            acc = acc + _roll(acc, w, 0)
            w *= 2
            sums.append(acc[tt:, :])
        u = xcat[tt:, :]
        outs = []
        for gi, w in enumerate(POOL_WINDOWS):
            sl = slice(gi * grp, (gi + 1) * grp)
            outs.append(sums[gi][:, sl] / jnp.minimum(tidx + 1.0, float(w)) - u[:, sl])
        return tuple(outs)

    return f


def _f_gate_pool(rows, params):
    m0, m1, m2, m3, g = rows
    (scale,) = params
    return (jnp.concatenate([m0, m1, m2, m3], axis=-1) * scale * _silu(g),)


def _rms(x, g):
    return x * lax.rsqrt(jnp.mean(jnp.square(x), axis=-1, keepdims=True) + RMS_EPS) * g


def _f_mla_pre(rows, params):
    c, cc, s_lo, s_hi = rows
    qn, kvn = params
    cq = c[:, :Q_LORA]
    ckv = c[:, Q_LORA:Q_LORA + KV_LORA]
    kr = c[:, Q_LORA + KV_LORA:]
    return _rms(cq, qn), _rms(ckv, kvn), _rope_lanes(kr, cc, s_lo, s_hi)


def _f_rope_q(rows, params):
    qc, cc, s_lo, s_hi = rows
    out = []
    for h in range(MLA_HEADS):
        out.append(qc[:, h * MLA_QK:h * MLA_QK + MLA_NOPE])
        out.append(_rope_lanes(qc[:, h * MLA_QK + MLA_NOPE:(h + 1) * MLA_QK], cc, s_lo, s_hi))
    return (jnp.concatenate(out, axis=-1),)


def _f_kcat(dv):
    per = MLA_NOPE + dv

    def f(rows, params):
        kv, krr = rows
        ks, vs = [], []
        for h in range(MLA_HEADS):
            ks.append(kv[:, h * per:h * per + MLA_NOPE])
            ks.append(krr)
            vs.append(kv[:, h * per + MLA_NOPE:(h + 1) * per])
        return jnp.concatenate(ks, axis=-1), jnp.concatenate(vs, axis=-1)

    return f


def _f_rope_ret(dk):
    half = dk // 2

    def f(rows, params):
        q, k, cos, sin = rows
        qs, ks = [], []
        for h in range(RET_HEADS):
            for src, dst, mult in ((q, qs, 1.0), (k, ks, dk ** -0.5)):
                x1 = src[:, h * dk:h * dk + half]
                x2 = src[:, h * dk + half:(h + 1) * dk]
                dst.append((x1 * cos - x2 * sin) * mult)
                dst.append((x2 * cos + x1 * sin) * mult)
        return jnp.concatenate(qs, axis=-1), jnp.concatenate(ks, axis=-1)

    return f


def _f_gate_gn(dv):
    def f(rows, params):
        o, g = rows
        out = []
        for h in range(RET_HEADS):
            oh = o[:, h * dv:(h + 1) * dv]
            mu = jnp.mean(oh, axis=-1, keepdims=True)
            var = jnp.mean(jnp.square(oh - mu), axis=-1, keepdims=True)
            out.append((oh - mu) * lax.rsqrt(var + LN_EPS))
        return (jnp.concatenate(out, axis=-1) * _silu(g),)

    return f


def _f_loss(rows, params):
    h, tgt, mask = rows
    per_row = jnp.mean(jnp.square(h - tgt), axis=-1, keepdims=True) * mask
    return (0.5 * jnp.sum(per_row, axis=0, keepdims=True),)


def _cat(parts, axis=1):
    return parts[0] if len(parts) == 1 else jnp.concatenate(parts, axis=axis)


def _cols(g):
    return jnp.transpose(g, (1, 0, 2)).reshape(g.shape[1], -1)


def _uncols(w):
    k, n = w.shape
    return jnp.transpose(w.reshape(k, N_DEV, n // N_DEV), (1, 0, 2))


def _heads(g):
    return jnp.transpose(g, (1, 0, 2, 3)).reshape(g.shape[1], -1, g.shape[3])


def _unheads(w):
    h, r, c = w.shape
    return jnp.transpose(w.reshape(h, N_DEV, r // N_DEV, c), (1, 0, 2, 3))


def _rope_tables(t_pad, d):
    inv = ROPE_BASE ** (-jnp.arange(0, d, 2, dtype=F32) / d)
    ang = jnp.arange(t_pad, dtype=F32)[:, None] * inv[None, :]
    return jnp.cos(ang), jnp.sin(ang)


def _row2(v):
    return v.reshape(1, -1)


def _train_local(sch, x2d, tgt_pad, S, *, t_pad):
    seq, d_model = x2d.shape
    branch = d_model
    t_real = N_META + seq
    tt = _pick(t_pad, TT_PREFS)
    n_win = len(POOL_WINDOWS)
    grp = branch // n_win
    lb = branch // LRU_HEADS
    dv2 = branch // MLA_HEADS
    dk3 = branch // RET_HEADS
    gS = {}
    RW, RWB = 0.06, 0.1

    def ln_fwd(h, br, layer):
        h1, hb1 = rw_fwd(sch, RW, _f_ln, [h, br], [_row2(S[f'l{layer}_ln_g']), _row2(S[f'l{layer}_ln_b'])],
                         [(d_model, F32, 0), (d_model, BF, 0)], name=f'l{layer}_ln')
        return h1, hb1

    def ln_bwd(h, br, dh1, layer):
        dh, dbr, dg, db = rw_bwd(sch, RWB, _f_ln, [h, br],
                                 [_row2(S[f'l{layer}_ln_g']), _row2(S[f'l{layer}_ln_b'])], [dh1],
                                 name=f'l{layer}_ln', row_dtypes=[F32, BF])
        gS[f'l{layer}_ln_g'], gS[f'l{layer}_ln_b'] = dg.reshape(-1), db.reshape(-1)
        return dh, dbr

    tidx = jnp.arange(t_pad, dtype=F32)[:, None]
    rowmask = ((tidx >= N_META) & (tidx < t_real)).astype(F32)

    meta = _cols(_cat(sch.get('meta_tokens')))
    h0 = jnp.concatenate([meta, x2d, jnp.zeros((t_pad - t_real, d_model), F32)], axis=0)
    hb0 = h0.astype(BF)

    w0_in = _cols(_cat(sch.get('l0_w_in')))
    w0_u, w0_g = w0_in[:, :branch], w0_in[:, branch:]
    u0 = mm(sch, hb0, w0_u, 'l0_in_u')
    g0 = mm(sch, hb0, w0_g, 'l0_in_g')
    conv_w = jnp.transpose(_cat(sch.get('l0_conv_w')), (1, 2, 0, 3)).reshape(CONV_W, branch)
    w_a = _heads(_cat(sch.get('l0_w_a'), axis=2))
    w_x = _heads(_cat(sch.get('l0_w_x'), axis=2))
    p0 = [conv_w, _row2(S['l0_conv_b']), w_a, _row2(S['l0_b_a']), w_x, _row2(S['l0_b_x']), _row2(S['l0_lam'])]
    f_lru = _f_lru(tt, branch)
    a0, xin0 = rw_fwd(sch, 0.13, f_lru, [u0], p0, [(branch, F32, 0), (branch, F32, 1)], name='l0_lru',
                      halo=(0,), tt=tt)
    hs0 = _ride(sch, 0.22, _scan_call, a=a0, b=xin0, mul=None, reverse=False, name='l0_scan')[0]
    (z0,) = rw_fwd(sch, RW, _f_gate, [hs0, g0], [], [(branch, BF, 0)], name='l0_gate')
    w0_out = _cat(sch.get('l0_w_out')).reshape(branch, d_model)
    br0 = mm(sch, z0, w0_out, 'l0_out')
    h1, hb1 = ln_fwd(h0, br0, 0)

    w1_in = _cols(_cat(sch.get('l1_w_in')))
    w1_u, w1_g = w1_in[:, :branch], w1_in[:, branch:]
    u1 = mm(sch, hb1, w1_u, 'l1_in_u')
    g1 = mm(sch, hb1, w1_g, 'l1_in_g')
    f_pool = _f_pool(tt, branch)
    ps1 = rw_fwd(sch, RW, f_pool, [u1, tidx], [], [(grp, BF, gi) for gi in range(n_win)], name='l1_pool',
                 halo=(0,), tt=tt)
    w1_grp = _heads(_cat(sch.get('l1_w_grp'), axis=2))
    mixed1 = [mm(sch, ps1[gi], w1_grp[gi], f'l1_grp{gi}') for gi in range(n_win)]
    p1 = [_row2(S['l1_scale'])]
    (z1,) = rw_fwd(sch, RW, _f_gate_pool, mixed1 + [g1], p1, [(branch, BF, 0)], name='l1_gate')
    w1_out = _cat(sch.get('l1_w_out')).reshape(branch, d_model)
    br1 = mm(sch, z1, w1_out, 'l1_out')
    h2, hb2 = ln_fwd(h1, br1, 1)

    w2_in = _cols(_cat(sch.get('l2_w_in')))
    w2_g = w2_in[:, :branch]
    w2_lat = jnp.pad(w2_in[:, branch:], ((0, 0), (0, 128 - MLA_ROPE)))
    g2 = mm(sch, hb2, w2_g, 'l2_in_g')
    c2 = mm(sch, hb2, w2_lat, 'l2_in_c')
    cos, sin = _rope_tables(t_pad, MLA_ROPE)
    zz = jnp.zeros_like(cos)
    tabs = [jnp.concatenate([cos, cos, zz, zz], axis=-1), jnp.concatenate([zz, sin, zz, zz], axis=-1),
            jnp.concatenate([-sin, zz, zz, zz], axis=-1)]
    p2 = [_row2(S['l2_q_norm']), _row2(S['l2_kv_norm'])]
    cqn2, ckvn2, krr2 = rw_fwd(sch, 0.04, _f_mla_pre, [c2] + tabs, p2,
                               [(Q_LORA, BF, 0), (KV_LORA, BF, 1), (128, F32, 2)], name='l2_pre')
    w2_uq = _cols(_cat(sch.get('l2_w_uq'))).reshape(Q_LORA, MLA_HEADS, MLA_NOPE + MLA_ROPE)
    w2_uq = jnp.pad(w2_uq, ((0, 0), (0, 0), (0, MLA_QK - MLA_NOPE - MLA_ROPE))).reshape(Q_LORA, MLA_HEADS * MLA_QK)
    w2_ukv = _cols(_cat(sch.get('l2_w_ukv')))
    qc2 = mm(sch, cqn2, w2_uq, 'l2_uq')
    kv2 = mm(sch, ckvn2, w2_ukv, 'l2_ukv')
    (qcr2,) = rw_fwd(sch, 0.09, _f_rope_q, [qc2] + tabs, [], [(MLA_HEADS * MLA_QK, BF, 0)], name='l2_rope_q')
    f_kcat = _f_kcat(dv2)
    kcat2, v2 = rw_fwd(sch, 0.1, f_kcat, [kv2, krr2], [], [(MLA_HEADS * MLA_QK, BF, 0), (branch, BF, 1)],
                       name='l2_kcat')
    no_decay = jnp.zeros((MLA_HEADS, 1, 128), F32)
    att2 = dict(heads=MLA_HEADS, softmax=True, scale=(MLA_NOPE + MLA_ROPE) ** -0.5, name='l2_att')
    o2, lse2 = _ride(sch, 1.3, _att_fwd_call, q=qcr2, k=kcat2, v=v2, lgt=no_decay, **att2)
    (z2,) = rw_fwd(sch, RW, _f_gate, [o2, g2], [], [(branch, BF, 0)], name='l2_gate')
    w2_out = _cat(sch.get('l2_w_out')).reshape(branch, d_model)
    br2 = mm(sch, z2, w2_out, 'l2_out')
    h3, hb3 = ln_fwd(h2, br2, 2)

    w3_in = _cols(_cat(sch.get('l3_w_in')))
    w3 = [w3_in[:, n * branch:(n + 1) * branch] for n in range(4)]
    q3 = mm(sch, hb3, w3[0], 'l3_in_q')
    k3 = mm(sch, hb3, w3[1], 'l3_in_k')
    v3 = mm(sch, hb3, w3[2], 'l3_in_v', out_dtype=BF)
    g3 = mm(sch, hb3, w3[3], 'l3_in_g')
    cs3 = list(_rope_tables(t_pad, dk3))
    f_rope3 = _f_rope_ret(dk3)
    qr3, kr3 = rw_fwd(sch, 0.09, f_rope3, [q3, k3] + cs3, [], [(branch, BF, 0), (branch, BF, 1)], name='l3_rope')
    log_g = jnp.log(1.0 - 2.0 ** (-5.0 - jnp.arange(RET_HEADS, dtype=F32)))
    lgt = jnp.broadcast_to(log_g[:, None, None], (RET_HEADS, 1, 128))
    att3 = dict(heads=RET_HEADS, softmax=False, scale=1.0, name='l3_ret')
    o3, lse3 = _ride(sch, 0.6, _att_fwd_call, q=qr3, k=kr3, v=v3, lgt=lgt, **att3)
    f_gn = _f_gate_gn(dk3)
    (z3,) = rw_fwd(sch, 0.08, f_gn, [o3, g3], [], [(branch, BF, 0)], name='l3_gate')
    w3_out = _cat(sch.get('l3_w_out')).reshape(branch, d_model)
    br3 = mm(sch, z3, w3_out, 'l3_out')
    (h4,) = rw_fwd(sch, RW, _f_ln, [h3, br3], [_row2(S['l3_ln_g']), _row2(S['l3_ln_b'])], [(d_model, F32, 0)],
                   name='l3_ln')

    (loss,) = rw_fwd(sch, 0.05, _f_loss, [h4, tgt_pad, rowmask], [], [], name='loss', n_reduce=1)

    sch.scale = BWD_RIDER_SHARE
    (dh4,) = rw_bwd(sch, 0.07, _f_loss, [h4, tgt_pad, rowmask], [], [jnp.ones((1, 1), F32)], name='loss',
                    n_reduce=1, nd_rows=(1, 2))

    dh3, dbr3 = ln_bwd(h3, br3, dh4, 3)
    dz3 = mm(sch, dbr3, w3_out, 'l3_out_dx', mode='nt')
    sch.push('l3_w_out', mm(sch, z3.T, dbr3, 'l3_out_dw', mode='nn', out_dtype=BF).reshape(N_DEV, -1, d_model),
             False, 1)
    do3, dg3 = rw_bwd(sch, 0.13, f_gn, [o3, g3], [], [dz3], name='l3_gate', row_dtypes=[F32, BF])
    dqr3, delta3 = _ride(sch, 0.7, _att_dq_call, q=qr3, k=kr3, v=v3, lgt=lgt, o=o3, lse=lse3, do=do3, **att3)
    dkr3, dv3 = _ride(sch, 0.8, _att_dkv_call, q=qr3, k=kr3, v=v3, lgt=lgt, lse_row=lse3.reshape(RET_HEADS, 1, -1),
                      delta_row=delta3.reshape(RET_HEADS, 1, -1), do=do3, **att3)
    dq3, dk3_ = rw_bwd(sch, 0.13, f_rope3, [q3, k3] + cs3, [], [dqr3, dkr3], name='l3_rope', nd_rows=(2, 3),
                       row_dtypes=[BF, BF])
    d3 = [dq3, dk3_, dv3, dg3]
    for n in range(4):
        dh3 = mm(sch, d3[n], w3[n], f'l3_in_dx{n}', mode='nt', add=dh3)
    hb3_t = hb3.T
    dw3 = [mm(sch, hb3_t, d3[n], f'l3_in_dw{n}', mode='nn', out_dtype=BF) for n in range(4)]
    sch.push('l3_w_in', _uncols(jnp.concatenate(dw3, axis=1)), False, 1)

    dh2, dbr2 = ln_bwd(h2, br2, dh3, 2)
    dz2 = mm(sch, dbr2, w2_out, 'l2_out_dx', mode='nt')
    sch.push('l2_w_out', mm(sch, z2.T, dbr2, 'l2_out_dw', mode='nn', out_dtype=BF).reshape(N_DEV, -1, d_model),
             False, 1)
    do2, dg2 = rw_bwd(sch, 0.1, _f_gate, [o2, g2], [], [dz2], name='l2_gate', row_dtypes=[F32, BF])
    dqcr2, delta2 = _ride(sch, 1.3, _att_dq_call, q=qcr2, k=kcat2, v=v2, lgt=no_decay, o=o2, lse=lse2, do=do2,
                          **att2)
    dkcat2, dv2_ = _ride(sch, 1.5, _att_dkv_call, q=qcr2, k=kcat2, v=v2, lgt=no_decay,
                         lse_row=lse2.reshape(MLA_HEADS, 1, -1), delta_row=delta2.reshape(MLA_HEADS, 1, -1),
                         do=do2, **att2)
    dkv2, dkrr2 = rw_bwd(sch, 0.14, f_kcat, [kv2, krr2], [], [dkcat2, dv2_], name='l2_kcat', row_dtypes=[BF, F32])
    (dqc2,) = rw_bwd(sch, 0.13, _f_rope_q, [qc2] + tabs, [], [dqcr2], name='l2_rope_q', nd_rows=(1, 2, 3),
                     row_dtypes=[BF])
    dckvn2 = mm(sch, dkv2, w2_ukv, 'l2_ukv_dx', mode='nt')
    dcqn2 = mm(sch, dqc2, w2_uq, 'l2_uq_dx', mode='nt')
    sch.push('l2_w_ukv', _uncols(mm(sch, ckvn2, dkv2, 'l2_ukv_dw', mode='tn', out_dtype=BF)), False, 1)
    dw_uq = mm(sch, cqn2, dqc2, 'l2_uq_dw', mode='tn', out_dtype=BF)
    dw_uq = dw_uq.reshape(Q_LORA, MLA_HEADS, MLA_QK)[:, :, :MLA_NOPE + MLA_ROPE].reshape(Q_LORA, -1)
    sch.push('l2_w_uq', _uncols(dw_uq), False, 1)
    dc2, dqn, dkvn = rw_bwd(sch, 0.05, _f_mla_pre, [c2] + tabs, p2, [dcqn2, dckvn2, dkrr2], name='l2_pre',
                            nd_rows=(1, 2, 3), row_dtypes=[BF])
    gS['l2_q_norm'], gS['l2_kv_norm'] = dqn.reshape(-1), dkvn.reshape(-1)
    dh2 = mm(sch, dg2, w2_g, 'l2_in_g_dx', mode='nt', add=dh2)
    dh2 = mm(sch, dc2, w2_lat, 'l2_in_c_dx', mode='nt', add=dh2)
    hb2_t = hb2.T
    dw2_g = mm(sch, hb2_t, dg2, 'l2_in_g_dw', mode='nn', out_dtype=BF)
    dw2_lat = mm(sch, hb2_t, dc2, 'l2_in_c_dw', mode='nn', out_dtype=BF)
    n_lat = Q_LORA + KV_LORA + MLA_ROPE
    sch.push('l2_w_in', _uncols(jnp.concatenate([dw2_g, dw2_lat[:, :n_lat]], axis=1)), False, 1)

    dh1, dbr1 = ln_bwd(h1, br1, dh2, 1)
    dz1 = mm(sch, dbr1, w1_out, 'l1_out_dx', mode='nt')
    sch.push('l1_w_out', mm(sch, z1.T, dbr1, 'l1_out_dw', mode='nn', out_dtype=BF).reshape(N_DEV, -1, d_model),
             False, 1)
    res = rw_bwd(sch, 0.11, _f_gate_pool, mixed1 + [g1], p1, [dz1], name='l1_gate', row_dtypes=[BF] * (n_win + 1))
    dmixed1, dg1, dscale = res[:n_win], res[n_win], res[n_win + 1]
    gS['l1_scale'] = dscale.reshape(-1)
    dps1 = [mm(sch, dmixed1[gi], w1_grp[gi], f'l1_grp{gi}_dx', mode='nt') for gi in range(n_win)]
    dw_grp = jnp.stack([mm(sch, ps1[gi], dmixed1[gi], f'l1_grp{gi}_dw', mode='tn', out_dtype=BF)
                        for gi in range(n_win)])
    sch.push('l1_w_grp', _unheads(dw_grp), False, 2)
    (du1,) = rw_bwd(sch, 0.09, f_pool, [u1, tidx], [], dps1, name='l1_pool', halo=(0,), nd_rows=(1,), tt=tt,
                    row_dtypes=[BF])
    dh1 = mm(sch, du1, w1_u, 'l1_in_u_dx', mode='nt', add=dh1)
    dh1 = mm(sch, dg1, w1_g, 'l1_in_g_dx', mode='nt', add=dh1)
    hb1_t = hb1.T
    dw1 = [mm(sch, hb1_t, du1, 'l1_in_u_dw', mode='nn', out_dtype=BF),
           mm(sch, hb1_t, dg1, 'l1_in_g_dw', mode='nn', out_dtype=BF)]
    sch.push('l1_w_in', _uncols(jnp.concatenate(dw1, axis=1)), False, 1)

    dh0, dbr0 = ln_bwd(h0, br0, dh1, 0)
    dz0 = mm(sch, dbr0, w0_out, 'l0_out_dx', mode='nt')
    sch.push('l0_w_out', mm(sch, z0.T, dbr0, 'l0_out_dw', mode='nn', out_dtype=BF).reshape(N_DEV, -1, d_model),
             False, 1)
    dhs0, dg0 = rw_bwd(sch, 0.1, _f_gate, [hs0, g0], [], [dz0], name='l0_gate', row_dtypes=[F32, BF])
    a_next = jnp.concatenate([a0[1:], jnp.ones_like(a0[:1])], axis=0)
    hs_prev = jnp.concatenate([jnp.zeros_like(hs0[:1]), hs0[:-1]], axis=0)
    dxin0, da0 = _ride(sch, 0.25, _scan_call, a=a_next, b=dhs0, mul=hs_prev, reverse=True, name='l0_scan_bwd')
    res = rw_bwd(sch, 0.3, f_lru, [u0], p0, [da0, dxin0], name='l0_lru', halo=(0,), tt=tt, row_dtypes=[BF])
    du0 = res[0]
    gS['l0_conv_b'], gS['l0_b_a'], gS['l0_b_x'], gS['l0_lam'] = [res[k].reshape(-1) for k in (2, 4, 6, 7)]
    sch.push('l0_w_a', _unheads(res[3]), False, 2)
    sch.push('l0_w_x', _unheads(res[5]), False, 2)
    sch.push('l0_conv_w', jnp.transpose(res[1].reshape(CONV_W, 1, N_DEV, -1), (2, 0, 1, 3)), False, 3)
    hb0_t = hb0.T
    dw0 = [mm(sch, hb0_t, du0, 'l0_in_u_dw', mode='nn', out_dtype=BF),
           mm(sch, hb0_t, dg0, 'l0_in_g_dw', mode='nn', out_dtype=BF)]
    sch.push('l0_w_in', _uncols(jnp.concatenate(dw0, axis=1)), False, 1, chunk_ms=EXCH_CHUNK_MS)
    sch.scale = 1.0
    dh0 = mm(sch, du0, w0_u, 'l0_in_u_dx', mode='nt', add=dh0)
    dh0 = mm(sch, dg0, w0_g, 'l0_in_g_dx', mode='nt', add=dh0)
    sch.push('meta_tokens', _uncols(dh0[:N_META]), False, 1)

    return loss[0, 0], dh0[N_META:t_real], gS


def _as2d(a):
    return a.reshape(-1, a.shape[-1])


def kernel(x, meta_tokens, l0_w_in, l0_conv_w, l0_conv_b, l0_w_a, l0_b_a, l0_w_x, l0_b_x, l0_lam, l0_w_out, l0_ln_g, l0_ln_b, l1_w_in, l1_w_grp, l1_scale, l1_w_out, l1_ln_g, l1_ln_b, l2_w_in, l2_q_norm, l2_w_uq, l2_kv_norm, l2_w_ukv, l2_w_out, l2_ln_g, l2_ln_b, l3_w_in, l3_w_out, l3_ln_g, l3_ln_b, loss_target, m_meta_tokens, m_l0_w_in, m_l0_conv_w, m_l0_conv_b, m_l0_w_a, m_l0_b_a, m_l0_w_x, m_l0_b_x, m_l0_lam, m_l0_w_out, m_l0_ln_g, m_l0_ln_b, m_l1_w_in, m_l1_w_grp, m_l1_scale, m_l1_w_out, m_l1_ln_g, m_l1_ln_b, m_l2_w_in, m_l2_q_norm, m_l2_w_uq, m_l2_kv_norm, m_l2_w_ukv, m_l2_w_out, m_l2_ln_g, m_l2_ln_b, m_l3_w_in, m_l3_w_out, m_l3_ln_g, m_l3_ln_b, v_meta_tokens, v_l0_w_in, v_l0_conv_w, v_l0_conv_b, v_l0_w_a, v_l0_b_a, v_l0_w_x, v_l0_b_x, v_l0_lam, v_l0_w_out, v_l0_ln_g, v_l0_ln_b, v_l1_w_in, v_l1_w_grp, v_l1_scale, v_l1_w_out, v_l1_ln_g, v_l1_ln_b, v_l2_w_in, v_l2_q_norm, v_l2_w_uq, v_l2_kv_norm, v_l2_w_ukv, v_l2_w_out, v_l2_ln_g, v_l2_ln_b, v_l3_w_in, v_l3_w_out, v_l3_ln_g, v_l3_ln_b):
    args = (meta_tokens, l0_w_in, l0_conv_w, l0_conv_b, l0_w_a, l0_b_a, l0_w_x, l0_b_x, l0_lam, l0_w_out, l0_ln_g, l0_ln_b, l1_w_in, l1_w_grp, l1_scale, l1_w_out, l1_ln_g, l1_ln_b, l2_w_in, l2_q_norm, l2_w_uq, l2_kv_norm, l2_w_ukv, l2_w_out, l2_ln_g, l2_ln_b, l3_w_in, l3_w_out, l3_ln_g, l3_ln_b)
    moms = (m_meta_tokens, m_l0_w_in, m_l0_conv_w, m_l0_conv_b, m_l0_w_a, m_l0_b_a, m_l0_w_x, m_l0_b_x, m_l0_lam, m_l0_w_out, m_l0_ln_g, m_l0_ln_b, m_l1_w_in, m_l1_w_grp, m_l1_scale, m_l1_w_out, m_l1_ln_g, m_l1_ln_b, m_l2_w_in, m_l2_q_norm, m_l2_w_uq, m_l2_kv_norm, m_l2_w_ukv, m_l2_w_out, m_l2_ln_g, m_l2_ln_b, m_l3_w_in, m_l3_w_out, m_l3_ln_g, m_l3_ln_b)
    vels = (v_meta_tokens, v_l0_w_in, v_l0_conv_w, v_l0_conv_b, v_l0_w_a, v_l0_b_a, v_l0_w_x, v_l0_b_x, v_l0_lam, v_l0_w_out, v_l0_ln_g, v_l0_ln_b, v_l1_w_in, v_l1_w_grp, v_l1_scale, v_l1_w_out, v_l1_ln_g, v_l1_ln_b, v_l2_w_in, v_l2_q_norm, v_l2_w_uq, v_l2_kv_norm, v_l2_w_ukv, v_l2_w_out, v_l2_ln_g, v_l2_ln_b, v_l3_w_in, v_l3_w_out, v_l3_ln_g, v_l3_ln_b)
    W = dict(zip(WEIGHTS, args))
    M = dict(zip(WEIGHTS, moms))
    V = dict(zip(WEIGHTS, vels))

    seq = x.shape[1]
    t_real = N_META + seq
    t_pad = -(-t_real // ROW_ALIGN) * ROW_ALIGN
    tgt_pad = jnp.pad(loss_target[0], ((N_META, t_pad - t_real), (0, 0)))

    sch = _Schedule()
    for n in GATHER_ORDER:
        shard = W[n].astype(BF) if n in BIG else W[n]
        sch.push(n, shard, True, shard.ndim - 2)
    S = {n: W[n] for n in REPLICATED}

    loss, gx, gS = _train_local(sch, x[0], tgt_pad, S, t_pad=t_pad)

    flat = jnp.concatenate([gS[n].reshape(-1) for n in REPLICATED]).reshape(-1, 128)
    sch.push('small_grads', flat, True, 0)

    out_g, out_d, out_m, out_v = {}, {}, {}, {}
    order = ['l3_w_out', 'l3_w_in', 'l2_w_out', 'l2_w_ukv', 'l2_w_uq', 'l2_w_in', 'l1_w_out', 'l1_w_grp',
             'l1_w_in', 'l0_w_out', 'l0_w_a', 'l0_w_x', 'l0_conv_w', 'l0_w_in', 'meta_tokens']
    for n in order:
        shp = W[n].shape
        w2, m2, v2 = _as2d(W[n]), _as2d(M[n]), _as2d(V[n])
        parts = [p.reshape((N_DEV, -1, w2.shape[1])) for p in sch.get(n)]
        res = _ride(sch, w2.size * ADAM_MS_PER_ELEM, _adamw_call, contribs=parts, w=w2, m=m2, v=v2, name='adamw_' + n)
        out_g[n], out_d[n], out_m[n], out_v[n] = [r.reshape(shp) for r in res]
    cat = lambda D: jnp.concatenate([D[n].reshape(-1) for n in REPLICATED]).reshape(-1, 128)
    res = _adamw_call(sch.get('small_grads'), cat(W), cat(M), cat(V), name='adamw_small')[0]
    off = 0
    for n in REPLICATED:
        size = W[n].size
        for dst, r in zip((out_g, out_d, out_m, out_v), res):
            dst[n] = r.reshape(-1)[off:off + size].reshape(W[n].shape)
        off += size
    sch.flush()

    loss = lax.psum(loss, ("x", "y", "c"))
    return (loss, gx[None], *[out_g[n] for n in WEIGHTS], *[out_d[n] for n in WEIGHTS],
            *[out_m[n] for n in WEIGHTS], *[out_v[n] for n in WEIGHTS])
```

```python
import functools

import jax
import jax.numpy as jnp
from jax import lax
from jax.experimental import pallas as pl
from jax.experimental.pallas import tpu as pltpu

F32 = jnp.float32
BF = jnp.bfloat16

N_DEV = 8
N_META = 16
ALPHA = (2.0 * 4) ** 0.25
LN_EPS = 1e-5
RMS_EPS = 1e-6
ROPE_BASE = 10000.0
LRU_HEADS = 16
CONV_W = 4
LRU_C = 8.0
POOL_WINDOWS = (2, 4, 8, 16)
MLA_HEADS = 32
MLA_NOPE = 128
MLA_ROPE = 64
MLA_QK = 256
Q_LORA = 1024
KV_LORA = 512
RET_HEADS = 16
ADAM_LR = 0.001
ADAM_B1 = 0.9
ADAM_B2 = 0.999
ADAM_EPS = 1e-08
ADAM_WD = 0.01
ADAM_STEP = 10

ROW_ALIGN = 128
VMEM_LIMIT_BYTES = 56 * 1024 * 1024
TT_PREFS = (128, 64, 32, 16, 8)
ATT_PREFS = (384, 256, 128)
MM_M_PREFS = (2048, 1408, 1024, 512, 384, 256, 128)
MM_N_PREFS = (1024, 640, 512, 384, 256, 128)
MM_K_PREFS = (1408, 1024, 512, 384, 256, 128)
MM_VMEM_BUDGET = 40 * 1024 * 1024
ADAM_BLOCK_ELEMS = 128 * 1024
EXCH_MS_PER_MB = 0.0857
EXCH_CHUNK_MS = 0.1
MXU_FLOPS_PER_MS = 7.8e11
BWD_RIDER_SHARE = 0.6
ADAM_MS_PER_ELEM = 2.3e-8

WEIGHTS = ['meta_tokens', 'l0_w_in', 'l0_conv_w', 'l0_conv_b', 'l0_w_a', 'l0_b_a', 'l0_w_x', 'l0_b_x', 'l0_lam',
           'l0_w_out', 'l0_ln_g', 'l0_ln_b', 'l1_w_in', 'l1_w_grp', 'l1_scale', 'l1_w_out', 'l1_ln_g', 'l1_ln_b',
           'l2_w_in', 'l2_q_norm', 'l2_w_uq', 'l2_kv_norm', 'l2_w_ukv', 'l2_w_out', 'l2_ln_g', 'l2_ln_b',
           'l3_w_in', 'l3_w_out', 'l3_ln_g', 'l3_ln_b']
BIG = ['l0_w_in', 'l0_w_out', 'l1_w_in', 'l1_w_grp', 'l1_w_out', 'l2_w_in', 'l2_w_uq', 'l2_w_ukv', 'l2_w_out',
       'l3_w_in', 'l3_w_out']
SHARDED_F32 = ['meta_tokens', 'l0_conv_w', 'l0_w_a', 'l0_w_x']
REPLICATED = [n for n in WEIGHTS if n not in BIG and n not in SHARDED_F32]
GATHER_ORDER = ['meta_tokens', 'l0_w_in', 'l0_conv_w', 'l0_w_a', 'l0_w_x', 'l0_w_out', 'l1_w_in', 'l1_w_grp',
                'l1_w_out', 'l2_w_in', 'l2_w_uq', 'l2_w_ukv', 'l2_w_out', 'l3_w_in', 'l3_w_out']


def _pick(n, prefs):
    for p in prefs:
        if n % p == 0:
            return p
    return n


def _exchange_copies(jobs, in_refs, out_refs, send_sems, recv_sems, local_sems):
    x, y, c = lax.axis_index("x"), lax.axis_index("y"), lax.axis_index("c")
    me = 4 * x + 2 * y + c
    sibling = (x, y, 1 - c)
    chips = [(1 - x, y), (x, 1 - y), (1 - x, 1 - y)]
    peers = [sibling] + [(px, py, c) for px, py in chips] + [(px, py, 1 - c) for px, py in chips]
    first, arrivals, forwards, rest = [], [], [], []

    def dev(px, py, pc):
        return 4 * px + 2 * py + pc

    for n, (_, gather) in enumerate(jobs):
        def remote(p, src, slot, to, n=n):
            k = n * (N_DEV - 1) + p
            return pltpu.make_async_remote_copy(
                src_ref=src, dst_ref=out_refs[n].at[slot], send_sem=send_sems.at[k], recv_sem=recv_sems.at[k],
                device_id=to, device_id_type=pl.DeviceIdType.MESH)

        if gather:
            first += [remote(p, in_refs[n], me, peers[p]) for p in range(4)]
            for j, (px, py) in enumerate(chips):
                landed = out_refs[n].at[dev(px, py, c)]
                arrivals.append(remote(1 + j, landed, dev(px, py, c), peers[1 + j]))
                forwards.append(remote(4 + j, landed, dev(px, py, c), sibling))
        else:
            first += [remote(p, in_refs[n].at[dev(*peers[p])], me, peers[p]) for p in range(N_DEV - 1)]
        rest.append(pltpu.make_async_copy(in_refs[n] if gather else in_refs[n].at[me], out_refs[n].at[me],
                                          local_sems.at[n]))

    def start():
        for cp in first + rest:
            cp.start()

    def finish():
        for arrived, forward in zip(arrivals, forwards):
            arrived.wait_recv()
            forward.start()
        for cp in first:
            cp.wait_send()
        for n, (_, gather) in enumerate(jobs):
            for p in range(N_DEV - 1):
                if not (gather and 1 <= p <= 3):
                    k = n * (N_DEV - 1) + p
                    pltpu.make_async_remote_copy(
                        src_ref=out_refs[n].at[me], dst_ref=out_refs[n].at[me], send_sem=send_sems.at[k],
                        recv_sem=recv_sems.at[k], device_id=sibling, device_id_type=pl.DeviceIdType.MESH).wait_recv()
        for cp in forwards:
            cp.wait_send()
        for cp in rest:
            cp.wait()

    return start, finish


def _exchange_shapes(jobs):
    shapes = []
    for arr, gather in jobs:
        blk = arr.shape if gather else arr.shape[1:]
        shapes.append(jax.ShapeDtypeStruct((N_DEV,) + tuple(blk), arr.dtype))
    return shapes


def _exchange_scratch(jobs):
    n = len(jobs)
    return [pltpu.SemaphoreType.DMA((n * (N_DEV - 1),)), pltpu.SemaphoreType.DMA((n * (N_DEV - 1),)),
            pltpu.SemaphoreType.DMA((n,))]


def _call(body, *, name, grid, in_specs, out_specs, out_shape, args, scratch=(), sem=None, jobs=()):
    jobs = list(jobs)
    n_in, n_out, n_sc, n_job = len(in_specs), len(out_specs), len(scratch), len(jobs)
    hbm = pl.BlockSpec(memory_space=pl.ANY)

    def kern(*refs):
        ins = refs[:n_in]
        job_ins = refs[n_in:n_in + n_job]
        pos = n_in + n_job
        outs = refs[pos:pos + n_out]
        job_outs = refs[pos + n_out:pos + n_out + n_job]
        pos += n_out + n_job
        scr = refs[pos:pos + n_sc]
        if n_job:
            ids = [pl.program_id(d) for d in range(len(grid))]
            first = functools.reduce(jnp.logical_and, [i == 0 for i in ids])
            last = functools.reduce(jnp.logical_and, [i == g - 1 for i, g in zip(ids, grid)])
            start, finish = _exchange_copies(jobs, job_ins, job_outs, *refs[pos + n_sc:])
            pl.when(first)(start)

        body(ins, outs, scr)
        if n_job:
            pl.when(last)(finish)

    kw = dict(vmem_limit_bytes=VMEM_LIMIT_BYTES)
    if sem is not None:
        kw['dimension_semantics'] = tuple("arbitrary" for _ in grid) if n_job else sem
    res = pl.pallas_call(
        kern, name=name, grid=grid,
        in_specs=list(in_specs) + [hbm] * n_job,
        out_specs=list(out_specs) + [hbm] * n_job,
        out_shape=list(out_shape) + _exchange_shapes(jobs),
        scratch_shapes=list(scratch) + (_exchange_scratch(jobs) if n_job else []),
        compiler_params=pltpu.CompilerParams(**kw),
    )(*args, *[a for a, _ in jobs])
    return list(res[:n_out]), list(res[n_out:])


def _exchange_alone(jobs, name):
    def body(*refs):
        n = len(jobs)
        start, finish = _exchange_copies(jobs, refs[:n], refs[n:2 * n], *refs[2 * n:])
        start()
        finish()

    hbm = pl.BlockSpec(memory_space=pl.ANY)
    return list(pl.pallas_call(
        body, name=name, in_specs=[hbm] * len(jobs), out_specs=[hbm] * len(jobs),
        out_shape=_exchange_shapes(jobs), scratch_shapes=_exchange_scratch(jobs),
    )(*[a for a, _ in jobs]))


class _Schedule:
    def __init__(self):
        self.pending = []
        self.done = {}
        self.chunks = {}
        self.count = 0
        self.scale = 1.0

    def push(self, name, arr, gather, row_axis, chunk_ms=None):
        mb = arr.size * arr.dtype.itemsize / (1 if gather else N_DEV) / 1e6
        cost = mb * EXCH_MS_PER_MB * (0.5 if gather else 1.0)
        rows = arr.shape[row_axis]
        n = 1
        leading = row_axis == (0 if gather else 1)
        target = chunk_ms or (EXCH_CHUNK_MS if gather else 2 * EXCH_CHUNK_MS)
        while leading and cost / n > target and rows % (2 * n) == 0 and rows // (2 * n) >= 16:
            n *= 2
        self.chunks[name] = n
        step = rows // n
        for k in range(n):
            piece = lax.slice_in_dim(arr, k * step, (k + 1) * step, axis=row_axis) if n > 1 else arr
            self.pending.append(((name, k), piece, gather, cost / n))

    def take(self, budget):
        jobs, spent = [], 0.0
        budget *= self.scale
        while self.pending and spent + 0.5 * self.pending[0][3] <= budget:
            job = self.pending.pop(0)
            jobs.append(job)
            spent += job[3]
        return jobs

    def deliver(self, jobs, results):
        for (key, _, _, _), r in zip(jobs, results):
            self.done[key] = r

    def get(self, name):
        mine = [j for j in self.pending if j[0][0] == name]
        if mine:
            self.pending = [j for j in self.pending if j[0][0] != name]
            self.count += 1
            self.deliver(mine, _exchange_alone([(j[1], j[2]) for j in mine], f'exchange_{self.count}_{name}'))
        parts = [self.done.pop((name, k)) for k in range(self.chunks[name])]
        return parts

    def flush(self):
        if self.pending:
            jobs, self.pending = self.pending, []
            self.count += 1
            self.deliver(jobs, _exchange_alone([(j[1], j[2]) for j in jobs], f'exchange_{self.count}_rest'))


def _ride(sch, budget, fn, **kw):
    jobs = sch.take(budget) if sch is not None else []
    outs, exch = fn(jobs=[(j[1], j[2]) for j in jobs], **kw)
    if jobs:
        sch.deliver(jobs, exch)
    return outs


@functools.partial(jax.custom_vjp, nondiff_argnums=(1, 2))
def _roll(x, shift, axis):
    return pltpu.roll(x, shift, axis)


def _roll_fwd(x, shift, axis):
    return pltpu.roll(x, shift, axis), None


def _roll_bwd(shift, axis, _, g):
    n = g.shape[axis]
    return (pltpu.roll(g, (n - shift) % n, axis),)


_roll.defvjp(_roll_fwd, _roll_bwd)


@jax.custom_vjp
def _bdot(x, w):
    return jnp.dot(x.astype(BF), w.astype(BF), preferred_element_type=F32)


def _bdot_fwd(x, w):
    return _bdot(x, w), (x, w)


def _bdot_bwd(res, g):
    x, w = res
    gb = g.astype(BF)
    dx = lax.dot_general(gb, w.astype(BF), (((1,), (1,)), ((), ())), preferred_element_type=F32)
    dw = lax.dot_general(x.astype(BF), gb, (((0,), (0,)), ((), ())), preferred_element_type=F32)
    return dx, dw


_bdot.defvjp(_bdot_fwd, _bdot_bwd)


def _silu(g):
    return g * jax.nn.sigmoid(g)


def _softplus(x):
    return jnp.maximum(x, 0.0) + jnp.log1p(jnp.exp(-jnp.abs(x)))


def _mm_call(a, b, *, mode, out_dtype, name, add=None, jobs=()):
    if mode == 'nn':
        (mo, kc), (_, no) = a.shape, b.shape
    elif mode == 'nt':
        (mo, kc), (no, _) = a.shape, b.shape
    else:
        (kc, mo), (_, no) = a.shape, b.shape
    tm = _pick(mo, MM_M_PREFS)
    tn = no if no <= 2048 and no % 512 != 0 else _pick(no, MM_N_PREFS)
    out_bytes = tm * tn * (2 * jnp.dtype(out_dtype).itemsize + 4 + (8 if add is not None else 0))
    tk = None
    for cand in MM_K_PREFS:
        if kc % cand == 0:
            tk = cand
            if out_bytes + 2 * cand * (tm * a.dtype.itemsize + tn * b.dtype.itemsize) <= MM_VMEM_BUDGET:
                break
    tk = tk or kc
    nk = kc // tk
    if mode == 'nn':
        a_spec = pl.BlockSpec((tm, tk), lambda i, j, k: (i, k))
        b_spec = pl.BlockSpec((tk, tn), lambda i, j, k: (k, j))
        dims = (((1,), (0,)), ((), ()))
    elif mode == 'nt':
        a_spec = pl.BlockSpec((tm, tk), lambda i, j, k: (i, k))
        b_spec = pl.BlockSpec((tn, tk), lambda i, j, k: (j, k))
        dims = (((1,), (1,)), ((), ()))
    else:
        a_spec = pl.BlockSpec((tk, tm), lambda i, j, k: (k, i))
        b_spec = pl.BlockSpec((tk, tn), lambda i, j, k: (k, j))
        dims = (((0,), (0,)), ((), ()))
    o_spec = pl.BlockSpec((tm, tn), lambda i, j, k: (i, j))

    def body(ins, outs, scr):
        a_ref, b_ref = ins[0], ins[1]
        (o_ref,), (acc_ref,) = outs, scr
        k = pl.program_id(2)

        @pl.when(k == 0)
        def _():
            acc_ref[...] = jnp.zeros_like(acc_ref) if add is None else ins[2][...]

        acc_ref[...] += lax.dot_general(a_ref[...].astype(BF), b_ref[...].astype(BF), dims,
                                        preferred_element_type=F32)

        @pl.when(k == nk - 1)
        def _():
            o_ref[...] = acc_ref[...].astype(o_ref.dtype)

    outs, exch = _call(
        body, name=name, grid=(mo // tm, no // tn, nk),
        in_specs=[a_spec, b_spec] + ([o_spec] if add is not None else []),
        out_specs=[o_spec], out_shape=[jax.ShapeDtypeStruct((mo, no), out_dtype)],
        args=[a, b] + ([add] if add is not None else []),
        scratch=[pltpu.VMEM((tm, tn), F32)], sem=("parallel", "parallel", "arbitrary"), jobs=jobs)
    return outs, exch


def mm(sch, a, b, name, mode='nn', out_dtype=F32, add=None):
    if mode == 'nn':
        flops = 2.0 * a.shape[0] * a.shape[1] * b.shape[1]
    elif mode == 'nt':
        flops = 2.0 * a.shape[0] * a.shape[1] * b.shape[0]
    else:
        flops = 2.0 * a.shape[0] * a.shape[1] * b.shape[1]
    return _ride(sch, flops / MXU_FLOPS_PER_MS, _mm_call, a=a, b=b, mode=mode, out_dtype=out_dtype, name=name,
                 add=add)[0]


def _full_spec(p):
    nd = p.ndim
    return pl.BlockSpec(p.shape, lambda i: (0,) * nd)


def _load_rows(refs, n_rows, halo, step_is_first):
    cur, prev, pos = [], [], 0
    for r in range(n_rows):
        cur.append(refs[pos][...].astype(F32))
        pos += 1
        if r in halo:
            keep = jnp.where(step_is_first, 0.0, 1.0).astype(F32)
            prev.append(refs[pos][...].astype(F32) * keep)
            pos += 1
        else:
            prev.append(None)
    return cur, prev, pos


def _join(cur, prev):
    return [c if p is None else jnp.concatenate([p, c], axis=0) for c, p in zip(cur, prev)]


def _rw_fwd(f, rows, params, outs, *, name, n_reduce=0, halo=(), tt=None, jobs=()):
    t_len = rows[0].shape[0]
    tt = tt or _pick(t_len, TT_PREFS)
    nt = t_len // tt
    n_rows, n_par, n_out = len(rows), len(params), len(outs)

    def body(ins, orefs, scr):
        i = pl.program_id(0)
        cur, prev, pos = _load_rows(ins, n_rows, halo, i == 0)
        pvals = [ins[pos + k][...] for k in range(n_par)]
        res = f(_join(cur, prev), pvals)
        n_f = len(res) - n_reduce
        for k, (_, _, src) in enumerate(outs):
            orefs[k][...] = res[src].astype(orefs[k].dtype)
        for k in range(n_reduce):
            ref, val = orefs[n_out + k], res[n_f + k]

            @pl.when(i == 0)
            def _():
                ref[...] = val

            @pl.when(i > 0)
            def _():
                ref[...] += val

    in_specs, args = [], []
    for r, x in enumerate(rows):
        c = x.shape[1]
        in_specs.append(pl.BlockSpec((tt, c), lambda i: (i, 0)))
        args.append(x)
        if r in halo:
            in_specs.append(pl.BlockSpec((tt, c), lambda i: (jnp.maximum(i - 1, 0), 0)))
            args.append(x)
    for p in params:
        in_specs.append(_full_spec(p))
        args.append(p)
    out_specs = [pl.BlockSpec((tt, c), lambda i: (i, 0)) for c, _, _ in outs]
    out_shape = [jax.ShapeDtypeStruct((t_len, c), dt) for c, dt, _ in outs]
    for _ in range(n_reduce):
        out_specs.append(pl.BlockSpec((1, 1), lambda i: (0, 0)))
        out_shape.append(jax.ShapeDtypeStruct((1, 1), F32))
    return _call(body, name=name, grid=(nt,), in_specs=in_specs, out_specs=out_specs, out_shape=out_shape,
                 args=args, sem=("arbitrary",), jobs=jobs)


def _rw_bwd(f, rows, params, cts, *, name, n_reduce=0, halo=(), nd_rows=(), nd_params=(), tt=None,
            row_dtypes=None, jobs=()):
    t_len = rows[0].shape[0]
    tt = tt or _pick(t_len, TT_PREFS)
    nt = t_len // tt
    n_rows, n_par, n_ct = len(rows), len(params), len(cts)
    d_rows = [r for r in range(n_rows) if r not in nd_rows]
    d_pars = [k for k in range(n_par) if k not in nd_params]
    h_rows = [r for r in d_rows if r in halo]

    def blk(j):
        return nt - 1 - j

    def body(ins, orefs, carry_refs):
        j = pl.program_id(0)
        cur, prev, pos = _load_rows(ins, n_rows, halo, blk(j) == 0)
        pvals = [ins[pos + k][...] for k in range(n_par)]
        pos += n_par
        ct_vals = [ins[pos + k][...].astype(F32) for k in range(n_ct)]

        def g(dcur, dprev, dpar):
            c, p, q = list(cur), list(prev), list(pvals)
            for r, v in zip(d_rows, dcur):
                c[r] = v
            for r, v in zip(h_rows, dprev):
                p[r] = v
            for k, v in zip(d_pars, dpar):
                q[k] = v
            return tuple(f(_join(c, p), q))

        _, vjp = jax.vjp(g, [cur[r] for r in d_rows], [prev[r] for r in h_rows], [pvals[k] for k in d_pars])
        g_cur, g_prev, g_par = vjp(tuple(ct_vals))

        for n, r in enumerate(d_rows):
            if r in halo:
                cref = carry_refs[h_rows.index(r)]

                @pl.when(j == 0)
                def _():
                    cref[...] = jnp.zeros_like(cref)

                orefs[n][...] = (g_cur[n] + cref[...]).astype(orefs[n].dtype)
                cref[...] = g_prev[h_rows.index(r)]
            else:
                orefs[n][...] = g_cur[n].astype(orefs[n].dtype)
        for n in range(len(d_pars)):
            ref, val = orefs[len(d_rows) + n], g_par[n]

            @pl.when(j == 0)
            def _():
                ref[...] = val

            @pl.when(j > 0)
            def _():
                ref[...] += val

    in_specs, args = [], []
    for r, x in enumerate(rows):
        c = x.shape[1]
        in_specs.append(pl.BlockSpec((tt, c), lambda j: (blk(j), 0)))
        args.append(x)
        if r in halo:
            in_specs.append(pl.BlockSpec((tt, c), lambda j: (jnp.maximum(blk(j) - 1, 0), 0)))
            args.append(x)
    for p in params:
        in_specs.append(_full_spec(p))
        args.append(p)
    for ct in cts:
        if ct.shape == (1, 1):
            in_specs.append(pl.BlockSpec((1, 1), lambda j: (0, 0)))
        else:
            in_specs.append(pl.BlockSpec((tt, ct.shape[1]), lambda j: (blk(j), 0)))
        args.append(ct)
    out_specs, out_shape, scratch = [], [], []
    for n, r in enumerate(d_rows):
        c = rows[r].shape[1]
        out_specs.append(pl.BlockSpec((tt, c), lambda j: (blk(j), 0)))
        out_shape.append(jax.ShapeDtypeStruct((t_len, c), row_dtypes[n] if row_dtypes else F32))
        if r in halo:
            scratch.append(pltpu.VMEM((tt, c), F32))
    for k in d_pars:
        out_specs.append(_full_spec(params[k]))
        out_shape.append(jax.ShapeDtypeStruct(params[k].shape, F32))
    return _call(body, name=name + '_bwd', grid=(nt,), in_specs=in_specs, out_specs=out_specs,
                 out_shape=out_shape, args=args, scratch=scratch, sem=("arbitrary",), jobs=jobs)


def rw_fwd(sch, budget, f, rows, params, outs, **kw):
    return _ride(sch, budget, functools.partial(_rw_fwd, f, list(rows), list(params), outs), **kw)


def rw_bwd(sch, budget, f, rows, params, cts, **kw):
    return _ride(sch, budget, functools.partial(_rw_bwd, f, list(rows), list(params), list(cts)), **kw)


def _scan_call(a, b, mul, *, reverse, name, jobs=()):
    t_len, c_len = a.shape
    tt = _pick(t_len, TT_PREFS)
    tc = _pick(c_len, (512, 256, 128))
    nt = t_len // tt

    def body(ins, orefs, scr):
        (carry,) = scr
        t = pl.program_id(1)
        av, bv = ins[0][...], ins[1][...]
        row = lax.broadcasted_iota(jnp.int32, av.shape, 0)
        s = 1
        while s < tt:
            if reverse:
                ok = row < tt - s
                a_sh = jnp.where(ok, pltpu.roll(av, tt - s, 0), 1.0)
                b_sh = jnp.where(ok, pltpu.roll(bv, tt - s, 0), 0.0)
            else:
                ok = row >= s
                a_sh = jnp.where(ok, pltpu.roll(av, s, 0), 1.0)
                b_sh = jnp.where(ok, pltpu.roll(bv, s, 0), 0.0)
            bv = av * b_sh + bv
            av = av * a_sh
            s *= 2

        @pl.when(t == 0)
        def _():
            carry[...] = jnp.zeros_like(carry)

        hs = bv + av * carry[...]
        orefs[0][...] = hs
        edge = 0 if reverse else tt - 1
        carry[...] = orefs[0][edge:edge + 1, :]
        if mul is not None:
            orefs[1][...] = hs * ins[2][...]

    def idx(c, t):
        return ((nt - 1 - t) if reverse else t, c)

    spec = pl.BlockSpec((tt, tc), idx)
    n_in, n_out = (2, 1) if mul is None else (3, 2)
    return _call(body, name=name, grid=(c_len // tc, nt), in_specs=[spec] * n_in, out_specs=[spec] * n_out,
                 out_shape=[jax.ShapeDtypeStruct((t_len, c_len), F32)] * n_out,
                 args=[a, b] if mul is None else [a, b, mul],
                 scratch=[pltpu.VMEM((1, tc), F32)], sem=("parallel", "arbitrary"), jobs=jobs)


NT = (((1,), (1,)), ((), ()))


def _for_blocks(lo, hi, fn, group=2):
    n = hi - lo

    def trip(p, c):
        for g in range(group):
            fn(lo + group * p + g)
        return c

    lax.fori_loop(0, n // group, trip, 0)
    for g in range(1, group):
        @pl.when(n % group >= g)
        def _():
            fn(hi - (n % group) + g - 1)


def _att_weights(s, rel, diag, lg, softmax, scale, t_axis):
    row = lax.broadcasted_iota(jnp.int32, s.shape, t_axis)
    col = lax.broadcasted_iota(jnp.int32, s.shape, 1 - t_axis)
    if softmax:
        s = s * scale
        return jnp.where(row >= col, s, -1e30) if diag else s
    diff = (rel + row - col).astype(F32)
    dec = jnp.exp(jnp.maximum(diff, 0.0) * lg)
    return jnp.where(diff >= 0.0, dec, 0.0) if diag else dec


def _att_fwd_call(q, k, v, lgt, *, heads, softmax, scale, name, jobs=()):
    t_len = q.shape[0]
    dqk, dv = q.shape[1] // heads, v.shape[1] // heads
    blk = _pick(t_len, ATT_PREFS)
    nb = t_len // blk

    lanes = 128
    n_fold = blk // lanes

    def body(ins, orefs, scr):
        q_ref, k_ref, v_ref, lg_ref = ins
        o_ref, lse_ref = orefs
        s_sc, m_sc, acc_sc = scr
        i = pl.program_id(1)
        lg = lg_ref[0:1, 0:1]
        qb = q_ref[...]
        acc_sc[...] = jnp.zeros_like(acc_sc)

        def block_rows(j):
            return pl.ds(pl.multiple_of(j * blk, blk), blk)

        if not softmax:
            def step(j, diag):
                s = lax.dot_general(qb, k_ref[block_rows(j), :], NT, preferred_element_type=F32)
                w = _att_weights(s, (i - j) * blk, diag, lg, softmax, scale, 0)
                acc_sc[...] += jnp.dot((s * w).astype(BF), v_ref[block_rows(j), :], preferred_element_type=F32)

            _for_blocks(0, i, lambda j: step(j, False), group=3)
            step(i, True)
            o_ref[...] = acc_sc[...]
            lse_ref[...] = jnp.zeros_like(lse_ref)
            return

        m_sc[...] = jnp.full_like(m_sc, -1e30)

        def score(j, diag):
            s = lax.dot_general(qb, k_ref[block_rows(j), :], NT, preferred_element_type=F32)
            w = _att_weights(s, (i - j) * blk, diag, lg, softmax, scale, 0)
            s_sc[j] = w
            top = w[:, :lanes]
            for n in range(1, n_fold):
                top = jnp.maximum(top, w[:, n * lanes:(n + 1) * lanes])
            m_sc[...] = jnp.maximum(m_sc[...], top)

        _for_blocks(0, i, lambda j: score(j, False), group=3)
        score(i, True)
        m = jnp.max(m_sc[...], axis=-1, keepdims=True)
        ones = jnp.ones((blk, lanes), BF)

        def accumulate(j):
            p = jnp.exp(s_sc[j] - m).astype(BF)
            v_aug = jnp.concatenate([v_ref[block_rows(j), :], ones], axis=1)
            acc_sc[...] += jnp.dot(p, v_aug, preferred_element_type=F32)

        _for_blocks(0, i + 1, accumulate, group=3)
        acc = acc_sc[...]
        l = acc[:, dv:dv + 1]
        o_ref[...] = acc[:, :dv] / l
        lse_ref[...] = m + jnp.log(l)

    scratch = [pltpu.VMEM((nb, blk, blk), F32), pltpu.VMEM((blk, lanes), F32),
               pltpu.VMEM((blk, dv + lanes if softmax else dv), F32)]
    return _call(
        body, name=name, grid=(heads, nb),
        in_specs=[
            pl.BlockSpec((blk, dqk), lambda h, i: (i, h)),
            pl.BlockSpec((t_len, dqk), lambda h, i: (0, h)),
            pl.BlockSpec((t_len, dv), lambda h, i: (0, h)),
            pl.BlockSpec((None, 1, 128), lambda h, i: (h, 0, 0)),
        ],
        out_specs=[
            pl.BlockSpec((blk, dv), lambda h, i: (i, h)),
            pl.BlockSpec((None, blk, 1), lambda h, i: (h, i, 0)),
        ],
        out_shape=[jax.ShapeDtypeStruct((t_len, heads * dv), F32), jax.ShapeDtypeStruct((heads, t_len, 1), F32)],
        args=[q, k, v, lgt],
        scratch=scratch if softmax else [pltpu.VMEM((8, lanes), F32), scratch[1], scratch[2]],
        sem=("parallel", "arbitrary"), jobs=jobs)


def _att_dq_call(q, k, v, lgt, o, lse, do, *, heads, softmax, scale, name, jobs=()):
    t_len = q.shape[0]
    dqk, dv = q.shape[1] // heads, v.shape[1] // heads
    blk = _pick(t_len, ATT_PREFS)
    nb = t_len // blk

    def body(ins, orefs, scr):
        q_ref, k_ref, v_ref, lg_ref, o_ref, do_ref, lse_ref = ins
        dq_ref, delta_ref = orefs
        (acc,) = scr
        i = pl.program_id(1)
        lg = lg_ref[0:1, 0:1]
        qb = q_ref[...]
        do = do_ref[...]
        dob = do.astype(BF)
        delta = jnp.sum(do * o_ref[...], axis=-1, keepdims=True)
        lse = lse_ref[...]
        delta_ref[...] = delta
        acc[...] = jnp.zeros_like(acc)

        def step(j, diag):
            rows = pl.ds(pl.multiple_of(j * blk, blk), blk)
            kb = k_ref[rows, :]
            s = lax.dot_general(qb, kb, NT, preferred_element_type=F32)
            w = _att_weights(s, (i - j) * blk, diag, lg, softmax, scale, 0)
            dp = lax.dot_general(dob, v_ref[rows, :], NT, preferred_element_type=F32)
            ds = jnp.exp(w - lse) * (dp - delta) * scale if softmax else dp * w
            acc[...] += jnp.dot(ds.astype(BF), kb, preferred_element_type=F32)

        _for_blocks(0, i, lambda j: step(j, False))
        step(i, True)
        dq_ref[...] = acc[...]

    return _call(
        body, name=name + '_dq', grid=(heads, nb),
        in_specs=[
            pl.BlockSpec((blk, dqk), lambda h, i: (i, h)),
            pl.BlockSpec((t_len, dqk), lambda h, i: (0, h)),
            pl.BlockSpec((t_len, dv), lambda h, i: (0, h)),
            pl.BlockSpec((None, 1, 128), lambda h, i: (h, 0, 0)),
            pl.BlockSpec((blk, dv), lambda h, i: (i, h)),
            pl.BlockSpec((blk, dv), lambda h, i: (i, h)),
            pl.BlockSpec((None, blk, 1), lambda h, i: (h, i, 0)),
        ],
        out_specs=[pl.BlockSpec((blk, dqk), lambda h, i: (i, h)),
                   pl.BlockSpec((None, blk, 1), lambda h, i: (h, i, 0))],
        out_shape=[jax.ShapeDtypeStruct(q.shape, F32), jax.ShapeDtypeStruct((heads, t_len, 1), F32)],
        args=[q, k, v, lgt, o, do, lse],
        scratch=[pltpu.VMEM((blk, dqk), F32)], sem=("parallel", "arbitrary"), jobs=jobs)


def _att_dkv_call(q, k, v, lgt, lse_row, delta_row, do, *, heads, softmax, scale, name, jobs=()):
    t_len = q.shape[0]
    dqk, dv = q.shape[1] // heads, v.shape[1] // heads
    blk = _pick(t_len, ATT_PREFS)
    nb = t_len // blk

    def body(ins, orefs, scr):
        q_ref, k_ref, v_ref, lg_ref, do_ref, lse_ref, delta_ref = ins
        dk_acc, dv_acc = scr
        j = pl.program_id(1)
        lg = lg_ref[0:1, 0:1]
        kb, vb = k_ref[...], v_ref[...]
        dk_acc[...] = jnp.zeros_like(dk_acc)
        dv_acc[...] = jnp.zeros_like(dv_acc)

        def step(i, diag):
            rows = pl.ds(pl.multiple_of(i * blk, blk), blk)
            qb = q_ref[rows, :]
            dob = do_ref[rows, :].astype(BF)
            s = lax.dot_general(kb, qb, NT, preferred_element_type=F32)
            w = _att_weights(s, (i - j) * blk, diag, lg, softmax, scale, 1)
            dp = lax.dot_general(vb, dob, NT, preferred_element_type=F32)
            if softmax:
                p = jnp.exp(w - lse_ref[:, rows])
                ds = p * (dp - delta_ref[:, rows]) * scale
            else:
                p, ds = s * w, dp * w
            dv_acc[...] += jnp.dot(p.astype(BF), dob, preferred_element_type=F32)
            dk_acc[...] += jnp.dot(ds.astype(BF), qb, preferred_element_type=F32)

        step(j, True)
        _for_blocks(j + 1, nb, lambda i: step(i, False), group=3)
        orefs[0][...] = dk_acc[...]
        orefs[1][...] = dv_acc[...]

    return _call(
        body, name=name + '_dkv', grid=(heads, nb),
        in_specs=[
            pl.BlockSpec((t_len, dqk), lambda h, j: (0, h)),
            pl.BlockSpec((blk, dqk), lambda h, j: (j, h)),
            pl.BlockSpec((blk, dv), lambda h, j: (j, h)),
            pl.BlockSpec((None, 1, 128), lambda h, j: (h, 0, 0)),
            pl.BlockSpec((t_len, dv), lambda h, j: (0, h)),
            pl.BlockSpec((None, 1, t_len), lambda h, j: (h, 0, 0)),
            pl.BlockSpec((None, 1, t_len), lambda h, j: (h, 0, 0)),
        ],
        out_specs=[
            pl.BlockSpec((blk, dqk), lambda h, j: (j, h)),
            pl.BlockSpec((blk, dv), lambda h, j: (j, h)),
        ],
        out_shape=[jax.ShapeDtypeStruct(k.shape, F32), jax.ShapeDtypeStruct(v.shape, F32)],
        args=[q, k, v, lgt, do, lse_row, delta_row],
        scratch=[pltpu.VMEM((blk, dqk), F32), pltpu.VMEM((blk, dv), F32)],
        sem=("parallel", "arbitrary"), jobs=jobs)


def _adamw_call(contribs, w, m, v, *, name, jobs=()):
    r_len, c_len = w.shape
    n_chunk = len(contribs)
    r_chunk = r_len // n_chunk
    cap = max(min(ADAM_BLOCK_ELEMS, 6 * ADAM_BLOCK_ELEMS // n_chunk) // c_len, 1)
    tr = r_chunk
    for cand in (512, 256, 128, 64, 32, 16):
        if cand <= cap and r_chunk % cand == 0:
            tr = cand
            break
    per = r_chunk // tr

    def body(ins, orefs, scr):
        w_ref, m_ref, v_ref = ins[n_chunk:]
        g_ref, d_ref, mo_ref, vo_ref = orefs
        i = pl.program_id(0)

        def update(c_ref):
            g = c_ref[0].astype(F32)
            for n in range(1, N_DEV):
                g = g + c_ref[n].astype(F32)
            m_new = ADAM_B1 * m_ref[...] + (1.0 - ADAM_B1) * g
            v_new = ADAM_B2 * v_ref[...] + (1.0 - ADAM_B2) * jnp.square(g)
            m_hat = m_new / (1.0 - ADAM_B1 ** ADAM_STEP)
            v_hat = v_new / (1.0 - ADAM_B2 ** ADAM_STEP)
            g_ref[...] = g
            d_ref[...] = -ADAM_LR * (m_hat / (jnp.sqrt(v_hat) + ADAM_EPS) + ADAM_WD * w_ref[...])
            mo_ref[...] = m_new
            vo_ref[...] = v_new

        if n_chunk == 1:
            update(ins[0])
        else:
            for n in range(n_chunk):
                @pl.when(i // per == n)
                def _():
                    update(ins[n])

    spec = pl.BlockSpec((tr, c_len), lambda i: (i, 0))
    c_specs = [pl.BlockSpec((N_DEV, tr, c_len), functools.partial(
        lambda i, n: (0, jnp.clip(i - n * per, 0, per - 1), 0), n=n)) for n in range(n_chunk)]
    return _call(body, name=name, grid=(r_len // tr,), in_specs=c_specs + [spec, spec, spec],
                 out_specs=[spec] * 4, out_shape=[jax.ShapeDtypeStruct((r_len, c_len), F32)] * 4,
                 args=list(contribs) + [w, m, v], sem=("arbitrary",), jobs=jobs)


def _rope_lanes(x, cc, s_lo, s_hi):
    return x * cc + _roll(x, 32, 1) * s_lo + _roll(x, 96, 1) * s_hi


def _f_lru(tt, branch):
    lb = branch // LRU_HEADS

    def f(rows, params):
        (xcat,) = rows
        cw, cb, wa, ba, wx, bx, lam = params
        conv = cb
        for j in range(CONV_W):
            sh = CONV_W - 1 - j
            xs = xcat if sh == 0 else _roll(xcat, sh, 0)
            conv = conv + cw[j:j + 1, :] * xs[tt:, :]
        rs, gs = [], []
        for h in range(LRU_HEADS):
            ub = conv[:, h * lb:(h + 1) * lb]
            rs.append(_bdot(ub, wa[h]))
            gs.append(_bdot(ub, wx[h]))
        r = jax.nn.sigmoid(jnp.concatenate(rs, axis=-1) + ba)
        gate = jax.nn.sigmoid(jnp.concatenate(gs, axis=-1) + bx)
        log_a = LRU_C * r * (-_softplus(-lam))
        a = jnp.exp(log_a)
        one_minus_a2 = -jnp.tanh(log_a) * (jnp.exp(2.0 * log_a) + 1.0)
        return a, (conv * gate) * jnp.sqrt(one_minus_a2)

    return f


def _f_gate(rows, params):
    hs, g = rows
    return (hs * _silu(g),)


def _f_ln(rows, params):
    h, br = rows
    g, b = params
    pre = ALPHA * h + br
    mu = jnp.mean(pre, axis=-1, keepdims=True)
    var = jnp.mean(jnp.square(pre - mu), axis=-1, keepdims=True)
    return ((pre - mu) * lax.rsqrt(var + LN_EPS) * g + b,)


def _f_pool(tt, branch):
    grp = branch // len(POOL_WINDOWS)

    def f(rows, params):
        xcat, tidx = rows
        sums, acc, w = [], xcat, 1
        while w < POOL_WINDOWS[-1]:
            acc = acc + _roll(acc, w, 0)
            w *= 2
            sums.append(acc[tt:, :])
        u = xcat[tt:, :]
        outs = []
        for gi, w in enumerate(POOL_WINDOWS):
            sl = slice(gi * grp, (gi + 1) * grp)
            outs.append(sums[gi][:, sl] / jnp.minimum(tidx + 1.0, float(w)) - u[:, sl])
        return tuple(outs)

    return f


def _f_gate_pool(rows, params):
    m0, m1, m2, m3, g = rows
    (scale,) = params
    return (jnp.concatenate([m0, m1, m2, m3], axis=-1) * scale * _silu(g),)


def _rms(x, g):
    return x * lax.rsqrt(jnp.mean(jnp.square(x), axis=-1, keepdims=True) + RMS_EPS) * g


def _f_mla_pre(rows, params):
    c, cc, s_lo, s_hi = rows
    qn, kvn = params
    cq = c[:, :Q_LORA]
    ckv = c[:, Q_LORA:Q_LORA + KV_LORA]
    kr = c[:, Q_LORA + KV_LORA:]
    return _rms(cq, qn), _rms(ckv, kvn), _rope_lanes(kr, cc, s_lo, s_hi)


def _f_rope_q(rows, params):
    qc, cc, s_lo, s_hi = rows
    out = []
    for h in range(MLA_HEADS):
        out.append(qc[:, h * MLA_QK:h * MLA_QK + MLA_NOPE])
        out.append(_rope_lanes(qc[:, h * MLA_QK + MLA_NOPE:(h + 1) * MLA_QK], cc, s_lo, s_hi))
    return (jnp.concatenate(out, axis=-1),)


def _f_kcat(dv):
    per = MLA_NOPE + dv

    def f(rows, params):
        kv, krr = rows
        ks, vs = [], []
        for h in range(MLA_HEADS):
            ks.append(kv[:, h * per:h * per + MLA_NOPE])
            ks.append(krr)
            vs.append(kv[:, h * per + MLA_NOPE:(h + 1) * per])
        return jnp.concatenate(ks, axis=-1), jnp.concatenate(vs, axis=-1)

    return f


def _f_rope_ret(dk):
    half = dk // 2

    def f(rows, params):
        q, k, cos, sin = rows
        qs, ks = [], []
        for h in range(RET_HEADS):
            for src, dst, mult in ((q, qs, 1.0), (k, ks, dk ** -0.5)):
                x1 = src[:, h * dk:h * dk + half]
                x2 = src[:, h * dk + half:(h + 1) * dk]
                dst.append((x1 * cos - x2 * sin) * mult)
                dst.append((x2 * cos + x1 * sin) * mult)
        return jnp.concatenate(qs, axis=-1), jnp.concatenate(ks, axis=-1)

    return f


def _f_gate_gn(dv):
    def f(rows, params):
        o, g = rows
        out = []
        for h in range(RET_HEADS):
            oh = o[:, h * dv:(h + 1) * dv]
            mu = jnp.mean(oh, axis=-1, keepdims=True)
            var = jnp.mean(jnp.square(oh - mu), axis=-1, keepdims=True)
            out.append((oh - mu) * lax.rsqrt(var + LN_EPS))
        return (jnp.concatenate(out, axis=-1) * _silu(g),)

    return f


def _f_loss(rows, params):
    h, tgt, mask = rows
    per_row = jnp.mean(jnp.square(h - tgt), axis=-1, keepdims=True) * mask
    return (0.5 * jnp.sum(per_row, axis=0, keepdims=True),)


def _cat(parts, axis=1):
    return parts[0] if len(parts) == 1 else jnp.concatenate(parts, axis=axis)


def _cols(g):
    return jnp.transpose(g, (1, 0, 2)).reshape(g.shape[1], -1)


def _uncols(w):
    k, n = w.shape
    return jnp.transpose(w.reshape(k, N_DEV, n // N_DEV), (1, 0, 2))


def _uncols_parts(parts):
    nc = sum(p.shape[1] for p in parts) // N_DEV
    return jnp.concatenate(
        [jnp.transpose(p.reshape(p.shape[0], p.shape[1] // nc, nc), (1, 0, 2)) for p in parts], axis=0)


def _heads(g):
    return jnp.transpose(g, (1, 0, 2, 3)).reshape(g.shape[1], -1, g.shape[3])


def _unheads(w):
    h, r, c = w.shape
    return jnp.transpose(w.reshape(h, N_DEV, r // N_DEV, c), (1, 0, 2, 3))


def _rope_tables(t_pad, d):
    inv = ROPE_BASE ** (-jnp.arange(0, d, 2, dtype=F32) / d)
    ang = jnp.arange(t_pad, dtype=F32)[:, None] * inv[None, :]
    return jnp.cos(ang), jnp.sin(ang)


def _row2(v):
    return v.reshape(1, -1)


def _train_local(sch, x2d, tgt_pad, S, *, t_pad):
    seq, d_model = x2d.shape
    branch = d_model
    t_real = N_META + seq
    tt = _pick(t_pad, TT_PREFS)
    n_win = len(POOL_WINDOWS)
    grp = branch // n_win
    lb = branch // LRU_HEADS
    dv2 = branch // MLA_HEADS
    dk3 = branch // RET_HEADS
    gS = {}
    RW, RWB = 0.06, 0.1

    def ln_fwd(h, br, layer):
        h1, hb1 = rw_fwd(sch, RW, _f_ln, [h, br], [_row2(S[f'l{layer}_ln_g']), _row2(S[f'l{layer}_ln_b'])],
                         [(d_model, F32, 0), (d_model, BF, 0)], name=f'l{layer}_ln')
        return h1, hb1

    def ln_bwd(h, br, dh1, layer):
        dh, dbr, dg, db = rw_bwd(sch, RWB, _f_ln, [h, br],
                                 [_row2(S[f'l{layer}_ln_g']), _row2(S[f'l{layer}_ln_b'])], [dh1],
                                 name=f'l{layer}_ln', row_dtypes=[F32, BF])
        gS[f'l{layer}_ln_g'], gS[f'l{layer}_ln_b'] = dg.reshape(-1), db.reshape(-1)
        return dh, dbr

    tidx = jnp.arange(t_pad, dtype=F32)[:, None]
    rowmask = ((tidx >= N_META) & (tidx < t_real)).astype(F32)

    meta = _cols(_cat(sch.get('meta_tokens')))
    h0 = jnp.concatenate([meta, x2d, jnp.zeros((t_pad - t_real, d_model), F32)], axis=0)
    hb0 = h0.astype(BF)

    w0_in = _cols(_cat(sch.get('l0_w_in')))
    w0_u, w0_g = w0_in[:, :branch], w0_in[:, branch:]
    u0 = mm(sch, hb0, w0_u, 'l0_in_u')
    g0 = mm(sch, hb0, w0_g, 'l0_in_g')
    conv_w = jnp.transpose(_cat(sch.get('l0_conv_w')), (1, 2, 0, 3)).reshape(CONV_W, branch)
    w_a = _heads(_cat(sch.get('l0_w_a'), axis=2))
    w_x = _heads(_cat(sch.get('l0_w_x'), axis=2))
    p0 = [conv_w, _row2(S['l0_conv_b']), w_a, _row2(S['l0_b_a']), w_x, _row2(S['l0_b_x']), _row2(S['l0_lam'])]
    f_lru = _f_lru(tt, branch)
    a0, xin0 = rw_fwd(sch, 0.13, f_lru, [u0], p0, [(branch, F32, 0), (branch, F32, 1)], name='l0_lru',
                      halo=(0,), tt=tt)
    hs0 = _ride(sch, 0.22, _scan_call, a=a0, b=xin0, mul=None, reverse=False, name='l0_scan')[0]
    (z0,) = rw_fwd(sch, RW, _f_gate, [hs0, g0], [], [(branch, BF, 0)], name='l0_gate')
    w0_out = _cat(sch.get('l0_w_out')).reshape(branch, d_model)
    br0 = mm(sch, z0, w0_out, 'l0_out')
    h1, hb1 = ln_fwd(h0, br0, 0)

    w1_in = _cols(_cat(sch.get('l1_w_in')))
    w1_u, w1_g = w1_in[:, :branch], w1_in[:, branch:]
    u1 = mm(sch, hb1, w1_u, 'l1_in_u')
    g1 = mm(sch, hb1, w1_g, 'l1_in_g')
    f_pool = _f_pool(tt, branch)
    ps1 = rw_fwd(sch, RW, f_pool, [u1, tidx], [], [(grp, BF, gi) for gi in range(n_win)], name='l1_pool',
                 halo=(0,), tt=tt)
    w1_grp = _heads(_cat(sch.get('l1_w_grp'), axis=2))
    mixed1 = [mm(sch, ps1[gi], w1_grp[gi], f'l1_grp{gi}') for gi in range(n_win)]
    p1 = [_row2(S['l1_scale'])]
    (z1,) = rw_fwd(sch, RW, _f_gate_pool, mixed1 + [g1], p1, [(branch, BF, 0)], name='l1_gate')
    w1_out = _cat(sch.get('l1_w_out')).reshape(branch, d_model)
    br1 = mm(sch, z1, w1_out, 'l1_out')
    h2, hb2 = ln_fwd(h1, br1, 1)

    w2_in = _cols(_cat(sch.get('l2_w_in')))
    w2_g = w2_in[:, :branch]
    w2_lat = jnp.pad(w2_in[:, branch:], ((0, 0), (0, 128 - MLA_ROPE)))
    g2 = mm(sch, hb2, w2_g, 'l2_in_g')
    c2 = mm(sch, hb2, w2_lat, 'l2_in_c')
    cos, sin = _rope_tables(t_pad, MLA_ROPE)
    zz = jnp.zeros_like(cos)
    tabs = [jnp.concatenate([cos, cos, zz, zz], axis=-1), jnp.concatenate([zz, sin, zz, zz], axis=-1),
            jnp.concatenate([-sin, zz, zz, zz], axis=-1)]
    p2 = [_row2(S['l2_q_norm']), _row2(S['l2_kv_norm'])]
    cqn2, ckvn2, krr2 = rw_fwd(sch, 0.04, _f_mla_pre, [c2] + tabs, p2,
                               [(Q_LORA, BF, 0), (KV_LORA, BF, 1), (128, F32, 2)], name='l2_pre')
    w2_uq = _cols(_cat(sch.get('l2_w_uq'))).reshape(Q_LORA, MLA_HEADS, MLA_NOPE + MLA_ROPE)
    w2_uq = jnp.pad(w2_uq, ((0, 0), (0, 0), (0, MLA_QK - MLA_NOPE - MLA_ROPE))).reshape(Q_LORA, MLA_HEADS * MLA_QK)
    w2_ukv = _cols(_cat(sch.get('l2_w_ukv')))
    qc2 = mm(sch, cqn2, w2_uq, 'l2_uq')
    kv2 = mm(sch, ckvn2, w2_ukv, 'l2_ukv')
    (qcr2,) = rw_fwd(sch, 0.09, _f_rope_q, [qc2] + tabs, [], [(MLA_HEADS * MLA_QK, BF, 0)], name='l2_rope_q')
    f_kcat = _f_kcat(dv2)
    kcat2, v2 = rw_fwd(sch, 0.1, f_kcat, [kv2, krr2], [], [(MLA_HEADS * MLA_QK, BF, 0), (branch, BF, 1)],
                       name='l2_kcat')
    no_decay = jnp.zeros((MLA_HEADS, 1, 128), F32)
    att2 = dict(heads=MLA_HEADS, softmax=True, scale=(MLA_NOPE + MLA_ROPE) ** -0.5, name='l2_att')
    o2, lse2 = _ride(sch, 1.3, _att_fwd_call, q=qcr2, k=kcat2, v=v2, lgt=no_decay, **att2)
    (z2,) = rw_fwd(sch, RW, _f_gate, [o2, g2], [], [(branch, BF, 0)], name='l2_gate')
    w2_out = _cat(sch.get('l2_w_out')).reshape(branch, d_model)
    br2 = mm(sch, z2, w2_out, 'l2_out')
    h3, hb3 = ln_fwd(h2, br2, 2)

    w3_in = _cols(_cat(sch.get('l3_w_in')))
    w3 = [w3_in[:, n * branch:(n + 1) * branch] for n in range(4)]
    q3 = mm(sch, hb3, w3[0], 'l3_in_q')
    k3 = mm(sch, hb3, w3[1], 'l3_in_k')
    v3 = mm(sch, hb3, w3[2], 'l3_in_v', out_dtype=BF)
    g3 = mm(sch, hb3, w3[3], 'l3_in_g')
    cs3 = list(_rope_tables(t_pad, dk3))
    f_rope3 = _f_rope_ret(dk3)
    qr3, kr3 = rw_fwd(sch, 0.09, f_rope3, [q3, k3] + cs3, [], [(branch, BF, 0), (branch, BF, 1)], name='l3_rope')
    log_g = jnp.log(1.0 - 2.0 ** (-5.0 - jnp.arange(RET_HEADS, dtype=F32)))
    lgt = jnp.broadcast_to(log_g[:, None, None], (RET_HEADS, 1, 128))
    att3 = dict(heads=RET_HEADS, softmax=False, scale=1.0, name='l3_ret')
    o3, lse3 = _ride(sch, 0.6, _att_fwd_call, q=qr3, k=kr3, v=v3, lgt=lgt, **att3)
    f_gn = _f_gate_gn(dk3)
    (z3,) = rw_fwd(sch, 0.08, f_gn, [o3, g3], [], [(branch, BF, 0)], name='l3_gate')
    w3_out = _cat(sch.get('l3_w_out')).reshape(branch, d_model)
    br3 = mm(sch, z3, w3_out, 'l3_out')
    (h4,) = rw_fwd(sch, RW, _f_ln, [h3, br3], [_row2(S['l3_ln_g']), _row2(S['l3_ln_b'])], [(d_model, F32, 0)],
                   name='l3_ln')

    (loss,) = rw_fwd(sch, 0.05, _f_loss, [h4, tgt_pad, rowmask], [], [], name='loss', n_reduce=1)

    sch.scale = BWD_RIDER_SHARE
    (dh4,) = rw_bwd(sch, 0.07, _f_loss, [h4, tgt_pad, rowmask], [], [jnp.ones((1, 1), F32)], name='loss',
                    n_reduce=1, nd_rows=(1, 2))

    dh3, dbr3 = ln_bwd(h3, br3, dh4, 3)
    dz3 = mm(sch, dbr3, w3_out, 'l3_out_dx', mode='nt')
    sch.push('l3_w_out', mm(sch, z3.T, dbr3, 'l3_out_dw', mode='nn', out_dtype=BF).reshape(N_DEV, -1, d_model),
             False, 1)
    do3, dg3 = rw_bwd(sch, 0.13, f_gn, [o3, g3], [], [dz3], name='l3_gate', row_dtypes=[F32, BF])
    dqr3, delta3 = _ride(sch, 0.7, _att_dq_call, q=qr3, k=kr3, v=v3, lgt=lgt, o=o3, lse=lse3, do=do3, **att3)
    dkr3, dv3 = _ride(sch, 0.8, _att_dkv_call, q=qr3, k=kr3, v=v3, lgt=lgt, lse_row=lse3.reshape(RET_HEADS, 1, -1),
                      delta_row=delta3.reshape(RET_HEADS, 1, -1), do=do3, **att3)
    dq3, dk3_ = rw_bwd(sch, 0.13, f_rope3, [q3, k3] + cs3, [], [dqr3, dkr3], name='l3_rope', nd_rows=(2, 3),
                       row_dtypes=[BF, BF])
    d3 = [dq3, dk3_, dv3, dg3]
    for n in range(4):
        dh3 = mm(sch, d3[n], w3[n], f'l3_in_dx{n}', mode='nt', add=dh3)
    hb3_t = hb3.T
    dw3 = [mm(sch, hb3_t, d3[n], f'l3_in_dw{n}', mode='nn', out_dtype=BF) for n in range(4)]
    sch.push('l3_w_in', _uncols_parts(dw3), False, 1)

    dh2, dbr2 = ln_bwd(h2, br2, dh3, 2)
    dz2 = mm(sch, dbr2, w2_out, 'l2_out_dx', mode='nt')
    sch.push('l2_w_out', mm(sch, z2.T, dbr2, 'l2_out_dw', mode='nn', out_dtype=BF).reshape(N_DEV, -1, d_model),
             False, 1)
    do2, dg2 = rw_bwd(sch, 0.1, _f_gate, [o2, g2], [], [dz2], name='l2_gate', row_dtypes=[F32, BF])
    dqcr2, delta2 = _ride(sch, 1.3, _att_dq_call, q=qcr2, k=kcat2, v=v2, lgt=no_decay, o=o2, lse=lse2, do=do2,
                          **att2)
    dkcat2, dv2_ = _ride(sch, 1.5, _att_dkv_call, q=qcr2, k=kcat2, v=v2, lgt=no_decay,
                         lse_row=lse2.reshape(MLA_HEADS, 1, -1), delta_row=delta2.reshape(MLA_HEADS, 1, -1),
                         do=do2, **att2)
    dkv2, dkrr2 = rw_bwd(sch, 0.14, f_kcat, [kv2, krr2], [], [dkcat2, dv2_], name='l2_kcat', row_dtypes=[BF, F32])
    (dqc2,) = rw_bwd(sch, 0.13, _f_rope_q, [qc2] + tabs, [], [dqcr2], name='l2_rope_q', nd_rows=(1, 2, 3),
                     row_dtypes=[BF])
    dckvn2 = mm(sch, dkv2, w2_ukv, 'l2_ukv_dx', mode='nt')
    dcqn2 = mm(sch, dqc2, w2_uq, 'l2_uq_dx', mode='nt')
    sch.push('l2_w_ukv', _uncols(mm(sch, ckvn2, dkv2, 'l2_ukv_dw', mode='tn', out_dtype=BF)), False, 1)
    dw_uq = mm(sch, cqn2, dqc2, 'l2_uq_dw', mode='tn', out_dtype=BF)
    dw_uq = dw_uq.reshape(Q_LORA, MLA_HEADS, MLA_QK)[:, :, :MLA_NOPE + MLA_ROPE].reshape(Q_LORA, -1)
    sch.push('l2_w_uq', _uncols(dw_uq), False, 1)
    dc2, dqn, dkvn = rw_bwd(sch, 0.05, _f_mla_pre, [c2] + tabs, p2, [dcqn2, dckvn2, dkrr2], name='l2_pre',
                            nd_rows=(1, 2, 3), row_dtypes=[BF])
    gS['l2_q_norm'], gS['l2_kv_norm'] = dqn.reshape(-1), dkvn.reshape(-1)
    dh2 = mm(sch, dg2, w2_g, 'l2_in_g_dx', mode='nt', add=dh2)
    dh2 = mm(sch, dc2, w2_lat, 'l2_in_c_dx', mode='nt', add=dh2)
    hb2_t = hb2.T
    dw2_g = mm(sch, hb2_t, dg2, 'l2_in_g_dw', mode='nn', out_dtype=BF)
    dw2_lat = mm(sch, hb2_t, dc2, 'l2_in_c_dw', mode='nn', out_dtype=BF)
    n_lat = Q_LORA + KV_LORA + MLA_ROPE
    sch.push('l2_w_in', _uncols(jnp.concatenate([dw2_g, dw2_lat[:, :n_lat]], axis=1)), False, 1)

    dh1, dbr1 = ln_bwd(h1, br1, dh2, 1)
    dz1 = mm(sch, dbr1, w1_out, 'l1_out_dx', mode='nt')
    sch.push('l1_w_out', mm(sch, z1.T, dbr1, 'l1_out_dw', mode='nn', out_dtype=BF).reshape(N_DEV, -1, d_model),
             False, 1)
    res = rw_bwd(sch, 0.11, _f_gate_pool, mixed1 + [g1], p1, [dz1], name='l1_gate', row_dtypes=[BF] * (n_win + 1))
    dmixed1, dg1, dscale = res[:n_win], res[n_win], res[n_win + 1]
    gS['l1_scale'] = dscale.reshape(-1)
    dps1 = [mm(sch, dmixed1[gi], w1_grp[gi], f'l1_grp{gi}_dx', mode='nt') for gi in range(n_win)]
    dw_grp = jnp.stack([mm(sch, ps1[gi], dmixed1[gi], f'l1_grp{gi}_dw', mode='tn', out_dtype=BF)
                        for gi in range(n_win)])
    sch.push('l1_w_grp', _unheads(dw_grp), False, 2)
    (du1,) = rw_bwd(sch, 0.09, f_pool, [u1, tidx], [], dps1, name='l1_pool', halo=(0,), nd_rows=(1,), tt=tt,
                    row_dtypes=[BF])
    dh1 = mm(sch, du1, w1_u, 'l1_in_u_dx', mode='nt', add=dh1)
    dh1 = mm(sch, dg1, w1_g, 'l1_in_g_dx', mode='nt', add=dh1)
    hb1_t = hb1.T
    dw1 = [mm(sch, hb1_t, du1, 'l1_in_u_dw', mode='nn', out_dtype=BF),
           mm(sch, hb1_t, dg1, 'l1_in_g_dw', mode='nn', out_dtype=BF)]
    sch.push('l1_w_in', _uncols_parts(dw1), False, 1)

    dh0, dbr0 = ln_bwd(h0, br0, dh1, 0)
    dz0 = mm(sch, dbr0, w0_out, 'l0_out_dx', mode='nt')
    sch.push('l0_w_out', mm(sch, z0.T, dbr0, 'l0_out_dw', mode='nn', out_dtype=BF).reshape(N_DEV, -1, d_model),
             False, 1)
    dhs0, dg0 = rw_bwd(sch, 0.1, _f_gate, [hs0, g0], [], [dz0], name='l0_gate', row_dtypes=[F32, BF])
    a_next = jnp.concatenate([a0[1:], jnp.ones_like(a0[:1])], axis=0)
    hs_prev = jnp.concatenate([jnp.zeros_like(hs0[:1]), hs0[:-1]], axis=0)
    dxin0, da0 = _ride(sch, 0.25, _scan_call, a=a_next, b=dhs0, mul=hs_prev, reverse=True, name='l0_scan_bwd')
    res = rw_bwd(sch, 0.3, f_lru, [u0], p0, [da0, dxin0], name='l0_lru', halo=(0,), tt=tt, row_dtypes=[BF])
    du0 = res[0]
    gS['l0_conv_b'], gS['l0_b_a'], gS['l0_b_x'], gS['l0_lam'] = [res[k].reshape(-1) for k in (2, 4, 6, 7)]
    sch.push('l0_w_a', _unheads(res[3]), False, 2)
    sch.push('l0_w_x', _unheads(res[5]), False, 2)
    sch.push('l0_conv_w', jnp.transpose(res[1].reshape(CONV_W, 1, N_DEV, -1), (2, 0, 1, 3)), False, 3)
    hb0_t = hb0.T
    dw0 = [mm(sch, hb0_t, du0, 'l0_in_u_dw', mode='nn', out_dtype=BF),
           mm(sch, hb0_t, dg0, 'l0_in_g_dw', mode='nn', out_dtype=BF)]
    sch.push('l0_w_in', _uncols_parts(dw0), False, 1, chunk_ms=EXCH_CHUNK_MS)
    sch.scale = 1.0
    dh0 = mm(sch, du0, w0_u, 'l0_in_u_dx', mode='nt', add=dh0)
    dh0 = mm(sch, dg0, w0_g, 'l0_in_g_dx', mode='nt', add=dh0)
    sch.push('meta_tokens', _uncols(dh0[:N_META]), False, 1)

    return loss[0, 0], dh0[N_META:t_real], gS


def _as2d(a):
    return a.reshape(-1, a.shape[-1])


def kernel(x, meta_tokens, l0_w_in, l0_conv_w, l0_conv_b, l0_w_a, l0_b_a, l0_w_x, l0_b_x, l0_lam, l0_w_out, l0_ln_g, l0_ln_b, l1_w_in, l1_w_grp, l1_scale, l1_w_out, l1_ln_g, l1_ln_b, l2_w_in, l2_q_norm, l2_w_uq, l2_kv_norm, l2_w_ukv, l2_w_out, l2_ln_g, l2_ln_b, l3_w_in, l3_w_out, l3_ln_g, l3_ln_b, loss_target, m_meta_tokens, m_l0_w_in, m_l0_conv_w, m_l0_conv_b, m_l0_w_a, m_l0_b_a, m_l0_w_x, m_l0_b_x, m_l0_lam, m_l0_w_out, m_l0_ln_g, m_l0_ln_b, m_l1_w_in, m_l1_w_grp, m_l1_scale, m_l1_w_out, m_l1_ln_g, m_l1_ln_b, m_l2_w_in, m_l2_q_norm, m_l2_w_uq, m_l2_kv_norm, m_l2_w_ukv, m_l2_w_out, m_l2_ln_g, m_l2_ln_b, m_l3_w_in, m_l3_w_out, m_l3_ln_g, m_l3_ln_b, v_meta_tokens, v_l0_w_in, v_l0_conv_w, v_l0_conv_b, v_l0_w_a, v_l0_b_a, v_l0_w_x, v_l0_b_x, v_l0_lam, v_l0_w_out, v_l0_ln_g, v_l0_ln_b, v_l1_w_in, v_l1_w_grp, v_l1_scale, v_l1_w_out, v_l1_ln_g, v_l1_ln_b, v_l2_w_in, v_l2_q_norm, v_l2_w_uq, v_l2_kv_norm, v_l2_w_ukv, v_l2_w_out, v_l2_ln_g, v_l2_ln_b, v_l3_w_in, v_l3_w_out, v_l3_ln_g, v_l3_ln_b):
    args = (meta_tokens, l0_w_in, l0_conv_w, l0_conv_b, l0_w_a, l0_b_a, l0_w_x, l0_b_x, l0_lam, l0_w_out, l0_ln_g, l0_ln_b, l1_w_in, l1_w_grp, l1_scale, l1_w_out, l1_ln_g, l1_ln_b, l2_w_in, l2_q_norm, l2_w_uq, l2_kv_norm, l2_w_ukv, l2_w_out, l2_ln_g, l2_ln_b, l3_w_in, l3_w_out, l3_ln_g, l3_ln_b)
    moms = (m_meta_tokens, m_l0_w_in, m_l0_conv_w, m_l0_conv_b, m_l0_w_a, m_l0_b_a, m_l0_w_x, m_l0_b_x, m_l0_lam, m_l0_w_out, m_l0_ln_g, m_l0_ln_b, m_l1_w_in, m_l1_w_grp, m_l1_scale, m_l1_w_out, m_l1_ln_g, m_l1_ln_b, m_l2_w_in, m_l2_q_norm, m_l2_w_uq, m_l2_kv_norm, m_l2_w_ukv, m_l2_w_out, m_l2_ln_g, m_l2_ln_b, m_l3_w_in, m_l3_w_out, m_l3_ln_g, m_l3_ln_b)
    vels = (v_meta_tokens, v_l0_w_in, v_l0_conv_w, v_l0_conv_b, v_l0_w_a, v_l0_b_a, v_l0_w_x, v_l0_b_x, v_l0_lam, v_l0_w_out, v_l0_ln_g, v_l0_ln_b, v_l1_w_in, v_l1_w_grp, v_l1_scale, v_l1_w_out, v_l1_ln_g, v_l1_ln_b, v_l2_w_in, v_l2_q_norm, v_l2_w_uq, v_l2_kv_norm, v_l2_w_ukv, v_l2_w_out, v_l2_ln_g, v_l2_ln_b, v_l3_w_in, v_l3_w_out, v_l3_ln_g, v_l3_ln_b)
    W = dict(zip(WEIGHTS, args))
    M = dict(zip(WEIGHTS, moms))
    V = dict(zip(WEIGHTS, vels))

    seq = x.shape[1]
    t_real = N_META + seq
    t_pad = -(-t_real // ROW_ALIGN) * ROW_ALIGN
    tgt_pad = jnp.pad(loss_target[0], ((N_META, t_pad - t_real), (0, 0)))

    sch = _Schedule()
    for n in GATHER_ORDER:
        shard = W[n].astype(BF) if n in BIG else W[n]
        sch.push(n, shard, True, shard.ndim - 2)
    S = {n: W[n] for n in REPLICATED}

    loss, gx, gS = _train_local(sch, x[0], tgt_pad, S, t_pad=t_pad)

    flat = jnp.concatenate([gS[n].reshape(-1) for n in REPLICATED]).reshape(-1, 128)
    sch.push('small_grads', flat, True, 0)

    out_g, out_d, out_m, out_v = {}, {}, {}, {}
    order = ['l3_w_out', 'l3_w_in', 'l2_w_out', 'l2_w_ukv', 'l2_w_uq', 'l2_w_in', 'l1_w_out', 'l1_w_grp',
             'l1_w_in', 'l0_w_out', 'l0_w_a', 'l0_w_x', 'l0_conv_w', 'l0_w_in', 'meta_tokens']
    for n in order:
        shp = W[n].shape
        w2, m2, v2 = _as2d(W[n]), _as2d(M[n]), _as2d(V[n])
        parts = [p.reshape((N_DEV, -1, w2.shape[1])) for p in sch.get(n)]
        res = _ride(sch, w2.size * ADAM_MS_PER_ELEM, _adamw_call, contribs=parts, w=w2, m=m2, v=v2, name='adamw_' + n)
        out_g[n], out_d[n], out_m[n], out_v[n] = [r.reshape(shp) for r in res]
    cat = lambda D: jnp.concatenate([D[n].reshape(-1) for n in REPLICATED]).reshape(-1, 128)
    res = _adamw_call(sch.get('small_grads'), cat(W), cat(M), cat(V), name='adamw_small')[0]
    off = 0
    for n in REPLICATED:
        size = W[n].size
        for dst, r in zip((out_g, out_d, out_m, out_v), res):
            dst[n] = r.reshape(-1)[off:off + size].reshape(W[n].shape)
        off += size
    sch.flush()

    loss = lax.psum(loss, ("x", "y", "c"))
    return (loss, gx[None], *[out_g[n] for n in WEIGHTS], *[out_d[n] for n in WEIGHTS],
            *[out_m[n] for n in WEIGHTS], *[out_v[n] for n in WEIGHTS])
```
